```python
import math
import jax, jax.numpy as jnp
from jax import lax
import numpy as np

D_MODEL = 1024
BATCH = 16
SEQ = 2048
DEPTH = 2
DEC_BATCH = 128
DEC_SEQ = 4
PAST_LEN = 16384
PAGE_SIZE = 128

MIX_WIDTH = (3 * D_MODEL) // 2
GROUP_WIDTH = MIX_WIDTH // 3
RET_HEADS = 4
RET_DK = GROUP_WIDTH // RET_HEADS
RET_DV = GROUP_WIDTH // RET_HEADS
RET_CHUNK = 128
LRU_WIDTH = GROUP_WIDTH
LRU_BLOCKS = 8
LRU_BLOCK_W = LRU_WIDTH // LRU_BLOCKS
LRU_C = 8.0
CONV_W = 4
SWA_HEADS = 4
SWA_KV_HEADS = 2
SWA_GROUP = SWA_HEADS // SWA_KV_HEADS
SWA_HEAD_DIM = GROUP_WIDTH // SWA_HEADS
WINDOW = 128
NORM_EPS = 1e-6
NEG_INF = -1e30

SPLIT_SIZES = (RET_HEADS * RET_DK, RET_HEADS * RET_DK, RET_HEADS * RET_DV, RET_HEADS * RET_DV,
               LRU_WIDTH, LRU_WIDTH,
               SWA_HEADS * SWA_HEAD_DIM, SWA_KV_HEADS * SWA_HEAD_DIM, SWA_KV_HEADS * SWA_HEAD_DIM,
               SWA_HEADS * SWA_HEAD_DIM)
IN_WIDTH = sum(SPLIT_SIZES)
SPLIT_IDX = tuple(sum(SPLIT_SIZES[:i + 1]) for i in range(len(SPLIT_SIZES) - 1))

kernel_name = 'hymba_retention_rglru_swasink_step'


def _rmsnorm(x, g):
    x32 = x.astype(jnp.float32)
    y = x32 * lax.rsqrt(jnp.mean(x32 * x32, axis=-1, keepdims=True) + NORM_EPS)
    return (y * g.astype(jnp.float32)).astype(x.dtype)


def _retention(q, k, v, s0):
    B, T, H, DK = q.shape
    DV = v.shape[-1]
    C = math.gcd(T, RET_CHUNK)
    n = T // C
    f32 = jnp.float32
    log_g = jnp.log1p(-jnp.power(2.0, -5.0 - jnp.arange(H, dtype=f32)))
    qc = q.astype(f32).reshape(B, n, C, H, DK)
    kc = (k.astype(f32) * (DK ** -0.5)).reshape(B, n, C, H, DK)
    vc = v.astype(f32).reshape(B, n, C, H, DV)
    idx = jnp.arange(C, dtype=f32)
    diff = idx[:, None] - idx[None, :]
    causal = diff >= 0
    decay_mask = jnp.where(causal[None], jnp.exp(jnp.where(causal, diff, 0.0)[None] * log_g[:, None, None]), 0.0)
    scores = jnp.einsum('bnihd,bnjhd->bnhij', qc, kc) * decay_mask[None, None]
    intra = jnp.einsum('bnhij,bnjhe->bnihe', scores, vc)
    k_dec = kc * jnp.exp((C - 1 - idx)[:, None] * log_g[None, :])[None, None, :, :, None]
    kv = jnp.einsum('bnjhd,bnjhe->nbhde', k_dec, vc)
    chunk_decay = jnp.exp(C * log_g)[None, :, None, None]

    def step(S, kv_c):
        return chunk_decay * S + kv_c, S

    s_final, s_prev = lax.scan(step, s0.astype(f32), kv)
    q_dec = qc * jnp.exp((idx + 1.0)[:, None] * log_g[None, :])[None, None, :, :, None]
    cross = jnp.einsum('bnihd,nbhde->bnihe', q_dec, s_prev)
    o = (intra + cross).reshape(B, T, H, DV)
    return o.astype(q.dtype), s_final.astype(s0.dtype)


def _lin_combine(left, right):
    a1, b1 = left
    a2, b2 = right
    return a1 * a2, a2 * b1 + b2


def _rglru(xb, conv_buf, h0, pos0, conv_w, conv_b, w_r, b_r, w_i, b_i, lam):
    B, T, W = xb.shape
    f32 = jnp.float32
    xc = jnp.concatenate([conv_buf.astype(xb.dtype), xb], axis=1)
    conv = conv_b.astype(f32)[None, None, :]
    for j in range(CONV_W):
        conv = conv + conv_w[j].astype(f32) * xc[:, j:j + T].astype(f32)
    new_buf = xc[:, T:]
    xg = conv.reshape(B, T, LRU_BLOCKS, LRU_BLOCK_W)
    r = jax.nn.sigmoid(jnp.einsum('btnd,nde->btne', xg, w_r.astype(f32)).reshape(B, T, W) + b_r.astype(f32))
    i = jax.nn.sigmoid(jnp.einsum('btnd,nde->btne', xg, w_i.astype(f32)).reshape(B, T, W) + b_i.astype(f32))
    log_a = -LRU_C * r * jax.nn.softplus(-lam.astype(f32))
    a = jnp.exp(log_a)
    pos = pos0 + jnp.arange(T)
    mult = jnp.where((pos == 0)[None, :, None], 1.0, jnp.sqrt(-jnp.expm1(2.0 * log_a)))
    u = mult * i * conv
    u = u.at[:, 0].add(a[:, 0] * h0.astype(f32))
    _, h = lax.associative_scan(_lin_combine, (a, u), axis=1)
    return h.astype(xb.dtype), h[:, -1].astype(h0.dtype), new_buf.astype(conv_buf.dtype)


def _alibi_slopes():
    return jnp.power(2.0, -8.0 * jnp.arange(1, SWA_HEADS + 1, dtype=jnp.float32) / SWA_HEADS)


def _swa_core(q, k, v, q_pos, k_pos, sinks):
    f32 = jnp.float32
    s = jnp.einsum('bnqhgd,bnkhd->bnhgqk', q.astype(f32), k.astype(f32)) * (SWA_HEAD_DIM ** -0.5)
    dist = q_pos[:, :, None] - k_pos[:, None, :]
    valid = (dist >= 0) & (dist < WINDOW) & (k_pos[:, None, :] >= 0)
    slopes = _alibi_slopes().reshape(SWA_KV_HEADS, SWA_GROUP)
    s = s - slopes[None, None, :, :, None, None] * dist.astype(f32)[None, :, None, None]
    s = jnp.where(valid[None, :, None, None], s, NEG_INF)
    sink = sinks.astype(f32).reshape(SWA_KV_HEADS, SWA_GROUP)[None, None, :, :, None, None]
    m = jnp.maximum(jnp.max(s, axis=-1, keepdims=True), sink)
    p = jnp.exp(s - m)
    denom = jnp.sum(p, axis=-1, keepdims=True) + jnp.exp(sink - m)
    return jnp.einsum('bnhgqk,bnkhd->bnqhgd', p / denom, v.astype(f32))


def _swa_prompt(q, k, v, sinks):
    B, T = q.shape[:2]
    nb = T // WINDOW
    qb = q.reshape(B, nb, WINDOW, SWA_KV_HEADS, SWA_GROUP, SWA_HEAD_DIM)
    kb = k.reshape(B, nb, WINDOW, SWA_KV_HEADS, SWA_HEAD_DIM)
    vb = v.reshape(B, nb, WINDOW, SWA_KV_HEADS, SWA_HEAD_DIM)
    k_band = jnp.concatenate([jnp.concatenate([jnp.zeros_like(kb[:, :1]), kb[:, :-1]], axis=1), kb], axis=2)
    v_band = jnp.concatenate([jnp.concatenate([jnp.zeros_like(vb[:, :1]), vb[:, :-1]], axis=1), vb], axis=2)
    pos = jnp.arange(T, dtype=jnp.int32).reshape(nb, WINDOW)
    k_pos = jnp.concatenate([pos - WINDOW, pos], axis=1)
    o = _swa_core(qb, k_band, v_band, pos, k_pos, sinks)
    return o.reshape(B, T, SWA_HEADS * SWA_HEAD_DIM).astype(q.dtype), k[:, T - WINDOW:], v[:, T - WINDOW:]


def _swa_sample(q, k, v, kbuf, vbuf, pos0, sinks):
    B, T = q.shape[:2]
    k_all = jnp.concatenate([kbuf.astype(k.dtype), k], axis=1)
    v_all = jnp.concatenate([vbuf.astype(v.dtype), v], axis=1)
    q_pos = (pos0 + jnp.arange(T, dtype=jnp.int32))[None]
    k_pos = (pos0 - WINDOW + jnp.arange(WINDOW + T, dtype=jnp.int32))[None]
    qb = q.reshape(B, 1, T, SWA_KV_HEADS, SWA_GROUP, SWA_HEAD_DIM)
    o = _swa_core(qb, k_all[:, None], v_all[:, None], q_pos, k_pos, sinks)
    new_k = k_all[:, T:].astype(kbuf.dtype)
    new_v = v_all[:, T:].astype(vbuf.dtype)
    return o.reshape(B, T, SWA_HEADS * SWA_HEAD_DIM).astype(q.dtype), new_k, new_v


def _layer(x, pos0, s0, h0, conv_buf, kbuf, vbuf, norm_g, w_in, w_out, ret_g, conv_w, conv_b,
           w_r, b_r, w_i, b_i, lam, q_g, k_g, sinks):
    B, T, _ = x.shape
    h = _rmsnorm(x, norm_g)
    proj = jnp.einsum('btd,de->bte', h, w_in)
    qa, ka, va, ga, xb, gb, qc, kc, vc, gc = jnp.split(proj, SPLIT_IDX, axis=-1)
    oa, s_new = _retention(qa.reshape(B, T, RET_HEADS, RET_DK), ka.reshape(B, T, RET_HEADS, RET_DK),
                           va.reshape(B, T, RET_HEADS, RET_DV), s0)
    oa = _rmsnorm(oa, ret_g.reshape(RET_HEADS, RET_DV)).reshape(B, T, RET_HEADS * RET_DV)
    ob, h_new, conv_new = _rglru(xb, conv_buf, h0, pos0, conv_w, conv_b, w_r, b_r, w_i, b_i, lam)
    qc = _rmsnorm(qc.reshape(B, T, SWA_HEADS, SWA_HEAD_DIM), q_g)
    kc = _rmsnorm(kc.reshape(B, T, SWA_KV_HEADS, SWA_HEAD_DIM), k_g)
    vc = vc.reshape(B, T, SWA_KV_HEADS, SWA_HEAD_DIM)
    if kbuf is None:
        oc, k_new, v_new = _swa_prompt(qc, kc, vc, sinks)
    else:
        oc, k_new, v_new = _swa_sample(qc, kc, vc, kbuf, vbuf, pos0, sinks)
    mixed = jnp.concatenate([oa * jax.nn.silu(ga), ob * jax.nn.silu(gb), oc * jax.nn.silu(gc)], axis=-1)
    y = x + jnp.einsum('bte,ed->btd', mixed, w_out)
    return y, s_new, h_new, conv_new, k_new, v_new


def setup_inputs(seed: int = 0) -> dict:
    key = jax.random.key(seed)
    ks = jax.random.split(key, 24)
    f32 = jnp.float32
    nrm = lambda k, shape: jax.random.normal(k, shape, dtype=f32)
    u = jax.random.uniform(ks[19], (DEPTH, LRU_WIDTH), dtype=f32, minval=0.9, maxval=0.999)
    p = jnp.power(u, 1.0 / LRU_C)
    lru_lambda = jnp.log(p) - jnp.log1p(-p)
    return {
        'x_prompt': nrm(ks[0], (BATCH, SEQ, D_MODEL)),
        'x_sample': nrm(ks[1], (DEC_BATCH, DEC_SEQ, D_MODEL)),
        'state_ret': 0.5 * nrm(ks[2], (DEPTH, DEC_BATCH, RET_HEADS, RET_DK, RET_DV)),
        'state_lru': 0.5 * nrm(ks[3], (DEPTH, DEC_BATCH, LRU_WIDTH)),
        'state_conv': nrm(ks[4], (DEPTH, DEC_BATCH, CONV_W - 1, LRU_WIDTH)),
        'cache_swa_k': nrm(ks[5], (DEPTH, DEC_BATCH, WINDOW, SWA_KV_HEADS, SWA_HEAD_DIM)),
        'cache_swa_v': nrm(ks[6], (DEPTH, DEC_BATCH, WINDOW, SWA_KV_HEADS, SWA_HEAD_DIM)),
        'norm_g': 1.0 + 0.02 * nrm(ks[7], (DEPTH, D_MODEL)),
        'w_in': nrm(ks[8], (DEPTH, D_MODEL, IN_WIDTH)) * D_MODEL ** -0.5,
        'w_out': nrm(ks[9], (DEPTH, MIX_WIDTH, D_MODEL)) * MIX_WIDTH ** -0.5,
        'ret_norm_g': 1.0 + 0.02 * nrm(ks[10], (DEPTH, RET_HEADS * RET_DV)),
        'conv_w': nrm(ks[11], (DEPTH, CONV_W, LRU_WIDTH)) * CONV_W ** -0.5,
        'conv_b': 0.02 * nrm(ks[12], (DEPTH, LRU_WIDTH)),
        'w_rgate': nrm(ks[13], (DEPTH, LRU_BLOCKS, LRU_BLOCK_W, LRU_BLOCK_W)) * LRU_BLOCK_W ** -0.5,
        'b_rgate': 0.02 * nrm(ks[14], (DEPTH, LRU_WIDTH)),
        'w_igate': nrm(ks[15], (DEPTH, LRU_BLOCKS, LRU_BLOCK_W, LRU_BLOCK_W)) * LRU_BLOCK_W ** -0.5,
        'b_igate': 0.02 * nrm(ks[16], (DEPTH, LRU_WIDTH)),
        'lru_lambda': lru_lambda,
        'q_norm_g': 1.0 + 0.02 * nrm(ks[17], (DEPTH, SWA_HEAD_DIM)),
        'k_norm_g': 1.0 + 0.02 * nrm(ks[18], (DEPTH, SWA_HEAD_DIM)),
        'attn_sinks': nrm(ks[20], (DEPTH, SWA_HEADS)),
    }


def reference(x_prompt, x_sample, state_ret, state_lru, state_conv, cache_swa_k, cache_swa_v,
              norm_g, w_in, w_out, ret_norm_g, conv_w, conv_b, w_rgate, b_rgate, w_igate, b_igate,
              lru_lambda, q_norm_g, k_norm_g, attn_sinks):
    yp, ys = x_prompt, x_sample
    Bp = x_prompt.shape[0]
    dt = x_prompt.dtype
    p_ret, p_lru, p_conv, p_k, p_v = [], [], [], [], []
    s_ret, s_lru, s_conv, s_k, s_v = [], [], [], [], []
    for l in range(DEPTH):
        wts = (norm_g[l], w_in[l], w_out[l], ret_norm_g[l], conv_w[l], conv_b[l],
               w_rgate[l], b_rgate[l], w_igate[l], b_igate[l], lru_lambda[l],
               q_norm_g[l], k_norm_g[l], attn_sinks[l])
        yp, a1, a2, a3, a4, a5 = _layer(
            yp, 0,
            jnp.zeros((Bp, RET_HEADS, RET_DK, RET_DV), dt),
            jnp.zeros((Bp, LRU_WIDTH), dt),
            jnp.zeros((Bp, CONV_W - 1, LRU_WIDTH), dt),
            None, None, *wts)
        p_ret.append(a1); p_lru.append(a2); p_conv.append(a3); p_k.append(a4); p_v.append(a5)
        ys, b1, b2, b3, b4, b5 = _layer(
            ys, PAST_LEN, state_ret[l], state_lru[l], state_conv[l], cache_swa_k[l], cache_swa_v[l], *wts)
        s_ret.append(b1); s_lru.append(b2); s_conv.append(b3); s_k.append(b4); s_v.append(b5)
    return (yp, ys,
            jnp.stack(p_ret), jnp.stack(p_lru), jnp.stack(p_conv), jnp.stack(p_k), jnp.stack(p_v),
            jnp.stack(s_ret), jnp.stack(s_lru), jnp.stack(s_conv), jnp.stack(s_k), jnp.stack(s_v))
```

```python
import functools
import math

import numpy as np
import jax
import jax.numpy as jnp
from jax import lax
from jax.experimental import pallas as pl
from jax.experimental.pallas import tpu as pltpu

D_MODEL = 1024
DEPTH = 2
PAST_LEN = 16384
GROUP_WIDTH = 512
RET_HEADS = 4
RET_DK = 128
RET_DV = 128
RET_CHUNK = 128
LRU_WIDTH = 512
LRU_BLOCKS = 8
LRU_C = 8.0
CONV_W = 4
SWA_HEADS = 4
SWA_KV_HEADS = 2
SWA_GROUP = 2
SWA_HEAD_DIM = 128
WINDOW = 128
NORM_EPS = 1e-6
NEG_INF = -1e30

IN_WIDTH = 4608
MIX_WIDTH = 1536
QA, KA, VA, GA, XB, GB, QC, KC, VC, GC = 0, 512, 1024, 1536, 2048, 2560, 3072, 3584, 3840, 4096

F32 = jnp.float32
BF16 = jnp.bfloat16

SUBLANES = 8
SEQ_PAD = 8
TOK0 = 3
VMEM_LIMIT = 56 * 1024 * 1024

_LOG_G = np.log1p(-np.power(np.float32(2.0), (-5.0 - np.arange(RET_HEADS)).astype(np.float32))).astype(np.float32)
_SLOPES = [2.0 ** (-8.0 * (h + 1) / SWA_HEADS) for h in range(SWA_HEADS)]


def _rms(x, g):
    ms = jnp.mean(x * x, axis=-1, keepdims=True)
    return x * lax.rsqrt(ms + NORM_EPS) * g


def _silu(x):
    return x * jax.nn.sigmoid(x)


def _softplus(x):
    return jnp.maximum(x, 0.0) + jnp.log1p(jnp.exp(-jnp.abs(x)))


def _dot(a, b):
    return jnp.dot(a, b, preferred_element_type=F32)


def _dot_nt(a, b):
    return lax.dot_general(a, b, (((1,), (1,)), ((), ())), preferred_element_type=F32)


def _dot_tn(a, b):
    return lax.dot_general(a, b, (((0,), (0,)), ((), ())), preferred_element_type=F32)


def _lru_gates(conv, wr_ref, wi_ref, br_ref, bi_ref, lam_ref):
    gin = conv.astype(BF16)
    r = jax.nn.sigmoid(_dot(gin, wr_ref[...]) + br_ref[...])
    i = jax.nn.sigmoid(_dot(gin, wi_ref[...]) + bi_ref[...])
    log_a = -LRU_C * r * _softplus(-lam_ref[...])
    a = jnp.exp(log_a)
    th = jnp.tanh(log_a)
    mult = jnp.sqrt(-2.0 * th / (1.0 - th))
    return a, mult, i


def _prompt_kernel(sinks_ref, x_ref, ng_ref, win_ref, wout_ref, retg_ref, convw_ref, convb_ref,
                   wr_ref, wi_ref, br_ref, bi_ref, lam_ref, qg_ref, kg_ref,
                   dmask_ref, qdec_ref, kdec_ref,
                   y_ref, s_ref, h_ref, conv_ref, kout_ref, vout_ref,
                   proj_s, mixed_s, xpad_s, kprev_s, vprev_s, hc_s, *, tt, nt):
    t = pl.program_id(1)
    nch = tt // RET_CHUNK
    C = RET_CHUNK

    @pl.when(t == 0)
    def _init():
        s_ref[...] = jnp.zeros_like(s_ref)
        xpad_s[0:SUBLANES, :] = jnp.zeros((SUBLANES, LRU_WIDTH), F32)
        kprev_s[...] = jnp.zeros_like(kprev_s)
        vprev_s[...] = jnp.zeros_like(vprev_s)
        hc_s[...] = jnp.zeros_like(hc_s)

    hb = _rms(x_ref[0], ng_ref[...]).astype(BF16)
    for j in range(IN_WIDTH // 512):
        proj_s[:, j * 512:(j + 1) * 512] = _dot(hb, win_ref[:, j * 512:(j + 1) * 512])

    xb = proj_s[:, XB:XB + LRU_WIDTH]
    xpad_s[SUBLANES:SUBLANES + tt, :] = xb
    conv = convb_ref[...] + convw_ref[0:1, :] * xpad_s[SUBLANES - 3:SUBLANES - 3 + tt, :]
    conv = conv + convw_ref[1:2, :] * xpad_s[SUBLANES - 2:SUBLANES - 2 + tt, :]
    conv = conv + convw_ref[2:3, :] * xpad_s[SUBLANES - 1:SUBLANES - 1 + tt, :]
    conv = conv + convw_ref[3:4, :] * xb

    @pl.when(t == nt - 1)
    def _conv_out():
        conv_ref[0] = xpad_s[SUBLANES + tt - 3:SUBLANES + tt, :]

    xpad_s[0:SUBLANES, :] = xpad_s[tt:tt + SUBLANES, :]

    a, mult, ig = _lru_gates(conv, wr_ref, wi_ref, br_ref, bi_ref, lam_ref)
    row = lax.broadcasted_iota(jnp.int32, (tt, LRU_WIDTH), 0)
    mult = jnp.where(row == jnp.where(t == 0, 0, -1), 1.0, mult)
    u = mult * ig * conv
    row8 = lax.broadcasted_iota(jnp.int32, (SUBLANES, LRU_WIDTH), 0)
    u = jnp.concatenate(
        [u[:SUBLANES] + jnp.where(row8 == 0, a[:SUBLANES] * hc_s[...], 0.0), u[SUBLANES:]], axis=0)
    sh = 1
    while sh < tt:
        if sh < SUBLANES:
            a_sh = jnp.where(row >= sh, pltpu.roll(a, sh, axis=0), 1.0)
            u_sh = jnp.where(row >= sh, pltpu.roll(u, sh, axis=0), 0.0)
            u = a * u_sh + u
            a = a * a_sh
        else:
            u = jnp.concatenate([u[:sh], a[sh:] * u[:tt - sh] + u[sh:]], axis=0)
            if sh * 2 < tt:
                a = jnp.concatenate([a[:sh], a[sh:] * a[:tt - sh]], axis=0)
        sh *= 2
    hc_s[...] = u[tt - 1:tt, :]
    mixed_s[:, GROUP_WIDTH:2 * GROUP_WIDTH] = u * _silu(proj_s[:, GB:GB + LRU_WIDTH])

    @pl.when(t == nt - 1)
    def _h_out():
        h_ref[0] = hc_s[...]

    ii = lax.broadcasted_iota(jnp.int32, (2 * C, 2 * C), 0)
    jj = lax.broadcasted_iota(jnp.int32, (2 * C, 2 * C), 1)
    qi = jnp.where(ii >= C, ii - C, ii)
    dist = qi + C - jj
    distf = dist.astype(F32)
    in_window = jnp.logical_and(dist >= 0, dist < WINDOW)
    rowc = lax.broadcasted_iota(jnp.int32, (2 * C, 1), 0)

    for c in range(nch):
        r0 = c * C
        blk = t * nch + c
        kmin = jnp.where(blk > 0, 0, C)
        valid = jnp.logical_and(in_window, jj >= kmin)

        for h in range(RET_HEADS):
            cs = slice(h * RET_DK, (h + 1) * RET_DK)
            q = proj_s[r0:r0 + C, QA + h * 128:QA + (h + 1) * 128]
            k = proj_s[r0:r0 + C, KA + h * 128:KA + (h + 1) * 128] * (RET_DK ** -0.5)
            vb = proj_s[r0:r0 + C, VA + h * 128:VA + (h + 1) * 128].astype(BF16)
            sc = _dot_nt(q.astype(BF16), k.astype(BF16)) * dmask_ref[h]
            intra = _dot(sc.astype(BF16), vb)
            s_prev = s_ref[0, h]
            cross = _dot((q * qdec_ref[h]).astype(BF16), s_prev.astype(BF16))
            kv = _dot_tn((k * kdec_ref[h]).astype(BF16), vb)
            s_ref[0, h] = float(np.exp(np.float32(C) * _LOG_G[h])) * s_prev + kv
            o = _rms(intra + cross, retg_ref[:, cs])
            mixed_s[r0:r0 + C, cs] = o * _silu(proj_s[r0:r0 + C, GA + h * 128:GA + (h + 1) * 128])

        for kh in range(SWA_KV_HEADS):
            ks = slice(kh * 128, (kh + 1) * 128)
            kn = _rms(proj_s[r0:r0 + C, KC + kh * 128:KC + (kh + 1) * 128], kg_ref[...])
            vv = proj_s[r0:r0 + C, VC + kh * 128:VC + (kh + 1) * 128]
            kband = jnp.concatenate([kprev_s[:, ks], kn], axis=0).astype(BF16)
            vband = jnp.concatenate([vprev_s[:, ks], vv], axis=0).astype(BF16)
            h0, h1 = kh * SWA_GROUP, kh * SWA_GROUP + 1
            q0 = _rms(proj_s[r0:r0 + C, QC + h0 * 128:QC + (h0 + 1) * 128], qg_ref[...])
            q1 = _rms(proj_s[r0:r0 + C, QC + h1 * 128:QC + (h1 + 1) * 128], qg_ref[...])
            qq = jnp.concatenate([q0, q1], axis=0).astype(BF16)
            s = _dot_nt(qq, kband) * (SWA_HEAD_DIM ** -0.5)
            slope = jnp.where(ii >= C, _SLOPES[h1], _SLOPES[h0])
            s = jnp.where(valid, s - slope * distf, NEG_INF)
            sink = jnp.where(rowc >= C, sinks_ref[h1], sinks_ref[h0])
            m = jnp.maximum(jnp.max(s, axis=-1, keepdims=True), sink)
            p = jnp.exp(s - m)
            denom = jnp.sum(p, axis=-1, keepdims=True) + jnp.exp(sink - m)
            o = _dot((p / denom).astype(BF16), vband)
            for g, hh in enumerate((h0, h1)):
                gc = proj_s[r0:r0 + C, GC + hh * 128:GC + (hh + 1) * 128]
                mixed_s[r0:r0 + C, 2 * GROUP_WIDTH + hh * 128:2 * GROUP_WIDTH + (hh + 1) * 128] = (
                    o[g * C:(g + 1) * C] * _silu(gc))
            kprev_s[:, ks] = kn
            vprev_s[:, ks] = vv

    @pl.when(t == nt - 1)
    def _kv_out():
        kout_ref[0] = kprev_s[...]
        vout_ref[0] = vprev_s[...]

    y_ref[0] = x_ref[0] + _dot(mixed_s[...].astype(BF16), wout_ref[...])


def _const_spec(shape):
    nd = len(shape)
    return pl.BlockSpec(shape, lambda *_: (0,) * nd)


def _prompt_layer(x, wts, tables, tt=256):
    B, T, D = x.shape
    nt = T // tt
    (sinks, ng, win, wout, retg, convw, convb, wr, wi, br, bi, lam, qg, kg) = wts
    dmask, qdec, kdec = tables
    vec_in = [ng, win, wout, retg, convw, convb, wr, wi, br, bi, lam, qg, kg, dmask, qdec, kdec]
    in_specs = ([pl.BlockSpec(memory_space=pltpu.SMEM),
                 pl.BlockSpec((1, tt, D), lambda b, t: (b, t, 0))]
                + [_const_spec(a.shape) for a in vec_in])
    out_shape = (
        jax.ShapeDtypeStruct((B, T, D), F32),
        jax.ShapeDtypeStruct((B, RET_HEADS, RET_DK, RET_DV), F32),
        jax.ShapeDtypeStruct((B, 1, LRU_WIDTH), F32),
        jax.ShapeDtypeStruct((B, CONV_W - 1, LRU_WIDTH), F32),
        jax.ShapeDtypeStruct((B, WINDOW, SWA_KV_HEADS * SWA_HEAD_DIM), F32),
        jax.ShapeDtypeStruct((B, WINDOW, SWA_KV_HEADS * SWA_HEAD_DIM), F32),
    )
    out_specs = (
        pl.BlockSpec((1, tt, D), lambda b, t: (b, t, 0)),
        pl.BlockSpec((1, RET_HEADS, RET_DK, RET_DV), lambda b, t: (b, 0, 0, 0)),
        pl.BlockSpec((1, 1, LRU_WIDTH), lambda b, t: (b, 0, 0)),
        pl.BlockSpec((1, CONV_W - 1, LRU_WIDTH), lambda b, t: (b, 0, 0)),
        pl.BlockSpec((1, WINDOW, 256), lambda b, t: (b, 0, 0)),
        pl.BlockSpec((1, WINDOW, 256), lambda b, t: (b, 0, 0)),
    )
    scratch = [
        pltpu.VMEM((tt, IN_WIDTH), F32),
        pltpu.VMEM((tt, MIX_WIDTH), F32),
        pltpu.VMEM((tt + SUBLANES, LRU_WIDTH), F32),
        pltpu.VMEM((WINDOW, 256), F32),
        pltpu.VMEM((WINDOW, 256), F32),
        pltpu.VMEM((1, LRU_WIDTH), F32),
    ]
    return pl.pallas_call(
        functools.partial(_prompt_kernel, tt=tt, nt=nt),
        grid=(B, nt),
        in_specs=in_specs,
        out_specs=out_specs,
        out_shape=out_shape,
        scratch_shapes=scratch,
        compiler_params=pltpu.CompilerParams(
            dimension_semantics=("arbitrary", "arbitrary"), vmem_limit_bytes=VMEM_LIMIT),
        name="prompt_layer",
    )(sinks, x, *vec_in)


def _sproj_kernel(x_ref, ng_ref, win_ref, o_ref):
    hb = _rms(x_ref[...], ng_ref[...]).astype(BF16)
    o_ref[...] = _dot(hb, win_ref[...])


def _sample_proj(xp, ng, win):
    R = xp.shape[0]
    nb = IN_WIDTH // 512
    return pl.pallas_call(
        _sproj_kernel,
        grid=(nb,),
        in_specs=[_const_spec(xp.shape), _const_spec(ng.shape),
                  pl.BlockSpec((D_MODEL, 512), lambda j: (0, j))],
        out_specs=pl.BlockSpec((R, 512), lambda j: (0, j)),
        out_shape=jax.ShapeDtypeStruct((R, IN_WIDTH), F32),
        compiler_params=pltpu.CompilerParams(
            dimension_semantics=("arbitrary",), vmem_limit_bytes=VMEM_LIMIT),
        name="sample_proj",
    )(xp, ng, win)


def _sample_mix_kernel(sinks_ref, proj_ref, s0_ref, kbuf_ref, vbuf_ref, retg_ref, qg_ref, kg_ref,
                       cm_ref, qdec_ref, kdec_ref,
                       mix_ref, snew_ref, knew_ref, vnew_ref):
    P = SEQ_PAD
    for h in range(RET_HEADS):
        cs = slice(h * 128, (h + 1) * 128)
        q = proj_ref[:, QA + h * 128:QA + (h + 1) * 128]
        k = proj_ref[:, KA + h * 128:KA + (h + 1) * 128] * (RET_DK ** -0.5)
        v = proj_ref[:, VA + h * 128:VA + (h + 1) * 128]
        intra = jnp.zeros((P, RET_DV), F32)
        for s in range(4):
            r_ = TOK0 + s
            w = jnp.sum(q * k[r_:r_ + 1, :], axis=-1, keepdims=True)
            intra = intra + (w * cm_ref[h, s]) * v[r_:r_ + 1, :]
        s_prev = s0_ref[0, h]
        cross = _dot((q * qdec_ref[h]).astype(BF16), s_prev.astype(BF16))
        kv = _dot_tn((k * kdec_ref[h]).astype(BF16), v.astype(BF16))
        snew_ref[0, h] = float(np.exp(np.float32(4.0) * _LOG_G[h])) * s_prev + kv
        o = _rms(intra + cross, retg_ref[:, cs])
        mix_ref[:, cs] = o * _silu(proj_ref[:, GA + h * 128:GA + (h + 1) * 128])

    r16 = lax.broadcasted_iota(jnp.int32, (2 * P, WINDOW), 0)
    j16 = lax.broadcasted_iota(jnp.int32, (2 * P, WINDOW), 1)
    rr = jnp.where(r16 >= P, r16 - P, r16)
    row_ok = jnp.logical_and(rr >= TOK0, rr < TOK0 + 4)
    dist = (rr - TOK0) + WINDOW - j16
    distf = dist.astype(F32)
    valid = jnp.logical_and(jnp.logical_and(dist >= 0, dist < WINDOW), row_ok)
    r16c = lax.broadcasted_iota(jnp.int32, (2 * P, 1), 0)
    rrc = jnp.where(r16c >= P, r16c - P, r16c)
    rowc_ok = jnp.logical_and(rrc >= TOK0, rrc < TOK0 + 4)
    for kh in range(SWA_KV_HEADS):
        ks = slice(kh * 128, (kh + 1) * 128)
        h0, h1 = kh * SWA_GROUP, kh * SWA_GROUP + 1
        kb = kbuf_ref[0, :, ks]
        vb = vbuf_ref[0, :, ks]
        kn = _rms(proj_ref[:, KC + kh * 128:KC + (kh + 1) * 128], kg_ref[...])
        vn = proj_ref[:, VC + kh * 128:VC + (kh + 1) * 128]
        q0 = _rms(proj_ref[:, QC + h0 * 128:QC + (h0 + 1) * 128], qg_ref[...])
        q1 = _rms(proj_ref[:, QC + h1 * 128:QC + (h1 + 1) * 128], qg_ref[...])
        qq = jnp.concatenate([q0, q1], axis=0)
        slope = jnp.where(r16 >= P, _SLOPES[h1], _SLOPES[h0])
        slopec = jnp.where(r16c >= P, _SLOPES[h1], _SLOPES[h0])
        sb = _dot_nt(qq.astype(BF16), kb.astype(BF16)) * (SWA_HEAD_DIM ** -0.5)
        sb = jnp.where(valid, sb - slope * distf, NEG_INF)
        sink = jnp.where(r16c >= P, sinks_ref[h1], sinks_ref[h0])
        m = jnp.maximum(jnp.max(sb, axis=-1, keepdims=True), sink)
        wn = []
        for s in range(4):
            r_ = TOK0 + s
            w = jnp.sum(qq * kn[r_:r_ + 1, :], axis=-1, keepdims=True) * (SWA_HEAD_DIM ** -0.5)
            dn = rrc - r_
            w = jnp.where(jnp.logical_and(dn >= 0, rowc_ok), w - slopec * dn.astype(F32), NEG_INF)
            wn.append(w)
            m = jnp.maximum(m, w)
        pb = jnp.exp(sb - m)
        denom = jnp.sum(pb, axis=-1, keepdims=True) + jnp.exp(sink - m)
        pn = [jnp.exp(w - m) for w in wn]
        for p_ in pn:
            denom = denom + p_
        o = _dot((pb / denom).astype(BF16), vb.astype(BF16))
        for s in range(4):
            r_ = TOK0 + s
            o = o + (pn[s] / denom) * vn[r_:r_ + 1, :]
        for g, hh in enumerate((h0, h1)):
            gc = proj_ref[:, GC + hh * 128:GC + (hh + 1) * 128]
            mix_ref[:, GROUP_WIDTH + hh * 128:GROUP_WIDTH + (hh + 1) * 128] = o[g * P:(g + 1) * P] * _silu(gc)
        knew_ref[0, 0:WINDOW - 4, ks] = kbuf_ref[0, 4:WINDOW, ks]
        knew_ref[0, WINDOW - 4:WINDOW, ks] = kn[TOK0:TOK0 + 4, :]
        vnew_ref[0, 0:WINDOW - 4, ks] = vbuf_ref[0, 4:WINDOW, ks]
        vnew_ref[0, WINDOW - 4:WINDOW, ks] = vn[TOK0:TOK0 + 4, :]


def _sample_mix(sinks, proj, s0, kbuf, vbuf, retg, qg, kg, tables):
    cm, qdec, kdec = tables
    B = s0.shape[0]
    consts = [retg, qg, kg, cm, qdec, kdec]
    in_specs = ([pl.BlockSpec(memory_space=pltpu.SMEM),
                 pl.BlockSpec((SEQ_PAD, IN_WIDTH), lambda b: (b, 0)),
                 pl.BlockSpec((1, RET_HEADS, RET_DK, RET_DV), lambda b: (b, 0, 0, 0)),
                 pl.BlockSpec((1, WINDOW, 256), lambda b: (b, 0, 0)),
                 pl.BlockSpec((1, WINDOW, 256), lambda b: (b, 0, 0))]
                + [_const_spec(a.shape) for a in consts])
    out_shape = (
        jax.ShapeDtypeStruct((B * SEQ_PAD, 2 * GROUP_WIDTH), F32),
        jax.ShapeDtypeStruct((B, RET_HEADS, RET_DK, RET_DV), F32),
        jax.ShapeDtypeStruct((B, WINDOW, 256), F32),
        jax.ShapeDtypeStruct((B, WINDOW, 256), F32),
    )
    out_specs = (
        pl.BlockSpec((SEQ_PAD, 2 * GROUP_WIDTH), lambda b: (b, 0)),
        pl.BlockSpec((1, RET_HEADS, RET_DK, RET_DV), lambda b: (b, 0, 0, 0)),
        pl.BlockSpec((1, WINDOW, 256), lambda b: (b, 0, 0)),
        pl.BlockSpec((1, WINDOW, 256), lambda b: (b, 0, 0)),
    )
    return pl.pallas_call(
        _sample_mix_kernel,
        grid=(B,),
        in_specs=in_specs,
        out_specs=out_specs,
        out_shape=out_shape,
        compiler_params=pltpu.CompilerParams(
            dimension_semantics=("arbitrary",), vmem_limit_bytes=VMEM_LIMIT),
        name="sample_mix",
    )(sinks, proj, s0, kbuf, vbuf, *consts)


def _sample_out_kernel(x_ref, xb_ref, gb_ref, mix_ref, conv8_ref, h8_ref, convw_ref, convb_ref,
                       wr_ref, wi_ref, br_ref, bi_ref, lam_ref, wout_ref,
                       y_ref, convo_ref, ho_ref):
    R = x_ref.shape[0]
    row = lax.broadcasted_iota(jnp.int32, (R, LRU_WIDTH), 0) & (SEQ_PAD - 1)
    xc = jnp.where(row < TOK0, conv8_ref[...], xb_ref[...])
    convo_ref[...] = xc
    conv = convb_ref[...] + convw_ref[0:1, :] * pltpu.roll(xc, 3, axis=0)
    conv = conv + convw_ref[1:2, :] * pltpu.roll(xc, 2, axis=0)
    conv = conv + convw_ref[2:3, :] * pltpu.roll(xc, 1, axis=0)
    conv = conv + convw_ref[3:4, :] * xc
    a, mult, ig = _lru_gates(conv, wr_ref, wi_ref, br_ref, bi_ref, lam_ref)
    u = mult * ig * conv
    h = h8_ref[...]
    for s in range(4):
        h = jnp.where(row == TOK0 + s, a * pltpu.roll(h, 1, axis=0) + u, h)
    ho_ref[...] = h
    ob = h * _silu(gb_ref[...])
    y = x_ref[...] + _dot(mix_ref[:, 0:GROUP_WIDTH].astype(BF16), wout_ref[0:GROUP_WIDTH, :])
    y = y + _dot(ob.astype(BF16), wout_ref[GROUP_WIDTH:2 * GROUP_WIDTH, :])
    y = y + _dot(mix_ref[:, GROUP_WIDTH:2 * GROUP_WIDTH].astype(BF16), wout_ref[2 * GROUP_WIDTH:, :])
    y_ref[...] = y


def _sample_out(xp, proj, mix, conv8, h8, convw, convb, wr, wi, br, bi, lam, wout, rows=256):
    R = xp.shape[0]
    consts = [convw, convb, wr, wi, br, bi, lam, wout]
    in_specs = ([pl.BlockSpec((rows, D_MODEL), lambda i: (i, 0)),
                 pl.BlockSpec((rows, LRU_WIDTH), lambda i: (i, XB // LRU_WIDTH)),
                 pl.BlockSpec((rows, LRU_WIDTH), lambda i: (i, GB // LRU_WIDTH)),
                 pl.BlockSpec((rows, 2 * GROUP_WIDTH), lambda i: (i, 0)),
                 pl.BlockSpec((rows, LRU_WIDTH), lambda i: (i, 0)),
                 pl.BlockSpec((rows, LRU_WIDTH), lambda i: (i, 0))]
                + [_const_spec(a.shape) for a in consts])
    out_shape = (
        jax.ShapeDtypeStruct((R, D_MODEL), F32),
        jax.ShapeDtypeStruct((R, LRU_WIDTH), F32),
        jax.ShapeDtypeStruct((R, LRU_WIDTH), F32),
    )
    out_specs = (
        pl.BlockSpec((rows, D_MODEL), lambda i: (i, 0)),
        pl.BlockSpec((rows, LRU_WIDTH), lambda i: (i, 0)),
        pl.BlockSpec((rows, LRU_WIDTH), lambda i: (i, 0)),
    )
    return pl.pallas_call(
        _sample_out_kernel,
        grid=(R // rows,),
        in_specs=in_specs,
        out_specs=out_specs,
        out_shape=out_shape,
        compiler_params=pltpu.CompilerParams(
            dimension_semantics=("arbitrary",), vmem_limit_bytes=VMEM_LIMIT),
        name="sample_out",
    )(xp, proj, proj, mix, conv8, h8, *consts)


def _prompt_tables():
    C = RET_CHUNK
    idx = np.arange(C, dtype=np.float32)
    diff = idx[:, None] - idx[None, :]
    causal = diff >= 0
    lg = _LOG_G[:, None, None]
    dmask = np.where(causal[None], np.exp(np.where(causal, diff, 0.0)[None] * lg), 0.0).astype(np.float32)
    qdec = np.exp((idx + 1.0)[None, :] * _LOG_G[:, None]).astype(np.float32)
    kdec = np.exp((C - 1 - idx)[None, :] * _LOG_G[:, None]).astype(np.float32)
    qdec = np.broadcast_to(qdec[:, :, None], (RET_HEADS, C, RET_DK)).copy()
    kdec = np.broadcast_to(kdec[:, :, None], (RET_HEADS, C, RET_DK)).copy()
    return jnp.asarray(dmask), jnp.asarray(qdec), jnp.asarray(kdec)


def _sample_tables():
    P = SEQ_PAD
    rows = np.arange(P, dtype=np.float32)
    i = rows - TOK0
    tok = (i >= 0) & (i < 4)
    cm = np.zeros((RET_HEADS, 4, P, RET_DV), np.float32)
    qdec = np.zeros((RET_HEADS, P, RET_DK), np.float32)
    kdec = np.zeros((RET_HEADS, P, RET_DK), np.float32)
    for h in range(RET_HEADS):
        for s in range(4):
            d = i - s
            col = np.where(tok & (d >= 0), np.exp(np.where(d >= 0, d, 0.0) * _LOG_G[h]), 0.0)
            cm[h, s] = col[:, None]
        qdec[h] = np.where(tok, np.exp((i + 1.0) * _LOG_G[h]), 0.0)[:, None]
        kdec[h] = np.where(tok, np.exp((3.0 - i) * _LOG_G[h]), 0.0)[:, None]
    return jnp.asarray(cm), jnp.asarray(qdec), jnp.asarray(kdec)


def _block_diag(w):
    n, d, _ = w.shape
    eye = jnp.eye(n, dtype=w.dtype)
    return (eye[:, None, :, None] * w[:, :, None, :]).reshape(n * d, n * d)


def kernel(x_prompt, x_sample, state_ret, state_lru, state_conv, cache_swa_k, cache_swa_v, norm_g, w_in, w_out, ret_norm_g, conv_w, conv_b, w_rgate, b_rgate, w_igate, b_igate, lru_lambda, q_norm_g, k_norm_g, attn_sinks):
    Bs, Ts, _ = x_sample.shape
    ptab = _prompt_tables()
    stab = _sample_tables()
    yp = x_prompt
    ys = jnp.pad(x_sample, ((0, 0), (TOK0, SEQ_PAD - TOK0 - Ts), (0, 0))).reshape(Bs * SEQ_PAD, D_MODEL)
    outs_p = [[] for _ in range(5)]
    outs_s = [[] for _ in range(5)]
    for l in range(DEPTH):
        ng = norm_g[l].reshape(1, -1)
        win = w_in[l].astype(BF16)
        wout = w_out[l].astype(BF16)
        retg = ret_norm_g[l].reshape(1, -1)
        convb = conv_b[l].reshape(1, -1)
        wr = _block_diag(w_rgate[l]).astype(BF16)
        wi = _block_diag(w_igate[l]).astype(BF16)
        br = b_rgate[l].reshape(1, -1)
        bi = b_igate[l].reshape(1, -1)
        lam = lru_lambda[l].reshape(1, -1)
        qg = q_norm_g[l].reshape(1, -1)
        kg = k_norm_g[l].reshape(1, -1)
        sinks = attn_sinks[l]

        wts = (sinks, ng, win, wout, retg, conv_w[l], convb, wr, wi, br, bi, lam, qg, kg)
        yp, p_s, p_h, p_c, p_k, p_v = _prompt_layer(yp, wts, ptab)
        outs_p[0].append(p_s)
        outs_p[1].append(p_h.reshape(-1, LRU_WIDTH))
        outs_p[2].append(p_c)
        outs_p[3].append(p_k.reshape(-1, WINDOW, SWA_KV_HEADS, SWA_HEAD_DIM))
        outs_p[4].append(p_v.reshape(-1, WINDOW, SWA_KV_HEADS, SWA_HEAD_DIM))

        proj = _sample_proj(ys, ng, win)
        kbuf = cache_swa_k[l].reshape(Bs, WINDOW, 256)
        vbuf = cache_swa_v[l].reshape(Bs, WINDOW, 256)
        mix, s_new, k_new, v_new = _sample_mix(sinks, proj, state_ret[l], kbuf, vbuf, retg, qg, kg, stab)
        conv8 = jnp.pad(state_conv[l], ((0, 0), (0, SEQ_PAD - (CONV_W - 1)), (0, 0))).reshape(Bs * SEQ_PAD, LRU_WIDTH)
        h8 = jnp.pad(state_lru[l][:, None, :], ((0, 0), (TOK0 - 1, SEQ_PAD - TOK0), (0, 0))).reshape(Bs * SEQ_PAD, LRU_WIDTH)
        ys, convo, ho = _sample_out(ys, proj, mix, conv8, h8, conv_w[l], convb, wr, wi, br, bi, lam, wout)
        outs_s[0].append(s_new)
        outs_s[1].append(ho.reshape(Bs, SEQ_PAD, LRU_WIDTH)[:, TOK0 + Ts - 1])
        outs_s[2].append(convo.reshape(Bs, SEQ_PAD, LRU_WIDTH)[:, TOK0 + Ts - (CONV_W - 1):TOK0 + Ts])
        outs_s[3].append(k_new.reshape(Bs, WINDOW, SWA_KV_HEADS, SWA_HEAD_DIM))
        outs_s[4].append(v_new.reshape(Bs, WINDOW, SWA_KV_HEADS, SWA_HEAD_DIM))

    y_sample = ys.reshape(Bs, SEQ_PAD, D_MODEL)[:, TOK0:TOK0 + Ts]
    return (yp, y_sample,
            *[jnp.stack(o) for o in outs_p],
            *[jnp.stack(o) for o in outs_s])
```

```python
import functools
import math

import numpy as np
import jax
import jax.numpy as jnp
from jax import lax
from jax.experimental import pallas as pl
from jax.experimental.pallas import tpu as pltpu

D_MODEL = 1024
DEPTH = 2
PAST_LEN = 16384
GROUP_WIDTH = 512
RET_HEADS = 4
RET_DK = 128
RET_DV = 128
RET_CHUNK = 128
LRU_WIDTH = 512
LRU_BLOCKS = 8
LRU_C = 8.0
CONV_W = 4
SWA_HEADS = 4
SWA_KV_HEADS = 2
SWA_GROUP = 2
SWA_HEAD_DIM = 128
WINDOW = 128
NORM_EPS = 1e-6
NEG_INF = -1e30

IN_WIDTH = 4608
MIX_WIDTH = 1536
QA, KA, VA, GA, XB, GB, QC, KC, VC, GC = 0, 512, 1024, 1536, 2048, 2560, 3072, 3584, 3840, 4096

F32 = jnp.float32
BF16 = jnp.bfloat16

SUBLANES = 8
SEQ_PAD = 8
TOK0 = 3
VMEM_LIMIT = 56 * 1024 * 1024

_LOG_G = np.log1p(-np.power(np.float32(2.0), (-5.0 - np.arange(RET_HEADS)).astype(np.float32))).astype(np.float32)
_SLOPES = [2.0 ** (-8.0 * (h + 1) / SWA_HEADS) for h in range(SWA_HEADS)]


def _rms(x, g):
    ms = jnp.mean(x * x, axis=-1, keepdims=True)
    return x * lax.rsqrt(ms + NORM_EPS) * g


def _silu(x):
    return x * jax.nn.sigmoid(x)


def _softplus(x):
    return jnp.maximum(x, 0.0) + jnp.log1p(jnp.exp(-jnp.abs(x)))


def _dot(a, b):
    return jnp.dot(a, b, preferred_element_type=F32)


def _dot_nt(a, b):
    return lax.dot_general(a, b, (((1,), (1,)), ((), ())), preferred_element_type=F32)


def _dot_tn(a, b):
    return lax.dot_general(a, b, (((0,), (0,)), ((), ())), preferred_element_type=F32)


def _lru_gates(conv, wr_ref, wi_ref, br_ref, bi_ref, lam_ref):
    gin = conv.astype(BF16)
    r = jax.nn.sigmoid(_dot(gin, wr_ref[...]) + br_ref[...])
    i = jax.nn.sigmoid(_dot(gin, wi_ref[...]) + bi_ref[...])
    log_a = -LRU_C * r * _softplus(-lam_ref[...])
    a = jnp.exp(log_a)
    th = jnp.tanh(log_a)
    mult = jnp.sqrt(-2.0 * th / (1.0 - th))
    return a, mult, i


def _in_proj(x, ng_ref, win_ref, proj_ref):
    hb = _rms(x, ng_ref[...]).astype(BF16)
    for j in range(IN_WIDTH // 512):
        proj_ref[:, j * 512:(j + 1) * 512] = _dot(hb, win_ref[:, j * 512:(j + 1) * 512])


def _mix_tile(proj_s, t, x, tt, sinks_ref, wout_ref, retg_ref, convw_ref, convb_ref,
              wr_ref, wi_ref, br_ref, bi_ref, lam_ref, qg_ref, kg_ref, dmask_ref, qdec_ref, kdec_ref,
              s_ref, mixed_s, xpad_s, kprev_s, vprev_s, hc_s):
    nch = tt // RET_CHUNK
    C = RET_CHUNK

    xb = proj_s[:, XB:XB + LRU_WIDTH]
    xpad_s[SUBLANES:SUBLANES + tt, :] = xb
    conv = convb_ref[...] + convw_ref[0:1, :] * xpad_s[SUBLANES - 3:SUBLANES - 3 + tt, :]
    conv = conv + convw_ref[1:2, :] * xpad_s[SUBLANES - 2:SUBLANES - 2 + tt, :]
    conv = conv + convw_ref[2:3, :] * xpad_s[SUBLANES - 1:SUBLANES - 1 + tt, :]
    conv = conv + convw_ref[3:4, :] * xb
    xpad_s[0:SUBLANES, :] = xpad_s[tt:tt + SUBLANES, :]

    a, mult, ig = _lru_gates(conv, wr_ref, wi_ref, br_ref, bi_ref, lam_ref)
    row = lax.broadcasted_iota(jnp.int32, (tt, LRU_WIDTH), 0)
    mult = jnp.where(row == jnp.where(t == 0, 0, -1), 1.0, mult)
    u = mult * ig * conv
    row8 = lax.broadcasted_iota(jnp.int32, (SUBLANES, LRU_WIDTH), 0)
    u = jnp.concatenate(
        [u[:SUBLANES] + jnp.where(row8 == 0, a[:SUBLANES] * hc_s[...], 0.0), u[SUBLANES:]], axis=0)
    sh = 1
    while sh < tt:
        if sh < SUBLANES:
            a_sh = jnp.where(row >= sh, pltpu.roll(a, sh, axis=0), 1.0)
            u_sh = jnp.where(row >= sh, pltpu.roll(u, sh, axis=0), 0.0)
            u = a * u_sh + u
            a = a * a_sh
        else:
            u = jnp.concatenate([u[:sh], a[sh:] * u[:tt - sh] + u[sh:]], axis=0)
            if sh * 2 < tt:
                a = jnp.concatenate([a[:sh], a[sh:] * a[:tt - sh]], axis=0)
        sh *= 2
    hc_s[...] = u[tt - 1:tt, :]
    mixed_s[:, GROUP_WIDTH:2 * GROUP_WIDTH] = u * _silu(proj_s[:, GB:GB + LRU_WIDTH])

    ii = lax.broadcasted_iota(jnp.int32, (2 * C, 2 * C), 0)
    jj = lax.broadcasted_iota(jnp.int32, (2 * C, 2 * C), 1)
    qi = jnp.where(ii >= C, ii - C, ii)
    dist = qi + C - jj
    distf = dist.astype(F32)
    in_window = jnp.logical_and(dist >= 0, dist < WINDOW)
    rowc = lax.broadcasted_iota(jnp.int32, (2 * C, 1), 0)

    for c in range(nch):
        r0 = c * C
        blk = t * nch + c
        kmin = jnp.where(blk > 0, 0, C)
        valid = jnp.logical_and(in_window, jj >= kmin)

        for h in range(RET_HEADS):
            cs = slice(h * RET_DK, (h + 1) * RET_DK)
            q = proj_s[r0:r0 + C, QA + h * 128:QA + (h + 1) * 128]
            k = proj_s[r0:r0 + C, KA + h * 128:KA + (h + 1) * 128] * (RET_DK ** -0.5)
            vb = proj_s[r0:r0 + C, VA + h * 128:VA + (h + 1) * 128].astype(BF16)
            sc = _dot_nt(q.astype(BF16), k.astype(BF16)) * dmask_ref[h]
            intra = _dot(sc.astype(BF16), vb)
            s_prev = s_ref[0, h]
            cross = _dot((q * qdec_ref[h]).astype(BF16), s_prev.astype(BF16))
            kv = _dot_tn((k * kdec_ref[h]).astype(BF16), vb)
            s_ref[0, h] = float(np.exp(np.float32(C) * _LOG_G[h])) * s_prev + kv
            o = _rms(intra + cross, retg_ref[:, cs])
            mixed_s[r0:r0 + C, cs] = o * _silu(proj_s[r0:r0 + C, GA + h * 128:GA + (h + 1) * 128])

        for kh in range(SWA_KV_HEADS):
            ks = slice(kh * 128, (kh + 1) * 128)
            kn = _rms(proj_s[r0:r0 + C, KC + kh * 128:KC + (kh + 1) * 128], kg_ref[...])
            vv = proj_s[r0:r0 + C, VC + kh * 128:VC + (kh + 1) * 128]
            kband = jnp.concatenate([kprev_s[:, ks], kn], axis=0).astype(BF16)
            vband = jnp.concatenate([vprev_s[:, ks], vv], axis=0).astype(BF16)
            h0, h1 = kh * SWA_GROUP, kh * SWA_GROUP + 1
            q0 = _rms(proj_s[r0:r0 + C, QC + h0 * 128:QC + (h0 + 1) * 128], qg_ref[...])
            q1 = _rms(proj_s[r0:r0 + C, QC + h1 * 128:QC + (h1 + 1) * 128], qg_ref[...])
            qq = jnp.concatenate([q0, q1], axis=0).astype(BF16)
            s = _dot_nt(qq, kband) * (SWA_HEAD_DIM ** -0.5)
            slope = jnp.where(ii >= C, _SLOPES[h1], _SLOPES[h0])
            s = jnp.where(valid, s - slope * distf, NEG_INF)
            sink = jnp.where(rowc >= C, sinks_ref[h1], sinks_ref[h0])
            m = jnp.maximum(jnp.max(s, axis=-1, keepdims=True), sink)
            p = jnp.exp(s - m)
            denom = jnp.sum(p, axis=-1, keepdims=True) + jnp.exp(sink - m)
            o = _dot((p / denom).astype(BF16), vband)
            for g, hh in enumerate((h0, h1)):
                gc = proj_s[r0:r0 + C, GC + hh * 128:GC + (hh + 1) * 128]
                mixed_s[r0:r0 + C, 2 * GROUP_WIDTH + hh * 128:2 * GROUP_WIDTH + (hh + 1) * 128] = (
                    o[g * C:(g + 1) * C] * _silu(gc))
            kprev_s[:, ks] = kn
            vprev_s[:, ks] = vv

    return x + _dot(mixed_s[...].astype(BF16), wout_ref[...])


def _prompt_kernel(sinks_ref, xa_ref, xn_ref, ng_ref, win_ref, wout_ref, retg_ref, convw_ref, convb_ref,
                   wr_ref, wi_ref, br_ref, bi_ref, lam_ref, qg_ref, kg_ref,
                   dmask_ref, qdec_ref, kdec_ref,
                   y_ref, s_ref, h_ref, conv_ref, kout_ref, vout_ref,
                   proj_a, proj_b, mixed_a, mixed_b, xpad_s, kprev_s, vprev_s, hc_s, *, tt, npairs):
    b = pl.program_id(0)
    p = pl.program_id(1)

    @pl.when(p == 0)
    def _init():
        s_ref[...] = jnp.zeros_like(s_ref)
        xpad_s[0:SUBLANES, :] = jnp.zeros((SUBLANES, LRU_WIDTH), F32)
        kprev_s[...] = jnp.zeros_like(kprev_s)
        vprev_s[...] = jnp.zeros_like(vprev_s)
        hc_s[...] = jnp.zeros_like(hc_s)

    @pl.when(jnp.logical_and(b == 0, p == 0))
    def _prologue():
        _in_proj(xa_ref[0, 0:tt, :], ng_ref, win_ref, proj_a)

    common = (sinks_ref, wout_ref, retg_ref, convw_ref, convb_ref, wr_ref, wi_ref, br_ref, bi_ref,
              lam_ref, qg_ref, kg_ref, dmask_ref, qdec_ref, kdec_ref, s_ref)
    state = (xpad_s, kprev_s, vprev_s, hc_s)

    _in_proj(xa_ref[0, tt:2 * tt, :], ng_ref, win_ref, proj_b)
    y_ref[0, 0:tt, :] = _mix_tile(proj_a, 2 * p, xa_ref[0, 0:tt, :], tt, *common, mixed_a, *state)
    _in_proj(xn_ref[0], ng_ref, win_ref, proj_a)
    y_ref[0, tt:2 * tt, :] = _mix_tile(proj_b, 2 * p + 1, xa_ref[0, tt:2 * tt, :], tt, *common, mixed_b, *state)

    @pl.when(p == npairs - 1)
    def _state_out():
        h_ref[0] = hc_s[...]
        conv_ref[0] = xpad_s[SUBLANES - 3:SUBLANES, :]
        kout_ref[0] = kprev_s[...]
        vout_ref[0] = vprev_s[...]


def _const_spec(shape):
    nd = len(shape)
    return pl.BlockSpec(shape, lambda *_: (0,) * nd)


def _resident_spec(shape):
    nd = len(shape)
    return pl.BlockSpec(shape, lambda *_: (0,) * nd, pipeline_mode=pl.Buffered(1))


def _prompt_layer(x, wts, tables, tt=256):
    B, T, D = x.shape
    npairs = T // (2 * tt)
    (sinks, ng, win, wout, retg, convw, convb, wr, wi, br, bi, lam, qg, kg) = wts
    dmask, qdec, kdec = tables
    vec_in = [ng, win, wout, retg, convw, convb, wr, wi, br, bi, lam, qg, kg, dmask, qdec, kdec]

    def next_tile(b, p):
        last = p == npairs - 1
        return (jnp.where(last, jnp.minimum(b + 1, B - 1), b), jnp.where(last, 0, 2 * p + 2), 0)

    in_specs = ([pl.BlockSpec(memory_space=pltpu.SMEM),
                 pl.BlockSpec((1, 2 * tt, D), lambda b, p: (b, p, 0)),
                 pl.BlockSpec((1, tt, D), next_tile)]
                + [_resident_spec(a.shape) for a in vec_in])
    out_shape = (
        jax.ShapeDtypeStruct((B, T, D), F32),
        jax.ShapeDtypeStruct((B, RET_HEADS, RET_DK, RET_DV), F32),
        jax.ShapeDtypeStruct((B, 1, LRU_WIDTH), F32),
        jax.ShapeDtypeStruct((B, CONV_W - 1, LRU_WIDTH), F32),
        jax.ShapeDtypeStruct((B, WINDOW, SWA_KV_HEADS * SWA_HEAD_DIM), F32),
        jax.ShapeDtypeStruct((B, WINDOW, SWA_KV_HEADS * SWA_HEAD_DIM), F32),
    )
    out_specs = (
        pl.BlockSpec((1, 2 * tt, D), lambda b, p: (b, p, 0)),
        pl.BlockSpec((1, RET_HEADS, RET_DK, RET_DV), lambda b, p: (b, 0, 0, 0)),
        pl.BlockSpec((1, 1, LRU_WIDTH), lambda b, p: (b, 0, 0)),
        pl.BlockSpec((1, CONV_W - 1, LRU_WIDTH), lambda b, p: (b, 0, 0)),
        pl.BlockSpec((1, WINDOW, 256), lambda b, p: (b, 0, 0)),
        pl.BlockSpec((1, WINDOW, 256), lambda b, p: (b, 0, 0)),
    )
    scratch = [
        pltpu.VMEM((tt, IN_WIDTH), F32),
        pltpu.VMEM((tt, IN_WIDTH), F32),
        pltpu.VMEM((tt, MIX_WIDTH), F32),
        pltpu.VMEM((tt, MIX_WIDTH), F32),
        pltpu.VMEM((tt + SUBLANES, LRU_WIDTH), F32),
        pltpu.VMEM((WINDOW, 256), F32),
        pltpu.VMEM((WINDOW, 256), F32),
        pltpu.VMEM((1, LRU_WIDTH), F32),
    ]
    return pl.pallas_call(
        functools.partial(_prompt_kernel, tt=tt, npairs=npairs),
        grid=(B, npairs),
        in_specs=in_specs,
        out_specs=out_specs,
        out_shape=out_shape,
        scratch_shapes=scratch,
        compiler_params=pltpu.CompilerParams(
            dimension_semantics=("arbitrary", "arbitrary"), vmem_limit_bytes=VMEM_LIMIT),
        name="prompt_layer",
    )(sinks, x, x, *vec_in)


def _sproj_kernel(x_ref, ng_ref, win_ref, o_ref):
    hb = _rms(x_ref[...], ng_ref[...]).astype(BF16)
    o_ref[...] = _dot(hb, win_ref[...])


def _sample_proj(xp, ng, win):
    R = xp.shape[0]
    nb = IN_WIDTH // 512
    return pl.pallas_call(
        _sproj_kernel,
        grid=(nb,),
        in_specs=[_const_spec(xp.shape), _const_spec(ng.shape),
                  pl.BlockSpec((D_MODEL, 512), lambda j: (0, j))],
        out_specs=pl.BlockSpec((R, 512), lambda j: (0, j)),
        out_shape=jax.ShapeDtypeStruct((R, IN_WIDTH), F32),
        compiler_params=pltpu.CompilerParams(
            dimension_semantics=("arbitrary",), vmem_limit_bytes=VMEM_LIMIT),
        name="sample_proj",
    )(xp, ng, win)


def _sample_mix_kernel(sinks_ref, proj_ref, s0_ref, kbuf_ref, vbuf_ref, retg_ref, qg_ref, kg_ref,
                       cm_ref, qdec_ref, kdec_ref,
                       mix_ref, snew_ref, knew_ref, vnew_ref):
    P = SEQ_PAD
    for h in range(RET_HEADS):
        cs = slice(h * 128, (h + 1) * 128)
        q = proj_ref[:, QA + h * 128:QA + (h + 1) * 128]
        k = proj_ref[:, KA + h * 128:KA + (h + 1) * 128] * (RET_DK ** -0.5)
        v = proj_ref[:, VA + h * 128:VA + (h + 1) * 128]
        intra = jnp.zeros((P, RET_DV), F32)
        for s in range(4):
            r_ = TOK0 + s
            w = jnp.sum(q * k[r_:r_ + 1, :], axis=-1, keepdims=True)
            intra = intra + (w * cm_ref[h, s]) * v[r_:r_ + 1, :]
        s_prev = s0_ref[0, h]
        cross = _dot((q * qdec_ref[h]).astype(BF16), s_prev.astype(BF16))
        kv = _dot_tn((k * kdec_ref[h]).astype(BF16), v.astype(BF16))
        snew_ref[0, h] = float(np.exp(np.float32(4.0) * _LOG_G[h])) * s_prev + kv
        o = _rms(intra + cross, retg_ref[:, cs])
        mix_ref[:, cs] = o * _silu(proj_ref[:, GA + h * 128:GA + (h + 1) * 128])

    r16 = lax.broadcasted_iota(jnp.int32, (2 * P, WINDOW), 0)
    j16 = lax.broadcasted_iota(jnp.int32, (2 * P, WINDOW), 1)
    rr = jnp.where(r16 >= P, r16 - P, r16)
    row_ok = jnp.logical_and(rr >= TOK0, rr < TOK0 + 4)
    dist = (rr - TOK0) + WINDOW - j16
    distf = dist.astype(F32)
    valid = jnp.logical_and(jnp.logical_and(dist >= 0, dist < WINDOW), row_ok)
    r16c = lax.broadcasted_iota(jnp.int32, (2 * P, 1), 0)
    rrc = jnp.where(r16c >= P, r16c - P, r16c)
    rowc_ok = jnp.logical_and(rrc >= TOK0, rrc < TOK0 + 4)
    for kh in range(SWA_KV_HEADS):
        ks = slice(kh * 128, (kh + 1) * 128)
        h0, h1 = kh * SWA_GROUP, kh * SWA_GROUP + 1
        kb = kbuf_ref[0, :, ks]
        vb = vbuf_ref[0, :, ks]
        kn = _rms(proj_ref[:, KC + kh * 128:KC + (kh + 1) * 128], kg_ref[...])
        vn = proj_ref[:, VC + kh * 128:VC + (kh + 1) * 128]
        q0 = _rms(proj_ref[:, QC + h0 * 128:QC + (h0 + 1) * 128], qg_ref[...])
        q1 = _rms(proj_ref[:, QC + h1 * 128:QC + (h1 + 1) * 128], qg_ref[...])
        qq = jnp.concatenate([q0, q1], axis=0)
        slope = jnp.where(r16 >= P, _SLOPES[h1], _SLOPES[h0])
        slopec = jnp.where(r16c >= P, _SLOPES[h1], _SLOPES[h0])
        sb = _dot_nt(qq.astype(BF16), kb.astype(BF16)) * (SWA_HEAD_DIM ** -0.5)
        sb = jnp.where(valid, sb - slope * distf, NEG_INF)
        sink = jnp.where(r16c >= P, sinks_ref[h1], sinks_ref[h0])
        m = jnp.maximum(jnp.max(sb, axis=-1, keepdims=True), sink)
        wn = []
        for s in range(4):
            r_ = TOK0 + s
            w = jnp.sum(qq * kn[r_:r_ + 1, :], axis=-1, keepdims=True) * (SWA_HEAD_DIM ** -0.5)
            dn = rrc - r_
            w = jnp.where(jnp.logical_and(dn >= 0, rowc_ok), w - slopec * dn.astype(F32), NEG_INF)
            wn.append(w)
            m = jnp.maximum(m, w)
        pb = jnp.exp(sb - m)
        denom = jnp.sum(pb, axis=-1, keepdims=True) + jnp.exp(sink - m)
        pn = [jnp.exp(w - m) for w in wn]
        for p_ in pn:
            denom = denom + p_
        o = _dot((pb / denom).astype(BF16), vb.astype(BF16))
        for s in range(4):
            r_ = TOK0 + s
            o = o + (pn[s] / denom) * vn[r_:r_ + 1, :]
        for g, hh in enumerate((h0, h1)):
            gc = proj_ref[:, GC + hh * 128:GC + (hh + 1) * 128]
            mix_ref[:, GROUP_WIDTH + hh * 128:GROUP_WIDTH + (hh + 1) * 128] = o[g * P:(g + 1) * P] * _silu(gc)
        knew_ref[0, 0:WINDOW - 4, ks] = kbuf_ref[0, 4:WINDOW, ks]
        knew_ref[0, WINDOW - 4:WINDOW, ks] = kn[TOK0:TOK0 + 4, :]
        vnew_ref[0, 0:WINDOW - 4, ks] = vbuf_ref[0, 4:WINDOW, ks]
        vnew_ref[0, WINDOW - 4:WINDOW, ks] = vn[TOK0:TOK0 + 4, :]


def _sample_mix(sinks, proj, s0, kbuf, vbuf, retg, qg, kg, tables):
    cm, qdec, kdec = tables
    B = s0.shape[0]
    consts = [retg, qg, kg, cm, qdec, kdec]
    in_specs = ([pl.BlockSpec(memory_space=pltpu.SMEM),
                 pl.BlockSpec((SEQ_PAD, IN_WIDTH), lambda b: (b, 0)),
                 pl.BlockSpec((1, RET_HEADS, RET_DK, RET_DV), lambda b: (b, 0, 0, 0)),
                 pl.BlockSpec((1, WINDOW, 256), lambda b: (b, 0, 0)),
                 pl.BlockSpec((1, WINDOW, 256), lambda b: (b, 0, 0))]
                + [_const_spec(a.shape) for a in consts])
    out_shape = (
        jax.ShapeDtypeStruct((B * SEQ_PAD, 2 * GROUP_WIDTH), F32),
        jax.ShapeDtypeStruct((B, RET_HEADS, RET_DK, RET_DV), F32),
        jax.ShapeDtypeStruct((B, WINDOW, 256), F32),
        jax.ShapeDtypeStruct((B, WINDOW, 256), F32),
    )
    out_specs = (
        pl.BlockSpec((SEQ_PAD, 2 * GROUP_WIDTH), lambda b: (b, 0)),
        pl.BlockSpec((1, RET_HEADS, RET_DK, RET_DV), lambda b: (b, 0, 0, 0)),
        pl.BlockSpec((1, WINDOW, 256), lambda b: (b, 0, 0)),
        pl.BlockSpec((1, WINDOW, 256), lambda b: (b, 0, 0)),
    )
    return pl.pallas_call(
        _sample_mix_kernel,
        grid=(B,),
        in_specs=in_specs,
        out_specs=out_specs,
        out_shape=out_shape,
        compiler_params=pltpu.CompilerParams(
            dimension_semantics=("arbitrary",), vmem_limit_bytes=VMEM_LIMIT),
        name="sample_mix",
    )(sinks, proj, s0, kbuf, vbuf, *consts)


def _sample_out_kernel(x_ref, xb_ref, gb_ref, mix_ref, conv8_ref, h8_ref, convw_ref, convb_ref,
                       wr_ref, wi_ref, br_ref, bi_ref, lam_ref, wout_ref,
                       y_ref, convo_ref, ho_ref):
    R = x_ref.shape[0]
    row = lax.broadcasted_iota(jnp.int32, (R, LRU_WIDTH), 0) & (SEQ_PAD - 1)
    xc = jnp.where(row < TOK0, conv8_ref[...], xb_ref[...])
    convo_ref[...] = xc
    conv = convb_ref[...] + convw_ref[0:1, :] * pltpu.roll(xc, 3, axis=0)
    conv = conv + convw_ref[1:2, :] * pltpu.roll(xc, 2, axis=0)
    conv = conv + convw_ref[2:3, :] * pltpu.roll(xc, 1, axis=0)
    conv = conv + convw_ref[3:4, :] * xc
    a, mult, ig = _lru_gates(conv, wr_ref, wi_ref, br_ref, bi_ref, lam_ref)
    u = mult * ig * conv
    h = h8_ref[...]
    for s in range(4):
        h = jnp.where(row == TOK0 + s, a * pltpu.roll(h, 1, axis=0) + u, h)
    ho_ref[...] = h
    ob = h * _silu(gb_ref[...])
    y = x_ref[...] + _dot(mix_ref[:, 0:GROUP_WIDTH].astype(BF16), wout_ref[0:GROUP_WIDTH, :])
    y = y + _dot(ob.astype(BF16), wout_ref[GROUP_WIDTH:2 * GROUP_WIDTH, :])
    y = y + _dot(mix_ref[:, GROUP_WIDTH:2 * GROUP_WIDTH].astype(BF16), wout_ref[2 * GROUP_WIDTH:, :])
    y_ref[...] = y


def _sample_out(xp, proj, mix, conv8, h8, convw, convb, wr, wi, br, bi, lam, wout, rows=256):
    R = xp.shape[0]
    consts = [convw, convb, wr, wi, br, bi, lam, wout]
    in_specs = ([pl.BlockSpec((rows, D_MODEL), lambda i: (i, 0)),
                 pl.BlockSpec((rows, LRU_WIDTH), lambda i: (i, XB // LRU_WIDTH)),
                 pl.BlockSpec((rows, LRU_WIDTH), lambda i: (i, GB // LRU_WIDTH)),
                 pl.BlockSpec((rows, 2 * GROUP_WIDTH), lambda i: (i, 0)),
                 pl.BlockSpec((rows, LRU_WIDTH), lambda i: (i, 0)),
                 pl.BlockSpec((rows, LRU_WIDTH), lambda i: (i, 0))]
                + [_const_spec(a.shape) for a in consts])
    out_shape = (
        jax.ShapeDtypeStruct((R, D_MODEL), F32),
        jax.ShapeDtypeStruct((R, LRU_WIDTH), F32),
        jax.ShapeDtypeStruct((R, LRU_WIDTH), F32),
    )
    out_specs = (
        pl.BlockSpec((rows, D_MODEL), lambda i: (i, 0)),
        pl.BlockSpec((rows, LRU_WIDTH), lambda i: (i, 0)),
        pl.BlockSpec((rows, LRU_WIDTH), lambda i: (i, 0)),
    )
    return pl.pallas_call(
        _sample_out_kernel,
        grid=(R // rows,),
        in_specs=in_specs,
        out_specs=out_specs,
        out_shape=out_shape,
        compiler_params=pltpu.CompilerParams(
            dimension_semantics=("arbitrary",), vmem_limit_bytes=VMEM_LIMIT),
        name="sample_out",
    )(xp, proj, proj, mix, conv8, h8, *consts)


def _prompt_tables():
    C = RET_CHUNK
    idx = np.arange(C, dtype=np.float32)
    diff = idx[:, None] - idx[None, :]
    causal = diff >= 0
    lg = _LOG_G[:, None, None]
    dmask = np.where(causal[None], np.exp(np.where(causal, diff, 0.0)[None] * lg), 0.0).astype(np.float32)
    qdec = np.exp((idx + 1.0)[None, :] * _LOG_G[:, None]).astype(np.float32)
    kdec = np.exp((C - 1 - idx)[None, :] * _LOG_G[:, None]).astype(np.float32)
    qdec = np.broadcast_to(qdec[:, :, None], (RET_HEADS, C, RET_DK)).copy()
    kdec = np.broadcast_to(kdec[:, :, None], (RET_HEADS, C, RET_DK)).copy()
    return jnp.asarray(dmask), jnp.asarray(qdec), jnp.asarray(kdec)


def _sample_tables():
    P = SEQ_PAD
    rows = np.arange(P, dtype=np.float32)
    i = rows - TOK0
    tok = (i >= 0) & (i < 4)
    cm = np.zeros((RET_HEADS, 4, P, RET_DV), np.float32)
    qdec = np.zeros((RET_HEADS, P, RET_DK), np.float32)
    kdec = np.zeros((RET_HEADS, P, RET_DK), np.float32)
    for h in range(RET_HEADS):
        for s in range(4):
            d = i - s
            col = np.where(tok & (d >= 0), np.exp(np.where(d >= 0, d, 0.0) * _LOG_G[h]), 0.0)
            cm[h, s] = col[:, None]
        qdec[h] = np.where(tok, np.exp((i + 1.0) * _LOG_G[h]), 0.0)[:, None]
        kdec[h] = np.where(tok, np.exp((3.0 - i) * _LOG_G[h]), 0.0)[:, None]
    return jnp.asarray(cm), jnp.asarray(qdec), jnp.asarray(kdec)


def _block_diag(w):
    n, d, _ = w.shape
    eye = jnp.eye(n, dtype=w.dtype)
    return (eye[:, None, :, None] * w[:, :, None, :]).reshape(n * d, n * d)


def kernel(x_prompt, x_sample, state_ret, state_lru, state_conv, cache_swa_k, cache_swa_v, norm_g, w_in, w_out, ret_norm_g, conv_w, conv_b, w_rgate, b_rgate, w_igate, b_igate, lru_lambda, q_norm_g, k_norm_g, attn_sinks):
    Bs, Ts, _ = x_sample.shape
    ptab = _prompt_tables()
    stab = _sample_tables()
    yp = x_prompt
    ys = jnp.pad(x_sample, ((0, 0), (TOK0, SEQ_PAD - TOK0 - Ts), (0, 0))).reshape(Bs * SEQ_PAD, D_MODEL)
    outs_p = [[] for _ in range(5)]
    outs_s = [[] for _ in range(5)]
    for l in range(DEPTH):
        ng = norm_g[l].reshape(1, -1)
        win = w_in[l].astype(BF16)
        wout = w_out[l].astype(BF16)
        retg = ret_norm_g[l].reshape(1, -1)
        convb = conv_b[l].reshape(1, -1)
        wr = _block_diag(w_rgate[l]).astype(BF16)
        wi = _block_diag(w_igate[l]).astype(BF16)
        br = b_rgate[l].reshape(1, -1)
        bi = b_igate[l].reshape(1, -1)
        lam = lru_lambda[l].reshape(1, -1)
        qg = q_norm_g[l].reshape(1, -1)
        kg = k_norm_g[l].reshape(1, -1)
        sinks = attn_sinks[l]

        wts = (sinks, ng, win, wout, retg, conv_w[l], convb, wr, wi, br, bi, lam, qg, kg)
        yp, p_s, p_h, p_c, p_k, p_v = _prompt_layer(yp, wts, ptab)
        outs_p[0].append(p_s)
        outs_p[1].append(p_h.reshape(-1, LRU_WIDTH))
        outs_p[2].append(p_c)
        outs_p[3].append(p_k.reshape(-1, WINDOW, SWA_KV_HEADS, SWA_HEAD_DIM))
        outs_p[4].append(p_v.reshape(-1, WINDOW, SWA_KV_HEADS, SWA_HEAD_DIM))

        proj = _sample_proj(ys, ng, win)
        kbuf = cache_swa_k[l].reshape(Bs, WINDOW, 256)
        vbuf = cache_swa_v[l].reshape(Bs, WINDOW, 256)
        mix, s_new, k_new, v_new = _sample_mix(sinks, proj, state_ret[l], kbuf, vbuf, retg, qg, kg, stab)
        conv8 = jnp.pad(state_conv[l], ((0, 0), (0, SEQ_PAD - (CONV_W - 1)), (0, 0))).reshape(Bs * SEQ_PAD, LRU_WIDTH)
        h8 = jnp.pad(state_lru[l][:, None, :], ((0, 0), (TOK0 - 1, SEQ_PAD - TOK0), (0, 0))).reshape(Bs * SEQ_PAD, LRU_WIDTH)
        ys, convo, ho = _sample_out(ys, proj, mix, conv8, h8, conv_w[l], convb, wr, wi, br, bi, lam, wout)
        outs_s[0].append(s_new)
        outs_s[1].append(ho.reshape(Bs, SEQ_PAD, LRU_WIDTH)[:, TOK0 + Ts - 1])
        outs_s[2].append(convo.reshape(Bs, SEQ_PAD, LRU_WIDTH)[:, TOK0 + Ts - (CONV_W - 1):TOK0 + Ts])
        outs_s[3].append(k_new.reshape(Bs, WINDOW, SWA_KV_HEADS, SWA_HEAD_DIM))
        outs_s[4].append(v_new.reshape(Bs, WINDOW, SWA_KV_HEADS, SWA_HEAD_DIM))

    y_sample = ys.reshape(Bs, SEQ_PAD, D_MODEL)[:, TOK0:TOK0 + Ts]
    return (yp, y_sample,
            *[jnp.stack(o) for o in outs_p],
            *[jnp.stack(o) for o in outs_s])
```

```python
import functools
import math

import numpy as np
import jax
import jax.numpy as jnp
from jax import lax
from jax.experimental import pallas as pl
from jax.experimental.pallas import tpu as pltpu

D_MODEL = 1024
DEPTH = 2
PAST_LEN = 16384
GROUP_WIDTH = 512
RET_HEADS = 4
RET_DK = 128
RET_DV = 128
RET_CHUNK = 128
LRU_WIDTH = 512
LRU_BLOCKS = 8
LRU_C = 8.0
CONV_W = 4
SWA_HEADS = 4
SWA_KV_HEADS = 2
SWA_GROUP = 2
SWA_HEAD_DIM = 128
WINDOW = 128
NORM_EPS = 1e-6
NEG_INF = -1e30

IN_WIDTH = 4608
MIX_WIDTH = 1536
QA, KA, VA, GA, XB, GB, QC, KC, VC, GC = 0, 512, 1024, 1536, 2048, 2560, 3072, 3584, 3840, 4096

F32 = jnp.float32
BF16 = jnp.bfloat16

SUBLANES = 8
SEQ_PAD = 8
TOK0 = 3
VMEM_LIMIT = 56 * 1024 * 1024

_LOG_G = np.log1p(-np.power(np.float32(2.0), (-5.0 - np.arange(RET_HEADS)).astype(np.float32))).astype(np.float32)
_SLOPES = [2.0 ** (-8.0 * (h + 1) / SWA_HEADS) for h in range(SWA_HEADS)]


def _rms(x, g):
    ms = jnp.mean(x * x, axis=-1, keepdims=True)
    return x * lax.rsqrt(ms + NORM_EPS) * g


def _silu(x):
    return x * jax.nn.sigmoid(x)


def _softplus(x):
    return jnp.maximum(x, 0.0) + jnp.log1p(jnp.exp(-jnp.abs(x)))


def _dot(a, b):
    return jnp.dot(a, b, preferred_element_type=F32)


def _dot_nt(a, b):
    return lax.dot_general(a, b, (((1,), (1,)), ((), ())), preferred_element_type=F32)


def _dot_tn(a, b):
    return lax.dot_general(a, b, (((0,), (0,)), ((), ())), preferred_element_type=F32)


def _lru_gates(conv, wr_ref, wi_ref, br_ref, bi_ref, lam_ref):
    gin = conv.astype(BF16)
    r = jax.nn.sigmoid(_dot(gin, wr_ref[...]) + br_ref[...])
    i = jax.nn.sigmoid(_dot(gin, wi_ref[...]) + bi_ref[...])
    log_a = -LRU_C * r * _softplus(-lam_ref[...])
    a = jnp.exp(log_a)
    th = jnp.tanh(log_a)
    mult = jnp.sqrt(-2.0 * th / (1.0 - th))
    return a, mult, i


def _run(gen):
    for _ in gen:
        pass


def _interleave(main, side, n_main, n_side):
    done_side = 0
    for i, _ in enumerate(main):
        want = ((i + 1) * n_side) // n_main
        while done_side < want:
            if next(side, StopIteration) is StopIteration:
                break
            done_side += 1
    _run(side)


IN_PROJ_STAGES = 1 + IN_WIDTH // 512


def _in_proj(x_ref, ng_ref, win_ref, proj_ref):
    hb = _rms(x_ref[...], ng_ref[...]).astype(BF16)
    yield
    for j in range(IN_WIDTH // 512):
        proj_ref[:, j * 512:(j + 1) * 512] = _dot(hb, win_ref[:, j * 512:(j + 1) * 512])
        yield


def _mix_stages(tt):
    return 5 + (tt // RET_CHUNK) * (RET_HEADS + 3 * SWA_KV_HEADS) + 1


def _mix_tile(proj_s, t, x_ref, y_ref, tt, sinks_ref, wout_ref, retg_ref, convw_ref, convb_ref,
              wr_ref, wi_ref, br_ref, bi_ref, lam_ref, qg_ref, kg_ref, dmask_ref, qdec_ref, kdec_ref,
              s_ref, mixed_s, xpad_s, kprev_s, vprev_s, hc_s):
    nch = tt // RET_CHUNK
    C = RET_CHUNK

    xb = proj_s[:, XB:XB + LRU_WIDTH]
    xpad_s[SUBLANES:SUBLANES + tt, :] = xb
    conv = convb_ref[...] + convw_ref[0:1, :] * xpad_s[SUBLANES - 3:SUBLANES - 3 + tt, :]
    conv = conv + convw_ref[1:2, :] * xpad_s[SUBLANES - 2:SUBLANES - 2 + tt, :]
    conv = conv + convw_ref[2:3, :] * xpad_s[SUBLANES - 1:SUBLANES - 1 + tt, :]
    conv = conv + convw_ref[3:4, :] * xb
    xpad_s[0:SUBLANES, :] = xpad_s[tt:tt + SUBLANES, :]
    yield

    a, mult, ig = _lru_gates(conv, wr_ref, wi_ref, br_ref, bi_ref, lam_ref)
    row = lax.broadcasted_iota(jnp.int32, (tt, LRU_WIDTH), 0)
    mult = jnp.where(row == jnp.where(t == 0, 0, -1), 1.0, mult)
    u = mult * ig * conv
    row8 = lax.broadcasted_iota(jnp.int32, (SUBLANES, LRU_WIDTH), 0)
    u = jnp.concatenate(
        [u[:SUBLANES] + jnp.where(row8 == 0, a[:SUBLANES] * hc_s[...], 0.0), u[SUBLANES:]], axis=0)
    yield
    sh = 1
    while sh < tt:
        if sh < SUBLANES:
            a_sh = jnp.where(row >= sh, pltpu.roll(a, sh, axis=0), 1.0)
            u_sh = jnp.where(row >= sh, pltpu.roll(u, sh, axis=0), 0.0)
            u = a * u_sh + u
            a = a * a_sh
        else:
            u = jnp.concatenate([u[:sh], a[sh:] * u[:tt - sh] + u[sh:]], axis=0)
            if sh * 2 < tt:
                a = jnp.concatenate([a[:sh], a[sh:] * a[:tt - sh]], axis=0)
        sh *= 2
        if sh == SUBLANES:
            yield
    yield
    hc_s[...] = u[tt - 1:tt, :]
    mixed_s[:, GROUP_WIDTH:2 * GROUP_WIDTH] = u * _silu(proj_s[:, GB:GB + LRU_WIDTH])
    yield

    ii = lax.broadcasted_iota(jnp.int32, (2 * C, 2 * C), 0)
    jj = lax.broadcasted_iota(jnp.int32, (2 * C, 2 * C), 1)
    qi = jnp.where(ii >= C, ii - C, ii)
    dist = qi + C - jj
    distf = dist.astype(F32)
    in_window = jnp.logical_and(dist >= 0, dist < WINDOW)
    rowc = lax.broadcasted_iota(jnp.int32, (2 * C, 1), 0)

    for c in range(nch):
        r0 = c * C
        blk = t * nch + c
        kmin = jnp.where(blk > 0, 0, C)
        valid = jnp.logical_and(in_window, jj >= kmin)

        for h in range(RET_HEADS):
            cs = slice(h * RET_DK, (h + 1) * RET_DK)
            q = proj_s[r0:r0 + C, QA + h * 128:QA + (h + 1) * 128]
            k = proj_s[r0:r0 + C, KA + h * 128:KA + (h + 1) * 128] * (RET_DK ** -0.5)
            vb = proj_s[r0:r0 + C, VA + h * 128:VA + (h + 1) * 128].astype(BF16)
            sc = _dot_nt(q.astype(BF16), k.astype(BF16)) * dmask_ref[h]
            intra = _dot(sc.astype(BF16), vb)
            s_prev = s_ref[0, h]
            cross = _dot((q * qdec_ref[h]).astype(BF16), s_prev.astype(BF16))
            kv = _dot_tn((k * kdec_ref[h]).astype(BF16), vb)
            s_ref[0, h] = float(np.exp(np.float32(C) * _LOG_G[h])) * s_prev + kv
            o = _rms(intra + cross, retg_ref[:, cs])
            mixed_s[r0:r0 + C, cs] = o * _silu(proj_s[r0:r0 + C, GA + h * 128:GA + (h + 1) * 128])
            yield

        for kh in range(SWA_KV_HEADS):
            ks = slice(kh * 128, (kh + 1) * 128)
            kn = _rms(proj_s[r0:r0 + C, KC + kh * 128:KC + (kh + 1) * 128], kg_ref[...])
            vv = proj_s[r0:r0 + C, VC + kh * 128:VC + (kh + 1) * 128]
            kband = jnp.concatenate([kprev_s[:, ks], kn], axis=0).astype(BF16)
            vband = jnp.concatenate([vprev_s[:, ks], vv], axis=0).astype(BF16)
            h0, h1 = kh * SWA_GROUP, kh * SWA_GROUP + 1
            q0 = _rms(proj_s[r0:r0 + C, QC + h0 * 128:QC + (h0 + 1) * 128], qg_ref[...])
            q1 = _rms(proj_s[r0:r0 + C, QC + h1 * 128:QC + (h1 + 1) * 128], qg_ref[...])
            qq = jnp.concatenate([q0, q1], axis=0).astype(BF16)
            s = _dot_nt(qq, kband) * (SWA_HEAD_DIM ** -0.5)
            slope = jnp.where(ii >= C, _SLOPES[h1], _SLOPES[h0])
            s = jnp.where(valid, s - slope * distf, NEG_INF)
            yield
            sink = jnp.where(rowc >= C, sinks_ref[h1], sinks_ref[h0])
            m = jnp.maximum(jnp.max(s, axis=-1, keepdims=True), sink)
            p = jnp.exp(s - m)
            denom = jnp.sum(p, axis=-1, keepdims=True) + jnp.exp(sink - m)
            yield
            o = _dot((p / denom).astype(BF16), vband)
            for g, hh in enumerate((h0, h1)):
                gc = proj_s[r0:r0 + C, GC + hh * 128:GC + (hh + 1) * 128]
                mixed_s[r0:r0 + C, 2 * GROUP_WIDTH + hh * 128:2 * GROUP_WIDTH + (hh + 1) * 128] = (
                    o[g * C:(g + 1) * C] * _silu(gc))
            kprev_s[:, ks] = kn
            vprev_s[:, ks] = vv
            yield

    y_ref[...] = x_ref[...] + _dot(mixed_s[...].astype(BF16), wout_ref[...])
    yield


def _prompt_kernel(sinks_ref, xa_ref, xn_ref, ng_ref, win_ref, wout_ref, retg_ref, convw_ref, convb_ref,
                   wr_ref, wi_ref, br_ref, bi_ref, lam_ref, qg_ref, kg_ref,
                   dmask_ref, qdec_ref, kdec_ref, *rest, tt, npairs):
    y_ref, s_ref, h_ref, conv_ref, kout_ref, vout_ref = rest[-14:-8]
    proj_a, proj_b, mixed_a, mixed_b, xpad_s, kprev_s, vprev_s, hc_s = rest[-8:]
    b = pl.program_id(0)
    p = pl.program_id(1)

    @pl.when(p == 0)
    def _init():
        s_ref[...] = jnp.zeros_like(s_ref)
        xpad_s[0:SUBLANES, :] = jnp.zeros((SUBLANES, LRU_WIDTH), F32)
        kprev_s[...] = jnp.zeros_like(kprev_s)
        vprev_s[...] = jnp.zeros_like(vprev_s)
        hc_s[...] = jnp.zeros_like(hc_s)

    xa0, xa1 = xa_ref.at[0, 0:tt, :], xa_ref.at[0, tt:2 * tt, :]
    ya0, ya1 = y_ref.at[0, 0:tt, :], y_ref.at[0, tt:2 * tt, :]

    @pl.when(jnp.logical_and(b == 0, p == 0))
    def _prologue():
        _run(_in_proj(xa0, ng_ref, win_ref, proj_a))

    common = (sinks_ref, wout_ref, retg_ref, convw_ref, convb_ref, wr_ref, wi_ref, br_ref, bi_ref,
              lam_ref, qg_ref, kg_ref, dmask_ref, qdec_ref, kdec_ref, s_ref.at[0])
    state = (xpad_s, kprev_s, vprev_s, hc_s)

    n_main = _mix_stages(tt)
    _interleave(_mix_tile(proj_a, 2 * p, xa0, ya0, tt, *common, mixed_a, *state),
                _in_proj(xa1, ng_ref, win_ref, proj_b), n_main, IN_PROJ_STAGES)
    _interleave(_mix_tile(proj_b, 2 * p + 1, xa1, ya1, tt, *common, mixed_b, *state),
                _in_proj(xn_ref.at[0], ng_ref, win_ref, proj_a), n_main, IN_PROJ_STAGES)

    @pl.when(p == npairs - 1)
    def _state_out():
        h_ref[0, 0] = hc_s[...]
        conv_ref[0, 0] = xpad_s[SUBLANES - 3:SUBLANES, :]
        for kh in range(SWA_KV_HEADS):
            ks = slice(kh * 128, (kh + 1) * 128)
            kout_ref[0, 0, pl.ds(kh, WINDOW, stride=SWA_KV_HEADS), :] = kprev_s[:, ks]
            vout_ref[0, 0, pl.ds(kh, WINDOW, stride=SWA_KV_HEADS), :] = vprev_s[:, ks]


def _const_spec(shape):
    nd = len(shape)
    return pl.BlockSpec(shape, lambda *_: (0,) * nd)


def _resident_spec(shape):
    nd = len(shape)
    return pl.BlockSpec(shape, lambda *_: (0,) * nd, pipeline_mode=pl.Buffered(1))


def _prompt_layer(layer, x, wts, tables, prev_out, tt=256):
    B, T, D = x.shape
    npairs = T // (2 * tt)
    (sinks, ng, win, wout, retg, convw, convb, wr, wi, br, bi, lam, qg, kg) = wts
    dmask, qdec, kdec = tables
    vec_in = [ng, win, wout, retg, convw, convb, wr, wi, br, bi, lam, qg, kg, dmask, qdec, kdec]

    def next_tile(b, p):
        last = p == npairs - 1
        return (jnp.where(last, jnp.minimum(b + 1, B - 1), b), jnp.where(last, 0, 2 * p + 2), 0)

    in_specs = ([pl.BlockSpec(memory_space=pltpu.SMEM),
                 pl.BlockSpec((1, 2 * tt, D), lambda b, p: (b, p, 0)),
                 pl.BlockSpec((1, tt, D), next_tile)]
                + [_resident_spec(a.shape) for a in vec_in]
                + [pl.BlockSpec(memory_space=pl.ANY) for _ in prev_out])
    out_shape = (
        jax.ShapeDtypeStruct((B, T, D), F32),
        jax.ShapeDtypeStruct((DEPTH, B, RET_HEADS, RET_DK, RET_DV), F32),
        jax.ShapeDtypeStruct((DEPTH, B, 1, LRU_WIDTH), F32),
        jax.ShapeDtypeStruct((DEPTH, B, CONV_W - 1, LRU_WIDTH), F32),
        jax.ShapeDtypeStruct((DEPTH, B, 2 * WINDOW, SWA_HEAD_DIM), F32),
        jax.ShapeDtypeStruct((DEPTH, B, 2 * WINDOW, SWA_HEAD_DIM), F32),
    )
    out_specs = (
        pl.BlockSpec((1, 2 * tt, D), lambda b, p: (b, p, 0)),
        pl.BlockSpec((1, 1, RET_HEADS, RET_DK, RET_DV), lambda b, p: (layer, b, 0, 0, 0)),
        pl.BlockSpec((1, 1, 1, LRU_WIDTH), lambda b, p: (layer, b, 0, 0)),
        pl.BlockSpec((1, 1, CONV_W - 1, LRU_WIDTH), lambda b, p: (layer, b, 0, 0)),
        pl.BlockSpec((1, 1, 2 * WINDOW, SWA_HEAD_DIM), lambda b, p: (layer, b, 0, 0)),
        pl.BlockSpec((1, 1, 2 * WINDOW, SWA_HEAD_DIM), lambda b, p: (layer, b, 0, 0)),
    )
    n_in = 3 + len(vec_in)
    aliases = {n_in + j: 1 + j for j in range(len(prev_out))}
    scratch = [
        pltpu.VMEM((tt, IN_WIDTH), F32),
        pltpu.VMEM((tt, IN_WIDTH), F32),
        pltpu.VMEM((tt, MIX_WIDTH), F32),
        pltpu.VMEM((tt, MIX_WIDTH), F32),
        pltpu.VMEM((tt + SUBLANES, LRU_WIDTH), F32),
        pltpu.VMEM((WINDOW, 256), F32),
        pltpu.VMEM((WINDOW, 256), F32),
        pltpu.VMEM((1, LRU_WIDTH), F32),
    ]
    return pl.pallas_call(
        functools.partial(_prompt_kernel, tt=tt, npairs=npairs),
        grid=(B, npairs),
        in_specs=in_specs,
        out_specs=out_specs,
        out_shape=out_shape,
        scratch_shapes=scratch,
        input_output_aliases=aliases,
        compiler_params=pltpu.CompilerParams(
            dimension_semantics=("arbitrary", "arbitrary"), vmem_limit_bytes=VMEM_LIMIT),
        name="prompt_layer",
    )(sinks, x, x, *vec_in, *prev_out)


def _sproj_kernel(x_ref, ng_ref, win_ref, o_ref):
    hb = _rms(x_ref[...], ng_ref[...]).astype(BF16)
    o_ref[...] = _dot(hb, win_ref[...])


def _sample_proj(xp, ng, win):
    R = xp.shape[0]
    nb = IN_WIDTH // 512
    return pl.pallas_call(
        _sproj_kernel,
        grid=(nb,),
        in_specs=[_const_spec(xp.shape), _const_spec(ng.shape),
                  pl.BlockSpec((D_MODEL, 512), lambda j: (0, j))],
        out_specs=pl.BlockSpec((R, 512), lambda j: (0, j)),
        out_shape=jax.ShapeDtypeStruct((R, IN_WIDTH), F32),
        compiler_params=pltpu.CompilerParams(
            dimension_semantics=("arbitrary",), vmem_limit_bytes=VMEM_LIMIT),
        name="sample_proj",
    )(xp, ng, win)


def _sample_mix_seq(sinks_ref, proj_ref, s0_ref, kbuf_ref, vbuf_ref, retg_ref, qg_ref, kg_ref,
                    cm_ref, qdec_ref, kdec_ref, mix_ref, snew_ref, knew_ref, vnew_ref):
    P = SEQ_PAD
    for h in range(RET_HEADS):
        cs = slice(h * 128, (h + 1) * 128)
        q = proj_ref[:, QA + h * 128:QA + (h + 1) * 128]
        k = proj_ref[:, KA + h * 128:KA + (h + 1) * 128] * (RET_DK ** -0.5)
        v = proj_ref[:, VA + h * 128:VA + (h + 1) * 128]
        intra = jnp.zeros((P, RET_DV), F32)
        for s in range(4):
            r_ = TOK0 + s
            w = jnp.sum(q * k[r_:r_ + 1, :], axis=-1, keepdims=True)
            intra = intra + (w * cm_ref[h, s]) * v[r_:r_ + 1, :]
        s_prev = s0_ref[h]
        cross = _dot((q * qdec_ref[h]).astype(BF16), s_prev.astype(BF16))
        kv = _dot_tn((k * kdec_ref[h]).astype(BF16), v.astype(BF16))
        snew_ref[h] = float(np.exp(np.float32(4.0) * _LOG_G[h])) * s_prev + kv
        o = _rms(intra + cross, retg_ref[:, cs])
        mix_ref[:, cs] = o * _silu(proj_ref[:, GA + h * 128:GA + (h + 1) * 128])
        yield

    r16 = lax.broadcasted_iota(jnp.int32, (2 * P, WINDOW), 0)
    j16 = lax.broadcasted_iota(jnp.int32, (2 * P, WINDOW), 1)
    rr = jnp.where(r16 >= P, r16 - P, r16)
    row_ok = jnp.logical_and(rr >= TOK0, rr < TOK0 + 4)
    dist = (rr - TOK0) + WINDOW - j16
    distf = dist.astype(F32)
    valid = jnp.logical_and(jnp.logical_and(dist >= 0, dist < WINDOW), row_ok)
    r16c = lax.broadcasted_iota(jnp.int32, (2 * P, 1), 0)
    rrc = jnp.where(r16c >= P, r16c - P, r16c)
    rowc_ok = jnp.logical_and(rrc >= TOK0, rrc < TOK0 + 4)
    knew_ref[0:2 * (WINDOW - 4), :] = kbuf_ref[2 * 4:2 * WINDOW, :]
    vnew_ref[0:2 * (WINDOW - 4), :] = vbuf_ref[2 * 4:2 * WINDOW, :]
    for kh in range(SWA_KV_HEADS):
        h0, h1 = kh * SWA_GROUP, kh * SWA_GROUP + 1
        kb = kbuf_ref[pl.ds(kh, WINDOW, stride=SWA_KV_HEADS), :]
        vb = vbuf_ref[pl.ds(kh, WINDOW, stride=SWA_KV_HEADS), :]
        kn = _rms(proj_ref[:, KC + kh * 128:KC + (kh + 1) * 128], kg_ref[...])
        vn = proj_ref[:, VC + kh * 128:VC + (kh + 1) * 128]
        q0 = _rms(proj_ref[:, QC + h0 * 128:QC + (h0 + 1) * 128], qg_ref[...])
        q1 = _rms(proj_ref[:, QC + h1 * 128:QC + (h1 + 1) * 128], qg_ref[...])
        qq = jnp.concatenate([q0, q1], axis=0)
        slope = jnp.where(r16 >= P, _SLOPES[h1], _SLOPES[h0])
        slopec = jnp.where(r16c >= P, _SLOPES[h1], _SLOPES[h0])
        sb = _dot_nt(qq.astype(BF16), kb.astype(BF16)) * (SWA_HEAD_DIM ** -0.5)
        sb = jnp.where(valid, sb - slope * distf, NEG_INF)
        yield
        sink = jnp.where(r16c >= P, sinks_ref[h1], sinks_ref[h0])
        m = jnp.maximum(jnp.max(sb, axis=-1, keepdims=True), sink)
        wn = []
        for s in range(4):
            r_ = TOK0 + s
            w = jnp.sum(qq * kn[r_:r_ + 1, :], axis=-1, keepdims=True) * (SWA_HEAD_DIM ** -0.5)
            dn = rrc - r_
            w = jnp.where(jnp.logical_and(dn >= 0, rowc_ok), w - slopec * dn.astype(F32), NEG_INF)
            wn.append(w)
            m = jnp.maximum(m, w)
        pb = jnp.exp(sb - m)
        denom = jnp.sum(pb, axis=-1, keepdims=True) + jnp.exp(sink - m)
        pn = [jnp.exp(w - m) for w in wn]
        for p_ in pn:
            denom = denom + p_
        yield
        o = _dot((pb / denom).astype(BF16), vb.astype(BF16))
        for s in range(4):
            r_ = TOK0 + s
            o = o + (pn[s] / denom) * vn[r_:r_ + 1, :]
        for g, hh in enumerate((h0, h1)):
            gc = proj_ref[:, GC + hh * 128:GC + (hh + 1) * 128]
            mix_ref[:, GROUP_WIDTH + hh * 128:GROUP_WIDTH + (hh + 1) * 128] = o[g * P:(g + 1) * P] * _silu(gc)
        for s in range(4):
            r_out = 2 * (WINDOW - 4 + s) + kh
            knew_ref[r_out:r_out + 1, :] = kn[TOK0 + s:TOK0 + s + 1, :]
            vnew_ref[r_out:r_out + 1, :] = vn[TOK0 + s:TOK0 + s + 1, :]
        yield


def _round_robin(gens):
    gens = list(gens)
    while gens:
        alive = []
        for g in gens:
            try:
                next(g)
                alive.append(g)
            except StopIteration:
                pass
        gens = alive


def _sample_mix_kernel(sinks_ref, proj_ref, s0_ref, kbuf_ref, vbuf_ref, retg_ref, qg_ref, kg_ref,
                       cm_ref, qdec_ref, kdec_ref, *rest, nseq):
    mix_ref, snew_ref, knew_ref, vnew_ref = rest[-4:]
    gens = []
    for i in range(nseq):
        rows = slice(i * SEQ_PAD, (i + 1) * SEQ_PAD)
        gens.append(_sample_mix_seq(
            sinks_ref, proj_ref.at[rows, :], s0_ref.at[0, i], kbuf_ref.at[0, i], vbuf_ref.at[0, i],
            retg_ref, qg_ref, kg_ref, cm_ref, qdec_ref, kdec_ref,
            mix_ref.at[rows, :], snew_ref.at[0, i], knew_ref.at[0, i], vnew_ref.at[0, i]))
    _round_robin(gens)


def _sample_mix(layer, sinks, proj, state_ret, cache_k, cache_v, retg, qg, kg, tables, prev_out, nseq=4):
    cm, qdec, kdec = tables
    depth, B = state_ret.shape[:2]
    consts = [retg, qg, kg, cm, qdec, kdec]
    st_spec = pl.BlockSpec((1, nseq, RET_HEADS, RET_DK, RET_DV), lambda i: (layer, i, 0, 0, 0))
    kv_spec = pl.BlockSpec((1, nseq, 2 * WINDOW, SWA_HEAD_DIM), lambda i: (layer, i, 0, 0))
    in_specs = ([pl.BlockSpec(memory_space=pltpu.SMEM),
                 pl.BlockSpec((nseq * SEQ_PAD, IN_WIDTH), lambda i: (i, 0)),
                 st_spec, kv_spec, kv_spec]
                + [_const_spec(a.shape) for a in consts]
                + [pl.BlockSpec(memory_space=pl.ANY) for _ in prev_out])
    out_shape = (
        jax.ShapeDtypeStruct((B * SEQ_PAD, 2 * GROUP_WIDTH), F32),
        jax.ShapeDtypeStruct(state_ret.shape, F32),
        jax.ShapeDtypeStruct(cache_k.shape, F32),
        jax.ShapeDtypeStruct(cache_v.shape, F32),
    )
    out_specs = (pl.BlockSpec((nseq * SEQ_PAD, 2 * GROUP_WIDTH), lambda i: (i, 0)), st_spec, kv_spec, kv_spec)
    n_in = 5 + len(consts)
    aliases = {n_in + j: 1 + j for j in range(len(prev_out))}
    return pl.pallas_call(
        functools.partial(_sample_mix_kernel, nseq=nseq),
        grid=(B // nseq,),
        in_specs=in_specs,
        out_specs=out_specs,
        out_shape=out_shape,
        input_output_aliases=aliases,
        compiler_params=pltpu.CompilerParams(
            dimension_semantics=("arbitrary",), vmem_limit_bytes=VMEM_LIMIT),
        name="sample_mix",
    )(sinks, proj, state_ret, cache_k, cache_v, *consts, *prev_out)


def _sample_out_kernel(x_ref, xb_ref, gb_ref, mix_ref, conv8_ref, h8_ref, convw_ref, convb_ref,
                       wr_ref, wi_ref, br_ref, bi_ref, lam_ref, wout_ref,
                       y_ref, convo_ref, ho_ref):
    R = x_ref.shape[0]
    row = lax.broadcasted_iota(jnp.int32, (R, LRU_WIDTH), 0) & (SEQ_PAD - 1)
    xc = jnp.where(row < TOK0, conv8_ref[...], xb_ref[...])
    convo_ref[...] = xc
    conv = convb_ref[...] + convw_ref[0:1, :] * pltpu.roll(xc, 3, axis=0)
    conv = conv + convw_ref[1:2, :] * pltpu.roll(xc, 2, axis=0)
    conv = conv + convw_ref[2:3, :] * pltpu.roll(xc, 1, axis=0)
    conv = conv + convw_ref[3:4, :] * xc
    a, mult, ig = _lru_gates(conv, wr_ref, wi_ref, br_ref, bi_ref, lam_ref)
    u = mult * ig * conv
    h = h8_ref[...]
    for s in range(4):
        h = jnp.where(row == TOK0 + s, a * pltpu.roll(h, 1, axis=0) + u, h)
    ho_ref[...] = h
    ob = h * _silu(gb_ref[...])
    y = x_ref[...] + _dot(mix_ref[:, 0:GROUP_WIDTH].astype(BF16), wout_ref[0:GROUP_WIDTH, :])
    y = y + _dot(ob.astype(BF16), wout_ref[GROUP_WIDTH:2 * GROUP_WIDTH, :])
    y = y + _dot(mix_ref[:, GROUP_WIDTH:2 * GROUP_WIDTH].astype(BF16), wout_ref[2 * GROUP_WIDTH:, :])
    y_ref[...] = y


def _sample_out(xp, proj, mix, conv8, h8, convw, convb, wr, wi, br, bi, lam, wout, rows=256):
    R = xp.shape[0]
    consts = [convw, convb, wr, wi, br, bi, lam, wout]
    in_specs = ([pl.BlockSpec((rows, D_MODEL), lambda i: (i, 0)),
                 pl.BlockSpec((rows, LRU_WIDTH), lambda i: (i, XB // LRU_WIDTH)),
                 pl.BlockSpec((rows, LRU_WIDTH), lambda i: (i, GB // LRU_WIDTH)),
                 pl.BlockSpec((rows, 2 * GROUP_WIDTH), lambda i: (i, 0)),
                 pl.BlockSpec((rows, LRU_WIDTH), lambda i: (i, 0)),
                 pl.BlockSpec((rows, LRU_WIDTH), lambda i: (i, 0))]
                + [_const_spec(a.shape) for a in consts])
    out_shape = (
        jax.ShapeDtypeStruct((R, D_MODEL), F32),
        jax.ShapeDtypeStruct((R, LRU_WIDTH), F32),
        jax.ShapeDtypeStruct((R, LRU_WIDTH), F32),
    )
    out_specs = (
        pl.BlockSpec((rows, D_MODEL), lambda i: (i, 0)),
        pl.BlockSpec((rows, LRU_WIDTH), lambda i: (i, 0)),
        pl.BlockSpec((rows, LRU_WIDTH), lambda i: (i, 0)),
    )
    return pl.pallas_call(
        _sample_out_kernel,
        grid=(R // rows,),
        in_specs=in_specs,
        out_specs=out_specs,
        out_shape=out_shape,
        compiler_params=pltpu.CompilerParams(
            dimension_semantics=("arbitrary",), vmem_limit_bytes=VMEM_LIMIT),
        name="sample_out",
    )(xp, proj, proj, mix, conv8, h8, *consts)


def _prompt_tables():
    C = RET_CHUNK
    idx = np.arange(C, dtype=np.float32)
    diff = idx[:, None] - idx[None, :]
    causal = diff >= 0
    lg = _LOG_G[:, None, None]
    dmask = np.where(causal[None], np.exp(np.where(causal, diff, 0.0)[None] * lg), 0.0).astype(np.float32)
    qdec = np.exp((idx + 1.0)[None, :] * _LOG_G[:, None]).astype(np.float32)
    kdec = np.exp((C - 1 - idx)[None, :] * _LOG_G[:, None]).astype(np.float32)
    qdec = np.broadcast_to(qdec[:, :, None], (RET_HEADS, C, RET_DK)).copy()
    kdec = np.broadcast_to(kdec[:, :, None], (RET_HEADS, C, RET_DK)).copy()
    return jnp.asarray(dmask), jnp.asarray(qdec), jnp.asarray(kdec)


def _sample_tables():
    P = SEQ_PAD
    rows = np.arange(P, dtype=np.float32)
    i = rows - TOK0
    tok = (i >= 0) & (i < 4)
    cm = np.zeros((RET_HEADS, 4, P, RET_DV), np.float32)
    qdec = np.zeros((RET_HEADS, P, RET_DK), np.float32)
    kdec = np.zeros((RET_HEADS, P, RET_DK), np.float32)
    for h in range(RET_HEADS):
        for s in range(4):
            d = i - s
            col = np.where(tok & (d >= 0), np.exp(np.where(d >= 0, d, 0.0) * _LOG_G[h]), 0.0)
            cm[h, s] = col[:, None]
        qdec[h] = np.where(tok, np.exp((i + 1.0) * _LOG_G[h]), 0.0)[:, None]
        kdec[h] = np.where(tok, np.exp((3.0 - i) * _LOG_G[h]), 0.0)[:, None]
    return jnp.asarray(cm), jnp.asarray(qdec), jnp.asarray(kdec)


def _block_diag(w):
    n, d, _ = w.shape
    eye = jnp.eye(n, dtype=w.dtype)
    return (eye[:, None, :, None] * w[:, :, None, :]).reshape(n * d, n * d)


def kernel(x_prompt, x_sample, state_ret, state_lru, state_conv, cache_swa_k, cache_swa_v, norm_g, w_in, w_out, ret_norm_g, conv_w, conv_b, w_rgate, b_rgate, w_igate, b_igate, lru_lambda, q_norm_g, k_norm_g, attn_sinks):
    Bs, Ts, _ = x_sample.shape
    ptab = _prompt_tables()
    stab = _sample_tables()
    yp = x_prompt
    ys = jnp.pad(x_sample, ((0, 0), (TOK0, SEQ_PAD - TOK0 - Ts), (0, 0))).reshape(Bs * SEQ_PAD, D_MODEL)
    cache_k = cache_swa_k.reshape(DEPTH, Bs, 2 * WINDOW, SWA_HEAD_DIM)
    cache_v = cache_swa_v.reshape(DEPTH, Bs, 2 * WINDOW, SWA_HEAD_DIM)
    p_state, s_state = (), ()
    s_h, s_conv = [], []
    for l in range(DEPTH):
        ng = norm_g[l].reshape(1, -1)
        win = w_in[l].astype(BF16)
        wout = w_out[l].astype(BF16)
        retg = ret_norm_g[l].reshape(1, -1)
        convb = conv_b[l].reshape(1, -1)
        wr = _block_diag(w_rgate[l]).astype(BF16)
        wi = _block_diag(w_igate[l]).astype(BF16)
        br = b_rgate[l].reshape(1, -1)
        bi = b_igate[l].reshape(1, -1)
        lam = lru_lambda[l].reshape(1, -1)
        qg = q_norm_g[l].reshape(1, -1)
        kg = k_norm_g[l].reshape(1, -1)
        sinks = attn_sinks[l]

        wts = (sinks, ng, win, wout, retg, conv_w[l], convb, wr, wi, br, bi, lam, qg, kg)
        yp, *p_state = _prompt_layer(l, yp, wts, ptab, tuple(p_state))

        proj = _sample_proj(ys, ng, win)
        mix, *s_state = _sample_mix(l, sinks, proj, state_ret, cache_k, cache_v, retg, qg, kg, stab,
                                    tuple(s_state))
        conv8 = jnp.pad(state_conv[l], ((0, 0), (0, SEQ_PAD - (CONV_W - 1)), (0, 0))).reshape(Bs * SEQ_PAD, LRU_WIDTH)
        h8 = jnp.pad(state_lru[l][:, None, :], ((0, 0), (TOK0 - 1, SEQ_PAD - TOK0), (0, 0))).reshape(Bs * SEQ_PAD, LRU_WIDTH)
        ys, convo, ho = _sample_out(ys, proj, mix, conv8, h8, conv_w[l], convb, wr, wi, br, bi, lam, wout)
        s_h.append(ho.reshape(Bs, SEQ_PAD, LRU_WIDTH)[:, TOK0 + Ts - 1])
        s_conv.append(convo.reshape(Bs, SEQ_PAD, LRU_WIDTH)[:, TOK0 + Ts - (CONV_W - 1):TOK0 + Ts])

    p_ret, p_h, p_conv, p_k, p_v = p_state
    s_ret, s_k, s_v = s_state
    Bp = x_prompt.shape[0]
    kv5 = lambda a, n: a.reshape(DEPTH, n, WINDOW, SWA_KV_HEADS, SWA_HEAD_DIM)
    y_sample = ys.reshape(Bs, SEQ_PAD, D_MODEL)[:, TOK0:TOK0 + Ts]
    return (yp, y_sample,
            p_ret, p_h.reshape(DEPTH, Bp, LRU_WIDTH), p_conv, kv5(p_k, Bp), kv5(p_v, Bp),
            s_ret, jnp.stack(s_h), jnp.stack(s_conv), kv5(s_k, Bs), kv5(s_v, Bs))
```

```python
import functools
import math

import numpy as np
import jax
import jax.numpy as jnp
from jax import lax
from jax.experimental import pallas as pl
from jax.experimental.pallas import tpu as pltpu

D_MODEL = 1024
DEPTH = 2
PAST_LEN = 16384
GROUP_WIDTH = 512
RET_HEADS = 4
RET_DK = 128
RET_DV = 128
RET_CHUNK = 128
LRU_WIDTH = 512
LRU_BLOCKS = 8
LRU_C = 8.0
CONV_W = 4
SWA_HEADS = 4
SWA_KV_HEADS = 2
SWA_GROUP = 2
SWA_HEAD_DIM = 128
WINDOW = 128
NORM_EPS = 1e-6
NEG_INF = -1e30

IN_WIDTH = 4608
MIX_WIDTH = 1536
QA, KA, VA, GA, XB, GB, QC, KC, VC, GC = 0, 512, 1024, 1536, 2048, 2560, 3072, 3584, 3840, 4096

F32 = jnp.float32
BF16 = jnp.bfloat16

SUBLANES = 8
SEQ_PAD = 8
TOK0 = 3
VMEM_LIMIT = 56 * 1024 * 1024

_LOG_G = np.log1p(-np.power(np.float32(2.0), (-5.0 - np.arange(RET_HEADS)).astype(np.float32))).astype(np.float32)
_SLOPES = [2.0 ** (-8.0 * (h + 1) / SWA_HEADS) for h in range(SWA_HEADS)]


def _rms(x, g):
    ms = jnp.mean(x * x, axis=-1, keepdims=True)
    return x * lax.rsqrt(ms + NORM_EPS) * g


def _silu(x):
    return x * jax.nn.sigmoid(x)


def _softplus(x):
    return jnp.maximum(x, 0.0) + jnp.log1p(jnp.exp(-jnp.abs(x)))


def _dot(a, b):
    return jnp.dot(a, b, preferred_element_type=F32)


def _dot_nt(a, b):
    return lax.dot_general(a, b, (((1,), (1,)), ((), ())), preferred_element_type=F32)


def _dot_tn(a, b):
    return lax.dot_general(a, b, (((0,), (0,)), ((), ())), preferred_element_type=F32)


def _lru_gates(conv, wr_ref, wi_ref, br_ref, bi_ref, lam_ref):
    gin = conv.astype(BF16)
    half = LRU_WIDTH // 2
    lo, hi = gin[:, :half], gin[:, half:]
    r = jax.nn.sigmoid(jnp.concatenate([_dot(lo, wr_ref[0]), _dot(hi, wr_ref[1])], axis=1) + br_ref[...])
    i = jax.nn.sigmoid(jnp.concatenate([_dot(lo, wi_ref[0]), _dot(hi, wi_ref[1])], axis=1) + bi_ref[...])
    log_a = -LRU_C * r * _softplus(-lam_ref[...])
    a = jnp.exp(log_a)
    m2 = 1.0 - a * a
    mult = jnp.where(m2 > 0.0, m2 * lax.rsqrt(m2), 0.0)
    return a, mult, i


_DONE = object()


def _run(gen):
    for _ in gen:
        pass


def _zip_stages(gens):
    gens = list(gens)
    while gens:
        gens = [g for g in gens if next(g, _DONE) is not _DONE]
        if gens:
            yield


def _chain(gens):
    for g in gens:
        yield from g


def _interleave(main, sides, n_main, n_side):
    side = _chain(sides)
    next(side)
    done_side = 1
    for i, _ in enumerate(main):
        want = ((i + 1) * n_side) // n_main
        while done_side < want and next(side, _DONE) is not _DONE:
            done_side += 1
    _run(side)


IN_PROJ_STAGES = 1 + IN_WIDTH // 512


def _in_proj(x_ref, ng_ref, win_ref, proj_ref):
    hb = _rms(x_ref[...], ng_ref[...]).astype(BF16)
    yield
    for j in range(IN_WIDTH // 512):
        proj_ref[:, j * 512:(j + 1) * 512] = _dot(hb, win_ref[:, j * 512:(j + 1) * 512])
        yield


OUT_PROJ_STAGES = D_MODEL // 256


def _out_proj(mixed_s, x_ref, y_ref, wout_ref):
    mixed = mixed_s[...]
    for j in range(OUT_PROJ_STAGES):
        cols = slice(j * 256, (j + 1) * 256)
        y_ref[:, cols] = x_ref[:, cols] + _dot(mixed, wout_ref[:, cols])
        yield


def _mix_stages(tt):
    return 4 + tt // SUBLANES // 8 + (tt // RET_CHUNK) * 5


def _scan_rows(a, u, h_in, tt):
    G = tt // SUBLANES
    W = a.shape[-1]
    a3 = a.reshape(G, SUBLANES, W)
    u3 = u.reshape(G, SUBLANES, W)
    r3 = lax.broadcasted_iota(jnp.int32, (G, SUBLANES, W), 1)
    sh = 1
    while sh < SUBLANES:
        keep = r3 >= sh
        a_sh = jnp.where(keep, pltpu.roll(a3, sh, axis=1), 1.0)
        u_sh = jnp.where(keep, pltpu.roll(u3, sh, axis=1), 0.0)
        u3 = a3 * u_sh + u3
        a3 = a3 * a_sh
        sh *= 2
    yield
    hs = []
    for g in range(G):
        hg = a3[g] * h_in + u3[g]
        hs.append(hg)
        h_in = hg[SUBLANES - 1:SUBLANES, :]
        if g % 8 == 7:
            yield
    return jnp.concatenate(hs, axis=0)


def _retention_head(h, proj_s, r0, s_ref, retg_ref, dmask_ref, qdec_ref, kdec_ref, mixed_s):
    C = RET_CHUNK
    cs = slice(h * RET_DK, (h + 1) * RET_DK)
    q = proj_s[r0:r0 + C, QA + h * 128:QA + (h + 1) * 128]
    k = proj_s[r0:r0 + C, KA + h * 128:KA + (h + 1) * 128] * (RET_DK ** -0.5)
    vb = proj_s[r0:r0 + C, VA + h * 128:VA + (h + 1) * 128].astype(BF16)
    sc = _dot_nt(q.astype(BF16), k.astype(BF16)) * dmask_ref[h]
    yield
    s_prev = s_ref[0, h]
    lhs = jnp.concatenate([sc.astype(BF16), (q * qdec_ref[h]).astype(BF16)], axis=1)
    rhs = jnp.concatenate([vb, s_prev.astype(BF16)], axis=0)
    o = _dot(lhs, rhs)
    yield
    kv = _dot_tn((k * kdec_ref[h]).astype(BF16), vb)
    s_ref[0, h] = float(np.exp(np.float32(C) * _LOG_G[h])) * s_prev + kv
    yield
    o = _rms(o, retg_ref[:, cs])
    mixed_s[r0:r0 + C, cs] = (o * _silu(proj_s[r0:r0 + C, GA + h * 128:GA + (h + 1) * 128])).astype(BF16)
    yield


def _swa_kv_head(kh, proj_s, r0, blk, sinks_ref, qg_ref, kg_ref, mixed_s, kprev_s, vprev_s):
    C = RET_CHUNK
    ks = slice(kh * 128, (kh + 1) * 128)
    h0, h1 = kh * SWA_GROUP, kh * SWA_GROUP + 1
    kn = _rms(proj_s[r0:r0 + C, KC + kh * 128:KC + (kh + 1) * 128], kg_ref[...])
    vv = proj_s[r0:r0 + C, VC + kh * 128:VC + (kh + 1) * 128]
    kband = jnp.concatenate([kprev_s[:, ks], kn], axis=0).astype(BF16)
    vband = jnp.concatenate([vprev_s[:, ks], vv], axis=0).astype(BF16)
    kprev_s[:, ks] = kn
    vprev_s[:, ks] = vv
    q0 = _rms(proj_s[r0:r0 + C, QC + h0 * 128:QC + (h0 + 1) * 128], qg_ref[...])
    q1 = _rms(proj_s[r0:r0 + C, QC + h1 * 128:QC + (h1 + 1) * 128], qg_ref[...])
    qq = jnp.concatenate([q0, q1], axis=0).astype(BF16)
    yield
    s = _dot_nt(qq, kband) * (SWA_HEAD_DIM ** -0.5)
    ii = lax.broadcasted_iota(jnp.int32, (2 * C, 2 * C), 0)
    jj = lax.broadcasted_iota(jnp.int32, (2 * C, 2 * C), 1)
    dist = jnp.where(ii >= C, ii - C, ii) + C - jj
    valid = jnp.logical_and(jnp.logical_and(dist >= 0, dist < WINDOW), jj >= jnp.where(blk > 0, 0, C))
    slope = jnp.where(ii >= C, _SLOPES[h1], _SLOPES[h0])
    s = jnp.where(valid, s - slope * dist.astype(F32), NEG_INF)
    yield
    rowc = lax.broadcasted_iota(jnp.int32, (2 * C, 1), 0)
    sink = jnp.where(rowc >= C, sinks_ref[h1], sinks_ref[h0])
    m = jnp.maximum(jnp.max(s, axis=-1, keepdims=True), sink)
    p = jnp.exp(s - m)
    denom = jnp.sum(p, axis=-1, keepdims=True) + jnp.exp(sink - m)
    yield
    o = _dot((p / denom).astype(BF16), vband)
    yield
    for g, hh in enumerate((h0, h1)):
        gc = proj_s[r0:r0 + C, GC + hh * 128:GC + (hh + 1) * 128]
        mixed_s[r0:r0 + C, 2 * GROUP_WIDTH + hh * 128:2 * GROUP_WIDTH + (hh + 1) * 128] = (
            (o[g * C:(g + 1) * C] * _silu(gc)).astype(BF16))
    yield


def _mix_tile(proj_s, t, tt, sinks_ref, retg_ref, convw_ref, convb_ref,
              wr_ref, wi_ref, br_ref, bi_ref, lam_ref, qg_ref, kg_ref, dmask_ref, qdec_ref, kdec_ref,
              s_ref, mixed_s, xtail_s, kprev_s, vprev_s, hc_s):
    nch = tt // RET_CHUNK
    C = RET_CHUNK

    xb = proj_s[:, XB:XB + LRU_WIDTH]
    xfull = jnp.concatenate([xtail_s[...], xb], axis=0)
    xtail_s[...] = xb[tt - SUBLANES:tt, :]
    conv = convb_ref[...] + convw_ref[0:1, :] * pltpu.roll(xfull, 3, axis=0)[SUBLANES:]
    conv = conv + convw_ref[1:2, :] * pltpu.roll(xfull, 2, axis=0)[SUBLANES:]
    conv = conv + convw_ref[2:3, :] * pltpu.roll(xfull, 1, axis=0)[SUBLANES:]
    conv = conv + convw_ref[3:4, :] * xb
    yield

    a, mult, ig = _lru_gates(conv, wr_ref, wi_ref, br_ref, bi_ref, lam_ref)
    yield
    row = lax.broadcasted_iota(jnp.int32, (tt, LRU_WIDTH), 0)
    mult = jnp.where(row == jnp.where(t == 0, 0, -1), 1.0, mult)
    u = mult * ig * conv
    hseq = yield from _scan_rows(a, u, hc_s[...], tt)
    hc_s[...] = hseq[tt - 1:tt, :]
    mixed_s[:, GROUP_WIDTH:2 * GROUP_WIDTH] = (hseq * _silu(proj_s[:, GB:GB + LRU_WIDTH])).astype(BF16)
    yield

    for c in range(nch):
        r0 = c * C
        blk = t * nch + c
        yield from _zip_stages(
            [_retention_head(h, proj_s, r0, s_ref, retg_ref, dmask_ref, qdec_ref, kdec_ref, mixed_s)
             for h in range(RET_HEADS)]
            + [_swa_kv_head(kh, proj_s, r0, blk, sinks_ref, qg_ref, kg_ref, mixed_s, kprev_s, vprev_s)
               for kh in range(SWA_KV_HEADS)])


def _prompt_kernel(sinks_ref, xa_ref, xn_ref, ng_ref, win_ref, wout_ref, retg_ref, convw_ref, convb_ref,
                   wr_ref, wi_ref, br_ref, bi_ref, lam_ref, qg_ref, kg_ref,
                   dmask_ref, qdec_ref, kdec_ref, *rest, tt, npairs):
    y_ref, s_ref, h_ref, conv_ref, kout_ref, vout_ref = rest[-14:-8]
    proj_a, proj_b, mixed_a, mixed_b, xtail_s, kprev_s, vprev_s, hc_s = rest[-8:]
    b = pl.program_id(0)
    p = pl.program_id(1)

    @pl.when(p == 0)
    def _init():
        s_ref[...] = jnp.zeros_like(s_ref)
        xtail_s[...] = jnp.zeros_like(xtail_s)
        kprev_s[...] = jnp.zeros_like(kprev_s)
        vprev_s[...] = jnp.zeros_like(vprev_s)
        hc_s[...] = jnp.zeros_like(hc_s)

    xa0, xa1 = xa_ref.at[0, 0:tt, :], xa_ref.at[0, tt:2 * tt, :]
    ya0, ya1 = y_ref.at[0, 0:tt, :], y_ref.at[0, tt:2 * tt, :]

    @pl.when(jnp.logical_and(b == 0, p == 0))
    def _prologue():
        _run(_in_proj(xa0, ng_ref, win_ref, proj_a))

    common = (sinks_ref, retg_ref, convw_ref, convb_ref, wr_ref, wi_ref, br_ref, bi_ref,
              lam_ref, qg_ref, kg_ref, dmask_ref, qdec_ref, kdec_ref, s_ref.at[0])
    state = (xtail_s, kprev_s, vprev_s, hc_s)

    n_main = _mix_stages(tt)
    _interleave(_mix_tile(proj_a, 2 * p, tt, *common, mixed_a, *state),
                [_in_proj(xa1, ng_ref, win_ref, proj_b)], n_main, IN_PROJ_STAGES)
    _interleave(_mix_tile(proj_b, 2 * p + 1, tt, *common, mixed_b, *state),
                [_out_proj(mixed_a, xa0, ya0, wout_ref), _in_proj(xn_ref.at[0], ng_ref, win_ref, proj_a)],
                n_main, OUT_PROJ_STAGES + IN_PROJ_STAGES)
    _run(_out_proj(mixed_b, xa1, ya1, wout_ref))

    @pl.when(p == npairs - 1)
    def _state_out():
        h_ref[0, 0] = hc_s[...]
        conv_ref[0, 0] = xtail_s[SUBLANES - (CONV_W - 1):SUBLANES, :]
        for kh in range(SWA_KV_HEADS):
            ks = slice(kh * 128, (kh + 1) * 128)
            kout_ref[0, 0, pl.ds(kh, WINDOW, stride=SWA_KV_HEADS), :] = kprev_s[:, ks]
            vout_ref[0, 0, pl.ds(kh, WINDOW, stride=SWA_KV_HEADS), :] = vprev_s[:, ks]


def _const_spec(shape):
    nd = len(shape)
    return pl.BlockSpec(shape, lambda *_: (0,) * nd)


def _resident_spec(shape):
    nd = len(shape)
    return pl.BlockSpec(shape, lambda *_: (0,) * nd, pipeline_mode=pl.Buffered(1))


def _prompt_layer(layer, x, wts, tables, prev_out, tt=256):
    B, T, D = x.shape
    npairs = T // (2 * tt)
    (sinks, ng, win, wout, retg, convw, convb, wr, wi, br, bi, lam, qg, kg) = wts
    dmask, qdec, kdec = tables
    vec_in = [ng, win, wout, retg, convw, convb, wr, wi, br, bi, lam, qg, kg, dmask, qdec, kdec]

    def next_tile(b, p):
        last = p == npairs - 1
        return (jnp.where(last, jnp.minimum(b + 1, B - 1), b), jnp.where(last, 0, 2 * p + 2), 0)

    in_specs = ([pl.BlockSpec(memory_space=pltpu.SMEM),
                 pl.BlockSpec((1, 2 * tt, D), lambda b, p: (b, p, 0)),
                 pl.BlockSpec((1, tt, D), next_tile)]
                + [_resident_spec(a.shape) for a in vec_in]
                + [pl.BlockSpec(memory_space=pl.ANY) for _ in prev_out])
    out_shape = (
        jax.ShapeDtypeStruct((B, T, D), F32),
        jax.ShapeDtypeStruct((DEPTH, B, RET_HEADS, RET_DK, RET_DV), F32),
        jax.ShapeDtypeStruct((DEPTH, B, 1, LRU_WIDTH), F32),
        jax.ShapeDtypeStruct((DEPTH, B, CONV_W - 1, LRU_WIDTH), F32),
        jax.ShapeDtypeStruct((DEPTH, B, 2 * WINDOW, SWA_HEAD_DIM), F32),
        jax.ShapeDtypeStruct((DEPTH, B, 2 * WINDOW, SWA_HEAD_DIM), F32),
    )
    out_specs = (
        pl.BlockSpec((1, 2 * tt, D), lambda b, p: (b, p, 0)),
        pl.BlockSpec((1, 1, RET_HEADS, RET_DK, RET_DV), lambda b, p: (layer, b, 0, 0, 0)),
        pl.BlockSpec((1, 1, 1, LRU_WIDTH), lambda b, p: (layer, b, 0, 0)),
        pl.BlockSpec((1, 1, CONV_W - 1, LRU_WIDTH), lambda b, p: (layer, b, 0, 0)),
        pl.BlockSpec((1, 1, 2 * WINDOW, SWA_HEAD_DIM), lambda b, p: (layer, b, 0, 0)),
        pl.BlockSpec((1, 1, 2 * WINDOW, SWA_HEAD_DIM), lambda b, p: (layer, b, 0, 0)),
    )
    n_in = 3 + len(vec_in)
    aliases = {n_in + j: 1 + j for j in range(len(prev_out))}
    scratch = [
        pltpu.VMEM((tt, IN_WIDTH), F32),
        pltpu.VMEM((tt, IN_WIDTH), F32),
        pltpu.VMEM((tt, MIX_WIDTH), BF16),
        pltpu.VMEM((tt, MIX_WIDTH), BF16),
        pltpu.VMEM((SUBLANES, LRU_WIDTH), F32),
        pltpu.VMEM((WINDOW, 256), F32),
        pltpu.VMEM((WINDOW, 256), F32),
        pltpu.VMEM((1, LRU_WIDTH), F32),
    ]
    return pl.pallas_call(
        functools.partial(_prompt_kernel, tt=tt, npairs=npairs),
        grid=(B, npairs),
        in_specs=in_specs,
        out_specs=out_specs,
        out_shape=out_shape,
        scratch_shapes=scratch,
        input_output_aliases=aliases,
        compiler_params=pltpu.CompilerParams(
            dimension_semantics=("arbitrary", "arbitrary"), vmem_limit_bytes=VMEM_LIMIT),
        name="prompt_layer",
    )(sinks, x, x, *vec_in, *prev_out)


def _sproj_kernel(x_ref, ng_ref, win_ref, o_ref):
    hb = _rms(x_ref[...], ng_ref[...]).astype(BF16)
    o_ref[...] = _dot(hb, win_ref[...])


def _sample_proj(xp, ng, win):
    R = xp.shape[0]
    nb = IN_WIDTH // 512
    return pl.pallas_call(
        _sproj_kernel,
        grid=(nb,),
        in_specs=[_const_spec(xp.shape), _const_spec(ng.shape),
                  pl.BlockSpec((D_MODEL, 512), lambda j: (0, j))],
        out_specs=pl.BlockSpec((R, 512), lambda j: (0, j)),
        out_shape=jax.ShapeDtypeStruct((R, IN_WIDTH), F32),
        compiler_params=pltpu.CompilerParams(
            dimension_semantics=("arbitrary",), vmem_limit_bytes=VMEM_LIMIT),
        name="sample_proj",
    )(xp, ng, win)


def _sample_mix_seq(sinks_ref, proj_ref, s0_ref, kbuf_ref, vbuf_ref, retg_ref, qg_ref, kg_ref,
                    cm_ref, qdec_ref, kdec_ref, mix_ref, snew_ref, knew_ref, vnew_ref):
    P = SEQ_PAD
    for h in range(RET_HEADS):
        cs = slice(h * 128, (h + 1) * 128)
        q = proj_ref[:, QA + h * 128:QA + (h + 1) * 128]
        k = proj_ref[:, KA + h * 128:KA + (h + 1) * 128] * (RET_DK ** -0.5)
        v = proj_ref[:, VA + h * 128:VA + (h + 1) * 128]
        intra = jnp.zeros((P, RET_DV), F32)
        for s in range(4):
            r_ = TOK0 + s
            w = jnp.sum(q * k[r_:r_ + 1, :], axis=-1, keepdims=True)
            intra = intra + (w * cm_ref[h, s]) * v[r_:r_ + 1, :]
        s_prev = s0_ref[h]
        cross = _dot((q * qdec_ref[h]).astype(BF16), s_prev.astype(BF16))
        kv = _dot_tn((k * kdec_ref[h]).astype(BF16), v.astype(BF16))
        snew_ref[h] = float(np.exp(np.float32(4.0) * _LOG_G[h])) * s_prev + kv
        o = _rms(intra + cross, retg_ref[:, cs])
        mix_ref[:, cs] = o * _silu(proj_ref[:, GA + h * 128:GA + (h + 1) * 128])
        yield

    r16 = lax.broadcasted_iota(jnp.int32, (2 * P, WINDOW), 0)
    j16 = lax.broadcasted_iota(jnp.int32, (2 * P, WINDOW), 1)
    rr = jnp.where(r16 >= P, r16 - P, r16)
    row_ok = jnp.logical_and(rr >= TOK0, rr < TOK0 + 4)
    dist = (rr - TOK0) + WINDOW - j16
    distf = dist.astype(F32)
    valid = jnp.logical_and(jnp.logical_and(dist >= 0, dist < WINDOW), row_ok)
    r16c = lax.broadcasted_iota(jnp.int32, (2 * P, 1), 0)
    rrc = jnp.where(r16c >= P, r16c - P, r16c)
    rowc_ok = jnp.logical_and(rrc >= TOK0, rrc < TOK0 + 4)
    knew_ref[0:2 * (WINDOW - 4), :] = kbuf_ref[2 * 4:2 * WINDOW, :]
    vnew_ref[0:2 * (WINDOW - 4), :] = vbuf_ref[2 * 4:2 * WINDOW, :]
    for kh in range(SWA_KV_HEADS):
        h0, h1 = kh * SWA_GROUP, kh * SWA_GROUP + 1
        kb = kbuf_ref[pl.ds(kh, WINDOW, stride=SWA_KV_HEADS), :]
        vb = vbuf_ref[pl.ds(kh, WINDOW, stride=SWA_KV_HEADS), :]
        kn = _rms(proj_ref[:, KC + kh * 128:KC + (kh + 1) * 128], kg_ref[...])
        vn = proj_ref[:, VC + kh * 128:VC + (kh + 1) * 128]
        q0 = _rms(proj_ref[:, QC + h0 * 128:QC + (h0 + 1) * 128], qg_ref[...])
        q1 = _rms(proj_ref[:, QC + h1 * 128:QC + (h1 + 1) * 128], qg_ref[...])
        qq = jnp.concatenate([q0, q1], axis=0)
        slope = jnp.where(r16 >= P, _SLOPES[h1], _SLOPES[h0])
        slopec = jnp.where(r16c >= P, _SLOPES[h1], _SLOPES[h0])
        sb = _dot_nt(qq.astype(BF16), kb.astype(BF16)) * (SWA_HEAD_DIM ** -0.5)
        sb = jnp.where(valid, sb - slope * distf, NEG_INF)
        yield
        sink = jnp.where(r16c >= P, sinks_ref[h1], sinks_ref[h0])
        m = jnp.maximum(jnp.max(sb, axis=-1, keepdims=True), sink)
        wn = []
        for s in range(4):
            r_ = TOK0 + s
            w = jnp.sum(qq * kn[r_:r_ + 1, :], axis=-1, keepdims=True) * (SWA_HEAD_DIM ** -0.5)
            dn = rrc - r_
            w = jnp.where(jnp.logical_and(dn >= 0, rowc_ok), w - slopec * dn.astype(F32), NEG_INF)
            wn.append(w)
            m = jnp.maximum(m, w)
        pb = jnp.exp(sb - m)
        denom = jnp.sum(pb, axis=-1, keepdims=True) + jnp.exp(sink - m)
        pn = [jnp.exp(w - m) for w in wn]
        for p_ in pn:
            denom = denom + p_
        yield
        o = _dot((pb / denom).astype(BF16), vb.astype(BF16))
        for s in range(4):
            r_ = TOK0 + s
            o = o + (pn[s] / denom) * vn[r_:r_ + 1, :]
        for g, hh in enumerate((h0, h1)):
            gc = proj_ref[:, GC + hh * 128:GC + (hh + 1) * 128]
            mix_ref[:, GROUP_WIDTH + hh * 128:GROUP_WIDTH + (hh + 1) * 128] = o[g * P:(g + 1) * P] * _silu(gc)
        for s in range(4):
            r_out = 2 * (WINDOW - 4 + s) + kh
            knew_ref[r_out:r_out + 1, :] = kn[TOK0 + s:TOK0 + s + 1, :]
            vnew_ref[r_out:r_out + 1, :] = vn[TOK0 + s:TOK0 + s + 1, :]
        yield


def _sample_mix_kernel(sinks_ref, proj_ref, s0_ref, kbuf_ref, vbuf_ref, retg_ref, qg_ref, kg_ref,
                       cm_ref, qdec_ref, kdec_ref, *rest, nseq):
    mix_ref, snew_ref, knew_ref, vnew_ref = rest[-4:]
    gens = []
    for i in range(nseq):
        rows = slice(i * SEQ_PAD, (i + 1) * SEQ_PAD)
        gens.append(_sample_mix_seq(
            sinks_ref, proj_ref.at[rows, :], s0_ref.at[0, i], kbuf_ref.at[0, i], vbuf_ref.at[0, i],
            retg_ref, qg_ref, kg_ref, cm_ref, qdec_ref, kdec_ref,
            mix_ref.at[rows, :], snew_ref.at[0, i], knew_ref.at[0, i], vnew_ref.at[0, i]))
    _run(_zip_stages(gens))


def _sample_mix(layer, sinks, proj, state_ret, cache_k, cache_v, retg, qg, kg, tables, prev_out, nseq=8):
    cm, qdec, kdec = tables
    depth, B = state_ret.shape[:2]
    consts = [retg, qg, kg, cm, qdec, kdec]
    st_spec = pl.BlockSpec((1, nseq, RET_HEADS, RET_DK, RET_DV), lambda i: (layer, i, 0, 0, 0))
    kv_spec = pl.BlockSpec((1, nseq, 2 * WINDOW, SWA_HEAD_DIM), lambda i: (layer, i, 0, 0))
    in_specs = ([pl.BlockSpec(memory_space=pltpu.SMEM),
                 pl.BlockSpec((nseq * SEQ_PAD, IN_WIDTH), lambda i: (i, 0)),
                 st_spec, kv_spec, kv_spec]
                + [_const_spec(a.shape) for a in consts]
                + [pl.BlockSpec(memory_space=pl.ANY) for _ in prev_out])
    out_shape = (
        jax.ShapeDtypeStruct((B * SEQ_PAD, 2 * GROUP_WIDTH), F32),
        jax.ShapeDtypeStruct(state_ret.shape, F32),
        jax.ShapeDtypeStruct(cache_k.shape, F32),
        jax.ShapeDtypeStruct(cache_v.shape, F32),
    )
    out_specs = (pl.BlockSpec((nseq * SEQ_PAD, 2 * GROUP_WIDTH), lambda i: (i, 0)), st_spec, kv_spec, kv_spec)
    n_in = 5 + len(consts)
    aliases = {n_in + j: 1 + j for j in range(len(prev_out))}
    return pl.pallas_call(
        functools.partial(_sample_mix_kernel, nseq=nseq),
        grid=(B // nseq,),
        in_specs=in_specs,
        out_specs=out_specs,
        out_shape=out_shape,
        input_output_aliases=aliases,
        compiler_params=pltpu.CompilerParams(
            dimension_semantics=("arbitrary",), vmem_limit_bytes=VMEM_LIMIT),
        name="sample_mix",
    )(sinks, proj, state_ret, cache_k, cache_v, *consts, *prev_out)


def _sample_out_kernel(x_ref, xb_ref, gb_ref, mix_ref, conv8_ref, h8_ref, convw_ref, convb_ref,
                       wr_ref, wi_ref, br_ref, bi_ref, lam_ref, wout_ref,
                       y_ref, convo_ref, ho_ref):
    R = x_ref.shape[0]
    row = lax.broadcasted_iota(jnp.int32, (R, LRU_WIDTH), 0) & (SEQ_PAD - 1)
    xc = jnp.where(row < TOK0, conv8_ref[...], xb_ref[...])
    convo_ref[...] = xc
    conv = convb_ref[...] + convw_ref[0:1, :] * pltpu.roll(xc, 3, axis=0)
    conv = conv + convw_ref[1:2, :] * pltpu.roll(xc, 2, axis=0)
    conv = conv + convw_ref[2:3, :] * pltpu.roll(xc, 1, axis=0)
    conv = conv + convw_ref[3:4, :] * xc
    a, mult, ig = _lru_gates(conv, wr_ref, wi_ref, br_ref, bi_ref, lam_ref)
    u = mult * ig * conv
    h = h8_ref[...]
    for s in range(4):
        h = jnp.where(row == TOK0 + s, a * pltpu.roll(h, 1, axis=0) + u, h)
    ho_ref[...] = h
    ob = h * _silu(gb_ref[...])
    y = x_ref[...] + _dot(mix_ref[:, 0:GROUP_WIDTH].astype(BF16), wout_ref[0:GROUP_WIDTH, :])
    y = y + _dot(ob.astype(BF16), wout_ref[GROUP_WIDTH:2 * GROUP_WIDTH, :])
    y = y + _dot(mix_ref[:, GROUP_WIDTH:2 * GROUP_WIDTH].astype(BF16), wout_ref[2 * GROUP_WIDTH:, :])
    y_ref[...] = y


def _sample_out(xp, proj, mix, conv8, h8, convw, convb, wr, wi, br, bi, lam, wout, rows=256):
    R = xp.shape[0]
    consts = [convw, convb, wr, wi, br, bi, lam, wout]
    in_specs = ([pl.BlockSpec((rows, D_MODEL), lambda i: (i, 0)),
                 pl.BlockSpec((rows, LRU_WIDTH), lambda i: (i, XB // LRU_WIDTH)),
                 pl.BlockSpec((rows, LRU_WIDTH), lambda i: (i, GB // LRU_WIDTH)),
                 pl.BlockSpec((rows, 2 * GROUP_WIDTH), lambda i: (i, 0)),
                 pl.BlockSpec((rows, LRU_WIDTH), lambda i: (i, 0)),
                 pl.BlockSpec((rows, LRU_WIDTH), lambda i: (i, 0))]
                + [_const_spec(a.shape) for a in consts])
    out_shape = (
        jax.ShapeDtypeStruct((R, D_MODEL), F32),
        jax.ShapeDtypeStruct((R, LRU_WIDTH), F32),
        jax.ShapeDtypeStruct((R, LRU_WIDTH), F32),
    )
    out_specs = (
        pl.BlockSpec((rows, D_MODEL), lambda i: (i, 0)),
        pl.BlockSpec((rows, LRU_WIDTH), lambda i: (i, 0)),
        pl.BlockSpec((rows, LRU_WIDTH), lambda i: (i, 0)),
    )
    return pl.pallas_call(
        _sample_out_kernel,
        grid=(R // rows,),
        in_specs=in_specs,
        out_specs=out_specs,
        out_shape=out_shape,
        compiler_params=pltpu.CompilerParams(
            dimension_semantics=("arbitrary",), vmem_limit_bytes=VMEM_LIMIT),
        name="sample_out",
    )(xp, proj, proj, mix, conv8, h8, *consts)


def _prompt_tables():
    C = RET_CHUNK
    idx = np.arange(C, dtype=np.float32)
    diff = idx[:, None] - idx[None, :]
    causal = diff >= 0
    lg = _LOG_G[:, None, None]
    dmask = np.where(causal[None], np.exp(np.where(causal, diff, 0.0)[None] * lg), 0.0).astype(np.float32)
    qdec = np.exp((idx + 1.0)[None, :] * _LOG_G[:, None]).astype(np.float32)
    kdec = np.exp((C - 1 - idx)[None, :] * _LOG_G[:, None]).astype(np.float32)
    qdec = np.broadcast_to(qdec[:, :, None], (RET_HEADS, C, RET_DK)).copy()
    kdec = np.broadcast_to(kdec[:, :, None], (RET_HEADS, C, RET_DK)).copy()
    return jnp.asarray(dmask), jnp.asarray(qdec), jnp.asarray(kdec)


def _sample_tables():
    P = SEQ_PAD
    rows = np.arange(P, dtype=np.float32)
    i = rows - TOK0
    tok = (i >= 0) & (i < 4)
    cm = np.zeros((RET_HEADS, 4, P, RET_DV), np.float32)
    qdec = np.zeros((RET_HEADS, P, RET_DK), np.float32)
    kdec = np.zeros((RET_HEADS, P, RET_DK), np.float32)
    for h in range(RET_HEADS):
        for s in range(4):
            d = i - s
            col = np.where(tok & (d >= 0), np.exp(np.where(d >= 0, d, 0.0) * _LOG_G[h]), 0.0)
            cm[h, s] = col[:, None]
        qdec[h] = np.where(tok, np.exp((i + 1.0) * _LOG_G[h]), 0.0)[:, None]
        kdec[h] = np.where(tok, np.exp((3.0 - i) * _LOG_G[h]), 0.0)[:, None]
    return jnp.asarray(cm), jnp.asarray(qdec), jnp.asarray(kdec)


def _block_diag(w):
    n, d, _ = w.shape
    g = n // 2
    w = w.reshape(2, g, d, d)
    eye = jnp.eye(g, dtype=w.dtype)
    return (eye[None, :, None, :, None] * w[:, :, :, None, :]).reshape(2, g * d, g * d)


def kernel(x_prompt, x_sample, state_ret, state_lru, state_conv, cache_swa_k, cache_swa_v, norm_g, w_in, w_out, ret_norm_g, conv_w, conv_b, w_rgate, b_rgate, w_igate, b_igate, lru_lambda, q_norm_g, k_norm_g, attn_sinks):
    Bs, Ts, _ = x_sample.shape
    ptab = _prompt_tables()
    stab = _sample_tables()
    yp = x_prompt
    ys = jnp.pad(x_sample, ((0, 0), (TOK0, SEQ_PAD - TOK0 - Ts), (0, 0))).reshape(Bs * SEQ_PAD, D_MODEL)
    cache_k = cache_swa_k.reshape(DEPTH, Bs, 2 * WINDOW, SWA_HEAD_DIM)
    cache_v = cache_swa_v.reshape(DEPTH, Bs, 2 * WINDOW, SWA_HEAD_DIM)
    p_state, s_state = (), ()
    s_h, s_conv = [], []
    for l in range(DEPTH):
        ng = norm_g[l].reshape(1, -1)
        win = w_in[l].astype(BF16)
        wout = w_out[l].astype(BF16)
        retg = ret_norm_g[l].reshape(1, -1)
        convb = conv_b[l].reshape(1, -1)
        wr = _block_diag(w_rgate[l]).astype(BF16)
        wi = _block_diag(w_igate[l]).astype(BF16)
        br = b_rgate[l].reshape(1, -1)
        bi = b_igate[l].reshape(1, -1)
        lam = lru_lambda[l].reshape(1, -1)
        qg = q_norm_g[l].reshape(1, -1)
        kg = k_norm_g[l].reshape(1, -1)
        sinks = attn_sinks[l]

        wts = (sinks, ng, win, wout, retg, conv_w[l], convb, wr, wi, br, bi, lam, qg, kg)
        yp, *p_state = _prompt_layer(l, yp, wts, ptab, tuple(p_state))

        proj = _sample_proj(ys, ng, win)
        mix, *s_state = _sample_mix(l, sinks, proj, state_ret, cache_k, cache_v, retg, qg, kg, stab,
                                    tuple(s_state))
        conv8 = jnp.pad(state_conv[l], ((0, 0), (0, SEQ_PAD - (CONV_W - 1)), (0, 0))).reshape(Bs * SEQ_PAD, LRU_WIDTH)
        h8 = jnp.pad(state_lru[l][:, None, :], ((0, 0), (TOK0 - 1, SEQ_PAD - TOK0), (0, 0))).reshape(Bs * SEQ_PAD, LRU_WIDTH)
        ys, convo, ho = _sample_out(ys, proj, mix, conv8, h8, conv_w[l], convb, wr, wi, br, bi, lam, wout)
        s_h.append(ho.reshape(Bs, SEQ_PAD, LRU_WIDTH)[:, TOK0 + Ts - 1])
        s_conv.append(convo.reshape(Bs, SEQ_PAD, LRU_WIDTH)[:, TOK0 + Ts - (CONV_W - 1):TOK0 + Ts])

    p_ret, p_h, p_conv, p_k, p_v = p_state
    s_ret, s_k, s_v = s_state
    Bp = x_prompt.shape[0]
    kv5 = lambda a, n: a.reshape(DEPTH, n, WINDOW, SWA_KV_HEADS, SWA_HEAD_DIM)
    y_sample = ys.reshape(Bs, SEQ_PAD, D_MODEL)[:, TOK0:TOK0 + Ts]
    return (yp, y_sample,
            p_ret, p_h.reshape(DEPTH, Bp, LRU_WIDTH), p_conv, kv5(p_k, Bp), kv5(p_v, Bp),
            s_ret, jnp.stack(s_h), jnp.stack(s_conv), kv5(s_k, Bs), kv5(s_v, Bs))
```

```python
import functools
import math

import numpy as np
import jax
import jax.numpy as jnp
from jax import lax
from jax.experimental import pallas as pl
from jax.experimental.pallas import tpu as pltpu

D_MODEL = 1024
DEPTH = 2
PAST_LEN = 16384
GROUP_WIDTH = 512
RET_HEADS = 4
RET_DK = 128
RET_DV = 128
RET_CHUNK = 128
LRU_WIDTH = 512
LRU_BLOCKS = 8
LRU_C = 8.0
CONV_W = 4
SWA_HEADS = 4
SWA_KV_HEADS = 2
SWA_GROUP = 2
SWA_HEAD_DIM = 128
WINDOW = 128
NORM_EPS = 1e-6
NEG_INF = -1e30

IN_WIDTH = 4608
MIX_WIDTH = 1536
QA, KA, VA, GA, XB, GB, QC, KC, VC, GC = 0, 512, 1024, 1536, 2048, 2560, 3072, 3584, 3840, 4096

F32 = jnp.float32
BF16 = jnp.bfloat16

SUBLANES = 8
SEQ_PAD = 8
TOK0 = 3
VMEM_LIMIT = 56 * 1024 * 1024

_LOG_G = np.log1p(-np.power(np.float32(2.0), (-5.0 - np.arange(RET_HEADS)).astype(np.float32))).astype(np.float32)
_SLOPES = [2.0 ** (-8.0 * (h + 1) / SWA_HEADS) for h in range(SWA_HEADS)]


def _rms(x, g):
    ms = jnp.mean(x * x, axis=-1, keepdims=True)
    return x * lax.rsqrt(ms + NORM_EPS) * g


def _silu(x):
    return x * jax.nn.sigmoid(x)


def _softplus(x):
    return jnp.maximum(x, 0.0) + jnp.log1p(jnp.exp(-jnp.abs(x)))


def _dot(a, b):
    return jnp.dot(a, b, preferred_element_type=F32)


def _dot_nt(a, b):
    return lax.dot_general(a, b, (((1,), (1,)), ((), ())), preferred_element_type=F32)


def _dot_tn(a, b):
    return lax.dot_general(a, b, (((0,), (0,)), ((), ())), preferred_element_type=F32)


def _lru_gates(conv, wr_ref, wi_ref, br_ref, bi_ref, lam_ref):
    gin = conv.astype(BF16)
    half = LRU_WIDTH // 2
    lo, hi = gin[:, :half], gin[:, half:]
    r = jax.nn.sigmoid(jnp.concatenate([_dot(lo, wr_ref[0]), _dot(hi, wr_ref[1])], axis=1) + br_ref[...])
    i = jax.nn.sigmoid(jnp.concatenate([_dot(lo, wi_ref[0]), _dot(hi, wi_ref[1])], axis=1) + bi_ref[...])
    log_a = -LRU_C * r * _softplus(-lam_ref[...])
    a = jnp.exp(log_a)
    m2 = 1.0 - a * a
    mult = jnp.where(m2 > 0.0, m2 * lax.rsqrt(m2), 0.0)
    return a, mult, i


_DONE = object()


def _run(gen):
    for _ in gen:
        pass


def _zip_stages(gens):
    gens = list(gens)
    while gens:
        gens = [g for g in gens if next(g, _DONE) is not _DONE]
        if gens:
            yield


def _chain(gens):
    for g in gens:
        yield from g


def _interleave(main, sides, n_main, n_side):
    side = _chain(sides)
    lead = 2
    for _ in range(lead):
        next(side)
    done_side = lead
    for i, _ in enumerate(main):
        want = lead + ((i + 1) * (n_side - lead)) // n_main
        while done_side < want and next(side, _DONE) is not _DONE:
            done_side += 1
    _run(side)


IN_PROJ_COLS = 512
IN_PROJ_STAGES = 1 + IN_WIDTH // IN_PROJ_COLS


def _in_proj(x_ref, ng_ref, win_ref, proj_ref):
    hb = _rms(x_ref[...], ng_ref[...]).astype(BF16)
    yield
    for j in range(IN_WIDTH // IN_PROJ_COLS):
        cols = slice(j * IN_PROJ_COLS, (j + 1) * IN_PROJ_COLS)
        proj_ref[:, cols] = _dot(hb, win_ref[:, cols])
        yield


OUT_PROJ_STAGES = D_MODEL // 256


def _out_proj(mixed_s, x_ref, y_ref, wout_ref):
    mixed = mixed_s[...]
    for j in range(OUT_PROJ_STAGES):
        cols = slice(j * 256, (j + 1) * 256)
        y_ref[:, cols] = x_ref[:, cols] + _dot(mixed, wout_ref[:, cols])
        yield


def _mix_stages(tt):
    return 4 + tt // SUBLANES // 8 + (tt // RET_CHUNK) * 5


def _scan_rows(a, u, h_in, tt):
    G = tt // SUBLANES
    W = a.shape[-1]
    a3 = a.reshape(G, SUBLANES, W)
    u3 = u.reshape(G, SUBLANES, W)
    r3 = lax.broadcasted_iota(jnp.int32, (G, SUBLANES, W), 1)
    sh = 1
    while sh < SUBLANES:
        keep = r3 >= sh
        a_sh = jnp.where(keep, pltpu.roll(a3, sh, axis=1), 1.0)
        u_sh = jnp.where(keep, pltpu.roll(u3, sh, axis=1), 0.0)
        u3 = a3 * u_sh + u3
        a3 = a3 * a_sh
        sh *= 2
    yield
    hs = []
    for g in range(G):
        hg = a3[g] * h_in + u3[g]
        hs.append(hg)
        h_in = hg[SUBLANES - 1:SUBLANES, :]
        if g % 8 == 7:
            yield
    return jnp.concatenate(hs, axis=0)


def _retention_head(h, proj_s, r0, s_ref, retg_ref, dmask_ref, qdec_ref, kdec_ref, mixed_s):
    C = RET_CHUNK
    cs = slice(h * RET_DK, (h + 1) * RET_DK)
    q = proj_s[r0:r0 + C, QA + h * 128:QA + (h + 1) * 128]
    k = proj_s[r0:r0 + C, KA + h * 128:KA + (h + 1) * 128] * (RET_DK ** -0.5)
    vb = proj_s[r0:r0 + C, VA + h * 128:VA + (h + 1) * 128].astype(BF16)
    sc = _dot_nt(q.astype(BF16), k.astype(BF16)) * dmask_ref[h]
    yield
    s_prev = s_ref[0, h]
    lhs = jnp.concatenate([sc.astype(BF16), (q * qdec_ref[h]).astype(BF16)], axis=1)
    rhs = jnp.concatenate([vb, s_prev.astype(BF16)], axis=0)
    o = _dot(lhs, rhs)
    yield
    kv = _dot_tn((k * kdec_ref[h]).astype(BF16), vb)
    s_ref[0, h] = float(np.exp(np.float32(C) * _LOG_G[h])) * s_prev + kv
    yield
    o = _rms(o, retg_ref[:, cs])
    mixed_s[r0:r0 + C, cs] = (o * _silu(proj_s[r0:r0 + C, GA + h * 128:GA + (h + 1) * 128])).astype(BF16)
    yield


def _swa_kv_head(kh, proj_s, r0, blk, sinks_ref, qg_ref, kg_ref, pen_ref, wnd_ref, mixed_s, kprev_s, vprev_s):
    C = RET_CHUNK
    ks = slice(kh * 128, (kh + 1) * 128)
    h0, h1 = kh * SWA_GROUP, kh * SWA_GROUP + 1
    kn = _rms(proj_s[r0:r0 + C, KC + kh * 128:KC + (kh + 1) * 128], kg_ref[...])
    vv = proj_s[r0:r0 + C, VC + kh * 128:VC + (kh + 1) * 128]
    kband = jnp.concatenate([kprev_s[:, ks], kn], axis=0).astype(BF16)
    vband = jnp.concatenate([vprev_s[:, ks], vv], axis=0).astype(BF16)
    kprev_s[:, ks] = kn
    vprev_s[:, ks] = vv
    q0 = _rms(proj_s[r0:r0 + C, QC + h0 * 128:QC + (h0 + 1) * 128], qg_ref[...])
    q1 = _rms(proj_s[r0:r0 + C, QC + h1 * 128:QC + (h1 + 1) * 128], qg_ref[...])
    qq = jnp.concatenate([q0, q1], axis=0).astype(BF16)
    yield
    s = _dot_nt(qq, kband) * (SWA_HEAD_DIM ** -0.5)
    win = jnp.where(blk > 0, wnd_ref[0], wnd_ref[1])
    s = jnp.where(win > 0.5, s - pen_ref[kh], NEG_INF)
    yield
    rowc = lax.broadcasted_iota(jnp.int32, (2 * C, 1), 0)
    sink = jnp.where(rowc >= C, sinks_ref[h1], sinks_ref[h0])
    m = jnp.maximum(jnp.max(s, axis=-1, keepdims=True), sink)
    p = jnp.exp(s - m)
    denom = jnp.sum(p, axis=-1, keepdims=True) + jnp.exp(sink - m)
    yield
    o = _dot((p / denom).astype(BF16), vband)
    yield
    for g, hh in enumerate((h0, h1)):
        gc = proj_s[r0:r0 + C, GC + hh * 128:GC + (hh + 1) * 128]
        mixed_s[r0:r0 + C, 2 * GROUP_WIDTH + hh * 128:2 * GROUP_WIDTH + (hh + 1) * 128] = (
            (o[g * C:(g + 1) * C] * _silu(gc)).astype(BF16))
    yield


def _mix_tile(proj_s, t, tt, sinks_ref, retg_ref, convw_ref, convb_ref,
              wr_ref, wi_ref, br_ref, bi_ref, lam_ref, qg_ref, kg_ref, dmask_ref, qdec_ref, kdec_ref,
              pen_ref, wnd_ref, s_ref, mixed_s, xtail_s, kprev_s, vprev_s, hc_s):
    nch = tt // RET_CHUNK
    C = RET_CHUNK

    xb = proj_s[:, XB:XB + LRU_WIDTH]
    xfull = jnp.concatenate([xtail_s[...], xb], axis=0)
    xtail_s[...] = xb[tt - SUBLANES:tt, :]
    conv = convb_ref[...] + convw_ref[0:1, :] * pltpu.roll(xfull, 3, axis=0)[SUBLANES:]
    conv = conv + convw_ref[1:2, :] * pltpu.roll(xfull, 2, axis=0)[SUBLANES:]
    conv = conv + convw_ref[2:3, :] * pltpu.roll(xfull, 1, axis=0)[SUBLANES:]
    conv = conv + convw_ref[3:4, :] * xb
    yield

    a, mult, ig = _lru_gates(conv, wr_ref, wi_ref, br_ref, bi_ref, lam_ref)
    yield
    row8 = lax.broadcasted_iota(jnp.int32, (SUBLANES, LRU_WIDTH), 0)
    mult = jnp.concatenate(
        [jnp.where(row8 == jnp.where(t == 0, 0, -1), 1.0, mult[:SUBLANES]), mult[SUBLANES:]], axis=0)
    u = mult * ig * conv
    hseq = yield from _scan_rows(a, u, hc_s[...], tt)
    hc_s[...] = hseq[tt - 1:tt, :]
    mixed_s[:, GROUP_WIDTH:2 * GROUP_WIDTH] = (hseq * _silu(proj_s[:, GB:GB + LRU_WIDTH])).astype(BF16)
    yield

    for c in range(nch):
        r0 = c * C
        blk = t * nch + c
        yield from _zip_stages(
            [_retention_head(h, proj_s, r0, s_ref, retg_ref, dmask_ref, qdec_ref, kdec_ref, mixed_s)
             for h in range(RET_HEADS)]
            + [_swa_kv_head(kh, proj_s, r0, blk, sinks_ref, qg_ref, kg_ref, pen_ref, wnd_ref,
                            mixed_s, kprev_s, vprev_s)
               for kh in range(SWA_KV_HEADS)])


def _prompt_kernel(sinks_ref, xa_ref, xn_ref, ng_ref, win_ref, wout_ref, retg_ref, convw_ref, convb_ref,
                   wr_ref, wi_ref, br_ref, bi_ref, lam_ref, qg_ref, kg_ref,
                   dmask_ref, qdec_ref, kdec_ref, pen_ref, wnd_ref, *rest, tt, ntile, npairs):
    y_ref, s_ref, h_ref, conv_ref, kout_ref, vout_ref = rest[-14:-8]
    proj_a, proj_b, mixed_a, mixed_b, xtail_s, kprev_s, vprev_s, hc_s = rest[-8:]
    b = pl.program_id(0)
    p = pl.program_id(1)

    @pl.when(p == 0)
    def _init():
        s_ref[...] = jnp.zeros_like(s_ref)
        xtail_s[...] = jnp.zeros_like(xtail_s)
        kprev_s[...] = jnp.zeros_like(kprev_s)
        vprev_s[...] = jnp.zeros_like(vprev_s)
        hc_s[...] = jnp.zeros_like(hc_s)

    xs = [xa_ref.at[0, i * tt:(i + 1) * tt, :] for i in range(ntile)] + [xn_ref.at[0]]
    ys = [y_ref.at[0, i * tt:(i + 1) * tt, :] for i in range(ntile)]
    projs = (proj_a, proj_b)
    mixeds = (mixed_a, mixed_b)

    @pl.when(jnp.logical_and(b == 0, p == 0))
    def _prologue():
        _run(_in_proj(xs[0], ng_ref, win_ref, proj_a))

    common = (sinks_ref, retg_ref, convw_ref, convb_ref, wr_ref, wi_ref, br_ref, bi_ref,
              lam_ref, qg_ref, kg_ref, dmask_ref, qdec_ref, kdec_ref, pen_ref, wnd_ref, s_ref.at[0])
    state = (xtail_s, kprev_s, vprev_s, hc_s)

    n_main = _mix_stages(tt)
    for i in range(ntile):
        sides, n_side = [], IN_PROJ_STAGES
        if i > 0:
            sides.append(_out_proj(mixeds[(i - 1) % 2], xs[i - 1], ys[i - 1], wout_ref))
            n_side += OUT_PROJ_STAGES
        sides.append(_in_proj(xs[i + 1], ng_ref, win_ref, projs[(i + 1) % 2]))
        _interleave(_mix_tile(projs[i % 2], ntile * p + i, tt, *common, mixeds[i % 2], *state),
                    sides, n_main, n_side)
    _run(_out_proj(mixeds[(ntile - 1) % 2], xs[ntile - 1], ys[ntile - 1], wout_ref))

    @pl.when(p == npairs - 1)
    def _state_out():
        h_ref[0, 0] = hc_s[...]
        conv_ref[0, 0] = xtail_s[SUBLANES - (CONV_W - 1):SUBLANES, :]
        for kh in range(SWA_KV_HEADS):
            ks = slice(kh * 128, (kh + 1) * 128)
            kout_ref[0, 0, pl.ds(kh, WINDOW, stride=SWA_KV_HEADS), :] = kprev_s[:, ks]
            vout_ref[0, 0, pl.ds(kh, WINDOW, stride=SWA_KV_HEADS), :] = vprev_s[:, ks]


def _const_spec(shape):
    nd = len(shape)
    return pl.BlockSpec(shape, lambda *_: (0,) * nd)


def _resident_spec(shape):
    nd = len(shape)
    return pl.BlockSpec(shape, lambda *_: (0,) * nd, pipeline_mode=pl.Buffered(1))


def _prompt_layer(layer, x, wts, tables, prev_out, tt=256, ntile=2):
    B, T, D = x.shape
    npairs = T // (ntile * tt)
    (sinks, ng, win, wout, retg, convw, convb, wr, wi, br, bi, lam, qg, kg) = wts
    dmask, qdec, kdec, pen, window = tables
    vec_in = [ng, win, wout, retg, convw, convb, wr, wi, br, bi, lam, qg, kg, dmask, qdec, kdec, pen, window]

    def next_tile(b, p):
        last = p == npairs - 1
        return (jnp.where(last, jnp.minimum(b + 1, B - 1), b), jnp.where(last, 0, ntile * (p + 1)), 0)

    in_specs = ([pl.BlockSpec(memory_space=pltpu.SMEM),
                 pl.BlockSpec((1, ntile * tt, D), lambda b, p: (b, p, 0)),
                 pl.BlockSpec((1, tt, D), next_tile)]
                + [_resident_spec(a.shape) for a in vec_in]
                + [pl.BlockSpec(memory_space=pl.ANY) for _ in prev_out])
    out_shape = (
        jax.ShapeDtypeStruct((B, T, D), F32),
        jax.ShapeDtypeStruct((DEPTH, B, RET_HEADS, RET_DK, RET_DV), F32),
        jax.ShapeDtypeStruct((DEPTH, B, 1, LRU_WIDTH), F32),
        jax.ShapeDtypeStruct((DEPTH, B, CONV_W - 1, LRU_WIDTH), F32),
        jax.ShapeDtypeStruct((DEPTH, B, 2 * WINDOW, SWA_HEAD_DIM), F32),
        jax.ShapeDtypeStruct((DEPTH, B, 2 * WINDOW, SWA_HEAD_DIM), F32),
    )
    out_specs = (
        pl.BlockSpec((1, ntile * tt, D), lambda b, p: (b, p, 0)),
        pl.BlockSpec((1, 1, RET_HEADS, RET_DK, RET_DV), lambda b, p: (layer, b, 0, 0, 0)),
        pl.BlockSpec((1, 1, 1, LRU_WIDTH), lambda b, p: (layer, b, 0, 0)),
        pl.BlockSpec((1, 1, CONV_W - 1, LRU_WIDTH), lambda b, p: (layer, b, 0, 0)),
        pl.BlockSpec((1, 1, 2 * WINDOW, SWA_HEAD_DIM), lambda b, p: (layer, b, 0, 0)),
        pl.BlockSpec((1, 1, 2 * WINDOW, SWA_HEAD_DIM), lambda b, p: (layer, b, 0, 0)),
    )
    n_in = 3 + len(vec_in)
    aliases = {n_in + j: 1 + j for j in range(len(prev_out))}
    scratch = [
        pltpu.VMEM((tt, IN_WIDTH), F32),
        pltpu.VMEM((tt, IN_WIDTH), F32),
        pltpu.VMEM((tt, MIX_WIDTH), BF16),
        pltpu.VMEM((tt, MIX_WIDTH), BF16),
        pltpu.VMEM((SUBLANES, LRU_WIDTH), F32),
        pltpu.VMEM((WINDOW, 256), F32),
        pltpu.VMEM((WINDOW, 256), F32),
        pltpu.VMEM((1, LRU_WIDTH), F32),
    ]
    return pl.pallas_call(
        functools.partial(_prompt_kernel, tt=tt, ntile=ntile, npairs=npairs),
        grid=(B, npairs),
        in_specs=in_specs,
        out_specs=out_specs,
        out_shape=out_shape,
        scratch_shapes=scratch,
        input_output_aliases=aliases,
        compiler_params=pltpu.CompilerParams(
            dimension_semantics=("arbitrary", "arbitrary"), vmem_limit_bytes=VMEM_LIMIT),
        name="prompt_layer",
    )(sinks, x, x, *vec_in, *prev_out)


def _sproj_kernel(x_ref, ng_ref, win_ref, o_ref):
    hb = _rms(x_ref[...], ng_ref[...]).astype(BF16)
    o_ref[...] = _dot(hb, win_ref[...])


def _sample_proj(xp, ng, win):
    R = xp.shape[0]
    nb = IN_WIDTH // 512
    return pl.pallas_call(
        _sproj_kernel,
        grid=(nb,),
        in_specs=[_const_spec(xp.shape), _const_spec(ng.shape),
                  pl.BlockSpec((D_MODEL, 512), lambda j: (0, j))],
        out_specs=pl.BlockSpec((R, 512), lambda j: (0, j)),
        out_shape=jax.ShapeDtypeStruct((R, IN_WIDTH), F32),
        compiler_params=pltpu.CompilerParams(
            dimension_semantics=("arbitrary",), vmem_limit_bytes=VMEM_LIMIT),
        name="sample_proj",
    )(xp, ng, win)


def _sample_mix_seq(sinks_ref, proj_ref, s0_ref, kbuf_ref, vbuf_ref, retg_ref, qg_ref, kg_ref,
                    cm_ref, qdec_ref, kdec_ref, mix_ref, snew_ref, knew_ref, vnew_ref):
    P = SEQ_PAD
    for h in range(RET_HEADS):
        cs = slice(h * 128, (h + 1) * 128)
        q = proj_ref[:, QA + h * 128:QA + (h + 1) * 128]
        k = proj_ref[:, KA + h * 128:KA + (h + 1) * 128] * (RET_DK ** -0.5)
        v = proj_ref[:, VA + h * 128:VA + (h + 1) * 128]
        intra = jnp.zeros((P, RET_DV), F32)
        for s in range(4):
            r_ = TOK0 + s
            w = jnp.sum(q * k[r_:r_ + 1, :], axis=-1, keepdims=True)
            intra = intra + (w * cm_ref[h, s]) * v[r_:r_ + 1, :]
        s_prev = s0_ref[h]
        cross = _dot((q * qdec_ref[h]).astype(BF16), s_prev.astype(BF16))
        kv = _dot_tn((k * kdec_ref[h]).astype(BF16), v.astype(BF16))
        snew_ref[h] = float(np.exp(np.float32(4.0) * _LOG_G[h])) * s_prev + kv
        o = _rms(intra + cross, retg_ref[:, cs])
        mix_ref[:, cs] = o * _silu(proj_ref[:, GA + h * 128:GA + (h + 1) * 128])
        yield

    r16 = lax.broadcasted_iota(jnp.int32, (2 * P, WINDOW), 0)
    j16 = lax.broadcasted_iota(jnp.int32, (2 * P, WINDOW), 1)
    rr = jnp.where(r16 >= P, r16 - P, r16)
    row_ok = jnp.logical_and(rr >= TOK0, rr < TOK0 + 4)
    dist = (rr - TOK0) + WINDOW - j16
    distf = dist.astype(F32)
    valid = jnp.logical_and(jnp.logical_and(dist >= 0, dist < WINDOW), row_ok)
    r16c = lax.broadcasted_iota(jnp.int32, (2 * P, 1), 0)
    rrc = jnp.where(r16c >= P, r16c - P, r16c)
    rowc_ok = jnp.logical_and(rrc >= TOK0, rrc < TOK0 + 4)
    knew_ref[0:2 * (WINDOW - 4), :] = kbuf_ref[2 * 4:2 * WINDOW, :]
    vnew_ref[0:2 * (WINDOW - 4), :] = vbuf_ref[2 * 4:2 * WINDOW, :]
    for kh in range(SWA_KV_HEADS):
        h0, h1 = kh * SWA_GROUP, kh * SWA_GROUP + 1
        kb = kbuf_ref[pl.ds(kh, WINDOW, stride=SWA_KV_HEADS), :]
        vb = vbuf_ref[pl.ds(kh, WINDOW, stride=SWA_KV_HEADS), :]
        kn = _rms(proj_ref[:, KC + kh * 128:KC + (kh + 1) * 128], kg_ref[...])
        vn = proj_ref[:, VC + kh * 128:VC + (kh + 1) * 128]
        q0 = _rms(proj_ref[:, QC + h0 * 128:QC + (h0 + 1) * 128], qg_ref[...])
        q1 = _rms(proj_ref[:, QC + h1 * 128:QC + (h1 + 1) * 128], qg_ref[...])
        qq = jnp.concatenate([q0, q1], axis=0)
        slope = jnp.where(r16 >= P, _SLOPES[h1], _SLOPES[h0])
        slopec = jnp.where(r16c >= P, _SLOPES[h1], _SLOPES[h0])
        sb = _dot_nt(qq.astype(BF16), kb.astype(BF16)) * (SWA_HEAD_DIM ** -0.5)
        sb = jnp.where(valid, sb - slope * distf, NEG_INF)
        yield
        sink = jnp.where(r16c >= P, sinks_ref[h1], sinks_ref[h0])
        m = jnp.maximum(jnp.max(sb, axis=-1, keepdims=True), sink)
        wn = []
        for s in range(4):
            r_ = TOK0 + s
            w = jnp.sum(qq * kn[r_:r_ + 1, :], axis=-1, keepdims=True) * (SWA_HEAD_DIM ** -0.5)
            dn = rrc - r_
            w = jnp.where(jnp.logical_and(dn >= 0, rowc_ok), w - slopec * dn.astype(F32), NEG_INF)
            wn.append(w)
            m = jnp.maximum(m, w)
        pb = jnp.exp(sb - m)
        denom = jnp.sum(pb, axis=-1, keepdims=True) + jnp.exp(sink - m)
        pn = [jnp.exp(w - m) for w in wn]
        for p_ in pn:
            denom = denom + p_
        yield
        o = _dot((pb / denom).astype(BF16), vb.astype(BF16))
        for s in range(4):
            r_ = TOK0 + s
            o = o + (pn[s] / denom) * vn[r_:r_ + 1, :]
        for g, hh in enumerate((h0, h1)):
            gc = proj_ref[:, GC + hh * 128:GC + (hh + 1) * 128]
            mix_ref[:, GROUP_WIDTH + hh * 128:GROUP_WIDTH + (hh + 1) * 128] = o[g * P:(g + 1) * P] * _silu(gc)
        for s in range(4):
            r_out = 2 * (WINDOW - 4 + s) + kh
            knew_ref[r_out:r_out + 1, :] = kn[TOK0 + s:TOK0 + s + 1, :]
            vnew_ref[r_out:r_out + 1, :] = vn[TOK0 + s:TOK0 + s + 1, :]
        yield


def _sample_mix_kernel(sinks_ref, proj_ref, s0_ref, kbuf_ref, vbuf_ref, retg_ref, qg_ref, kg_ref,
                       cm_ref, qdec_ref, kdec_ref, *rest, nseq):
    mix_ref, snew_ref, knew_ref, vnew_ref = rest[-4:]
    gens = []
    for i in range(nseq):
        rows = slice(i * SEQ_PAD, (i + 1) * SEQ_PAD)
        gens.append(_sample_mix_seq(
            sinks_ref, proj_ref.at[rows, :], s0_ref.at[0, i], kbuf_ref.at[0, i], vbuf_ref.at[0, i],
            retg_ref, qg_ref, kg_ref, cm_ref, qdec_ref, kdec_ref,
            mix_ref.at[rows, :], snew_ref.at[0, i], knew_ref.at[0, i], vnew_ref.at[0, i]))
    _run(_zip_stages(gens))


def _sample_mix(layer, sinks, proj, state_ret, cache_k, cache_v, retg, qg, kg, tables, prev_out, nseq=8):
    cm, qdec, kdec = tables
    depth, B = state_ret.shape[:2]
    consts = [retg, qg, kg, cm, qdec, kdec]
    st_spec = pl.BlockSpec((1, nseq, RET_HEADS, RET_DK, RET_DV), lambda i: (layer, i, 0, 0, 0))
    kv_spec = pl.BlockSpec((1, nseq, 2 * WINDOW, SWA_HEAD_DIM), lambda i: (layer, i, 0, 0))
    in_specs = ([pl.BlockSpec(memory_space=pltpu.SMEM),
                 pl.BlockSpec((nseq * SEQ_PAD, IN_WIDTH), lambda i: (i, 0)),
                 st_spec, kv_spec, kv_spec]
                + [_const_spec(a.shape) for a in consts]
                + [pl.BlockSpec(memory_space=pl.ANY) for _ in prev_out])
    out_shape = (
        jax.ShapeDtypeStruct((B * SEQ_PAD, 2 * GROUP_WIDTH), F32),
        jax.ShapeDtypeStruct(state_ret.shape, F32),
        jax.ShapeDtypeStruct(cache_k.shape, F32),
        jax.ShapeDtypeStruct(cache_v.shape, F32),
    )
    out_specs = (pl.BlockSpec((nseq * SEQ_PAD, 2 * GROUP_WIDTH), lambda i: (i, 0)), st_spec, kv_spec, kv_spec)
    n_in = 5 + len(consts)
    aliases = {n_in + j: 1 + j for j in range(len(prev_out))}
    return pl.pallas_call(
        functools.partial(_sample_mix_kernel, nseq=nseq),
        grid=(B // nseq,),
        in_specs=in_specs,
        out_specs=out_specs,
        out_shape=out_shape,
        input_output_aliases=aliases,
        compiler_params=pltpu.CompilerParams(
            dimension_semantics=("arbitrary",), vmem_limit_bytes=VMEM_LIMIT),
        name="sample_mix",
    )(sinks, proj, state_ret, cache_k, cache_v, *consts, *prev_out)


def _sample_out_kernel(x_ref, xb_ref, gb_ref, mix_ref, conv8_ref, h8_ref, convw_ref, convb_ref,
                       wr_ref, wi_ref, br_ref, bi_ref, lam_ref, wout_ref,
                       y_ref, convo_ref, ho_ref):
    R = x_ref.shape[0]
    row = lax.broadcasted_iota(jnp.int32, (R, LRU_WIDTH), 0) & (SEQ_PAD - 1)
    xc = jnp.where(row < TOK0, conv8_ref[...], xb_ref[...])
    convo_ref[...] = xc
    conv = convb_ref[...] + convw_ref[0:1, :] * pltpu.roll(xc, 3, axis=0)
    conv = conv + convw_ref[1:2, :] * pltpu.roll(xc, 2, axis=0)
    conv = conv + convw_ref[2:3, :] * pltpu.roll(xc, 1, axis=0)
    conv = conv + convw_ref[3:4, :] * xc
    a, mult, ig = _lru_gates(conv, wr_ref, wi_ref, br_ref, bi_ref, lam_ref)
    u = mult * ig * conv
    h = h8_ref[...]
    for s in range(4):
        h = jnp.where(row == TOK0 + s, a * pltpu.roll(h, 1, axis=0) + u, h)
    ho_ref[...] = h
    ob = h * _silu(gb_ref[...])
    y = x_ref[...] + _dot(mix_ref[:, 0:GROUP_WIDTH].astype(BF16), wout_ref[0:GROUP_WIDTH, :])
    y = y + _dot(ob.astype(BF16), wout_ref[GROUP_WIDTH:2 * GROUP_WIDTH, :])
    y = y + _dot(mix_ref[:, GROUP_WIDTH:2 * GROUP_WIDTH].astype(BF16), wout_ref[2 * GROUP_WIDTH:, :])
    y_ref[...] = y


def _sample_out(xp, proj, mix, conv8, h8, convw, convb, wr, wi, br, bi, lam, wout, rows=256):
    R = xp.shape[0]
    consts = [convw, convb, wr, wi, br, bi, lam, wout]
    in_specs = ([pl.BlockSpec((rows, D_MODEL), lambda i: (i, 0)),
                 pl.BlockSpec((rows, LRU_WIDTH), lambda i: (i, XB // LRU_WIDTH)),
                 pl.BlockSpec((rows, LRU_WIDTH), lambda i: (i, GB // LRU_WIDTH)),
                 pl.BlockSpec((rows, 2 * GROUP_WIDTH), lambda i: (i, 0)),
                 pl.BlockSpec((rows, LRU_WIDTH), lambda i: (i, 0)),
                 pl.BlockSpec((rows, LRU_WIDTH), lambda i: (i, 0))]
                + [_const_spec(a.shape) for a in consts])
    out_shape = (
        jax.ShapeDtypeStruct((R, D_MODEL), F32),
        jax.ShapeDtypeStruct((R, LRU_WIDTH), F32),
        jax.ShapeDtypeStruct((R, LRU_WIDTH), F32),
    )
    out_specs = (
        pl.BlockSpec((rows, D_MODEL), lambda i: (i, 0)),
        pl.BlockSpec((rows, LRU_WIDTH), lambda i: (i, 0)),
        pl.BlockSpec((rows, LRU_WIDTH), lambda i: (i, 0)),
    )
    return pl.pallas_call(
        _sample_out_kernel,
        grid=(R // rows,),
        in_specs=in_specs,
        out_specs=out_specs,
        out_shape=out_shape,
        compiler_params=pltpu.CompilerParams(
            dimension_semantics=("arbitrary",), vmem_limit_bytes=VMEM_LIMIT),
        name="sample_out",
    )(xp, proj, proj, mix, conv8, h8, *consts)


def _prompt_tables():
    C = RET_CHUNK
    idx = np.arange(C, dtype=np.float32)
    diff = idx[:, None] - idx[None, :]
    causal = diff >= 0
    lg = _LOG_G[:, None, None]
    dmask = np.where(causal[None], np.exp(np.where(causal, diff, 0.0)[None] * lg), 0.0).astype(np.float32)
    qdec = np.exp((idx + 1.0)[None, :] * _LOG_G[:, None]).astype(np.float32)
    kdec = np.exp((C - 1 - idx)[None, :] * _LOG_G[:, None]).astype(np.float32)
    qdec = np.broadcast_to(qdec[:, :, None], (RET_HEADS, C, RET_DK)).copy()
    kdec = np.broadcast_to(kdec[:, :, None], (RET_HEADS, C, RET_DK)).copy()
    ii = np.arange(2 * C)[:, None]
    jj = np.arange(2 * C)[None, :]
    dist = (ii % C) + C - jj
    inside = (dist >= 0) & (dist < WINDOW)
    window = np.stack([inside, inside & (jj >= C)]).astype(np.float32)
    pen = np.stack([np.where(ii >= C, _SLOPES[kh * SWA_GROUP + 1], _SLOPES[kh * SWA_GROUP]) * dist
                    for kh in range(SWA_KV_HEADS)]).astype(np.float32)
    return jnp.asarray(dmask), jnp.asarray(qdec), jnp.asarray(kdec), jnp.asarray(pen), jnp.asarray(window)


def _sample_tables():
    P = SEQ_PAD
    rows = np.arange(P, dtype=np.float32)
    i = rows - TOK0
    tok = (i >= 0) & (i < 4)
    cm = np.zeros((RET_HEADS, 4, P, RET_DV), np.float32)
    qdec = np.zeros((RET_HEADS, P, RET_DK), np.float32)
    kdec = np.zeros((RET_HEADS, P, RET_DK), np.float32)
    for h in range(RET_HEADS):
        for s in range(4):
            d = i - s
            col = np.where(tok & (d >= 0), np.exp(np.where(d >= 0, d, 0.0) * _LOG_G[h]), 0.0)
            cm[h, s] = col[:, None]
        qdec[h] = np.where(tok, np.exp((i + 1.0) * _LOG_G[h]), 0.0)[:, None]
        kdec[h] = np.where(tok, np.exp((3.0 - i) * _LOG_G[h]), 0.0)[:, None]
    return jnp.asarray(cm), jnp.asarray(qdec), jnp.asarray(kdec)


def _block_diag(w):
    n, d, _ = w.shape
    g = n // 2
    w = w.reshape(2, g, d, d)
    eye = jnp.eye(g, dtype=w.dtype)
    return (eye[None, :, None, :, None] * w[:, :, :, None, :]).reshape(2, g * d, g * d)


def kernel(x_prompt, x_sample, state_ret, state_lru, state_conv, cache_swa_k, cache_swa_v, norm_g, w_in, w_out, ret_norm_g, conv_w, conv_b, w_rgate, b_rgate, w_igate, b_igate, lru_lambda, q_norm_g, k_norm_g, attn_sinks):
    Bs, Ts, _ = x_sample.shape
    ptab = _prompt_tables()
    stab = _sample_tables()
    yp = x_prompt
    ys = jnp.pad(x_sample, ((0, 0), (TOK0, SEQ_PAD - TOK0 - Ts), (0, 0))).reshape(Bs * SEQ_PAD, D_MODEL)
    cache_k = cache_swa_k.reshape(DEPTH, Bs, 2 * WINDOW, SWA_HEAD_DIM)
    cache_v = cache_swa_v.reshape(DEPTH, Bs, 2 * WINDOW, SWA_HEAD_DIM)
    p_state, s_state = (), ()
    s_h, s_conv = [], []
    for l in range(DEPTH):
        ng = norm_g[l].reshape(1, -1)
        win = w_in[l].astype(BF16)
        wout = w_out[l].astype(BF16)
        retg = ret_norm_g[l].reshape(1, -1)
        convb = conv_b[l].reshape(1, -1)
        wr = _block_diag(w_rgate[l]).astype(BF16)
        wi = _block_diag(w_igate[l]).astype(BF16)
        br = b_rgate[l].reshape(1, -1)
        bi = b_igate[l].reshape(1, -1)
        lam = lru_lambda[l].reshape(1, -1)
        qg = q_norm_g[l].reshape(1, -1)
        kg = k_norm_g[l].reshape(1, -1)
        sinks = attn_sinks[l]

        wts = (sinks, ng, win, wout, retg, conv_w[l], convb, wr, wi, br, bi, lam, qg, kg)
        yp, *p_state = _prompt_layer(l, yp, wts, ptab, tuple(p_state))

        proj = _sample_proj(ys, ng, win)
        mix, *s_state = _sample_mix(l, sinks, proj, state_ret, cache_k, cache_v, retg, qg, kg, stab,
                                    tuple(s_state))
        conv8 = jnp.pad(state_conv[l], ((0, 0), (0, SEQ_PAD - (CONV_W - 1)), (0, 0))).reshape(Bs * SEQ_PAD, LRU_WIDTH)
        h8 = jnp.pad(state_lru[l][:, None, :], ((0, 0), (TOK0 - 1, SEQ_PAD - TOK0), (0, 0))).reshape(Bs * SEQ_PAD, LRU_WIDTH)
        ys, convo, ho = _sample_out(ys, proj, mix, conv8, h8, conv_w[l], convb, wr, wi, br, bi, lam, wout)
        s_h.append(ho.reshape(Bs, SEQ_PAD, LRU_WIDTH)[:, TOK0 + Ts - 1])
        s_conv.append(convo.reshape(Bs, SEQ_PAD, LRU_WIDTH)[:, TOK0 + Ts - (CONV_W - 1):TOK0 + Ts])

    p_ret, p_h, p_conv, p_k, p_v = p_state
    s_ret, s_k, s_v = s_state
    Bp = x_prompt.shape[0]
    kv5 = lambda a, n: a.reshape(DEPTH, n, WINDOW, SWA_KV_HEADS, SWA_HEAD_DIM)
    y_sample = ys.reshape(Bs, SEQ_PAD, D_MODEL)[:, TOK0:TOK0 + Ts]
    return (yp, y_sample,
            p_ret, p_h.reshape(DEPTH, Bp, LRU_WIDTH), p_conv, kv5(p_k, Bp), kv5(p_v, Bp),
            s_ret, jnp.stack(s_h), jnp.stack(s_conv), kv5(s_k, Bs), kv5(s_v, Bs))
```

```python
import functools
import math

import numpy as np
import jax
import jax.numpy as jnp
from jax import lax
from jax.experimental import pallas as pl
from jax.experimental.pallas import tpu as pltpu

D_MODEL = 1024
DEPTH = 2
PAST_LEN = 16384
GROUP_WIDTH = 512
RET_HEADS = 4
RET_DK = 128
RET_DV = 128
RET_CHUNK = 128
LRU_WIDTH = 512
LRU_BLOCKS = 8
LRU_C = 8.0
CONV_W = 4
SWA_HEADS = 4
SWA_KV_HEADS = 2
SWA_GROUP = 2
SWA_HEAD_DIM = 128
WINDOW = 128
NORM_EPS = 1e-6
NEG_INF = -1e30

IN_WIDTH = 4608
MIX_WIDTH = 1536
QA, KA, VA, GA, XB, GB, QC, KC, VC, GC = 0, 512, 1024, 1536, 2048, 2560, 3072, 3584, 3840, 4096

F32 = jnp.float32
BF16 = jnp.bfloat16

SUBLANES = 8
SEQ_PAD = 8
TOK0 = 3
VMEM_LIMIT = 56 * 1024 * 1024

_LOG_G = np.log1p(-np.power(np.float32(2.0), (-5.0 - np.arange(RET_HEADS)).astype(np.float32))).astype(np.float32)
_SLOPES = [2.0 ** (-8.0 * (h + 1) / SWA_HEADS) for h in range(SWA_HEADS)]


def _rms(x, g):
    ms = jnp.mean(x * x, axis=-1, keepdims=True)
    return x * lax.rsqrt(ms + NORM_EPS) * g


def _silu(x):
    return x * jax.nn.sigmoid(x)


def _softplus(x):
    return jnp.maximum(x, 0.0) + jnp.log1p(jnp.exp(-jnp.abs(x)))


def _dot(a, b):
    return jnp.dot(a, b, preferred_element_type=F32)


def _dot_nt(a, b):
    return lax.dot_general(a, b, (((1,), (1,)), ((), ())), preferred_element_type=F32)


def _dot_tn(a, b):
    return lax.dot_general(a, b, (((0,), (0,)), ((), ())), preferred_element_type=F32)


def _lru_gates(conv, wr_ref, wi_ref, br_ref, bi_ref, lam_ref):
    gin = conv.astype(BF16)
    half = LRU_WIDTH // 2
    lo, hi = gin[:, :half], gin[:, half:]
    r = jax.nn.sigmoid(jnp.concatenate([_dot(lo, wr_ref[0]), _dot(hi, wr_ref[1])], axis=1) + br_ref[...])
    i = jax.nn.sigmoid(jnp.concatenate([_dot(lo, wi_ref[0]), _dot(hi, wi_ref[1])], axis=1) + bi_ref[...])
    log_a = -LRU_C * r * _softplus(-lam_ref[...])
    a = jnp.exp(log_a)
    m2 = 1.0 - a * a
    mult = jnp.where(m2 > 0.0, m2 * lax.rsqrt(m2), 0.0)
    return a, mult, i


_DONE = object()


def _run(gen):
    for _ in gen:
        pass


def _zip_stages(gens):
    gens = list(gens)
    while gens:
        gens = [g for g in gens if next(g, _DONE) is not _DONE]
        if gens:
            yield


def _chain(gens):
    for g in gens:
        yield from g


def _interleave(main, side, n_main, n_side):
    done_side = 0
    if n_side > 0 and next(side, _DONE) is not _DONE:
        done_side = 1
    for i, _ in enumerate(main):
        want = ((i + 1) * n_side) // n_main
        while done_side < want and next(side, _DONE) is not _DONE:
            done_side += 1


IN_PROJ_COLS = 256
IN_PROJ_STAGES = 2 + IN_WIDTH // IN_PROJ_COLS


def _in_proj(x0_ref, x1_ref, ng_ref, win_ref, proj_ref):
    h0 = _rms(x0_ref[...], ng_ref[...]).astype(BF16)
    yield
    h1 = _rms(x1_ref[...], ng_ref[...]).astype(BF16)
    hb = jnp.concatenate([h0, h1], axis=0)
    yield
    for j in range(IN_WIDTH // IN_PROJ_COLS):
        cols = slice(j * IN_PROJ_COLS, (j + 1) * IN_PROJ_COLS)
        proj_ref[:, cols] = _dot(hb, win_ref[:, cols])
        yield


OUT_PROJ_STAGES = D_MODEL // 256


def _out_proj(mixed_s, x_ref, y_ref, wout_ref):
    mixed = mixed_s[...]
    for j in range(OUT_PROJ_STAGES):
        cols = slice(j * 256, (j + 1) * 256)
        y_ref[:, cols] = x_ref[:, cols] + _dot(mixed, wout_ref[:, cols])
        yield


def _mix_stages(tt):
    return 4 + tt // SUBLANES // 8 + (tt // RET_CHUNK) * 5


def _scan_rows(a, u, h_in, tt):
    G = tt // SUBLANES
    W = a.shape[-1]
    a3 = a.reshape(G, SUBLANES, W)
    u3 = u.reshape(G, SUBLANES, W)
    r3 = lax.broadcasted_iota(jnp.int32, (G, SUBLANES, W), 1)
    sh = 1
    while sh < SUBLANES:
        keep = r3 >= sh
        a_sh = jnp.where(keep, pltpu.roll(a3, sh, axis=1), 1.0)
        u_sh = jnp.where(keep, pltpu.roll(u3, sh, axis=1), 0.0)
        u3 = a3 * u_sh + u3
        a3 = a3 * a_sh
        sh *= 2
    yield
    hs = []
    for g in range(G):
        hg = a3[g] * h_in + u3[g]
        hs.append(hg)
        h_in = hg[SUBLANES - 1:SUBLANES, :]
        if g % 8 == 7:
            yield
    return jnp.concatenate(hs, axis=0)


def _retention_head(h, proj_s, r0, s_ref, retg_ref, dmask_ref, qdec_ref, kdec_ref, mixed_s):
    C = RET_CHUNK
    cs = slice(h * RET_DK, (h + 1) * RET_DK)
    q = proj_s[r0:r0 + C, QA + h * 128:QA + (h + 1) * 128]
    k = proj_s[r0:r0 + C, KA + h * 128:KA + (h + 1) * 128] * (RET_DK ** -0.5)
    vb = proj_s[r0:r0 + C, VA + h * 128:VA + (h + 1) * 128].astype(BF16)
    sc = _dot_nt(q.astype(BF16), k.astype(BF16)) * dmask_ref[h]
    yield
    s_prev = s_ref[0, h]
    lhs = jnp.concatenate([sc.astype(BF16), (q * qdec_ref[h]).astype(BF16)], axis=1)
    rhs = jnp.concatenate([vb, s_prev.astype(BF16)], axis=0)
    o = _dot(lhs, rhs)
    yield
    kv = _dot_tn((k * kdec_ref[h]).astype(BF16), vb)
    s_ref[0, h] = float(np.exp(np.float32(C) * _LOG_G[h])) * s_prev + kv
    yield
    o = _rms(o, retg_ref[:, cs])
    mixed_s[r0:r0 + C, cs] = (o * _silu(proj_s[r0:r0 + C, GA + h * 128:GA + (h + 1) * 128])).astype(BF16)
    yield


def _swa_kv_head(kh, proj_s, r0, blk, sinks_ref, qg_ref, kg_ref, mixed_s, kprev_s, vprev_s):
    C = RET_CHUNK
    ks = slice(kh * 128, (kh + 1) * 128)
    h0, h1 = kh * SWA_GROUP, kh * SWA_GROUP + 1
    kn = _rms(proj_s[r0:r0 + C, KC + kh * 128:KC + (kh + 1) * 128], kg_ref[...])
    vv = proj_s[r0:r0 + C, VC + kh * 128:VC + (kh + 1) * 128]
    kband = jnp.concatenate([kprev_s[:, ks], kn], axis=0).astype(BF16)
    vband = jnp.concatenate([vprev_s[:, ks], vv], axis=0).astype(BF16)
    kprev_s[:, ks] = kn
    vprev_s[:, ks] = vv
    q0 = _rms(proj_s[r0:r0 + C, QC + h0 * 128:QC + (h0 + 1) * 128], qg_ref[...])
    q1 = _rms(proj_s[r0:r0 + C, QC + h1 * 128:QC + (h1 + 1) * 128], qg_ref[...])
    qq = jnp.concatenate([q0, q1], axis=0).astype(BF16)
    yield
    s = _dot_nt(qq, kband) * (SWA_HEAD_DIM ** -0.5)
    ii = lax.broadcasted_iota(jnp.int32, (2 * C, 2 * C), 0)
    jj = lax.broadcasted_iota(jnp.int32, (2 * C, 2 * C), 1)
    dist = jnp.where(ii >= C, ii - C, ii) + C - jj
    valid = jnp.logical_and(jnp.logical_and(dist >= 0, dist < WINDOW), jj >= jnp.where(blk > 0, 0, C))
    slope = jnp.where(ii >= C, _SLOPES[h1], _SLOPES[h0])
    s = jnp.where(valid, s - slope * dist.astype(F32), NEG_INF)
    yield
    rowc = lax.broadcasted_iota(jnp.int32, (2 * C, 1), 0)
    sink = jnp.where(rowc >= C, sinks_ref[h1], sinks_ref[h0])
    m = jnp.maximum(jnp.max(s, axis=-1, keepdims=True), sink)
    p = jnp.exp(s - m)
    denom = jnp.sum(p, axis=-1, keepdims=True) + jnp.exp(sink - m)
    yield
    o = _dot((p / denom).astype(BF16), vband)
    yield
    for g, hh in enumerate((h0, h1)):
        gc = proj_s[r0:r0 + C, GC + hh * 128:GC + (hh + 1) * 128]
        mixed_s[r0:r0 + C, 2 * GROUP_WIDTH + hh * 128:2 * GROUP_WIDTH + (hh + 1) * 128] = (
            (o[g * C:(g + 1) * C] * _silu(gc)).astype(BF16))
    yield


def _mix_tile(proj_s, t, tt, sinks_ref, retg_ref, convw_ref, convb_ref,
              wr_ref, wi_ref, br_ref, bi_ref, lam_ref, qg_ref, kg_ref, dmask_ref, qdec_ref, kdec_ref,
              s_ref, mixed_s, xtail_s, kprev_s, vprev_s, hc_s):
    nch = tt // RET_CHUNK
    C = RET_CHUNK

    xb = proj_s[:, XB:XB + LRU_WIDTH]
    xfull = jnp.concatenate([xtail_s[...], xb], axis=0)
    xtail_s[...] = xb[tt - SUBLANES:tt, :]
    conv = convb_ref[...] + convw_ref[0:1, :] * pltpu.roll(xfull, 3, axis=0)[SUBLANES:]
    conv = conv + convw_ref[1:2, :] * pltpu.roll(xfull, 2, axis=0)[SUBLANES:]
    conv = conv + convw_ref[2:3, :] * pltpu.roll(xfull, 1, axis=0)[SUBLANES:]
    conv = conv + convw_ref[3:4, :] * xb
    yield

    a, mult, ig = _lru_gates(conv, wr_ref, wi_ref, br_ref, bi_ref, lam_ref)
    yield
    row = lax.broadcasted_iota(jnp.int32, (tt, LRU_WIDTH), 0)
    mult = jnp.where(row == jnp.where(t == 0, 0, -1), 1.0, mult)
    u = mult * ig * conv
    hseq = yield from _scan_rows(a, u, hc_s[...], tt)
    hc_s[...] = hseq[tt - 1:tt, :]
    mixed_s[:, GROUP_WIDTH:2 * GROUP_WIDTH] = (hseq * _silu(proj_s[:, GB:GB + LRU_WIDTH])).astype(BF16)
    yield

    for c in range(nch):
        r0 = c * C
        blk = t * nch + c
        yield from _zip_stages(
            [_retention_head(h, proj_s, r0, s_ref, retg_ref, dmask_ref, qdec_ref, kdec_ref, mixed_s)
             for h in range(RET_HEADS)]
            + [_swa_kv_head(kh, proj_s, r0, blk, sinks_ref, qg_ref, kg_ref, mixed_s, kprev_s, vprev_s)
               for kh in range(SWA_KV_HEADS)])


def _prompt_kernel(sinks_ref, xa_ref, xn_ref, ng_ref, win_ref, wout_ref, retg_ref, convw_ref, convb_ref,
                   wr_ref, wi_ref, br_ref, bi_ref, lam_ref, qg_ref, kg_ref,
                   dmask_ref, qdec_ref, kdec_ref, *rest, tt, npairs):
    y_ref, s_ref, h_ref, conv_ref, kout_ref, vout_ref = rest[-14:-8]
    proj_0, proj_1, mixed_a, mixed_b, xtail_s, kprev_s, vprev_s, hc_s = rest[-8:]
    b = pl.program_id(0)
    p = pl.program_id(1)

    @pl.when(p == 0)
    def _init():
        s_ref[...] = jnp.zeros_like(s_ref)
        xtail_s[...] = jnp.zeros_like(xtail_s)
        kprev_s[...] = jnp.zeros_like(kprev_s)
        vprev_s[...] = jnp.zeros_like(vprev_s)
        hc_s[...] = jnp.zeros_like(hc_s)

    xs = [xa_ref.at[0, i * tt:(i + 1) * tt, :] for i in range(2)]
    xnext = [xn_ref.at[0, i * tt:(i + 1) * tt, :] for i in range(2)]
    ys = [y_ref.at[0, i * tt:(i + 1) * tt, :] for i in range(2)]

    @pl.when(jnp.logical_and(b == 0, p == 0))
    def _prologue():
        _run(_in_proj(xs[0], xs[1], ng_ref, win_ref, proj_0))

    common = (sinks_ref, retg_ref, convw_ref, convb_ref, wr_ref, wi_ref, br_ref, bi_ref,
              lam_ref, qg_ref, kg_ref, dmask_ref, qdec_ref, kdec_ref, s_ref.at[0])
    state = (xtail_s, kprev_s, vprev_s, hc_s)
    n_main = _mix_stages(tt)

    def step(cur, nxt):
        nxt_proj = _in_proj(xnext[0], xnext[1], ng_ref, win_ref, nxt)
        first = 2 + (5 * (IN_PROJ_STAGES - 2)) // 9
        _interleave(_mix_tile(cur.at[0:tt, :], 2 * p, tt, *common, mixed_a, *state),
                    nxt_proj, n_main, first)
        _interleave(_mix_tile(cur.at[tt:2 * tt, :], 2 * p + 1, tt, *common, mixed_b, *state),
                    _chain([_out_proj(mixed_a, xs[0], ys[0], wout_ref), nxt_proj]),
                    n_main, OUT_PROJ_STAGES + IN_PROJ_STAGES - first)
        _run(nxt_proj)
        _run(_out_proj(mixed_b, xs[1], ys[1], wout_ref))

    pl.when(p % 2 == 0)(lambda: step(proj_0, proj_1))
    pl.when(p % 2 == 1)(lambda: step(proj_1, proj_0))

    @pl.when(p == npairs - 1)
    def _state_out():
        h_ref[0, 0] = hc_s[...]
        conv_ref[0, 0] = xtail_s[SUBLANES - (CONV_W - 1):SUBLANES, :]
        for kh in range(SWA_KV_HEADS):
            ks = slice(kh * 128, (kh + 1) * 128)
            kout_ref[0, 0, pl.ds(kh, WINDOW, stride=SWA_KV_HEADS), :] = kprev_s[:, ks]
            vout_ref[0, 0, pl.ds(kh, WINDOW, stride=SWA_KV_HEADS), :] = vprev_s[:, ks]


def _const_spec(shape):
    nd = len(shape)
    return pl.BlockSpec(shape, lambda *_: (0,) * nd)


def _resident_spec(shape):
    nd = len(shape)
    return pl.BlockSpec(shape, lambda *_: (0,) * nd, pipeline_mode=pl.Buffered(1))


def _prompt_layer(layer, x, wts, tables, prev_out, tt=256):
    B, T, D = x.shape
    npairs = T // (2 * tt)
    assert npairs % 2 == 0
    (sinks, ng, win, wout, retg, convw, convb, wr, wi, br, bi, lam, qg, kg) = wts
    dmask, qdec, kdec = tables
    vec_in = [ng, win, wout, retg, convw, convb, wr, wi, br, bi, lam, qg, kg, dmask, qdec, kdec]

    def next_pair(b, p):
        last = p == npairs - 1
        return (jnp.where(last, jnp.minimum(b + 1, B - 1), b), jnp.where(last, 0, p + 1), 0)

    in_specs = ([pl.BlockSpec(memory_space=pltpu.SMEM),
                 pl.BlockSpec((1, 2 * tt, D), lambda b, p: (b, p, 0)),
                 pl.BlockSpec((1, 2 * tt, D), next_pair)]
                + [_resident_spec(a.shape) for a in vec_in]
                + [pl.BlockSpec(memory_space=pl.ANY) for _ in prev_out])
    out_shape = (
        jax.ShapeDtypeStruct((B, T, D), F32),
        jax.ShapeDtypeStruct((DEPTH, B, RET_HEADS, RET_DK, RET_DV), F32),
        jax.ShapeDtypeStruct((DEPTH, B, 1, LRU_WIDTH), F32),
        jax.ShapeDtypeStruct((DEPTH, B, CONV_W - 1, LRU_WIDTH), F32),
        jax.ShapeDtypeStruct((DEPTH, B, 2 * WINDOW, SWA_HEAD_DIM), F32),
        jax.ShapeDtypeStruct((DEPTH, B, 2 * WINDOW, SWA_HEAD_DIM), F32),
    )
    out_specs = (
        pl.BlockSpec((1, 2 * tt, D), lambda b, p: (b, p, 0)),
        pl.BlockSpec((1, 1, RET_HEADS, RET_DK, RET_DV), lambda b, p: (layer, b, 0, 0, 0)),
        pl.BlockSpec((1, 1, 1, LRU_WIDTH), lambda b, p: (layer, b, 0, 0)),
        pl.BlockSpec((1, 1, CONV_W - 1, LRU_WIDTH), lambda b, p: (layer, b, 0, 0)),
        pl.BlockSpec((1, 1, 2 * WINDOW, SWA_HEAD_DIM), lambda b, p: (layer, b, 0, 0)),
        pl.BlockSpec((1, 1, 2 * WINDOW, SWA_HEAD_DIM), lambda b, p: (layer, b, 0, 0)),
    )
    n_in = 3 + len(vec_in)
    aliases = {n_in + j: 1 + j for j in range(len(prev_out))}
    scratch = [
        pltpu.VMEM((2 * tt, IN_WIDTH), F32),
        pltpu.VMEM((2 * tt, IN_WIDTH), F32),
        pltpu.VMEM((tt, MIX_WIDTH), BF16),
        pltpu.VMEM((tt, MIX_WIDTH), BF16),
        pltpu.VMEM((SUBLANES, LRU_WIDTH), F32),
        pltpu.VMEM((WINDOW, 256), F32),
        pltpu.VMEM((WINDOW, 256), F32),
        pltpu.VMEM((1, LRU_WIDTH), F32),
    ]
    return pl.pallas_call(
        functools.partial(_prompt_kernel, tt=tt, npairs=npairs),
        grid=(B, npairs),
        in_specs=in_specs,
        out_specs=out_specs,
        out_shape=out_shape,
        scratch_shapes=scratch,
        input_output_aliases=aliases,
        compiler_params=pltpu.CompilerParams(
            dimension_semantics=("arbitrary", "arbitrary"), vmem_limit_bytes=VMEM_LIMIT),
        name="prompt_layer",
    )(sinks, x, x, *vec_in, *prev_out)


def _sproj_kernel(x_ref, ng_ref, win_ref, o_ref):
    hb = _rms(x_ref[...], ng_ref[...]).astype(BF16)
    o_ref[...] = _dot(hb, win_ref[...])


def _sample_proj(xp, ng, win):
    R = xp.shape[0]
    nb = IN_WIDTH // 512
    return pl.pallas_call(
        _sproj_kernel,
        grid=(nb,),
        in_specs=[_const_spec(xp.shape), _const_spec(ng.shape),
                  pl.BlockSpec((D_MODEL, 512), lambda j: (0, j))],
        out_specs=pl.BlockSpec((R, 512), lambda j: (0, j)),
        out_shape=jax.ShapeDtypeStruct((R, IN_WIDTH), F32),
        compiler_params=pltpu.CompilerParams(
            dimension_semantics=("arbitrary",), vmem_limit_bytes=VMEM_LIMIT),
        name="sample_proj",
    )(xp, ng, win)


def _sample_mix_seq(sinks_ref, proj_ref, s0_ref, kbuf_ref, vbuf_ref, retg_ref, qg_ref, kg_ref,
                    cm_ref, qdec_ref, kdec_ref, mix_ref, snew_ref, knew_ref, vnew_ref):
    P = SEQ_PAD
    for h in range(RET_HEADS):
        cs = slice(h * 128, (h + 1) * 128)
        q = proj_ref[:, QA + h * 128:QA + (h + 1) * 128]
        k = proj_ref[:, KA + h * 128:KA + (h + 1) * 128] * (RET_DK ** -0.5)
        v = proj_ref[:, VA + h * 128:VA + (h + 1) * 128]
        intra = jnp.zeros((P, RET_DV), F32)
        for s in range(4):
            r_ = TOK0 + s
            w = jnp.sum(q * k[r_:r_ + 1, :], axis=-1, keepdims=True)
            intra = intra + (w * cm_ref[h, s]) * v[r_:r_ + 1, :]
        s_prev = s0_ref[h]
        cross = _dot((q * qdec_ref[h]).astype(BF16), s_prev.astype(BF16))
        kv = _dot_tn((k * kdec_ref[h]).astype(BF16), v.astype(BF16))
        snew_ref[h] = float(np.exp(np.float32(4.0) * _LOG_G[h])) * s_prev + kv
        o = _rms(intra + cross, retg_ref[:, cs])
        mix_ref[:, cs] = o * _silu(proj_ref[:, GA + h * 128:GA + (h + 1) * 128])
        yield

    r16 = lax.broadcasted_iota(jnp.int32, (2 * P, WINDOW), 0)
    j16 = lax.broadcasted_iota(jnp.int32, (2 * P, WINDOW), 1)
    rr = jnp.where(r16 >= P, r16 - P, r16)
    row_ok = jnp.logical_and(rr >= TOK0, rr < TOK0 + 4)
    dist = (rr - TOK0) + WINDOW - j16
    distf = dist.astype(F32)
    valid = jnp.logical_and(jnp.logical_and(dist >= 0, dist < WINDOW), row_ok)
    r16c = lax.broadcasted_iota(jnp.int32, (2 * P, 1), 0)
    rrc = jnp.where(r16c >= P, r16c - P, r16c)
    rowc_ok = jnp.logical_and(rrc >= TOK0, rrc < TOK0 + 4)
    knew_ref[0:2 * (WINDOW - 4), :] = kbuf_ref[2 * 4:2 * WINDOW, :]
    vnew_ref[0:2 * (WINDOW - 4), :] = vbuf_ref[2 * 4:2 * WINDOW, :]
    for kh in range(SWA_KV_HEADS):
        h0, h1 = kh * SWA_GROUP, kh * SWA_GROUP + 1
        kb = kbuf_ref[pl.ds(kh, WINDOW, stride=SWA_KV_HEADS), :]
        vb = vbuf_ref[pl.ds(kh, WINDOW, stride=SWA_KV_HEADS), :]
        kn = _rms(proj_ref[:, KC + kh * 128:KC + (kh + 1) * 128], kg_ref[...])
        vn = proj_ref[:, VC + kh * 128:VC + (kh + 1) * 128]
        q0 = _rms(proj_ref[:, QC + h0 * 128:QC + (h0 + 1) * 128], qg_ref[...])
        q1 = _rms(proj_ref[:, QC + h1 * 128:QC + (h1 + 1) * 128], qg_ref[...])
        qq = jnp.concatenate([q0, q1], axis=0)
        slope = jnp.where(r16 >= P, _SLOPES[h1], _SLOPES[h0])
        slopec = jnp.where(r16c >= P, _SLOPES[h1], _SLOPES[h0])
        sb = _dot_nt(qq.astype(BF16), kb.astype(BF16)) * (SWA_HEAD_DIM ** -0.5)
        sb = jnp.where(valid, sb - slope * distf, NEG_INF)
        yield
        sink = jnp.where(r16c >= P, sinks_ref[h1], sinks_ref[h0])
        m = jnp.maximum(jnp.max(sb, axis=-1, keepdims=True), sink)
        wn = []
        for s in range(4):
            r_ = TOK0 + s
            w = jnp.sum(qq * kn[r_:r_ + 1, :], axis=-1, keepdims=True) * (SWA_HEAD_DIM ** -0.5)
            dn = rrc - r_
            w = jnp.where(jnp.logical_and(dn >= 0, rowc_ok), w - slopec * dn.astype(F32), NEG_INF)
            wn.append(w)
            m = jnp.maximum(m, w)
        pb = jnp.exp(sb - m)
        denom = jnp.sum(pb, axis=-1, keepdims=True) + jnp.exp(sink - m)
        pn = [jnp.exp(w - m) for w in wn]
        for p_ in pn:
            denom = denom + p_
        yield
        o = _dot((pb / denom).astype(BF16), vb.astype(BF16))
        for s in range(4):
            r_ = TOK0 + s
            o = o + (pn[s] / denom) * vn[r_:r_ + 1, :]
        for g, hh in enumerate((h0, h1)):
            gc = proj_ref[:, GC + hh * 128:GC + (hh + 1) * 128]
            mix_ref[:, GROUP_WIDTH + hh * 128:GROUP_WIDTH + (hh + 1) * 128] = o[g * P:(g + 1) * P] * _silu(gc)
        for s in range(4):
            r_out = 2 * (WINDOW - 4 + s) + kh
            knew_ref[r_out:r_out + 1, :] = kn[TOK0 + s:TOK0 + s + 1, :]
            vnew_ref[r_out:r_out + 1, :] = vn[TOK0 + s:TOK0 + s + 1, :]
        yield


def _sample_mix_kernel(sinks_ref, proj_ref, s0_ref, kbuf_ref, vbuf_ref, retg_ref, qg_ref, kg_ref,
                       cm_ref, qdec_ref, kdec_ref, *rest, nseq):
    mix_ref, snew_ref, knew_ref, vnew_ref = rest[-4:]
    gens = []
    for i in range(nseq):
        rows = slice(i * SEQ_PAD, (i + 1) * SEQ_PAD)
        gens.append(_sample_mix_seq(
            sinks_ref, proj_ref.at[rows, :], s0_ref.at[0, i], kbuf_ref.at[0, i], vbuf_ref.at[0, i],
            retg_ref, qg_ref, kg_ref, cm_ref, qdec_ref, kdec_ref,
            mix_ref.at[rows, :], snew_ref.at[0, i], knew_ref.at[0, i], vnew_ref.at[0, i]))
    _run(_zip_stages(gens))


def _sample_mix(layer, sinks, proj, state_ret, cache_k, cache_v, retg, qg, kg, tables, prev_out, nseq=8):
    cm, qdec, kdec = tables
    depth, B = state_ret.shape[:2]
    consts = [retg, qg, kg, cm, qdec, kdec]
    st_spec = pl.BlockSpec((1, nseq, RET_HEADS, RET_DK, RET_DV), lambda i: (layer, i, 0, 0, 0))
    kv_spec = pl.BlockSpec((1, nseq, 2 * WINDOW, SWA_HEAD_DIM), lambda i: (layer, i, 0, 0))
    in_specs = ([pl.BlockSpec(memory_space=pltpu.SMEM),
                 pl.BlockSpec((nseq * SEQ_PAD, IN_WIDTH), lambda i: (i, 0)),
                 st_spec, kv_spec, kv_spec]
                + [_const_spec(a.shape) for a in consts]
                + [pl.BlockSpec(memory_space=pl.ANY) for _ in prev_out])
    out_shape = (
        jax.ShapeDtypeStruct((B * SEQ_PAD, 2 * GROUP_WIDTH), F32),
        jax.ShapeDtypeStruct(state_ret.shape, F32),
        jax.ShapeDtypeStruct(cache_k.shape, F32),
        jax.ShapeDtypeStruct(cache_v.shape, F32),
    )
    out_specs = (pl.BlockSpec((nseq * SEQ_PAD, 2 * GROUP_WIDTH), lambda i: (i, 0)), st_spec, kv_spec, kv_spec)
    n_in = 5 + len(consts)
    aliases = {n_in + j: 1 + j for j in range(len(prev_out))}
    return pl.pallas_call(
        functools.partial(_sample_mix_kernel, nseq=nseq),
        grid=(B // nseq,),
        in_specs=in_specs,
        out_specs=out_specs,
        out_shape=out_shape,
        input_output_aliases=aliases,
        compiler_params=pltpu.CompilerParams(
            dimension_semantics=("arbitrary",), vmem_limit_bytes=VMEM_LIMIT),
        name="sample_mix",
    )(sinks, proj, state_ret, cache_k, cache_v, *consts, *prev_out)


def _sample_out_kernel(x_ref, xb_ref, gb_ref, mix_ref, conv8_ref, h8_ref, convw_ref, convb_ref,
                       wr_ref, wi_ref, br_ref, bi_ref, lam_ref, wout_ref,
                       y_ref, convo_ref, ho_ref):
    R = x_ref.shape[0]
    row = lax.broadcasted_iota(jnp.int32, (R, LRU_WIDTH), 0) & (SEQ_PAD - 1)
    xc = jnp.where(row < TOK0, conv8_ref[...], xb_ref[...])
    convo_ref[...] = xc
    conv = convb_ref[...] + convw_ref[0:1, :] * pltpu.roll(xc, 3, axis=0)
    conv = conv + convw_ref[1:2, :] * pltpu.roll(xc, 2, axis=0)
    conv = conv + convw_ref[2:3, :] * pltpu.roll(xc, 1, axis=0)
    conv = conv + convw_ref[3:4, :] * xc
    a, mult, ig = _lru_gates(conv, wr_ref, wi_ref, br_ref, bi_ref, lam_ref)
    u = mult * ig * conv
    h = h8_ref[...]
    for s in range(4):
        h = jnp.where(row == TOK0 + s, a * pltpu.roll(h, 1, axis=0) + u, h)
    ho_ref[...] = h
    ob = h * _silu(gb_ref[...])
    y = x_ref[...] + _dot(mix_ref[:, 0:GROUP_WIDTH].astype(BF16), wout_ref[0:GROUP_WIDTH, :])
    y = y + _dot(ob.astype(BF16), wout_ref[GROUP_WIDTH:2 * GROUP_WIDTH, :])
    y = y + _dot(mix_ref[:, GROUP_WIDTH:2 * GROUP_WIDTH].astype(BF16), wout_ref[2 * GROUP_WIDTH:, :])
    y_ref[...] = y


def _sample_out(xp, proj, mix, conv8, h8, convw, convb, wr, wi, br, bi, lam, wout, rows=256):
    R = xp.shape[0]
    consts = [convw, convb, wr, wi, br, bi, lam, wout]
    in_specs = ([pl.BlockSpec((rows, D_MODEL), lambda i: (i, 0)),
                 pl.BlockSpec((rows, LRU_WIDTH), lambda i: (i, XB // LRU_WIDTH)),
                 pl.BlockSpec((rows, LRU_WIDTH), lambda i: (i, GB // LRU_WIDTH)),
                 pl.BlockSpec((rows, 2 * GROUP_WIDTH), lambda i: (i, 0)),
                 pl.BlockSpec((rows, LRU_WIDTH), lambda i: (i, 0)),
                 pl.BlockSpec((rows, LRU_WIDTH), lambda i: (i, 0))]
                + [_const_spec(a.shape) for a in consts])
    out_shape = (
        jax.ShapeDtypeStruct((R, D_MODEL), F32),
        jax.ShapeDtypeStruct((R, LRU_WIDTH), F32),
        jax.ShapeDtypeStruct((R, LRU_WIDTH), F32),
    )
    out_specs = (
        pl.BlockSpec((rows, D_MODEL), lambda i: (i, 0)),
        pl.BlockSpec((rows, LRU_WIDTH), lambda i: (i, 0)),
        pl.BlockSpec((rows, LRU_WIDTH), lambda i: (i, 0)),
    )
    return pl.pallas_call(
        _sample_out_kernel,
        grid=(R // rows,),
        in_specs=in_specs,
        out_specs=out_specs,
        out_shape=out_shape,
        compiler_params=pltpu.CompilerParams(
            dimension_semantics=("arbitrary",), vmem_limit_bytes=VMEM_LIMIT),
        name="sample_out",
    )(xp, proj, proj, mix, conv8, h8, *consts)


def _prompt_tables():
    C = RET_CHUNK
    idx = np.arange(C, dtype=np.float32)
    diff = idx[:, None] - idx[None, :]
    causal = diff >= 0
    lg = _LOG_G[:, None, None]
    dmask = np.where(causal[None], np.exp(np.where(causal, diff, 0.0)[None] * lg), 0.0).astype(np.float32)
    qdec = np.exp((idx + 1.0)[None, :] * _LOG_G[:, None]).astype(np.float32)
    kdec = np.exp((C - 1 - idx)[None, :] * _LOG_G[:, None]).astype(np.float32)
    qdec = np.broadcast_to(qdec[:, :, None], (RET_HEADS, C, RET_DK)).copy()
    kdec = np.broadcast_to(kdec[:, :, None], (RET_HEADS, C, RET_DK)).copy()
    return jnp.asarray(dmask), jnp.asarray(qdec), jnp.asarray(kdec)


def _sample_tables():
    P = SEQ_PAD
    rows = np.arange(P, dtype=np.float32)
    i = rows - TOK0
    tok = (i >= 0) & (i < 4)
    cm = np.zeros((RET_HEADS, 4, P, RET_DV), np.float32)
    qdec = np.zeros((RET_HEADS, P, RET_DK), np.float32)
    kdec = np.zeros((RET_HEADS, P, RET_DK), np.float32)
    for h in range(RET_HEADS):
        for s in range(4):
            d = i - s
            col = np.where(tok & (d >= 0), np.exp(np.where(d >= 0, d, 0.0) * _LOG_G[h]), 0.0)
            cm[h, s] = col[:, None]
        qdec[h] = np.where(tok, np.exp((i + 1.0) * _LOG_G[h]), 0.0)[:, None]
        kdec[h] = np.where(tok, np.exp((3.0 - i) * _LOG_G[h]), 0.0)[:, None]
    return jnp.asarray(cm), jnp.asarray(qdec), jnp.asarray(kdec)


def _block_diag(w):
    n, d, _ = w.shape
    g = n // 2
    w = w.reshape(2, g, d, d)
    eye = jnp.eye(g, dtype=w.dtype)
    return (eye[None, :, None, :, None] * w[:, :, :, None, :]).reshape(2, g * d, g * d)


def kernel(x_prompt, x_sample, state_ret, state_lru, state_conv, cache_swa_k, cache_swa_v, norm_g, w_in, w_out, ret_norm_g, conv_w, conv_b, w_rgate, b_rgate, w_igate, b_igate, lru_lambda, q_norm_g, k_norm_g, attn_sinks):
    Bs, Ts, _ = x_sample.shape
    ptab = _prompt_tables()
    stab = _sample_tables()
    yp = x_prompt
    ys = jnp.pad(x_sample, ((0, 0), (TOK0, SEQ_PAD - TOK0 - Ts), (0, 0))).reshape(Bs * SEQ_PAD, D_MODEL)
    cache_k = cache_swa_k.reshape(DEPTH, Bs, 2 * WINDOW, SWA_HEAD_DIM)
    cache_v = cache_swa_v.reshape(DEPTH, Bs, 2 * WINDOW, SWA_HEAD_DIM)
    p_state, s_state = (), ()
    s_h, s_conv = [], []
    for l in range(DEPTH):
        ng = norm_g[l].reshape(1, -1)
        win = w_in[l].astype(BF16)
        wout = w_out[l].astype(BF16)
        retg = ret_norm_g[l].reshape(1, -1)
        convb = conv_b[l].reshape(1, -1)
        wr = _block_diag(w_rgate[l]).astype(BF16)
        wi = _block_diag(w_igate[l]).astype(BF16)
        br = b_rgate[l].reshape(1, -1)
        bi = b_igate[l].reshape(1, -1)
        lam = lru_lambda[l].reshape(1, -1)
        qg = q_norm_g[l].reshape(1, -1)
        kg = k_norm_g[l].reshape(1, -1)
        sinks = attn_sinks[l]

        wts = (sinks, ng, win, wout, retg, conv_w[l], convb, wr, wi, br, bi, lam, qg, kg)
        yp, *p_state = _prompt_layer(l, yp, wts, ptab, tuple(p_state))

        proj = _sample_proj(ys, ng, win)
        mix, *s_state = _sample_mix(l, sinks, proj, state_ret, cache_k, cache_v, retg, qg, kg, stab,
                                    tuple(s_state))
        conv8 = jnp.pad(state_conv[l], ((0, 0), (0, SEQ_PAD - (CONV_W - 1)), (0, 0))).reshape(Bs * SEQ_PAD, LRU_WIDTH)
        h8 = jnp.pad(state_lru[l][:, None, :], ((0, 0), (TOK0 - 1, SEQ_PAD - TOK0), (0, 0))).reshape(Bs * SEQ_PAD, LRU_WIDTH)
        ys, convo, ho = _sample_out(ys, proj, mix, conv8, h8, conv_w[l], convb, wr, wi, br, bi, lam, wout)
        s_h.append(ho.reshape(Bs, SEQ_PAD, LRU_WIDTH)[:, TOK0 + Ts - 1])
        s_conv.append(convo.reshape(Bs, SEQ_PAD, LRU_WIDTH)[:, TOK0 + Ts - (CONV_W - 1):TOK0 + Ts])

    p_ret, p_h, p_conv, p_k, p_v = p_state
    s_ret, s_k, s_v = s_state
    Bp = x_prompt.shape[0]
    kv5 = lambda a, n: a.reshape(DEPTH, n, WINDOW, SWA_KV_HEADS, SWA_HEAD_DIM)
    y_sample = ys.reshape(Bs, SEQ_PAD, D_MODEL)[:, TOK0:TOK0 + Ts]
    return (yp, y_sample,
            p_ret, p_h.reshape(DEPTH, Bp, LRU_WIDTH), p_conv, kv5(p_k, Bp), kv5(p_v, Bp),
            s_ret, jnp.stack(s_h), jnp.stack(s_conv), kv5(s_k, Bs), kv5(s_v, Bs))
```

```python
import functools
import math

import numpy as np
import jax
import jax.numpy as jnp
from jax import lax
from jax.experimental import pallas as pl
from jax.experimental.pallas import tpu as pltpu

D_MODEL = 1024
DEPTH = 2
PAST_LEN = 16384
GROUP_WIDTH = 512
RET_HEADS = 4
RET_DK = 128
RET_DV = 128
RET_CHUNK = 128
LRU_WIDTH = 512
LRU_BLOCKS = 8
LRU_C = 8.0
CONV_W = 4
SWA_HEADS = 4
SWA_KV_HEADS = 2
SWA_GROUP = 2
SWA_HEAD_DIM = 128
WINDOW = 128
NORM_EPS = 1e-6
NEG_INF = -1e30

IN_WIDTH = 4608
MIX_WIDTH = 1536
QA, KA, VA, GA, XB, GB, QC, KC, VC, GC = 0, 512, 1024, 1536, 2048, 2560, 3072, 3584, 3840, 4096

F32 = jnp.float32
BF16 = jnp.bfloat16

SUBLANES = 8
SEQ_PAD = 8
TOK0 = 3
VMEM_LIMIT = 56 * 1024 * 1024

_LOG_G = np.log1p(-np.power(np.float32(2.0), (-5.0 - np.arange(RET_HEADS)).astype(np.float32))).astype(np.float32)
_SLOPES = [2.0 ** (-8.0 * (h + 1) / SWA_HEADS) for h in range(SWA_HEADS)]


def _rms(x, g):
    ms = jnp.mean(x * x, axis=-1, keepdims=True)
    return x * lax.rsqrt(ms + NORM_EPS) * g


def _silu(x):
    return x * jax.nn.sigmoid(x)


def _softplus(x):
    return jnp.maximum(x, 0.0) + jnp.log1p(jnp.exp(-jnp.abs(x)))


def _dot(a, b):
    return jnp.dot(a, b, preferred_element_type=F32)


def _dot_nt(a, b):
    return lax.dot_general(a, b, (((1,), (1,)), ((), ())), preferred_element_type=F32)


def _dot_tn(a, b):
    return lax.dot_general(a, b, (((0,), (0,)), ((), ())), preferred_element_type=F32)


def _lru_gates(conv, wr_ref, wi_ref, br_ref, bi_ref, lam_ref):
    gin = conv.astype(BF16)
    half = LRU_WIDTH // 2
    lo, hi = gin[:, :half], gin[:, half:]
    r = jax.nn.sigmoid(jnp.concatenate([_dot(lo, wr_ref[0]), _dot(hi, wr_ref[1])], axis=1) + br_ref[...])
    i = jax.nn.sigmoid(jnp.concatenate([_dot(lo, wi_ref[0]), _dot(hi, wi_ref[1])], axis=1) + bi_ref[...])
    log_a = -LRU_C * r * _softplus(-lam_ref[...])
    a = jnp.exp(log_a)
    m2 = 1.0 - a * a
    mult = jnp.where(m2 > 0.0, m2 * lax.rsqrt(m2), 0.0)
    return a, mult, i


_DONE = object()


def _run(gen):
    for _ in gen:
        pass


def _zip_stages(gens):
    gens = list(gens)
    while gens:
        gens = [g for g in gens if next(g, _DONE) is not _DONE]
        if gens:
            yield


def _chain(gens):
    for g in gens:
        yield from g


def _interleave(main, sides, n_main, n_side):
    side = _chain(sides)
    next(side)
    done_side = 1
    for i, _ in enumerate(main):
        want = ((i + 1) * n_side) // n_main
        while done_side < want and next(side, _DONE) is not _DONE:
            done_side += 1
    _run(side)


IN_PROJ_STAGES = 1 + IN_WIDTH // 512


def _in_proj(x_ref, ng_ref, win_ref, proj_ref):
    hb = _rms(x_ref[...], ng_ref[...]).astype(BF16)
    yield
    for j in range(IN_WIDTH // 512):
        proj_ref[:, j * 512:(j + 1) * 512] = _dot(hb, win_ref[:, j * 512:(j + 1) * 512])
        yield


OUT_PROJ_STAGES = D_MODEL // 256


def _out_proj(mixed_s, x_ref, y_ref, wout_ref):
    mixed = mixed_s[...]
    for j in range(OUT_PROJ_STAGES):
        cols = slice(j * 256, (j + 1) * 256)
        y_ref[:, cols] = x_ref[:, cols] + _dot(mixed, wout_ref[:, cols])
        yield


def _mix_stages(tt):
    return 4 + tt // SUBLANES // 8 + (tt // RET_CHUNK) * 5


def _scan_rows(a, u, h_in, tt):
    G = tt // SUBLANES
    W = a.shape[-1]
    a3 = a.reshape(G, SUBLANES, W)
    u3 = u.reshape(G, SUBLANES, W)
    r3 = lax.broadcasted_iota(jnp.int32, (G, SUBLANES, W), 1)
    sh = 1
    while sh < SUBLANES:
        keep = r3 >= sh
        a_sh = jnp.where(keep, pltpu.roll(a3, sh, axis=1), 1.0)
        u_sh = jnp.where(keep, pltpu.roll(u3, sh, axis=1), 0.0)
        u3 = a3 * u_sh + u3
        a3 = a3 * a_sh
        sh *= 2
    yield
    hs = []
    for g in range(G):
        hg = a3[g] * h_in + u3[g]
        hs.append(hg)
        h_in = hg[SUBLANES - 1:SUBLANES, :]
        if g % 8 == 7:
            yield
    return jnp.concatenate(hs, axis=0)


def _retention_head(h, proj_s, r0, s_ref, retg_ref, dmask_ref, qdec_ref, kdec_ref, mixed_s):
    C = RET_CHUNK
    cs = slice(h * RET_DK, (h + 1) * RET_DK)
    q = proj_s[r0:r0 + C, QA + h * 128:QA + (h + 1) * 128]
    k = proj_s[r0:r0 + C, KA + h * 128:KA + (h + 1) * 128] * (RET_DK ** -0.5)
    vb = proj_s[r0:r0 + C, VA + h * 128:VA + (h + 1) * 128].astype(BF16)
    sc = _dot_nt(q.astype(BF16), k.astype(BF16)) * dmask_ref[h]
    yield
    s_prev = s_ref[0, h]
    lhs = jnp.concatenate([sc.astype(BF16), (q * qdec_ref[h]).astype(BF16)], axis=1)
    rhs = jnp.concatenate([vb, s_prev.astype(BF16)], axis=0)
    o = _dot(lhs, rhs)
    yield
    kv = _dot_tn((k * kdec_ref[h]).astype(BF16), vb)
    s_ref[0, h] = float(np.exp(np.float32(C) * _LOG_G[h])) * s_prev + kv
    yield
    o = _rms(o, retg_ref[:, cs])
    mixed_s[r0:r0 + C, cs] = (o * _silu(proj_s[r0:r0 + C, GA + h * 128:GA + (h + 1) * 128])).astype(BF16)
    yield


def _swa_kv_head(kh, proj_s, r0, blk, sinks_ref, qg_ref, kg_ref, mixed_s, kprev_s, vprev_s):
    C = RET_CHUNK
    ks = slice(kh * 128, (kh + 1) * 128)
    h0, h1 = kh * SWA_GROUP, kh * SWA_GROUP + 1
    kn = _rms(proj_s[r0:r0 + C, KC + kh * 128:KC + (kh + 1) * 128], kg_ref[...])
    vv = proj_s[r0:r0 + C, VC + kh * 128:VC + (kh + 1) * 128]
    kband = jnp.concatenate([kprev_s[:, ks], kn], axis=0).astype(BF16)
    vband = jnp.concatenate([vprev_s[:, ks], vv], axis=0).astype(BF16)
    kprev_s[:, ks] = kn
    vprev_s[:, ks] = vv
    q0 = _rms(proj_s[r0:r0 + C, QC + h0 * 128:QC + (h0 + 1) * 128], qg_ref[...])
    q1 = _rms(proj_s[r0:r0 + C, QC + h1 * 128:QC + (h1 + 1) * 128], qg_ref[...])
    qq = jnp.concatenate([q0, q1], axis=0).astype(BF16)
    yield
    s = _dot_nt(qq, kband) * (SWA_HEAD_DIM ** -0.5)
    ii = lax.broadcasted_iota(jnp.int32, (2 * C, 2 * C), 0)
    jj = lax.broadcasted_iota(jnp.int32, (2 * C, 2 * C), 1)
    dist = jnp.where(ii >= C, ii - C, ii) + C - jj
    valid = jnp.logical_and(jnp.logical_and(dist >= 0, dist < WINDOW), jj >= jnp.where(blk > 0, 0, C))
    slope = jnp.where(ii >= C, _SLOPES[h1], _SLOPES[h0])
    s = jnp.where(valid, s - slope * dist.astype(F32), NEG_INF)
    yield
    rowc = lax.broadcasted_iota(jnp.int32, (2 * C, 1), 0)
    sink = jnp.where(rowc >= C, sinks_ref[h1], sinks_ref[h0])
    m = jnp.maximum(jnp.max(s, axis=-1, keepdims=True), sink)
    p = jnp.exp(s - m)
    denom = jnp.sum(p, axis=-1, keepdims=True) + jnp.exp(sink - m)
    yield
    o = _dot((p / denom).astype(BF16), vband)
    yield
    for g, hh in enumerate((h0, h1)):
        gc = proj_s[r0:r0 + C, GC + hh * 128:GC + (hh + 1) * 128]
        mixed_s[r0:r0 + C, 2 * GROUP_WIDTH + hh * 128:2 * GROUP_WIDTH + (hh + 1) * 128] = (
            (o[g * C:(g + 1) * C] * _silu(gc)).astype(BF16))
    yield


def _mix_tile(proj_s, t, tt, sinks_ref, retg_ref, convw_ref, convb_ref,
              wr_ref, wi_ref, br_ref, bi_ref, lam_ref, qg_ref, kg_ref, dmask_ref, qdec_ref, kdec_ref,
              s_ref, mixed_s, xtail_s, kprev_s, vprev_s, hc_s):
    nch = tt // RET_CHUNK
    C = RET_CHUNK

    xb = proj_s[:, XB:XB + LRU_WIDTH]
    xfull = jnp.concatenate([xtail_s[...], xb], axis=0)
    xtail_s[...] = xb[tt - SUBLANES:tt, :]
    conv = convb_ref[...] + convw_ref[0:1, :] * pltpu.roll(xfull, 3, axis=0)[SUBLANES:]
    conv = conv + convw_ref[1:2, :] * pltpu.roll(xfull, 2, axis=0)[SUBLANES:]
    conv = conv + convw_ref[2:3, :] * pltpu.roll(xfull, 1, axis=0)[SUBLANES:]
    conv = conv + convw_ref[3:4, :] * xb
    yield

    a, mult, ig = _lru_gates(conv, wr_ref, wi_ref, br_ref, bi_ref, lam_ref)
    yield
    row = lax.broadcasted_iota(jnp.int32, (tt, LRU_WIDTH), 0)
    mult = jnp.where(row == jnp.where(t == 0, 0, -1), 1.0, mult)
    u = mult * ig * conv
    hseq = yield from _scan_rows(a, u, hc_s[...], tt)
    hc_s[...] = hseq[tt - 1:tt, :]
    mixed_s[:, GROUP_WIDTH:2 * GROUP_WIDTH] = (hseq * _silu(proj_s[:, GB:GB + LRU_WIDTH])).astype(BF16)
    yield

    for c in range(nch):
        r0 = c * C
        blk = t * nch + c
        yield from _zip_stages(
            [_retention_head(h, proj_s, r0, s_ref, retg_ref, dmask_ref, qdec_ref, kdec_ref, mixed_s)
             for h in range(RET_HEADS)]
            + [_swa_kv_head(kh, proj_s, r0, blk, sinks_ref, qg_ref, kg_ref, mixed_s, kprev_s, vprev_s)
               for kh in range(SWA_KV_HEADS)])


def _prompt_kernel(sinks_ref, xa_ref, xn_ref, ng_ref, win_ref, wout_ref, retg_ref, convw_ref, convb_ref,
                   wr_ref, wi_ref, br_ref, bi_ref, lam_ref, qg_ref, kg_ref,
                   dmask_ref, qdec_ref, kdec_ref, *rest, tt, npairs, layer):
    y_ref, s_ref, h_ref, conv_ref, kout_ref, vout_ref = rest[-14:-8]
    proj_a, proj_b, mixed_a, mixed_b, xtail_s, kprev_s, vprev_s, hc_s = rest[-8:]
    sinks_ref = sinks_ref.at[layer]
    b = pl.program_id(0)
    p = pl.program_id(1)

    @pl.when(p == 0)
    def _init():
        s_ref[...] = jnp.zeros_like(s_ref)
        xtail_s[...] = jnp.zeros_like(xtail_s)
        kprev_s[...] = jnp.zeros_like(kprev_s)
        vprev_s[...] = jnp.zeros_like(vprev_s)
        hc_s[...] = jnp.zeros_like(hc_s)

    xa0, xa1 = xa_ref.at[0, 0:tt, :], xa_ref.at[0, tt:2 * tt, :]
    ya0, ya1 = y_ref.at[0, 0:tt, :], y_ref.at[0, tt:2 * tt, :]

    @pl.when(jnp.logical_and(b == 0, p == 0))
    def _prologue():
        _run(_in_proj(xa0, ng_ref, win_ref, proj_a))

    common = (sinks_ref, retg_ref, convw_ref, convb_ref, wr_ref, wi_ref, br_ref, bi_ref,
              lam_ref, qg_ref, kg_ref, dmask_ref, qdec_ref, kdec_ref, s_ref.at[0])
    state = (xtail_s, kprev_s, vprev_s, hc_s)

    n_main = _mix_stages(tt)
    _interleave(_mix_tile(proj_a, 2 * p, tt, *common, mixed_a, *state),
                [_in_proj(xa1, ng_ref, win_ref, proj_b)], n_main, IN_PROJ_STAGES)
    _interleave(_mix_tile(proj_b, 2 * p + 1, tt, *common, mixed_b, *state),
                [_out_proj(mixed_a, xa0, ya0, wout_ref), _in_proj(xn_ref.at[0], ng_ref, win_ref, proj_a)],
                n_main, OUT_PROJ_STAGES + IN_PROJ_STAGES)
    _run(_out_proj(mixed_b, xa1, ya1, wout_ref))

    @pl.when(p == npairs - 1)
    def _state_out():
        h_ref[0, 0] = hc_s[...]
        conv_ref[0, 0] = xtail_s[SUBLANES - (CONV_W - 1):SUBLANES, :]
        for kh in range(SWA_KV_HEADS):
            ks = slice(kh * 128, (kh + 1) * 128)
            kout_ref[0, 0, pl.ds(kh, WINDOW, stride=SWA_KV_HEADS), :] = kprev_s[:, ks]
            vout_ref[0, 0, pl.ds(kh, WINDOW, stride=SWA_KV_HEADS), :] = vprev_s[:, ks]


def _const_spec(shape):
    nd = len(shape)
    return pl.BlockSpec(shape, lambda *_: (0,) * nd)


def _resident_spec(shape):
    nd = len(shape)
    return pl.BlockSpec(shape, lambda *_: (0,) * nd, pipeline_mode=pl.Buffered(1))


def _layer_spec(a, layer, resident=False):
    nd = a.ndim
    kw = dict(pipeline_mode=pl.Buffered(1)) if resident else {}
    return pl.BlockSpec((None,) + a.shape[1:], lambda *_: (layer,) + (0,) * (nd - 1), **kw)


def _prompt_layer(layer, x, prm, tables, prev_out, tt=256):
    B, T, D = x.shape
    npairs = T // (2 * tt)
    layer_in = [prm[k] for k in ("ng", "win", "wout", "retg", "convw", "convb", "wr", "wi", "br", "bi",
                                 "lam", "qg", "kg")]
    vec_in = layer_in + list(tables)

    def next_tile(b, p):
        last = p == npairs - 1
        return (jnp.where(last, jnp.minimum(b + 1, B - 1), b), jnp.where(last, 0, 2 * p + 2), 0)

    in_specs = ([pl.BlockSpec(memory_space=pltpu.SMEM),
                 pl.BlockSpec((1, 2 * tt, D), lambda b, p: (b, p, 0)),
                 pl.BlockSpec((1, tt, D), next_tile)]
                + [_layer_spec(a, layer, resident=True) for a in layer_in]
                + [_resident_spec(a.shape) for a in tables]
                + [pl.BlockSpec(memory_space=pl.ANY) for _ in prev_out])
    out_shape = (
        jax.ShapeDtypeStruct((B, T, D), F32),
        jax.ShapeDtypeStruct((DEPTH, B, RET_HEADS, RET_DK, RET_DV), F32),
        jax.ShapeDtypeStruct((DEPTH, B, 1, LRU_WIDTH), F32),
        jax.ShapeDtypeStruct((DEPTH, B, CONV_W - 1, LRU_WIDTH), F32),
        jax.ShapeDtypeStruct((DEPTH, B, 2 * WINDOW, SWA_HEAD_DIM), F32),
        jax.ShapeDtypeStruct((DEPTH, B, 2 * WINDOW, SWA_HEAD_DIM), F32),
    )
    out_specs = (
        pl.BlockSpec((1, 2 * tt, D), lambda b, p: (b, p, 0)),
        pl.BlockSpec((1, 1, RET_HEADS, RET_DK, RET_DV), lambda b, p: (layer, b, 0, 0, 0)),
        pl.BlockSpec((1, 1, 1, LRU_WIDTH), lambda b, p: (layer, b, 0, 0)),
        pl.BlockSpec((1, 1, CONV_W - 1, LRU_WIDTH), lambda b, p: (layer, b, 0, 0)),
        pl.BlockSpec((1, 1, 2 * WINDOW, SWA_HEAD_DIM), lambda b, p: (layer, b, 0, 0)),
        pl.BlockSpec((1, 1, 2 * WINDOW, SWA_HEAD_DIM), lambda b, p: (layer, b, 0, 0)),
    )
    n_in = 3 + len(vec_in)
    aliases = {n_in + j: 1 + j for j in range(len(prev_out))}
    scratch = [
        pltpu.VMEM((tt, IN_WIDTH), F32),
        pltpu.VMEM((tt, IN_WIDTH), F32),
        pltpu.VMEM((tt, MIX_WIDTH), BF16),
        pltpu.VMEM((tt, MIX_WIDTH), BF16),
        pltpu.VMEM((SUBLANES, LRU_WIDTH), F32),
        pltpu.VMEM((WINDOW, 256), F32),
        pltpu.VMEM((WINDOW, 256), F32),
        pltpu.VMEM((1, LRU_WIDTH), F32),
    ]
    return pl.pallas_call(
        functools.partial(_prompt_kernel, tt=tt, npairs=npairs, layer=layer),
        grid=(B, npairs),
        in_specs=in_specs,
        out_specs=out_specs,
        out_shape=out_shape,
        scratch_shapes=scratch,
        input_output_aliases=aliases,
        compiler_params=pltpu.CompilerParams(
            dimension_semantics=("arbitrary", "arbitrary"), vmem_limit_bytes=VMEM_LIMIT),
        name="prompt_layer",
    )(prm["sinks"], x, x, *vec_in, *prev_out)


def _sproj_kernel(x_ref, ng_ref, win_ref, o_ref, hb_s):
    @pl.when(pl.program_id(0) == 0)
    def _norm_once():
        hb_s[...] = _rms(x_ref[...], ng_ref[...]).astype(BF16)

    o_ref[...] = _dot(hb_s[...], win_ref[...])


def _sample_proj(layer, xp, prm):
    R = xp.shape[0]
    nb = IN_WIDTH // 512
    ng, win = prm["ng"], prm["win"]
    return pl.pallas_call(
        _sproj_kernel,
        grid=(nb,),
        in_specs=[_const_spec(xp.shape), _layer_spec(ng, layer),
                  pl.BlockSpec((None, D_MODEL, 512), lambda j: (layer, 0, j))],
        out_specs=pl.BlockSpec((R, 512), lambda j: (0, j)),
        out_shape=jax.ShapeDtypeStruct((R, IN_WIDTH), F32),
        scratch_shapes=[pltpu.VMEM((R, D_MODEL), BF16)],
        compiler_params=pltpu.CompilerParams(
            dimension_semantics=("arbitrary",), vmem_limit_bytes=VMEM_LIMIT),
        name="sample_proj",
    )(xp, ng, win)


def _sample_mix_seq(sinks_ref, proj_ref, s0_ref, kbuf_ref, vbuf_ref, retg_ref, qg_ref, kg_ref,
                    cm_ref, qdec_ref, kdec_ref, mix_ref, snew_ref, knew_ref, vnew_ref):
    P = SEQ_PAD
    for h in range(RET_HEADS):
        cs = slice(h * 128, (h + 1) * 128)
        q = proj_ref[:, QA + h * 128:QA + (h + 1) * 128]
        k = proj_ref[:, KA + h * 128:KA + (h + 1) * 128] * (RET_DK ** -0.5)
        v = proj_ref[:, VA + h * 128:VA + (h + 1) * 128]
        intra = jnp.zeros((P, RET_DV), F32)
        for s in range(4):
            r_ = TOK0 + s
            w = jnp.sum(q * k[r_:r_ + 1, :], axis=-1, keepdims=True)
            intra = intra + (w * cm_ref[h, s]) * v[r_:r_ + 1, :]
        s_prev = s0_ref[h]
        cross = _dot((q * qdec_ref[h]).astype(BF16), s_prev.astype(BF16))
        kv = _dot_tn((k * kdec_ref[h]).astype(BF16), v.astype(BF16))
        yield
        snew_ref[h] = float(np.exp(np.float32(4.0) * _LOG_G[h])) * s_prev + kv
        o = _rms(intra + cross, retg_ref[:, cs])
        mix_ref[:, cs] = o * _silu(proj_ref[:, GA + h * 128:GA + (h + 1) * 128])
        yield

    r16 = lax.broadcasted_iota(jnp.int32, (2 * P, WINDOW), 0)
    j16 = lax.broadcasted_iota(jnp.int32, (2 * P, WINDOW), 1)
    rr = jnp.where(r16 >= P, r16 - P, r16)
    row_ok = jnp.logical_and(rr >= TOK0, rr < TOK0 + 4)
    dist = (rr - TOK0) + WINDOW - j16
    distf = dist.astype(F32)
    valid = jnp.logical_and(jnp.logical_and(dist >= 0, dist < WINDOW), row_ok)
    r16c = lax.broadcasted_iota(jnp.int32, (2 * P, 1), 0)
    rrc = jnp.where(r16c >= P, r16c - P, r16c)
    rowc_ok = jnp.logical_and(rrc >= TOK0, rrc < TOK0 + 4)
    knew_ref[0:2 * (WINDOW - 4), :] = kbuf_ref[2 * 4:2 * WINDOW, :]
    vnew_ref[0:2 * (WINDOW - 4), :] = vbuf_ref[2 * 4:2 * WINDOW, :]
    for kh in range(SWA_KV_HEADS):
        h0, h1 = kh * SWA_GROUP, kh * SWA_GROUP + 1
        kb = kbuf_ref[pl.ds(kh, WINDOW, stride=SWA_KV_HEADS), :]
        vb = vbuf_ref[pl.ds(kh, WINDOW, stride=SWA_KV_HEADS), :]
        kn = _rms(proj_ref[:, KC + kh * 128:KC + (kh + 1) * 128], kg_ref[...])
        vn = proj_ref[:, VC + kh * 128:VC + (kh + 1) * 128]
        q0 = _rms(proj_ref[:, QC + h0 * 128:QC + (h0 + 1) * 128], qg_ref[...])
        q1 = _rms(proj_ref[:, QC + h1 * 128:QC + (h1 + 1) * 128], qg_ref[...])
        qq = jnp.concatenate([q0, q1], axis=0)
        slope = jnp.where(r16 >= P, _SLOPES[h1], _SLOPES[h0])
        slopec = jnp.where(r16c >= P, _SLOPES[h1], _SLOPES[h0])
        sb = _dot_nt(qq.astype(BF16), kb.astype(BF16)) * (SWA_HEAD_DIM ** -0.5)
        sb = jnp.where(valid, sb - slope * distf, NEG_INF)
        yield
        sink = jnp.where(r16c >= P, sinks_ref[h1], sinks_ref[h0])
        m = jnp.maximum(jnp.max(sb, axis=-1, keepdims=True), sink)
        wn = []
        for s in range(4):
            r_ = TOK0 + s
            w = jnp.sum(qq * kn[r_:r_ + 1, :], axis=-1, keepdims=True) * (SWA_HEAD_DIM ** -0.5)
            dn = rrc - r_
            w = jnp.where(jnp.logical_and(dn >= 0, rowc_ok), w - slopec * dn.astype(F32), NEG_INF)
            wn.append(w)
            m = jnp.maximum(m, w)
        pb = jnp.exp(sb - m)
        denom = jnp.sum(pb, axis=-1, keepdims=True) + jnp.exp(sink - m)
        pn = [jnp.exp(w - m) for w in wn]
        for p_ in pn:
            denom = denom + p_
        yield
        o = _dot((pb / denom).astype(BF16), vb.astype(BF16))
        for s in range(4):
            r_ = TOK0 + s
            o = o + (pn[s] / denom) * vn[r_:r_ + 1, :]
        for g, hh in enumerate((h0, h1)):
            gc = proj_ref[:, GC + hh * 128:GC + (hh + 1) * 128]
            mix_ref[:, GROUP_WIDTH + hh * 128:GROUP_WIDTH + (hh + 1) * 128] = o[g * P:(g + 1) * P] * _silu(gc)
        for s in range(4):
            r_out = 2 * (WINDOW - 4 + s) + kh
            knew_ref[r_out:r_out + 1, :] = kn[TOK0 + s:TOK0 + s + 1, :]
            vnew_ref[r_out:r_out + 1, :] = vn[TOK0 + s:TOK0 + s + 1, :]
        yield


def _sample_mix_kernel(sinks_ref, proj_ref, s0_ref, kbuf_ref, vbuf_ref, retg_ref, qg_ref, kg_ref,
                       cm_ref, qdec_ref, kdec_ref, *rest, nseq, layer):
    mix_ref, snew_ref, knew_ref, vnew_ref = rest[-4:]
    sinks_ref = sinks_ref.at[layer]
    gens = []
    for i in range(nseq):
        rows = slice(i * SEQ_PAD, (i + 1) * SEQ_PAD)
        gens.append(_sample_mix_seq(
            sinks_ref, proj_ref.at[rows, :], s0_ref.at[0, i], kbuf_ref.at[0, i], vbuf_ref.at[0, i],
            retg_ref, qg_ref, kg_ref, cm_ref, qdec_ref, kdec_ref,
            mix_ref.at[rows, :], snew_ref.at[0, i], knew_ref.at[0, i], vnew_ref.at[0, i]))
    _run(_zip_stages(gens))


def _sample_mix(layer, proj, state_ret, cache_k, cache_v, prm, tables, prev_out, nseq=8):
    depth, B = state_ret.shape[:2]
    layer_in = [prm[k] for k in ("retg", "qg", "kg")]
    consts = layer_in + list(tables)
    st_spec = pl.BlockSpec((1, nseq, RET_HEADS, RET_DK, RET_DV), lambda i: (layer, i, 0, 0, 0))
    kv_spec = pl.BlockSpec((1, nseq, 2 * WINDOW, SWA_HEAD_DIM), lambda i: (layer, i, 0, 0))
    in_specs = ([pl.BlockSpec(memory_space=pltpu.SMEM),
                 pl.BlockSpec((nseq * SEQ_PAD, IN_WIDTH), lambda i: (i, 0)),
                 st_spec, kv_spec, kv_spec]
                + [_layer_spec(a, layer) for a in layer_in]
                + [_const_spec(a.shape) for a in tables]
                + [pl.BlockSpec(memory_space=pl.ANY) for _ in prev_out])
    out_shape = (
        jax.ShapeDtypeStruct((B * SEQ_PAD, 2 * GROUP_WIDTH), F32),
        jax.ShapeDtypeStruct(state_ret.shape, F32),
        jax.ShapeDtypeStruct(cache_k.shape, F32),
        jax.ShapeDtypeStruct(cache_v.shape, F32),
    )
    out_specs = (pl.BlockSpec((nseq * SEQ_PAD, 2 * GROUP_WIDTH), lambda i: (i, 0)), st_spec, kv_spec, kv_spec)
    n_in = 5 + len(consts)
    aliases = {n_in + j: 1 + j for j in range(len(prev_out))}
    return pl.pallas_call(
        functools.partial(_sample_mix_kernel, nseq=nseq, layer=layer),
        grid=(B // nseq,),
        in_specs=in_specs,
        out_specs=out_specs,
        out_shape=out_shape,
        input_output_aliases=aliases,
        compiler_params=pltpu.CompilerParams(
            dimension_semantics=("arbitrary",), vmem_limit_bytes=VMEM_LIMIT),
        name="sample_mix",
    )(prm["sinks"], proj, state_ret, cache_k, cache_v, *consts, *prev_out)


def _sample_out_kernel(x_ref, xb_ref, gb_ref, mix_ref, conv8_ref, h8_ref, convw_ref, convb_ref,
                       wr_ref, wi_ref, br_ref, bi_ref, lam_ref, wout_ref,
                       y_ref, convo_ref, ho_ref):
    R = x_ref.shape[0]
    row = lax.broadcasted_iota(jnp.int32, (R, LRU_WIDTH), 0) & (SEQ_PAD - 1)
    xc = jnp.where(row < TOK0, conv8_ref[...], xb_ref[...])
    convo_ref[...] = xc
    conv = convb_ref[...] + convw_ref[0:1, :] * pltpu.roll(xc, 3, axis=0)
    conv = conv + convw_ref[1:2, :] * pltpu.roll(xc, 2, axis=0)
    conv = conv + convw_ref[2:3, :] * pltpu.roll(xc, 1, axis=0)
    conv = conv + convw_ref[3:4, :] * xc
    a, mult, ig = _lru_gates(conv, wr_ref, wi_ref, br_ref, bi_ref, lam_ref)
    u = mult * ig * conv
    h = h8_ref[...]
    for s in range(4):
        h = jnp.where(row == TOK0 + s, a * pltpu.roll(h, 1, axis=0) + u, h)
    ho_ref[...] = h
    ob = h * _silu(gb_ref[...])
    y = x_ref[...] + _dot(mix_ref[:, 0:GROUP_WIDTH].astype(BF16), wout_ref[0:GROUP_WIDTH, :])
    y = y + _dot(ob.astype(BF16), wout_ref[GROUP_WIDTH:2 * GROUP_WIDTH, :])
    y = y + _dot(mix_ref[:, GROUP_WIDTH:2 * GROUP_WIDTH].astype(BF16), wout_ref[2 * GROUP_WIDTH:, :])
    y_ref[...] = y


def _sample_out(layer, xp, proj, mix, conv8, h8, prm, rows=256):
    R = xp.shape[0]
    consts = [prm[k] for k in ("convw", "convb", "wr", "wi", "br", "bi", "lam", "wout")]
    in_specs = ([pl.BlockSpec((rows, D_MODEL), lambda i: (i, 0)),
                 pl.BlockSpec((rows, LRU_WIDTH), lambda i: (i, XB // LRU_WIDTH)),
                 pl.BlockSpec((rows, LRU_WIDTH), lambda i: (i, GB // LRU_WIDTH)),
                 pl.BlockSpec((rows, 2 * GROUP_WIDTH), lambda i: (i, 0)),
                 pl.BlockSpec((rows, LRU_WIDTH), lambda i: (i, 0)),
                 pl.BlockSpec((rows, LRU_WIDTH), lambda i: (i, 0))]
                + [_layer_spec(a, layer) for a in consts])
    out_shape = (
        jax.ShapeDtypeStruct((R, D_MODEL), F32),
        jax.ShapeDtypeStruct((R, LRU_WIDTH), F32),
        jax.ShapeDtypeStruct((R, LRU_WIDTH), F32),
    )
    out_specs = (
        pl.BlockSpec((rows, D_MODEL), lambda i: (i, 0)),
        pl.BlockSpec((rows, LRU_WIDTH), lambda i: (i, 0)),
        pl.BlockSpec((rows, LRU_WIDTH), lambda i: (i, 0)),
    )
    return pl.pallas_call(
        _sample_out_kernel,
        grid=(R // rows,),
        in_specs=in_specs,
        out_specs=out_specs,
        out_shape=out_shape,
        compiler_params=pltpu.CompilerParams(
            dimension_semantics=("arbitrary",), vmem_limit_bytes=VMEM_LIMIT),
        name="sample_out",
    )(xp, proj, proj, mix, conv8, h8, *consts)


def _prompt_tables():
    C = RET_CHUNK
    idx = np.arange(C, dtype=np.float32)
    diff = idx[:, None] - idx[None, :]
    causal = diff >= 0
    lg = _LOG_G[:, None, None]
    dmask = np.where(causal[None], np.exp(np.where(causal, diff, 0.0)[None] * lg), 0.0).astype(np.float32)
    qdec = np.exp((idx + 1.0)[None, :] * _LOG_G[:, None]).astype(np.float32)
    kdec = np.exp((C - 1 - idx)[None, :] * _LOG_G[:, None]).astype(np.float32)
    qdec = np.broadcast_to(qdec[:, :, None], (RET_HEADS, C, RET_DK)).copy()
    kdec = np.broadcast_to(kdec[:, :, None], (RET_HEADS, C, RET_DK)).copy()
    return jnp.asarray(dmask), jnp.asarray(qdec), jnp.asarray(kdec)


def _sample_tables():
    P = SEQ_PAD
    rows = np.arange(P, dtype=np.float32)
    i = rows - TOK0
    tok = (i >= 0) & (i < 4)
    cm = np.zeros((RET_HEADS, 4, P, RET_DV), np.float32)
    qdec = np.zeros((RET_HEADS, P, RET_DK), np.float32)
    kdec = np.zeros((RET_HEADS, P, RET_DK), np.float32)
    for h in range(RET_HEADS):
        for s in range(4):
            d = i - s
            col = np.where(tok & (d >= 0), np.exp(np.where(d >= 0, d, 0.0) * _LOG_G[h]), 0.0)
            cm[h, s] = col[:, None]
        qdec[h] = np.where(tok, np.exp((i + 1.0) * _LOG_G[h]), 0.0)[:, None]
        kdec[h] = np.where(tok, np.exp((3.0 - i) * _LOG_G[h]), 0.0)[:, None]
    return jnp.asarray(cm), jnp.asarray(qdec), jnp.asarray(kdec)


def _block_diag(w):
    L, n, d, _ = w.shape
    g = n // 2
    w = w.reshape(L, 2, g, d, d)
    eye = jnp.eye(g, dtype=w.dtype)
    return (eye[None, None, :, None, :, None] * w[:, :, :, :, None, :]).reshape(L, 2, g * d, g * d)


def kernel(x_prompt, x_sample, state_ret, state_lru, state_conv, cache_swa_k, cache_swa_v, norm_g, w_in, w_out, ret_norm_g, conv_w, conv_b, w_rgate, b_rgate, w_igate, b_igate, lru_lambda, q_norm_g, k_norm_g, attn_sinks):
    Bs, Ts, _ = x_sample.shape
    ptab = _prompt_tables()
    stab = _sample_tables()
    yp = x_prompt
    ys = jnp.pad(x_sample, ((0, 0), (TOK0, SEQ_PAD - TOK0 - Ts), (0, 0))).reshape(Bs * SEQ_PAD, D_MODEL)
    cache_k = cache_swa_k.reshape(DEPTH, Bs, 2 * WINDOW, SWA_HEAD_DIM)
    cache_v = cache_swa_v.reshape(DEPTH, Bs, 2 * WINDOW, SWA_HEAD_DIM)
    p_state, s_state = (), ()
    s_h, s_conv = [], []
    row = lambda a: a.reshape(DEPTH, 1, -1)
    prm = dict(
        sinks=attn_sinks, ng=row(norm_g), win=w_in.astype(BF16), wout=w_out.astype(BF16), retg=row(ret_norm_g),
        convw=conv_w, convb=row(conv_b), wr=_block_diag(w_rgate).astype(BF16), wi=_block_diag(w_igate).astype(BF16),
        br=row(b_rgate), bi=row(b_igate), lam=row(lru_lambda), qg=row(q_norm_g), kg=row(k_norm_g))
    for l in range(DEPTH):
        yp, *p_state = _prompt_layer(l, yp, prm, ptab, tuple(p_state))

        proj = _sample_proj(l, ys, prm)
        mix, *s_state = _sample_mix(l, proj, state_ret, cache_k, cache_v, prm, stab, tuple(s_state))
        conv8 = jnp.pad(state_conv[l], ((0, 0), (0, SEQ_PAD - (CONV_W - 1)), (0, 0))).reshape(Bs * SEQ_PAD, LRU_WIDTH)
        h8 = jnp.pad(state_lru[l][:, None, :], ((0, 0), (TOK0 - 1, SEQ_PAD - TOK0), (0, 0))).reshape(Bs * SEQ_PAD, LRU_WIDTH)
        ys, convo, ho = _sample_out(l, ys, proj, mix, conv8, h8, prm)
        s_h.append(ho.reshape(Bs, SEQ_PAD, LRU_WIDTH)[:, TOK0 + Ts - 1])
        s_conv.append(convo.reshape(Bs, SEQ_PAD, LRU_WIDTH)[:, TOK0 + Ts - (CONV_W - 1):TOK0 + Ts])

    p_ret, p_h, p_conv, p_k, p_v = p_state
    s_ret, s_k, s_v = s_state
    Bp = x_prompt.shape[0]
    kv5 = lambda a, n: a.reshape(DEPTH, n, WINDOW, SWA_KV_HEADS, SWA_HEAD_DIM)
    y_sample = ys.reshape(Bs, SEQ_PAD, D_MODEL)[:, TOK0:TOK0 + Ts]
    return (yp, y_sample,
            p_ret, p_h.reshape(DEPTH, Bp, LRU_WIDTH), p_conv, kv5(p_k, Bp), kv5(p_v, Bp),
            s_ret, jnp.stack(s_h), jnp.stack(s_conv), kv5(s_k, Bs), kv5(s_v, Bs))
```

```python
import functools
import math

import numpy as np
import jax
import jax.numpy as jnp
from jax import lax
from jax.experimental import pallas as pl
from jax.experimental.pallas import tpu as pltpu

D_MODEL = 1024
DEPTH = 2
PAST_LEN = 16384
GROUP_WIDTH = 512
RET_HEADS = 4
RET_DK = 128
RET_DV = 128
RET_CHUNK = 128
LRU_WIDTH = 512
LRU_BLOCKS = 8
LRU_C = 8.0
CONV_W = 4
SWA_HEADS = 4
SWA_KV_HEADS = 2
SWA_GROUP = 2
SWA_HEAD_DIM = 128
WINDOW = 128
NORM_EPS = 1e-6
NEG_INF = -1e30

IN_WIDTH = 4608
MIX_WIDTH = 1536
QA, KA, VA, GA, XB, GB, QC, KC, VC, GC = 0, 512, 1024, 1536, 2048, 2560, 3072, 3584, 3840, 4096

F32 = jnp.float32
BF16 = jnp.bfloat16

SUBLANES = 8
SEQ_PAD = 8
TOK0 = 3
VMEM_LIMIT = 56 * 1024 * 1024

_LOG_G = np.log1p(-np.power(np.float32(2.0), (-5.0 - np.arange(RET_HEADS)).astype(np.float32))).astype(np.float32)
_SLOPES = [2.0 ** (-8.0 * (h + 1) / SWA_HEADS) for h in range(SWA_HEADS)]


def _rms(x, g):
    ms = jnp.mean(x * x, axis=-1, keepdims=True)
    return x * lax.rsqrt(ms + NORM_EPS) * g


def _silu(x):
    return x * jax.nn.sigmoid(x)


def _softplus(x):
    return jnp.maximum(x, 0.0) + jnp.log1p(jnp.exp(-jnp.abs(x)))


def _dot(a, b):
    return jnp.dot(a, b, preferred_element_type=F32)


def _dot_nt(a, b):
    return lax.dot_general(a, b, (((1,), (1,)), ((), ())), preferred_element_type=F32)


def _dot_tn(a, b):
    return lax.dot_general(a, b, (((0,), (0,)), ((), ())), preferred_element_type=F32)


def _lru_gates(conv, wr_ref, wi_ref, br_ref, bi_ref, lam_ref):
    gin = conv.astype(BF16)
    half = LRU_WIDTH // 2
    lo, hi = gin[:, :half], gin[:, half:]
    r = jax.nn.sigmoid(jnp.concatenate([_dot(lo, wr_ref[0]), _dot(hi, wr_ref[1])], axis=1) + br_ref[...])
    i = jax.nn.sigmoid(jnp.concatenate([_dot(lo, wi_ref[0]), _dot(hi, wi_ref[1])], axis=1) + bi_ref[...])
    log_a = -LRU_C * r * _softplus(-lam_ref[...])
    a = jnp.exp(log_a)
    m2 = 1.0 - a * a
    mult = jnp.where(m2 > 0.0, m2 * lax.rsqrt(m2), 0.0)
    return a, mult, i


_DONE = object()


def _run(gen):
    for _ in gen:
        pass


def _zip_stages(gens):
    gens = list(gens)
    while gens:
        gens = [g for g in gens if next(g, _DONE) is not _DONE]
        if gens:
            yield


def _chain(gens):
    for g in gens:
        yield from g


def _interleave(main, sides, n_main, n_side):
    side = _chain(sides)
    next(side)
    done_side = 1
    for i, _ in enumerate(main):
        want = ((i + 1) * n_side) // n_main
        while done_side < want and next(side, _DONE) is not _DONE:
            done_side += 1
    _run(side)


IN_PROJ_STAGES = 1 + IN_WIDTH // 512


def _in_proj(x_ref, ng_ref, win_ref, proj_ref):
    hb = _rms(x_ref[...], ng_ref[...]).astype(BF16)
    yield
    for j in range(IN_WIDTH // 512):
        proj_ref[:, j * 512:(j + 1) * 512] = _dot(hb, win_ref[:, j * 512:(j + 1) * 512])
        yield


OUT_PROJ_STAGES = D_MODEL // 256


def _out_proj(mixed_s, x_ref, y_ref, wout_ref):
    mixed = mixed_s[...]
    for j in range(OUT_PROJ_STAGES):
        cols = slice(j * 256, (j + 1) * 256)
        y_ref[:, cols] = x_ref[:, cols] + _dot(mixed, wout_ref[:, cols])
        yield


def _mix_stages(tt):
    return 4 + tt // SUBLANES // 8 + (tt // RET_CHUNK) * 5


def _scan_rows(a, u, h_in, tt):
    G = tt // SUBLANES
    W = a.shape[-1]
    a3 = a.reshape(G, SUBLANES, W)
    u3 = u.reshape(G, SUBLANES, W)
    r3 = lax.broadcasted_iota(jnp.int32, (G, SUBLANES, W), 1)
    sh = 1
    while sh < SUBLANES:
        keep = r3 >= sh
        a_sh = jnp.where(keep, pltpu.roll(a3, sh, axis=1), 1.0)
        u_sh = jnp.where(keep, pltpu.roll(u3, sh, axis=1), 0.0)
        u3 = a3 * u_sh + u3
        a3 = a3 * a_sh
        sh *= 2
    yield
    hs = []
    for g in range(G):
        hg = a3[g] * h_in + u3[g]
        hs.append(hg)
        h_in = hg[SUBLANES - 1:SUBLANES, :]
        if g % 8 == 7:
            yield
    return jnp.concatenate(hs, axis=0)


def _retention_head(h, proj_s, r0, s_ref, retg_ref, dmask_ref, qdec_ref, kdec_ref, mixed_s):
    C = RET_CHUNK
    cs = slice(h * RET_DK, (h + 1) * RET_DK)
    q = proj_s[r0:r0 + C, QA + h * 128:QA + (h + 1) * 128]
    k = proj_s[r0:r0 + C, KA + h * 128:KA + (h + 1) * 128] * (RET_DK ** -0.5)
    vb = proj_s[r0:r0 + C, VA + h * 128:VA + (h + 1) * 128].astype(BF16)
    sc = _dot_nt(q.astype(BF16), k.astype(BF16)) * dmask_ref[h]
    yield
    s_prev = s_ref[0, h]
    lhs = jnp.concatenate([sc.astype(BF16), (q * qdec_ref[h]).astype(BF16)], axis=1)
    rhs = jnp.concatenate([vb, s_prev.astype(BF16)], axis=0)
    o = _dot(lhs, rhs)
    yield
    kv = _dot_tn((k * kdec_ref[h]).astype(BF16), vb)
    s_ref[0, h] = float(np.exp(np.float32(C) * _LOG_G[h])) * s_prev + kv
    yield
    o = _rms(o, retg_ref[:, cs])
    mixed_s[r0:r0 + C, cs] = (o * _silu(proj_s[r0:r0 + C, GA + h * 128:GA + (h + 1) * 128])).astype(BF16)
    yield


def _swa_kv_head(kh, proj_s, r0, blk, sinks_ref, qg_ref, kg_ref, mixed_s, kprev_s, vprev_s):
    C = RET_CHUNK
    ks = slice(kh * 128, (kh + 1) * 128)
    h0, h1 = kh * SWA_GROUP, kh * SWA_GROUP + 1
    kn = _rms(proj_s[r0:r0 + C, KC + kh * 128:KC + (kh + 1) * 128], kg_ref[...])
    vv = proj_s[r0:r0 + C, VC + kh * 128:VC + (kh + 1) * 128]
    kband = jnp.concatenate([kprev_s[:, ks], kn], axis=0).astype(BF16)
    vband = jnp.concatenate([vprev_s[:, ks], vv], axis=0).astype(BF16)
    kprev_s[:, ks] = kn
    vprev_s[:, ks] = vv
    q0 = _rms(proj_s[r0:r0 + C, QC + h0 * 128:QC + (h0 + 1) * 128], qg_ref[...])
    q1 = _rms(proj_s[r0:r0 + C, QC + h1 * 128:QC + (h1 + 1) * 128], qg_ref[...])
    qq = jnp.concatenate([q0, q1], axis=0).astype(BF16)
    yield
    s = _dot_nt(qq, kband) * (SWA_HEAD_DIM ** -0.5)
    ii = lax.broadcasted_iota(jnp.int32, (2 * C, 2 * C), 0)
    jj = lax.broadcasted_iota(jnp.int32, (2 * C, 2 * C), 1)
    dist = jnp.where(ii >= C, ii - C, ii) + C - jj
    valid = jnp.logical_and(jnp.logical_and(dist >= 0, dist < WINDOW), jj >= jnp.where(blk > 0, 0, C))
    slope = jnp.where(ii >= C, _SLOPES[h1], _SLOPES[h0])
    s = jnp.where(valid, s - slope * dist.astype(F32), NEG_INF)
    yield
    rowc = lax.broadcasted_iota(jnp.int32, (2 * C, 1), 0)
    sink = jnp.where(rowc >= C, sinks_ref[h1], sinks_ref[h0])
    m = jnp.maximum(jnp.max(s, axis=-1, keepdims=True), sink)
    p = jnp.exp(s - m)
    denom = jnp.sum(p, axis=-1, keepdims=True) + jnp.exp(sink - m)
    yield
    o = _dot((p / denom).astype(BF16), vband)
    yield
    for g, hh in enumerate((h0, h1)):
        gc = proj_s[r0:r0 + C, GC + hh * 128:GC + (hh + 1) * 128]
        mixed_s[r0:r0 + C, 2 * GROUP_WIDTH + hh * 128:2 * GROUP_WIDTH + (hh + 1) * 128] = (
            (o[g * C:(g + 1) * C] * _silu(gc)).astype(BF16))
    yield


def _mix_tile(proj_s, t, tt, sinks_ref, retg_ref, convw_ref, convb_ref,
              wr_ref, wi_ref, br_ref, bi_ref, lam_ref, qg_ref, kg_ref, dmask_ref, qdec_ref, kdec_ref,
              s_ref, mixed_s, xtail_s, kprev_s, vprev_s, hc_s):
    nch = tt // RET_CHUNK
    C = RET_CHUNK

    xb = proj_s[:, XB:XB + LRU_WIDTH]
    xfull = jnp.concatenate([xtail_s[...], xb], axis=0)
    xtail_s[...] = xb[tt - SUBLANES:tt, :]
    conv = convb_ref[...] + convw_ref[0:1, :] * pltpu.roll(xfull, 3, axis=0)[SUBLANES:]
    conv = conv + convw_ref[1:2, :] * pltpu.roll(xfull, 2, axis=0)[SUBLANES:]
    conv = conv + convw_ref[2:3, :] * pltpu.roll(xfull, 1, axis=0)[SUBLANES:]
    conv = conv + convw_ref[3:4, :] * xb
    yield

    a, mult, ig = _lru_gates(conv, wr_ref, wi_ref, br_ref, bi_ref, lam_ref)
    yield
    row = lax.broadcasted_iota(jnp.int32, (tt, LRU_WIDTH), 0)
    mult = jnp.where(row == jnp.where(t == 0, 0, -1), 1.0, mult)
    u = mult * ig * conv
    hseq = yield from _scan_rows(a, u, hc_s[...], tt)
    hc_s[...] = hseq[tt - 1:tt, :]
    mixed_s[:, GROUP_WIDTH:2 * GROUP_WIDTH] = (hseq * _silu(proj_s[:, GB:GB + LRU_WIDTH])).astype(BF16)
    yield

    for c in range(nch):
        r0 = c * C
        blk = t * nch + c
        yield from _zip_stages(
            [_retention_head(h, proj_s, r0, s_ref, retg_ref, dmask_ref, qdec_ref, kdec_ref, mixed_s)
             for h in range(RET_HEADS)]
            + [_swa_kv_head(kh, proj_s, r0, blk, sinks_ref, qg_ref, kg_ref, mixed_s, kprev_s, vprev_s)
               for kh in range(SWA_KV_HEADS)])


def _prompt_kernel(sinks_ref, xa_ref, xn_ref, ng_ref, win_ref, wout_ref, retg_ref, convw_ref, convb_ref,
                   wr_ref, wi_ref, br_ref, bi_ref, lam_ref, qg_ref, kg_ref,
                   dmask_ref, qdec_ref, kdec_ref, *rest, tt, npairs, layer):
    y_ref, s_ref, h_ref, conv_ref, kout_ref, vout_ref = rest[-14:-8]
    proj_a, proj_b, mixed_a, mixed_b, xtail_s, kprev_s, vprev_s, hc_s = rest[-8:]
    sinks_ref = sinks_ref.at[layer]
    b = pl.program_id(0)
    p = pl.program_id(1)

    @pl.when(p == 0)
    def _init():
        s_ref[...] = jnp.zeros_like(s_ref)
        xtail_s[...] = jnp.zeros_like(xtail_s)
        kprev_s[...] = jnp.zeros_like(kprev_s)
        vprev_s[...] = jnp.zeros_like(vprev_s)
        hc_s[...] = jnp.zeros_like(hc_s)

    xa0, xa1 = xa_ref.at[0, 0:tt, :], xa_ref.at[0, tt:2 * tt, :]
    ya0, ya1 = y_ref.at[0, 0:tt, :], y_ref.at[0, tt:2 * tt, :]

    @pl.when(jnp.logical_and(b == 0, p == 0))
    def _prologue():
        _run(_in_proj(xa0, ng_ref, win_ref, proj_a))

    common = (sinks_ref, retg_ref, convw_ref, convb_ref, wr_ref, wi_ref, br_ref, bi_ref,
              lam_ref, qg_ref, kg_ref, dmask_ref, qdec_ref, kdec_ref, s_ref.at[0])
    state = (xtail_s, kprev_s, vprev_s, hc_s)

    n_main = _mix_stages(tt)
    _interleave(_mix_tile(proj_a, 2 * p, tt, *common, mixed_a, *state),
                [_in_proj(xa1, ng_ref, win_ref, proj_b)], n_main, IN_PROJ_STAGES)
    _interleave(_mix_tile(proj_b, 2 * p + 1, tt, *common, mixed_b, *state),
                [_out_proj(mixed_a, xa0, ya0, wout_ref), _in_proj(xn_ref.at[0], ng_ref, win_ref, proj_a)],
                n_main, OUT_PROJ_STAGES + IN_PROJ_STAGES)
    _run(_out_proj(mixed_b, xa1, ya1, wout_ref))

    @pl.when(p == npairs - 1)
    def _state_out():
        h_ref[0, 0] = hc_s[...]
        conv_ref[0, 0] = xtail_s[SUBLANES - (CONV_W - 1):SUBLANES, :]
        for kh in range(SWA_KV_HEADS):
            ks = slice(kh * 128, (kh + 1) * 128)
            kout_ref[0, 0, pl.ds(kh, WINDOW, stride=SWA_KV_HEADS), :] = kprev_s[:, ks]
            vout_ref[0, 0, pl.ds(kh, WINDOW, stride=SWA_KV_HEADS), :] = vprev_s[:, ks]


def _const_spec(shape):
    nd = len(shape)
    return pl.BlockSpec(shape, lambda *_: (0,) * nd)


def _resident_spec(shape):
    nd = len(shape)
    return pl.BlockSpec(shape, lambda *_: (0,) * nd, pipeline_mode=pl.Buffered(1))


def _layer_spec(a, layer, resident=False):
    nd = a.ndim
    kw = dict(pipeline_mode=pl.Buffered(1)) if resident else {}
    return pl.BlockSpec((None,) + a.shape[1:], lambda *_: (layer,) + (0,) * (nd - 1), **kw)


def _prompt_layer(layer, x, prm, tables, prev_out, tt=256):
    B, T, D = x.shape
    npairs = T // (2 * tt)
    layer_in = [prm[k] for k in ("ng", "win", "wout", "retg", "convw", "convb", "wr", "wi", "br", "bi",
                                 "lam", "qg", "kg")]
    vec_in = layer_in + list(tables)

    def next_tile(b, p):
        last = p == npairs - 1
        return (jnp.where(last, jnp.minimum(b + 1, B - 1), b), jnp.where(last, 0, 2 * p + 2), 0)

    in_specs = ([pl.BlockSpec(memory_space=pltpu.SMEM),
                 pl.BlockSpec((1, 2 * tt, D), lambda b, p: (b, p, 0)),
                 pl.BlockSpec((1, tt, D), next_tile)]
                + [_layer_spec(a, layer, resident=True) for a in layer_in]
                + [_resident_spec(a.shape) for a in tables]
                + [pl.BlockSpec(memory_space=pl.ANY) for _ in prev_out])
    out_shape = (
        jax.ShapeDtypeStruct((B, T, D), F32),
        jax.ShapeDtypeStruct((DEPTH, B, RET_HEADS, RET_DK, RET_DV), F32),
        jax.ShapeDtypeStruct((DEPTH, B, 1, LRU_WIDTH), F32),
        jax.ShapeDtypeStruct((DEPTH, B, CONV_W - 1, LRU_WIDTH), F32),
        jax.ShapeDtypeStruct((DEPTH, B, 2 * WINDOW, SWA_HEAD_DIM), F32),
        jax.ShapeDtypeStruct((DEPTH, B, 2 * WINDOW, SWA_HEAD_DIM), F32),
    )
    out_specs = (
        pl.BlockSpec((1, 2 * tt, D), lambda b, p: (b, p, 0)),
        pl.BlockSpec((1, 1, RET_HEADS, RET_DK, RET_DV), lambda b, p: (layer, b, 0, 0, 0)),
        pl.BlockSpec((1, 1, 1, LRU_WIDTH), lambda b, p: (layer, b, 0, 0)),
        pl.BlockSpec((1, 1, CONV_W - 1, LRU_WIDTH), lambda b, p: (layer, b, 0, 0)),
        pl.BlockSpec((1, 1, 2 * WINDOW, SWA_HEAD_DIM), lambda b, p: (layer, b, 0, 0)),
        pl.BlockSpec((1, 1, 2 * WINDOW, SWA_HEAD_DIM), lambda b, p: (layer, b, 0, 0)),
    )
    n_in = 3 + len(vec_in)
    aliases = {n_in + j: 1 + j for j in range(len(prev_out))}
    scratch = [
        pltpu.VMEM((tt, IN_WIDTH), F32),
        pltpu.VMEM((tt, IN_WIDTH), F32),
        pltpu.VMEM((tt, MIX_WIDTH), BF16),
        pltpu.VMEM((tt, MIX_WIDTH), BF16),
        pltpu.VMEM((SUBLANES, LRU_WIDTH), F32),
        pltpu.VMEM((WINDOW, 256), F32),
        pltpu.VMEM((WINDOW, 256), F32),
        pltpu.VMEM((1, LRU_WIDTH), F32),
    ]
    return pl.pallas_call(
        functools.partial(_prompt_kernel, tt=tt, npairs=npairs, layer=layer),
        grid=(B, npairs),
        in_specs=in_specs,
        out_specs=out_specs,
        out_shape=out_shape,
        scratch_shapes=scratch,
        input_output_aliases=aliases,
        compiler_params=pltpu.CompilerParams(
            dimension_semantics=("arbitrary", "arbitrary"), vmem_limit_bytes=VMEM_LIMIT),
        name="prompt_layer",
    )(prm["sinks"], x, x, *vec_in, *prev_out)


def _sproj_kernel(x_ref, ng_ref, win_ref, o_ref, hb_s):
    @pl.when(pl.program_id(0) == 0)
    def _norm_once():
        hb_s[...] = _rms(x_ref[...], ng_ref[...]).astype(BF16)

    o_ref[...] = _dot(hb_s[...], win_ref[...])


def _sample_proj(layer, xp, prm):
    R = xp.shape[0]
    nb = IN_WIDTH // 512
    ng, win = prm["ng"], prm["win"]
    return pl.pallas_call(
        _sproj_kernel,
        grid=(nb,),
        in_specs=[_const_spec(xp.shape), _layer_spec(ng, layer),
                  pl.BlockSpec((None, D_MODEL, 512), lambda j: (layer, 0, j))],
        out_specs=pl.BlockSpec((R, 512), lambda j: (0, j)),
        out_shape=jax.ShapeDtypeStruct((R, IN_WIDTH), F32),
        scratch_shapes=[pltpu.VMEM((R, D_MODEL), BF16)],
        compiler_params=pltpu.CompilerParams(
            dimension_semantics=("arbitrary",), vmem_limit_bytes=VMEM_LIMIT),
        name="sample_proj",
    )(xp, ng, win)


def _sample_mix_seq(sinks_ref, proj_ref, s0_ref, kbuf_ref, vbuf_ref, retg_ref, qg_ref, kg_ref,
                    cm_ref, qdec_ref, kdec_ref, mix_ref, snew_ref, knew_ref, vnew_ref):
    P = SEQ_PAD
    for h in range(RET_HEADS):
        cs = slice(h * 128, (h + 1) * 128)
        q = proj_ref[:, QA + h * 128:QA + (h + 1) * 128]
        k = proj_ref[:, KA + h * 128:KA + (h + 1) * 128] * (RET_DK ** -0.5)
        v = proj_ref[:, VA + h * 128:VA + (h + 1) * 128]
        intra = jnp.zeros((P, RET_DV), F32)
        for s in range(4):
            r_ = TOK0 + s
            w = jnp.sum(q * k[r_:r_ + 1, :], axis=-1, keepdims=True)
            intra = intra + (w * cm_ref[h, s]) * v[r_:r_ + 1, :]
        s_prev = s0_ref[h]
        cross = _dot((q * qdec_ref[h]).astype(BF16), s_prev.astype(BF16))
        kv = _dot_tn((k * kdec_ref[h]).astype(BF16), v.astype(BF16))
        yield
        snew_ref[h] = float(np.exp(np.float32(4.0) * _LOG_G[h])) * s_prev + kv
        o = _rms(intra + cross, retg_ref[:, cs])
        mix_ref[:, cs] = o * _silu(proj_ref[:, GA + h * 128:GA + (h + 1) * 128])
        yield

    r16 = lax.broadcasted_iota(jnp.int32, (2 * P, WINDOW), 0)
    j16 = lax.broadcasted_iota(jnp.int32, (2 * P, WINDOW), 1)
    rr = jnp.where(r16 >= P, r16 - P, r16)
    row_ok = jnp.logical_and(rr >= TOK0, rr < TOK0 + 4)
    dist = (rr - TOK0) + WINDOW - j16
    distf = dist.astype(F32)
    valid = jnp.logical_and(jnp.logical_and(dist >= 0, dist < WINDOW), row_ok)
    r16c = lax.broadcasted_iota(jnp.int32, (2 * P, 1), 0)
    rrc = jnp.where(r16c >= P, r16c - P, r16c)
    rowc_ok = jnp.logical_and(rrc >= TOK0, rrc < TOK0 + 4)
    knew_ref[0:2 * (WINDOW - 4), :] = kbuf_ref[2 * 4:2 * WINDOW, :]
    vnew_ref[0:2 * (WINDOW - 4), :] = vbuf_ref[2 * 4:2 * WINDOW, :]
    for kh in range(SWA_KV_HEADS):
        h0, h1 = kh * SWA_GROUP, kh * SWA_GROUP + 1
        kb = kbuf_ref[pl.ds(kh, WINDOW, stride=SWA_KV_HEADS), :]
        vb = vbuf_ref[pl.ds(kh, WINDOW, stride=SWA_KV_HEADS), :]
        kn = _rms(proj_ref[:, KC + kh * 128:KC + (kh + 1) * 128], kg_ref[...])
        vn = proj_ref[:, VC + kh * 128:VC + (kh + 1) * 128]
        q0 = _rms(proj_ref[:, QC + h0 * 128:QC + (h0 + 1) * 128], qg_ref[...])
        q1 = _rms(proj_ref[:, QC + h1 * 128:QC + (h1 + 1) * 128], qg_ref[...])
        qq = jnp.concatenate([q0, q1], axis=0)
        slope = jnp.where(r16 >= P, _SLOPES[h1], _SLOPES[h0])
        slopec = jnp.where(r16c >= P, _SLOPES[h1], _SLOPES[h0])
        sb = _dot_nt(qq.astype(BF16), kb.astype(BF16)) * (SWA_HEAD_DIM ** -0.5)
        sb = jnp.where(valid, sb - slope * distf, NEG_INF)
        yield
        sink = jnp.where(r16c >= P, sinks_ref[h1], sinks_ref[h0])
        m = jnp.maximum(jnp.max(sb, axis=-1, keepdims=True), sink)
        wn = []
        for s in range(4):
            r_ = TOK0 + s
            w = jnp.sum(qq * kn[r_:r_ + 1, :], axis=-1, keepdims=True) * (SWA_HEAD_DIM ** -0.5)
            dn = rrc - r_
            w = jnp.where(jnp.logical_and(dn >= 0, rowc_ok), w - slopec * dn.astype(F32), NEG_INF)
            wn.append(w)
            m = jnp.maximum(m, w)
        pb = jnp.exp(sb - m)
        denom = jnp.sum(pb, axis=-1, keepdims=True) + jnp.exp(sink - m)
        pn = [jnp.exp(w - m) for w in wn]
        for p_ in pn:
            denom = denom + p_
        yield
        o = _dot((pb / denom).astype(BF16), vb.astype(BF16))
        for s in range(4):
            r_ = TOK0 + s
            o = o + (pn[s] / denom) * vn[r_:r_ + 1, :]
        for g, hh in enumerate((h0, h1)):
            gc = proj_ref[:, GC + hh * 128:GC + (hh + 1) * 128]
            mix_ref[:, GROUP_WIDTH + hh * 128:GROUP_WIDTH + (hh + 1) * 128] = o[g * P:(g + 1) * P] * _silu(gc)
        for s in range(4):
            r_out = 2 * (WINDOW - 4 + s) + kh
            knew_ref[r_out:r_out + 1, :] = kn[TOK0 + s:TOK0 + s + 1, :]
            vnew_ref[r_out:r_out + 1, :] = vn[TOK0 + s:TOK0 + s + 1, :]
        yield


def _sample_mix_kernel(sinks_ref, proj_ref, s0_ref, kbuf_ref, vbuf_ref, retg_ref, qg_ref, kg_ref,
                       cm_ref, qdec_ref, kdec_ref, *rest, nseq, layer):
    mix_ref, snew_ref, knew_ref, vnew_ref = rest[-4:]
    sinks_ref = sinks_ref.at[layer]
    gens = []
    for i in range(nseq):
        rows = slice(i * SEQ_PAD, (i + 1) * SEQ_PAD)
        gens.append(_sample_mix_seq(
            sinks_ref, proj_ref.at[rows, :], s0_ref.at[0, i], kbuf_ref.at[0, i], vbuf_ref.at[0, i],
            retg_ref, qg_ref, kg_ref, cm_ref, qdec_ref, kdec_ref,
            mix_ref.at[rows, :], snew_ref.at[0, i], knew_ref.at[0, i], vnew_ref.at[0, i]))
    _run(_zip_stages(gens))


def _sample_mix(layer, proj, state_ret, cache_k, cache_v, prm, tables, prev_out, nseq=8):
    depth, B = state_ret.shape[:2]
    layer_in = [prm[k] for k in ("retg", "qg", "kg")]
    consts = layer_in + list(tables)
    st_spec = pl.BlockSpec((1, nseq, RET_HEADS, RET_DK, RET_DV), lambda i: (layer, i, 0, 0, 0))
    kv_spec = pl.BlockSpec((1, nseq, 2 * WINDOW, SWA_HEAD_DIM), lambda i: (layer, i, 0, 0))
    in_specs = ([pl.BlockSpec(memory_space=pltpu.SMEM),
                 pl.BlockSpec((nseq * SEQ_PAD, IN_WIDTH), lambda i: (i, 0)),
                 st_spec, kv_spec, kv_spec]
                + [_layer_spec(a, layer) for a in layer_in]
                + [_const_spec(a.shape) for a in tables]
                + [pl.BlockSpec(memory_space=pl.ANY) for _ in prev_out])
    out_shape = (
        jax.ShapeDtypeStruct((B * SEQ_PAD, 2 * GROUP_WIDTH), F32),
        jax.ShapeDtypeStruct(state_ret.shape, F32),
        jax.ShapeDtypeStruct(cache_k.shape, F32),
        jax.ShapeDtypeStruct(cache_v.shape, F32),
    )
    out_specs = (pl.BlockSpec((nseq * SEQ_PAD, 2 * GROUP_WIDTH), lambda i: (i, 0)), st_spec, kv_spec, kv_spec)
    n_in = 5 + len(consts)
    aliases = {n_in + j: 1 + j for j in range(len(prev_out))}
    return pl.pallas_call(
        functools.partial(_sample_mix_kernel, nseq=nseq, layer=layer),
        grid=(B // nseq,),
        in_specs=in_specs,
        out_specs=out_specs,
        out_shape=out_shape,
        input_output_aliases=aliases,
        compiler_params=pltpu.CompilerParams(
            dimension_semantics=("arbitrary",), vmem_limit_bytes=VMEM_LIMIT),
        name="sample_mix",
    )(prm["sinks"], proj, state_ret, cache_k, cache_v, *consts, *prev_out)


def _sample_out_kernel(x_ref, xb_ref, gb_ref, mix_ref, conv8_ref, h8_ref, convw_ref, convb_ref,
                       wr_ref, wi_ref, br_ref, bi_ref, lam_ref, wout_ref,
                       y_ref, convo_ref, ho_ref):
    R = x_ref.shape[0]
    row = lax.broadcasted_iota(jnp.int32, (R, LRU_WIDTH), 0) & (SEQ_PAD - 1)
    xc = jnp.where(row < TOK0, conv8_ref[...], xb_ref[...])
    convo_ref[...] = xc
    conv = convb_ref[...] + convw_ref[0:1, :] * pltpu.roll(xc, 3, axis=0)
    conv = conv + convw_ref[1:2, :] * pltpu.roll(xc, 2, axis=0)
    conv = conv + convw_ref[2:3, :] * pltpu.roll(xc, 1, axis=0)
    conv = conv + convw_ref[3:4, :] * xc
    a, mult, ig = _lru_gates(conv, wr_ref, wi_ref, br_ref, bi_ref, lam_ref)
    u = mult * ig * conv
    h = h8_ref[...]
    for s in range(4):
        h = jnp.where(row == TOK0 + s, a * pltpu.roll(h, 1, axis=0) + u, h)
    ho_ref[...] = h
    ob = h * _silu(gb_ref[...])
    y = x_ref[...] + _dot(mix_ref[:, 0:GROUP_WIDTH].astype(BF16), wout_ref[0:GROUP_WIDTH, :])
    y = y + _dot(ob.astype(BF16), wout_ref[GROUP_WIDTH:2 * GROUP_WIDTH, :])
    y = y + _dot(mix_ref[:, GROUP_WIDTH:2 * GROUP_WIDTH].astype(BF16), wout_ref[2 * GROUP_WIDTH:, :])
    y_ref[...] = y


def _sample_out(layer, xp, proj, mix, conv8, h8, prm, rows=256):
    R = xp.shape[0]
    consts = [prm[k] for k in ("convw", "convb", "wr", "wi", "br", "bi", "lam", "wout")]
    in_specs = ([pl.BlockSpec((rows, D_MODEL), lambda i: (i, 0)),
                 pl.BlockSpec((rows, LRU_WIDTH), lambda i: (i, XB // LRU_WIDTH)),
                 pl.BlockSpec((rows, LRU_WIDTH), lambda i: (i, GB // LRU_WIDTH)),
                 pl.BlockSpec((rows, 2 * GROUP_WIDTH), lambda i: (i, 0)),
                 pl.BlockSpec((rows, LRU_WIDTH), lambda i: (i, 0)),
                 pl.BlockSpec((rows, LRU_WIDTH), lambda i: (i, 0))]
                + [_layer_spec(a, layer) for a in consts])
    out_shape = (
        jax.ShapeDtypeStruct((R, D_MODEL), F32),
        jax.ShapeDtypeStruct((R, LRU_WIDTH), F32),
        jax.ShapeDtypeStruct((R, LRU_WIDTH), F32),
    )
    out_specs = (
        pl.BlockSpec((rows, D_MODEL), lambda i: (i, 0)),
        pl.BlockSpec((rows, LRU_WIDTH), lambda i: (i, 0)),
        pl.BlockSpec((rows, LRU_WIDTH), lambda i: (i, 0)),
    )
    return pl.pallas_call(
        _sample_out_kernel,
        grid=(R // rows,),
        in_specs=in_specs,
        out_specs=out_specs,
        out_shape=out_shape,
        compiler_params=pltpu.CompilerParams(
            dimension_semantics=("arbitrary",), vmem_limit_bytes=VMEM_LIMIT),
        name="sample_out",
    )(xp, proj, proj, mix, conv8, h8, *consts)


def _prompt_tables():
    C = RET_CHUNK
    idx = np.arange(C, dtype=np.float32)
    diff = idx[:, None] - idx[None, :]
    causal = diff >= 0
    lg = _LOG_G[:, None, None]
    dmask = np.where(causal[None], np.exp(np.where(causal, diff, 0.0)[None] * lg), 0.0).astype(np.float32)
    qdec = np.exp((idx + 1.0)[None, :] * _LOG_G[:, None]).astype(np.float32)
    kdec = np.exp((C - 1 - idx)[None, :] * _LOG_G[:, None]).astype(np.float32)
    qdec = np.broadcast_to(qdec[:, :, None], (RET_HEADS, C, RET_DK)).copy()
    kdec = np.broadcast_to(kdec[:, :, None], (RET_HEADS, C, RET_DK)).copy()
    return jnp.asarray(dmask), jnp.asarray(qdec), jnp.asarray(kdec)


def _sample_tables():
    P = SEQ_PAD
    rows = np.arange(P, dtype=np.float32)
    i = rows - TOK0
    tok = (i >= 0) & (i < 4)
    cm = np.zeros((RET_HEADS, 4, P, RET_DV), np.float32)
    qdec = np.zeros((RET_HEADS, P, RET_DK), np.float32)
    kdec = np.zeros((RET_HEADS, P, RET_DK), np.float32)
    for h in range(RET_HEADS):
        for s in range(4):
            d = i - s
            col = np.where(tok & (d >= 0), np.exp(np.where(d >= 0, d, 0.0) * _LOG_G[h]), 0.0)
            cm[h, s] = col[:, None]
        qdec[h] = np.where(tok, np.exp((i + 1.0) * _LOG_G[h]), 0.0)[:, None]
        kdec[h] = np.where(tok, np.exp((3.0 - i) * _LOG_G[h]), 0.0)[:, None]
    return jnp.asarray(cm), jnp.asarray(qdec), jnp.asarray(kdec)


def _block_diag(w):
    L, n, d, _ = w.shape
    g = n // 2
    w = w.reshape(L, 2, g, d, d)
    eye = jnp.eye(g, dtype=w.dtype)
    return (eye[None, None, :, None, :, None] * w[:, :, :, :, None, :]).reshape(L, 2, g * d, g * d)


def kernel(x_prompt, x_sample, state_ret, state_lru, state_conv, cache_swa_k, cache_swa_v, norm_g, w_in, w_out, ret_norm_g, conv_w, conv_b, w_rgate, b_rgate, w_igate, b_igate, lru_lambda, q_norm_g, k_norm_g, attn_sinks):
    Bs, Ts, _ = x_sample.shape
    ptab = _prompt_tables()
    stab = _sample_tables()
    yp = x_prompt
    ys = jnp.pad(x_sample, ((0, 0), (TOK0, SEQ_PAD - TOK0 - Ts), (0, 0))).reshape(Bs * SEQ_PAD, D_MODEL)
    cache_k = cache_swa_k.reshape(DEPTH, Bs, 2 * WINDOW, SWA_HEAD_DIM)
    cache_v = cache_swa_v.reshape(DEPTH, Bs, 2 * WINDOW, SWA_HEAD_DIM)
    Bp = x_prompt.shape[0]
    p_state = tuple(jnp.zeros((DEPTH, Bp) + s, F32) for s in (
        (RET_HEADS, RET_DK, RET_DV), (1, LRU_WIDTH), (CONV_W - 1, LRU_WIDTH),
        (2 * WINDOW, SWA_HEAD_DIM), (2 * WINDOW, SWA_HEAD_DIM)))
    s_state = (jnp.zeros_like(state_ret), jnp.zeros_like(cache_k), jnp.zeros_like(cache_v))
    s_h, s_conv = [], []
    row = lambda a: a.reshape(DEPTH, 1, -1)
    prm = dict(
        sinks=attn_sinks, ng=row(norm_g), win=w_in.astype(BF16), wout=w_out.astype(BF16), retg=row(ret_norm_g),
        convw=conv_w, convb=row(conv_b), wr=_block_diag(w_rgate).astype(BF16), wi=_block_diag(w_igate).astype(BF16),
        br=row(b_rgate), bi=row(b_igate), lam=row(lru_lambda), qg=row(q_norm_g), kg=row(k_norm_g))
    for l in range(DEPTH):
        yp, *p_state = _prompt_layer(l, yp, prm, ptab, tuple(p_state))

        proj = _sample_proj(l, ys, prm)
        mix, *s_state = _sample_mix(l, proj, state_ret, cache_k, cache_v, prm, stab, tuple(s_state))
        conv8 = jnp.pad(state_conv[l], ((0, 0), (0, SEQ_PAD - (CONV_W - 1)), (0, 0))).reshape(Bs * SEQ_PAD, LRU_WIDTH)
        h8 = jnp.pad(state_lru[l][:, None, :], ((0, 0), (TOK0 - 1, SEQ_PAD - TOK0), (0, 0))).reshape(Bs * SEQ_PAD, LRU_WIDTH)
        ys, convo, ho = _sample_out(l, ys, proj, mix, conv8, h8, prm)
        s_h.append(ho.reshape(Bs, SEQ_PAD, LRU_WIDTH)[:, TOK0 + Ts - 1])
        s_conv.append(convo.reshape(Bs, SEQ_PAD, LRU_WIDTH)[:, TOK0 + Ts - (CONV_W - 1):TOK0 + Ts])

    p_ret, p_h, p_conv, p_k, p_v = p_state
    s_ret, s_k, s_v = s_state
    kv5 = lambda a, n: a.reshape(DEPTH, n, WINDOW, SWA_KV_HEADS, SWA_HEAD_DIM)
    y_sample = ys.reshape(Bs, SEQ_PAD, D_MODEL)[:, TOK0:TOK0 + Ts]
    return (yp, y_sample,
            p_ret, p_h.reshape(DEPTH, Bp, LRU_WIDTH), p_conv, kv5(p_k, Bp), kv5(p_v, Bp),
            s_ret, jnp.stack(s_h), jnp.stack(s_conv), kv5(s_k, Bs), kv5(s_v, Bs))
```

```python
import functools
import math

import numpy as np
import jax
import jax.numpy as jnp
from jax import lax
from jax.experimental import pallas as pl
from jax.experimental.pallas import tpu as pltpu

D_MODEL = 1024
DEPTH = 2
PAST_LEN = 16384
GROUP_WIDTH = 512
RET_HEADS = 4
RET_DK = 128
RET_DV = 128
RET_CHUNK = 128
LRU_WIDTH = 512
LRU_BLOCKS = 8
LRU_C = 8.0
CONV_W = 4
SWA_HEADS = 4
SWA_KV_HEADS = 2
SWA_GROUP = 2
SWA_HEAD_DIM = 128
WINDOW = 128
NORM_EPS = 1e-6
NEG_INF = -1e30

IN_WIDTH = 4608
MIX_WIDTH = 1536
QA, KA, VA, GA, XB, GB, QC, KC, VC, GC = 0, 512, 1024, 1536, 2048, 2560, 3072, 3584, 3840, 4096

F32 = jnp.float32
BF16 = jnp.bfloat16

SUBLANES = 8
SEQ_PAD = 8
TOK0 = 3
VMEM_LIMIT = 56 * 1024 * 1024

_LOG_G = np.log1p(-np.power(np.float32(2.0), (-5.0 - np.arange(RET_HEADS)).astype(np.float32))).astype(np.float32)
_SLOPES = [2.0 ** (-8.0 * (h + 1) / SWA_HEADS) for h in range(SWA_HEADS)]


def _rms(x, g):
    ms = jnp.mean(x * x, axis=-1, keepdims=True)
    return x * lax.rsqrt(ms + NORM_EPS) * g


def _silu(x):
    return x * jax.nn.sigmoid(x)


def _softplus(x):
    return jnp.maximum(x, 0.0) + jnp.log1p(jnp.exp(-jnp.abs(x)))


def _dot(a, b):
    return jnp.dot(a, b, preferred_element_type=F32)


def _dot_nt(a, b):
    return lax.dot_general(a, b, (((1,), (1,)), ((), ())), preferred_element_type=F32)


def _dot_tn(a, b):
    return lax.dot_general(a, b, (((0,), (0,)), ((), ())), preferred_element_type=F32)


def _lru_gates(conv, wr_ref, wi_ref, br_ref, bi_ref, lam_ref):
    gin = conv.astype(BF16)
    half = LRU_WIDTH // 2
    lo, hi = gin[:, :half], gin[:, half:]
    r = jax.nn.sigmoid(jnp.concatenate([_dot(lo, wr_ref[0]), _dot(hi, wr_ref[1])], axis=1) + br_ref[...])
    i = jax.nn.sigmoid(jnp.concatenate([_dot(lo, wi_ref[0]), _dot(hi, wi_ref[1])], axis=1) + bi_ref[...])
    log_a = -LRU_C * r * _softplus(-lam_ref[...])
    a = jnp.exp(log_a)
    m2 = 1.0 - a * a
    mult = jnp.where(m2 > 0.0, m2 * lax.rsqrt(m2), 0.0)
    return a, mult, i


_DONE = object()


def _run(gen):
    for _ in gen:
        pass


def _zip_stages(gens):
    gens = list(gens)
    while gens:
        gens = [g for g in gens if next(g, _DONE) is not _DONE]
        if gens:
            yield


def _chain(gens):
    for g in gens:
        yield from g


def _interleave(main, sides, n_main, n_side):
    side = _chain(sides)
    next(side)
    done_side = 1
    for i, _ in enumerate(main):
        want = ((i + 1) * n_side) // n_main
        while done_side < want and next(side, _DONE) is not _DONE:
            done_side += 1
    _run(side)


IN_PROJ_STAGES = 1 + IN_WIDTH // 512


def _in_proj(x_ref, ng_ref, win_ref, proj_ref):
    hb = _rms(x_ref[...], ng_ref[...]).astype(BF16)
    yield
    for j in range(IN_WIDTH // 512):
        proj_ref[:, j * 512:(j + 1) * 512] = _dot(hb, win_ref[:, j * 512:(j + 1) * 512])
        yield


OUT_PROJ_STAGES = D_MODEL // 256


def _out_proj(mixed_s, x_ref, y_ref, wout_ref):
    mixed = mixed_s[...]
    for j in range(OUT_PROJ_STAGES):
        cols = slice(j * 256, (j + 1) * 256)
        y_ref[:, cols] = x_ref[:, cols] + _dot(mixed, wout_ref[:, cols])
        yield


def _mix_stages(tt):
    return 4 + tt // SUBLANES // 8 + (tt // RET_CHUNK) * 5


def _scan_rows(a, u, h_in, tt):
    G = tt // SUBLANES
    W = a.shape[-1]
    a3 = a.reshape(G, SUBLANES, W)
    u3 = u.reshape(G, SUBLANES, W)
    r3 = lax.broadcasted_iota(jnp.int32, (G, SUBLANES, W), 1)
    sh = 1
    while sh < SUBLANES:
        keep = r3 >= sh
        a_sh = jnp.where(keep, pltpu.roll(a3, sh, axis=1), 1.0)
        u_sh = jnp.where(keep, pltpu.roll(u3, sh, axis=1), 0.0)
        u3 = a3 * u_sh + u3
        a3 = a3 * a_sh
        sh *= 2
    yield
    hs = []
    for g in range(G):
        hg = a3[g] * h_in + u3[g]
        hs.append(hg)
        h_in = hg[SUBLANES - 1:SUBLANES, :]
        if g % 8 == 7:
            yield
    return jnp.concatenate(hs, axis=0)


def _retention_head(h, proj_s, r0, s_ref, retg_ref, dmask_ref, qdec_ref, kdec_ref, mixed_s):
    C = RET_CHUNK
    cs = slice(h * RET_DK, (h + 1) * RET_DK)
    q = proj_s[r0:r0 + C, QA + h * 128:QA + (h + 1) * 128]
    k = proj_s[r0:r0 + C, KA + h * 128:KA + (h + 1) * 128] * (RET_DK ** -0.5)
    vb = proj_s[r0:r0 + C, VA + h * 128:VA + (h + 1) * 128].astype(BF16)
    sc = _dot_nt(q.astype(BF16), k.astype(BF16)) * dmask_ref[h]
    yield
    s_prev = s_ref[0, h]
    lhs = jnp.concatenate([sc.astype(BF16), (q * qdec_ref[h]).astype(BF16)], axis=1)
    rhs = jnp.concatenate([vb, s_prev.astype(BF16)], axis=0)
    o = _dot(lhs, rhs)
    yield
    kv = _dot_tn((k * kdec_ref[h]).astype(BF16), vb)
    s_ref[0, h] = float(np.exp(np.float32(C) * _LOG_G[h])) * s_prev + kv
    yield
    o = _rms(o, retg_ref[:, cs])
    mixed_s[r0:r0 + C, cs] = (o * _silu(proj_s[r0:r0 + C, GA + h * 128:GA + (h + 1) * 128])).astype(BF16)
    yield


def _swa_kv_head(kh, proj_s, r0, blk, sinks_ref, qg_ref, kg_ref, mixed_s, kprev_s, vprev_s):
    C = RET_CHUNK
    ks = slice(kh * 128, (kh + 1) * 128)
    h0, h1 = kh * SWA_GROUP, kh * SWA_GROUP + 1
    kn = _rms(proj_s[r0:r0 + C, KC + kh * 128:KC + (kh + 1) * 128], kg_ref[...])
    vv = proj_s[r0:r0 + C, VC + kh * 128:VC + (kh + 1) * 128]
    kband = jnp.concatenate([kprev_s[:, ks], kn], axis=0).astype(BF16)
    vband = jnp.concatenate([vprev_s[:, ks], vv], axis=0).astype(BF16)
    kprev_s[:, ks] = kn
    vprev_s[:, ks] = vv
    q0 = _rms(proj_s[r0:r0 + C, QC + h0 * 128:QC + (h0 + 1) * 128], qg_ref[...])
    q1 = _rms(proj_s[r0:r0 + C, QC + h1 * 128:QC + (h1 + 1) * 128], qg_ref[...])
    qq = jnp.concatenate([q0, q1], axis=0).astype(BF16)
    yield
    s = _dot_nt(qq, kband) * (SWA_HEAD_DIM ** -0.5)
    ii = lax.broadcasted_iota(jnp.int32, (2 * C, 2 * C), 0)
    jj = lax.broadcasted_iota(jnp.int32, (2 * C, 2 * C), 1)
    dist = jnp.where(ii >= C, ii - C, ii) + C - jj
    valid = jnp.logical_and(jnp.logical_and(dist >= 0, dist < WINDOW), jj >= jnp.where(blk > 0, 0, C))
    slope = jnp.where(ii >= C, _SLOPES[h1], _SLOPES[h0])
    s = jnp.where(valid, s - slope * dist.astype(F32), NEG_INF)
    yield
    rowc = lax.broadcasted_iota(jnp.int32, (2 * C, 1), 0)
    sink = jnp.where(rowc >= C, sinks_ref[h1], sinks_ref[h0])
    m = jnp.maximum(jnp.max(s, axis=-1, keepdims=True), sink)
    p = jnp.exp(s - m)
    denom = jnp.sum(p, axis=-1, keepdims=True) + jnp.exp(sink - m)
    yield
    o = _dot((p / denom).astype(BF16), vband)
    yield
    for g, hh in enumerate((h0, h1)):
        gc = proj_s[r0:r0 + C, GC + hh * 128:GC + (hh + 1) * 128]
        mixed_s[r0:r0 + C, 2 * GROUP_WIDTH + hh * 128:2 * GROUP_WIDTH + (hh + 1) * 128] = (
            (o[g * C:(g + 1) * C] * _silu(gc)).astype(BF16))
    yield


def _mix_tile(proj_s, t, tt, sinks_ref, retg_ref, convw_ref, convb_ref,
              wr_ref, wi_ref, br_ref, bi_ref, lam_ref, qg_ref, kg_ref, dmask_ref, qdec_ref, kdec_ref,
              s_ref, mixed_s, xtail_s, kprev_s, vprev_s, hc_s):
    nch = tt // RET_CHUNK
    C = RET_CHUNK

    xb = proj_s[:, XB:XB + LRU_WIDTH]
    xfull = jnp.concatenate([xtail_s[...], xb], axis=0)
    xtail_s[...] = xb[tt - SUBLANES:tt, :]
    conv = convb_ref[...] + convw_ref[0:1, :] * pltpu.roll(xfull, 3, axis=0)[SUBLANES:]
    conv = conv + convw_ref[1:2, :] * pltpu.roll(xfull, 2, axis=0)[SUBLANES:]
    conv = conv + convw_ref[2:3, :] * pltpu.roll(xfull, 1, axis=0)[SUBLANES:]
    conv = conv + convw_ref[3:4, :] * xb
    yield

    a, mult, ig = _lru_gates(conv, wr_ref, wi_ref, br_ref, bi_ref, lam_ref)
    yield
    row = lax.broadcasted_iota(jnp.int32, (tt, LRU_WIDTH), 0)
    mult = jnp.where(row == jnp.where(t == 0, 0, -1), 1.0, mult)
    u = mult * ig * conv
    hseq = yield from _scan_rows(a, u, hc_s[...], tt)
    hc_s[...] = hseq[tt - 1:tt, :]
    mixed_s[:, GROUP_WIDTH:2 * GROUP_WIDTH] = (hseq * _silu(proj_s[:, GB:GB + LRU_WIDTH])).astype(BF16)
    yield

    for c in range(nch):
        r0 = c * C
        blk = t * nch + c
        yield from _zip_stages(
            [_retention_head(h, proj_s, r0, s_ref, retg_ref, dmask_ref, qdec_ref, kdec_ref, mixed_s)
             for h in range(RET_HEADS)]
            + [_swa_kv_head(kh, proj_s, r0, blk, sinks_ref, qg_ref, kg_ref, mixed_s, kprev_s, vprev_s)
               for kh in range(SWA_KV_HEADS)])


def _prompt_kernel(sinks_ref, xa_ref, xn_ref, ng_ref, win_ref, wout_ref, retg_ref, convw_ref, convb_ref,
                   wr_ref, wi_ref, br_ref, bi_ref, lam_ref, qg_ref, kg_ref,
                   dmask_ref, qdec_ref, kdec_ref, *rest, tt, npairs, layer, slot, fill):
    y_ref, s_ref, h_ref, conv_ref, kout_ref, vout_ref = rest[-14:-8]
    proj_a, proj_b, mixed_a, mixed_b, xtail_s, kprev_s, vprev_s, hc_s = rest[-8:]
    sinks_ref = sinks_ref.at[layer]
    b = pl.program_id(0)
    p = pl.program_id(1)

    @pl.when(p == 0)
    def _init():
        s_ref[...] = jnp.zeros_like(s_ref)
        xtail_s[...] = jnp.zeros_like(xtail_s)
        kprev_s[...] = jnp.zeros_like(kprev_s)
        vprev_s[...] = jnp.zeros_like(vprev_s)
        hc_s[...] = jnp.zeros_like(hc_s)

    xa0, xa1 = xa_ref.at[0, 0:tt, :], xa_ref.at[0, tt:2 * tt, :]
    ya0, ya1 = y_ref.at[0, 0:tt, :], y_ref.at[0, tt:2 * tt, :]

    @pl.when(jnp.logical_and(b == 0, p == 0))
    def _prologue():
        _run(_in_proj(xa0, ng_ref, win_ref, proj_a))

    common = (sinks_ref, retg_ref, convw_ref, convb_ref, wr_ref, wi_ref, br_ref, bi_ref,
              lam_ref, qg_ref, kg_ref, dmask_ref, qdec_ref, kdec_ref, s_ref.at[slot])
    state = (xtail_s, kprev_s, vprev_s, hc_s)

    n_main = _mix_stages(tt)
    _interleave(_mix_tile(proj_a, 2 * p, tt, *common, mixed_a, *state),
                [_in_proj(xa1, ng_ref, win_ref, proj_b)], n_main, IN_PROJ_STAGES)
    _interleave(_mix_tile(proj_b, 2 * p + 1, tt, *common, mixed_b, *state),
                [_out_proj(mixed_a, xa0, ya0, wout_ref), _in_proj(xn_ref.at[0], ng_ref, win_ref, proj_a)],
                n_main, OUT_PROJ_STAGES + IN_PROJ_STAGES)
    _run(_out_proj(mixed_b, xa1, ya1, wout_ref))

    @pl.when(p == npairs - 1)
    def _state_out():
        h_ref[slot, 0] = hc_s[...]
        conv_ref[slot, 0] = xtail_s[SUBLANES - (CONV_W - 1):SUBLANES, :]
        for kh in range(SWA_KV_HEADS):
            ks = slice(kh * 128, (kh + 1) * 128)
            kout_ref[slot, 0, pl.ds(kh, WINDOW, stride=SWA_KV_HEADS), :] = kprev_s[:, ks]
            vout_ref[slot, 0, pl.ds(kh, WINDOW, stride=SWA_KV_HEADS), :] = vprev_s[:, ks]
        if fill:
            for other in [o for o in range(DEPTH) if o != slot]:
                for ref in (h_ref, conv_ref, kout_ref, vout_ref):
                    ref[other] = jnp.zeros(ref.shape[1:], F32)


def _const_spec(shape):
    nd = len(shape)
    return pl.BlockSpec(shape, lambda *_: (0,) * nd)


def _resident_spec(shape):
    nd = len(shape)
    return pl.BlockSpec(shape, lambda *_: (0,) * nd, pipeline_mode=pl.Buffered(1))


def _layer_spec(a, layer, resident=False):
    nd = a.ndim
    kw = dict(pipeline_mode=pl.Buffered(1)) if resident else {}
    return pl.BlockSpec((None,) + a.shape[1:], lambda *_: (layer,) + (0,) * (nd - 1), **kw)


def _prompt_layer(layer, x, prm, tables, prev_out, tt=256):
    B, T, D = x.shape
    npairs = T // (2 * tt)
    layer_in = [prm[k] for k in ("ng", "win", "wout", "retg", "convw", "convb", "wr", "wi", "br", "bi",
                                 "lam", "qg", "kg")]
    vec_in = layer_in + list(tables)

    def next_tile(b, p):
        last = p == npairs - 1
        return (jnp.where(last, jnp.minimum(b + 1, B - 1), b), jnp.where(last, 0, 2 * p + 2), 0)

    in_specs = ([pl.BlockSpec(memory_space=pltpu.SMEM),
                 pl.BlockSpec((1, 2 * tt, D), lambda b, p: (b, p, 0)),
                 pl.BlockSpec((1, tt, D), next_tile)]
                + [_layer_spec(a, layer, resident=True) for a in layer_in]
                + [_resident_spec(a.shape) for a in tables]
                + [pl.BlockSpec(memory_space=pl.ANY) for _ in prev_out])
    out_shape = (
        jax.ShapeDtypeStruct((B, T, D), F32),
        jax.ShapeDtypeStruct((DEPTH, B, RET_HEADS, RET_DK, RET_DV), F32),
        jax.ShapeDtypeStruct((DEPTH, B, 1, LRU_WIDTH), F32),
        jax.ShapeDtypeStruct((DEPTH, B, CONV_W - 1, LRU_WIDTH), F32),
        jax.ShapeDtypeStruct((DEPTH, B, 2 * WINDOW, SWA_HEAD_DIM), F32),
        jax.ShapeDtypeStruct((DEPTH, B, 2 * WINDOW, SWA_HEAD_DIM), F32),
    )
    fill = not prev_out
    ld, li, slot = (DEPTH, 0, layer) if fill else (1, layer, 0)
    out_specs = (
        pl.BlockSpec((1, 2 * tt, D), lambda b, p: (b, p, 0)),
        pl.BlockSpec((ld, 1, RET_HEADS, RET_DK, RET_DV), lambda b, p: (li, b, 0, 0, 0)),
        pl.BlockSpec((ld, 1, 1, LRU_WIDTH), lambda b, p: (li, b, 0, 0)),
        pl.BlockSpec((ld, 1, CONV_W - 1, LRU_WIDTH), lambda b, p: (li, b, 0, 0)),
        pl.BlockSpec((ld, 1, 2 * WINDOW, SWA_HEAD_DIM), lambda b, p: (li, b, 0, 0)),
        pl.BlockSpec((ld, 1, 2 * WINDOW, SWA_HEAD_DIM), lambda b, p: (li, b, 0, 0)),
    )
    n_in = 3 + len(vec_in)
    aliases = {n_in + j: 1 + j for j in range(len(prev_out))}
    scratch = [
        pltpu.VMEM((tt, IN_WIDTH), F32),
        pltpu.VMEM((tt, IN_WIDTH), F32),
        pltpu.VMEM((tt, MIX_WIDTH), BF16),
        pltpu.VMEM((tt, MIX_WIDTH), BF16),
        pltpu.VMEM((SUBLANES, LRU_WIDTH), F32),
        pltpu.VMEM((WINDOW, 256), F32),
        pltpu.VMEM((WINDOW, 256), F32),
        pltpu.VMEM((1, LRU_WIDTH), F32),
    ]
    return pl.pallas_call(
        functools.partial(_prompt_kernel, tt=tt, npairs=npairs, layer=layer, slot=slot, fill=fill),
        grid=(B, npairs),
        in_specs=in_specs,
        out_specs=out_specs,
        out_shape=out_shape,
        scratch_shapes=scratch,
        input_output_aliases=aliases,
        compiler_params=pltpu.CompilerParams(
            dimension_semantics=("arbitrary", "arbitrary"), vmem_limit_bytes=VMEM_LIMIT),
        name="prompt_layer",
    )(prm["sinks"], x, x, *vec_in, *prev_out)


def _sproj_kernel(x_ref, ng_ref, win_ref, o_ref, hb_s):
    @pl.when(pl.program_id(0) == 0)
    def _norm_once():
        hb_s[...] = _rms(x_ref[...], ng_ref[...]).astype(BF16)

    o_ref[...] = _dot(hb_s[...], win_ref[...])


def _sample_proj(layer, xp, prm):
    R = xp.shape[0]
    nb = IN_WIDTH // 512
    ng, win = prm["ng"], prm["win"]
    return pl.pallas_call(
        _sproj_kernel,
        grid=(nb,),
        in_specs=[_const_spec(xp.shape), _layer_spec(ng, layer),
                  pl.BlockSpec((None, D_MODEL, 512), lambda j: (layer, 0, j))],
        out_specs=pl.BlockSpec((R, 512), lambda j: (0, j)),
        out_shape=jax.ShapeDtypeStruct((R, IN_WIDTH), F32),
        scratch_shapes=[pltpu.VMEM((R, D_MODEL), BF16)],
        compiler_params=pltpu.CompilerParams(
            dimension_semantics=("arbitrary",), vmem_limit_bytes=VMEM_LIMIT),
        name="sample_proj",
    )(xp, ng, win)


def _sample_mix_seq(sinks_ref, proj_ref, s0_ref, kbuf_ref, vbuf_ref, retg_ref, qg_ref, kg_ref,
                    cm_ref, qdec_ref, kdec_ref, mix_ref, snew_ref, knew_ref, vnew_ref):
    P = SEQ_PAD
    for h in range(RET_HEADS):
        cs = slice(h * 128, (h + 1) * 128)
        q = proj_ref[:, QA + h * 128:QA + (h + 1) * 128]
        k = proj_ref[:, KA + h * 128:KA + (h + 1) * 128] * (RET_DK ** -0.5)
        v = proj_ref[:, VA + h * 128:VA + (h + 1) * 128]
        intra = jnp.zeros((P, RET_DV), F32)
        for s in range(4):
            r_ = TOK0 + s
            w = jnp.sum(q * k[r_:r_ + 1, :], axis=-1, keepdims=True)
            intra = intra + (w * cm_ref[h, s]) * v[r_:r_ + 1, :]
        s_prev = s0_ref[h]
        cross = _dot((q * qdec_ref[h]).astype(BF16), s_prev.astype(BF16))
        kv = _dot_tn((k * kdec_ref[h]).astype(BF16), v.astype(BF16))
        yield
        snew_ref[h] = float(np.exp(np.float32(4.0) * _LOG_G[h])) * s_prev + kv
        o = _rms(intra + cross, retg_ref[:, cs])
        mix_ref[:, cs] = o * _silu(proj_ref[:, GA + h * 128:GA + (h + 1) * 128])
        yield

    r16 = lax.broadcasted_iota(jnp.int32, (2 * P, WINDOW), 0)
    j16 = lax.broadcasted_iota(jnp.int32, (2 * P, WINDOW), 1)
    rr = jnp.where(r16 >= P, r16 - P, r16)
    row_ok = jnp.logical_and(rr >= TOK0, rr < TOK0 + 4)
    dist = (rr - TOK0) + WINDOW - j16
    distf = dist.astype(F32)
    valid = jnp.logical_and(jnp.logical_and(dist >= 0, dist < WINDOW), row_ok)
    r16c = lax.broadcasted_iota(jnp.int32, (2 * P, 1), 0)
    rrc = jnp.where(r16c >= P, r16c - P, r16c)
    rowc_ok = jnp.logical_and(rrc >= TOK0, rrc < TOK0 + 4)
    knew_ref[0:2 * (WINDOW - 4), :] = kbuf_ref[2 * 4:2 * WINDOW, :]
    vnew_ref[0:2 * (WINDOW - 4), :] = vbuf_ref[2 * 4:2 * WINDOW, :]
    for kh in range(SWA_KV_HEADS):
        h0, h1 = kh * SWA_GROUP, kh * SWA_GROUP + 1
        kb = kbuf_ref[pl.ds(kh, WINDOW, stride=SWA_KV_HEADS), :]
        vb = vbuf_ref[pl.ds(kh, WINDOW, stride=SWA_KV_HEADS), :]
        kn = _rms(proj_ref[:, KC + kh * 128:KC + (kh + 1) * 128], kg_ref[...])
        vn = proj_ref[:, VC + kh * 128:VC + (kh + 1) * 128]
        q0 = _rms(proj_ref[:, QC + h0 * 128:QC + (h0 + 1) * 128], qg_ref[...])
        q1 = _rms(proj_ref[:, QC + h1 * 128:QC + (h1 + 1) * 128], qg_ref[...])
        qq = jnp.concatenate([q0, q1], axis=0)
        slope = jnp.where(r16 >= P, _SLOPES[h1], _SLOPES[h0])
        slopec = jnp.where(r16c >= P, _SLOPES[h1], _SLOPES[h0])
        sb = _dot_nt(qq.astype(BF16), kb.astype(BF16)) * (SWA_HEAD_DIM ** -0.5)
        sb = jnp.where(valid, sb - slope * distf, NEG_INF)
        yield
        sink = jnp.where(r16c >= P, sinks_ref[h1], sinks_ref[h0])
        m = jnp.maximum(jnp.max(sb, axis=-1, keepdims=True), sink)
        wn = []
        for s in range(4):
            r_ = TOK0 + s
            w = jnp.sum(qq * kn[r_:r_ + 1, :], axis=-1, keepdims=True) * (SWA_HEAD_DIM ** -0.5)
            dn = rrc - r_
            w = jnp.where(jnp.logical_and(dn >= 0, rowc_ok), w - slopec * dn.astype(F32), NEG_INF)
            wn.append(w)
            m = jnp.maximum(m, w)
        pb = jnp.exp(sb - m)
        denom = jnp.sum(pb, axis=-1, keepdims=True) + jnp.exp(sink - m)
        pn = [jnp.exp(w - m) for w in wn]
        for p_ in pn:
            denom = denom + p_
        yield
        o = _dot((pb / denom).astype(BF16), vb.astype(BF16))
        for s in range(4):
            r_ = TOK0 + s
            o = o + (pn[s] / denom) * vn[r_:r_ + 1, :]
        for g, hh in enumerate((h0, h1)):
            gc = proj_ref[:, GC + hh * 128:GC + (hh + 1) * 128]
            mix_ref[:, GROUP_WIDTH + hh * 128:GROUP_WIDTH + (hh + 1) * 128] = o[g * P:(g + 1) * P] * _silu(gc)
        for s in range(4):
            r_out = 2 * (WINDOW - 4 + s) + kh
            knew_ref[r_out:r_out + 1, :] = kn[TOK0 + s:TOK0 + s + 1, :]
            vnew_ref[r_out:r_out + 1, :] = vn[TOK0 + s:TOK0 + s + 1, :]
        yield


def _sample_mix_kernel(sinks_ref, proj_ref, s0_ref, kbuf_ref, vbuf_ref, retg_ref, qg_ref, kg_ref,
                       cm_ref, qdec_ref, kdec_ref, *rest, nseq, layer, slot, fill):
    mix_ref, snew_ref, knew_ref, vnew_ref = rest[-4:]
    sinks_ref = sinks_ref.at[layer]
    if fill:
        for other in [o for o in range(DEPTH) if o != slot]:
            for ref in (snew_ref, knew_ref, vnew_ref):
                ref[other] = jnp.zeros(ref.shape[1:], F32)
    gens = []
    for i in range(nseq):
        rows = slice(i * SEQ_PAD, (i + 1) * SEQ_PAD)
        gens.append(_sample_mix_seq(
            sinks_ref, proj_ref.at[rows, :], s0_ref.at[0, i], kbuf_ref.at[0, i], vbuf_ref.at[0, i],
            retg_ref, qg_ref, kg_ref, cm_ref, qdec_ref, kdec_ref,
            mix_ref.at[rows, :], snew_ref.at[slot, i], knew_ref.at[slot, i], vnew_ref.at[slot, i]))
    _run(_zip_stages(gens))


def _sample_mix(layer, proj, state_ret, cache_k, cache_v, prm, tables, prev_out, nseq=8):
    depth, B = state_ret.shape[:2]
    layer_in = [prm[k] for k in ("retg", "qg", "kg")]
    consts = layer_in + list(tables)
    st_spec = pl.BlockSpec((1, nseq, RET_HEADS, RET_DK, RET_DV), lambda i: (layer, i, 0, 0, 0))
    kv_spec = pl.BlockSpec((1, nseq, 2 * WINDOW, SWA_HEAD_DIM), lambda i: (layer, i, 0, 0))
    in_specs = ([pl.BlockSpec(memory_space=pltpu.SMEM),
                 pl.BlockSpec((nseq * SEQ_PAD, IN_WIDTH), lambda i: (i, 0)),
                 st_spec, kv_spec, kv_spec]
                + [_layer_spec(a, layer) for a in layer_in]
                + [_const_spec(a.shape) for a in tables]
                + [pl.BlockSpec(memory_space=pl.ANY) for _ in prev_out])
    out_shape = (
        jax.ShapeDtypeStruct((B * SEQ_PAD, 2 * GROUP_WIDTH), F32),
        jax.ShapeDtypeStruct(state_ret.shape, F32),
        jax.ShapeDtypeStruct(cache_k.shape, F32),
        jax.ShapeDtypeStruct(cache_v.shape, F32),
    )
    fill = not prev_out
    ld, li, slot = (DEPTH, 0, layer) if fill else (1, layer, 0)
    out_specs = (pl.BlockSpec((nseq * SEQ_PAD, 2 * GROUP_WIDTH), lambda i: (i, 0)),
                 pl.BlockSpec((ld, nseq, RET_HEADS, RET_DK, RET_DV), lambda i: (li, i, 0, 0, 0)),
                 pl.BlockSpec((ld, nseq, 2 * WINDOW, SWA_HEAD_DIM), lambda i: (li, i, 0, 0)),
                 pl.BlockSpec((ld, nseq, 2 * WINDOW, SWA_HEAD_DIM), lambda i: (li, i, 0, 0)))
    n_in = 5 + len(consts)
    aliases = {n_in + j: 1 + j for j in range(len(prev_out))}
    return pl.pallas_call(
        functools.partial(_sample_mix_kernel, nseq=nseq, layer=layer, slot=slot, fill=fill),
        grid=(B // nseq,),
        in_specs=in_specs,
        out_specs=out_specs,
        out_shape=out_shape,
        input_output_aliases=aliases,
        compiler_params=pltpu.CompilerParams(
            dimension_semantics=("arbitrary",), vmem_limit_bytes=VMEM_LIMIT),
        name="sample_mix",
    )(prm["sinks"], proj, state_ret, cache_k, cache_v, *consts, *prev_out)


def _sample_out_kernel(x_ref, xb_ref, gb_ref, mix_ref, conv8_ref, h8_ref, convw_ref, convb_ref,
                       wr_ref, wi_ref, br_ref, bi_ref, lam_ref, wout_ref,
                       y_ref, convo_ref, ho_ref):
    R = x_ref.shape[0]
    row = lax.broadcasted_iota(jnp.int32, (R, LRU_WIDTH), 0) & (SEQ_PAD - 1)
    xc = jnp.where(row < TOK0, conv8_ref[...], xb_ref[...])
    convo_ref[...] = xc
    conv = convb_ref[...] + convw_ref[0:1, :] * pltpu.roll(xc, 3, axis=0)
    conv = conv + convw_ref[1:2, :] * pltpu.roll(xc, 2, axis=0)
    conv = conv + convw_ref[2:3, :] * pltpu.roll(xc, 1, axis=0)
    conv = conv + convw_ref[3:4, :] * xc
    a, mult, ig = _lru_gates(conv, wr_ref, wi_ref, br_ref, bi_ref, lam_ref)
    u = mult * ig * conv
    h = h8_ref[...]
    for s in range(4):
        h = jnp.where(row == TOK0 + s, a * pltpu.roll(h, 1, axis=0) + u, h)
    ho_ref[...] = h
    ob = h * _silu(gb_ref[...])
    y = x_ref[...] + _dot(mix_ref[:, 0:GROUP_WIDTH].astype(BF16), wout_ref[0:GROUP_WIDTH, :])
    y = y + _dot(ob.astype(BF16), wout_ref[GROUP_WIDTH:2 * GROUP_WIDTH, :])
    y = y + _dot(mix_ref[:, GROUP_WIDTH:2 * GROUP_WIDTH].astype(BF16), wout_ref[2 * GROUP_WIDTH:, :])
    y_ref[...] = y


def _sample_out(layer, xp, proj, mix, conv8, h8, prm, rows=256):
    R = xp.shape[0]
    consts = [prm[k] for k in ("convw", "convb", "wr", "wi", "br", "bi", "lam", "wout")]
    in_specs = ([pl.BlockSpec((rows, D_MODEL), lambda i: (i, 0)),
                 pl.BlockSpec((rows, LRU_WIDTH), lambda i: (i, XB // LRU_WIDTH)),
                 pl.BlockSpec((rows, LRU_WIDTH), lambda i: (i, GB // LRU_WIDTH)),
                 pl.BlockSpec((rows, 2 * GROUP_WIDTH), lambda i: (i, 0)),
                 pl.BlockSpec((rows, LRU_WIDTH), lambda i: (i, 0)),
                 pl.BlockSpec((rows, LRU_WIDTH), lambda i: (i, 0))]
                + [_layer_spec(a, layer) for a in consts])
    out_shape = (
        jax.ShapeDtypeStruct((R, D_MODEL), F32),
        jax.ShapeDtypeStruct((R, LRU_WIDTH), F32),
        jax.ShapeDtypeStruct((R, LRU_WIDTH), F32),
    )
    out_specs = (
        pl.BlockSpec((rows, D_MODEL), lambda i: (i, 0)),
        pl.BlockSpec((rows, LRU_WIDTH), lambda i: (i, 0)),
        pl.BlockSpec((rows, LRU_WIDTH), lambda i: (i, 0)),
    )
    return pl.pallas_call(
        _sample_out_kernel,
        grid=(R // rows,),
        in_specs=in_specs,
        out_specs=out_specs,
        out_shape=out_shape,
        compiler_params=pltpu.CompilerParams(
            dimension_semantics=("arbitrary",), vmem_limit_bytes=VMEM_LIMIT),
        name="sample_out",
    )(xp, proj, proj, mix, conv8, h8, *consts)


def _prompt_tables():
    C = RET_CHUNK
    idx = np.arange(C, dtype=np.float32)
    diff = idx[:, None] - idx[None, :]
    causal = diff >= 0
    lg = _LOG_G[:, None, None]
    dmask = np.where(causal[None], np.exp(np.where(causal, diff, 0.0)[None] * lg), 0.0).astype(np.float32)
    qdec = np.exp((idx + 1.0)[None, :] * _LOG_G[:, None]).astype(np.float32)
    kdec = np.exp((C - 1 - idx)[None, :] * _LOG_G[:, None]).astype(np.float32)
    qdec = np.broadcast_to(qdec[:, :, None], (RET_HEADS, C, RET_DK)).copy()
    kdec = np.broadcast_to(kdec[:, :, None], (RET_HEADS, C, RET_DK)).copy()
    return jnp.asarray(dmask), jnp.asarray(qdec), jnp.asarray(kdec)


def _sample_tables():
    P = SEQ_PAD
    rows = np.arange(P, dtype=np.float32)
    i = rows - TOK0
    tok = (i >= 0) & (i < 4)
    cm = np.zeros((RET_HEADS, 4, P, RET_DV), np.float32)
    qdec = np.zeros((RET_HEADS, P, RET_DK), np.float32)
    kdec = np.zeros((RET_HEADS, P, RET_DK), np.float32)
    for h in range(RET_HEADS):
        for s in range(4):
            d = i - s
            col = np.where(tok & (d >= 0), np.exp(np.where(d >= 0, d, 0.0) * _LOG_G[h]), 0.0)
            cm[h, s] = col[:, None]
        qdec[h] = np.where(tok, np.exp((i + 1.0) * _LOG_G[h]), 0.0)[:, None]
        kdec[h] = np.where(tok, np.exp((3.0 - i) * _LOG_G[h]), 0.0)[:, None]
    return jnp.asarray(cm), jnp.asarray(qdec), jnp.asarray(kdec)


def _block_diag(w):
    L, n, d, _ = w.shape
    g = n // 2
    w = w.reshape(L, 2, g, d, d)
    eye = jnp.eye(g, dtype=w.dtype)
    return (eye[None, None, :, None, :, None] * w[:, :, :, :, None, :]).reshape(L, 2, g * d, g * d)


def kernel(x_prompt, x_sample, state_ret, state_lru, state_conv, cache_swa_k, cache_swa_v, norm_g, w_in, w_out, ret_norm_g, conv_w, conv_b, w_rgate, b_rgate, w_igate, b_igate, lru_lambda, q_norm_g, k_norm_g, attn_sinks):
    Bs, Ts, _ = x_sample.shape
    ptab = _prompt_tables()
    stab = _sample_tables()
    yp = x_prompt
    ys = jnp.pad(x_sample, ((0, 0), (TOK0, SEQ_PAD - TOK0 - Ts), (0, 0))).reshape(Bs * SEQ_PAD, D_MODEL)
    cache_k = cache_swa_k.reshape(DEPTH, Bs, 2 * WINDOW, SWA_HEAD_DIM)
    cache_v = cache_swa_v.reshape(DEPTH, Bs, 2 * WINDOW, SWA_HEAD_DIM)
    Bp = x_prompt.shape[0]
    p_state, s_state = (), ()
    s_h, s_conv = [], []
    row = lambda a: a.reshape(DEPTH, 1, -1)
    prm = dict(
        sinks=attn_sinks, ng=row(norm_g), win=w_in.astype(BF16), wout=w_out.astype(BF16), retg=row(ret_norm_g),
        convw=conv_w, convb=row(conv_b), wr=_block_diag(w_rgate).astype(BF16), wi=_block_diag(w_igate).astype(BF16),
        br=row(b_rgate), bi=row(b_igate), lam=row(lru_lambda), qg=row(q_norm_g), kg=row(k_norm_g))
    for l in range(DEPTH):
        yp, *p_state = _prompt_layer(l, yp, prm, ptab, tuple(p_state))

        proj = _sample_proj(l, ys, prm)
        mix, *s_state = _sample_mix(l, proj, state_ret, cache_k, cache_v, prm, stab, tuple(s_state))
        conv8 = jnp.pad(state_conv[l], ((0, 0), (0, SEQ_PAD - (CONV_W - 1)), (0, 0))).reshape(Bs * SEQ_PAD, LRU_WIDTH)
        h8 = jnp.pad(state_lru[l][:, None, :], ((0, 0), (TOK0 - 1, SEQ_PAD - TOK0), (0, 0))).reshape(Bs * SEQ_PAD, LRU_WIDTH)
        ys, convo, ho = _sample_out(l, ys, proj, mix, conv8, h8, prm)
        s_h.append(ho.reshape(Bs, SEQ_PAD, LRU_WIDTH)[:, TOK0 + Ts - 1])
        s_conv.append(convo.reshape(Bs, SEQ_PAD, LRU_WIDTH)[:, TOK0 + Ts - (CONV_W - 1):TOK0 + Ts])

    p_ret, p_h, p_conv, p_k, p_v = p_state
    s_ret, s_k, s_v = s_state
    kv5 = lambda a, n: a.reshape(DEPTH, n, WINDOW, SWA_KV_HEADS, SWA_HEAD_DIM)
    y_sample = ys.reshape(Bs, SEQ_PAD, D_MODEL)[:, TOK0:TOK0 + Ts]
    return (yp, y_sample,
            p_ret, p_h.reshape(DEPTH, Bp, LRU_WIDTH), p_conv, kv5(p_k, Bp), kv5(p_v, Bp),
            s_ret, jnp.stack(s_h), jnp.stack(s_conv), kv5(s_k, Bs), kv5(s_v, Bs))
```

```python
import functools
import math

import numpy as np
import jax
import jax.numpy as jnp
from jax import lax
from jax.experimental import pallas as pl
from jax.experimental.pallas import tpu as pltpu

D_MODEL = 1024
DEPTH = 2
PAST_LEN = 16384
GROUP_WIDTH = 512
RET_HEADS = 4
RET_DK = 128
RET_DV = 128
RET_CHUNK = 128
LRU_WIDTH = 512
LRU_BLOCKS = 8
LRU_C = 8.0
CONV_W = 4
SWA_HEADS = 4
SWA_KV_HEADS = 2
SWA_GROUP = 2
SWA_HEAD_DIM = 128
WINDOW = 128
NORM_EPS = 1e-6
NEG_INF = -1e30

IN_WIDTH = 4608
MIX_WIDTH = 1536
QA, KA, VA, GA, XB, GB, QC, KC, VC, GC = 0, 512, 1024, 1536, 2048, 2560, 3072, 3584, 3840, 4096

F32 = jnp.float32
BF16 = jnp.bfloat16

SUBLANES = 8
SEQ_PAD = 8
TOK0 = 3
VMEM_LIMIT = 56 * 1024 * 1024

_LOG_G = np.log1p(-np.power(np.float32(2.0), (-5.0 - np.arange(RET_HEADS)).astype(np.float32))).astype(np.float32)
_SLOPES = [2.0 ** (-8.0 * (h + 1) / SWA_HEADS) for h in range(SWA_HEADS)]


def _rms(x, g):
    ms = jnp.mean(x * x, axis=-1, keepdims=True)
    return x * lax.rsqrt(ms + NORM_EPS) * g


def _silu(x):
    return x * jax.nn.sigmoid(x)


def _softplus(x):
    return jnp.maximum(x, 0.0) + jnp.log1p(jnp.exp(-jnp.abs(x)))


def _dot(a, b):
    return jnp.dot(a, b, preferred_element_type=F32)


def _dot_nt(a, b):
    return lax.dot_general(a, b, (((1,), (1,)), ((), ())), preferred_element_type=F32)


def _dot_tn(a, b):
    return lax.dot_general(a, b, (((0,), (0,)), ((), ())), preferred_element_type=F32)


def _lru_gates(conv, wr_ref, wi_ref, br_ref, bi_ref, lam_ref):
    gin = conv.astype(BF16)
    half = LRU_WIDTH // 2
    lo, hi = gin[:, :half], gin[:, half:]
    r = jax.nn.sigmoid(jnp.concatenate([_dot(lo, wr_ref[0]), _dot(hi, wr_ref[1])], axis=1) + br_ref[...])
    i = jax.nn.sigmoid(jnp.concatenate([_dot(lo, wi_ref[0]), _dot(hi, wi_ref[1])], axis=1) + bi_ref[...])
    log_a = -LRU_C * r * _softplus(-lam_ref[...])
    a = jnp.exp(log_a)
    th = jnp.tanh(log_a)
    m2 = -2.0 * th / (1.0 - th)
    mult = jnp.where(m2 > 0.0, m2 * lax.rsqrt(m2), 0.0)
    return a, mult, i


_DONE = object()


def _run(gen):
    for _ in gen:
        pass


def _zip_stages(gens):
    gens = list(gens)
    while gens:
        gens = [g for g in gens if next(g, _DONE) is not _DONE]
        if gens:
            yield


def _chain(gens):
    for g in gens:
        yield from g


def _interleave(main, sides, n_main, n_side):
    side = _chain(sides)
    next(side)
    done_side = 1
    for i, _ in enumerate(main):
        want = ((i + 1) * n_side) // n_main
        while done_side < want and next(side, _DONE) is not _DONE:
            done_side += 1
    _run(side)


IN_PROJ_COLS = 256
IN_PROJ_STAGES = 1 + IN_WIDTH // IN_PROJ_COLS


def _in_proj(x_ref, ng_ref, win_ref, proj_ref):
    hb = _rms(x_ref[...], ng_ref[...]).astype(BF16)
    yield
    for j in range(IN_WIDTH // IN_PROJ_COLS):
        cols = slice(j * IN_PROJ_COLS, (j + 1) * IN_PROJ_COLS)
        proj_ref[:, cols] = _dot(hb, win_ref[:, cols])
        yield


OUT_PROJ_STAGES = D_MODEL // 256


def _out_proj(mixed_s, x_ref, y_ref, wout_ref):
    mixed = mixed_s[...]
    for j in range(OUT_PROJ_STAGES):
        cols = slice(j * 256, (j + 1) * 256)
        y_ref[:, cols] = x_ref[:, cols] + _dot(mixed, wout_ref[:, cols])
        yield


def _mix_stages(tt):
    return 4 + tt // SUBLANES // 8 + (tt // RET_CHUNK) * 5


def _scan_rows(a, u, h_in, tt):
    G = tt // SUBLANES
    W = a.shape[-1]
    a3 = a.reshape(G, SUBLANES, W)
    u3 = u.reshape(G, SUBLANES, W)
    r3 = lax.broadcasted_iota(jnp.int32, (G, SUBLANES, W), 1)
    sh = 1
    while sh < SUBLANES:
        keep = r3 >= sh
        a_sh = jnp.where(keep, pltpu.roll(a3, sh, axis=1), 1.0)
        u_sh = jnp.where(keep, pltpu.roll(u3, sh, axis=1), 0.0)
        u3 = a3 * u_sh + u3
        a3 = a3 * a_sh
        sh *= 2
    yield
    hs = []
    for g in range(G):
        hg = a3[g] * h_in + u3[g]
        hs.append(hg)
        h_in = hg[SUBLANES - 1:SUBLANES, :]
        if g % 8 == 7:
            yield
    return jnp.concatenate(hs, axis=0)


def _retention_head(h, proj_s, r0, s_ref, retg_ref, dmask_ref, qdec_ref, kdec_ref, mixed_s):
    C = RET_CHUNK
    cs = slice(h * RET_DK, (h + 1) * RET_DK)
    q = proj_s[r0:r0 + C, QA + h * 128:QA + (h + 1) * 128]
    k = proj_s[r0:r0 + C, KA + h * 128:KA + (h + 1) * 128] * (RET_DK ** -0.5)
    vb = proj_s[r0:r0 + C, VA + h * 128:VA + (h + 1) * 128].astype(BF16)
    sc = _dot_nt(q.astype(BF16), k.astype(BF16)) * dmask_ref[h]
    yield
    s_prev = s_ref[0, h]
    lhs = jnp.concatenate([sc.astype(BF16), (q * qdec_ref[h]).astype(BF16)], axis=1)
    rhs = jnp.concatenate([vb, s_prev.astype(BF16)], axis=0)
    o = _dot(lhs, rhs)
    yield
    kv = _dot_tn((k * kdec_ref[h]).astype(BF16), vb)
    s_ref[0, h] = float(np.exp(np.float32(C) * _LOG_G[h])) * s_prev + kv
    yield
    o = _rms(o, retg_ref[:, cs])
    mixed_s[r0:r0 + C, cs] = (o * _silu(proj_s[r0:r0 + C, GA + h * 128:GA + (h + 1) * 128])).astype(BF16)
    yield


def _swa_kv_head(kh, proj_s, r0, blk, sinks_ref, qg_ref, kg_ref, mixed_s, kprev_s, vprev_s):
    C = RET_CHUNK
    ks = slice(kh * 128, (kh + 1) * 128)
    h0, h1 = kh * SWA_GROUP, kh * SWA_GROUP + 1
    kn = _rms(proj_s[r0:r0 + C, KC + kh * 128:KC + (kh + 1) * 128], kg_ref[...])
    vv = proj_s[r0:r0 + C, VC + kh * 128:VC + (kh + 1) * 128]
    kband = jnp.concatenate([kprev_s[:, ks], kn], axis=0).astype(BF16)
    vband = jnp.concatenate([vprev_s[:, ks], vv], axis=0).astype(BF16)
    kprev_s[:, ks] = kn
    vprev_s[:, ks] = vv
    q0 = _rms(proj_s[r0:r0 + C, QC + h0 * 128:QC + (h0 + 1) * 128], qg_ref[...])
    q1 = _rms(proj_s[r0:r0 + C, QC + h1 * 128:QC + (h1 + 1) * 128], qg_ref[...])
    qq = jnp.concatenate([q0, q1], axis=0).astype(BF16)
    yield
    s = _dot_nt(qq, kband) * (SWA_HEAD_DIM ** -0.5)
    ii = lax.broadcasted_iota(jnp.int32, (2 * C, 2 * C), 0)
    jj = lax.broadcasted_iota(jnp.int32, (2 * C, 2 * C), 1)
    dist = jnp.where(ii >= C, ii - C, ii) + C - jj
    valid = jnp.logical_and(jnp.logical_and(dist >= 0, dist < WINDOW), jj >= jnp.where(blk > 0, 0, C))
    slope = jnp.where(ii >= C, _SLOPES[h1], _SLOPES[h0])
    s = jnp.where(valid, s - slope * dist.astype(F32), NEG_INF)
    yield
    rowc = lax.broadcasted_iota(jnp.int32, (2 * C, 1), 0)
    sink = jnp.where(rowc >= C, sinks_ref[h1], sinks_ref[h0])
    m = jnp.maximum(jnp.max(s, axis=-1, keepdims=True), sink)
    p = jnp.exp(s - m)
    denom = jnp.sum(p, axis=-1, keepdims=True) + jnp.exp(sink - m)
    yield
    o = _dot((p / denom).astype(BF16), vband)
    yield
    for g, hh in enumerate((h0, h1)):
        gc = proj_s[r0:r0 + C, GC + hh * 128:GC + (hh + 1) * 128]
        mixed_s[r0:r0 + C, 2 * GROUP_WIDTH + hh * 128:2 * GROUP_WIDTH + (hh + 1) * 128] = (
            (o[g * C:(g + 1) * C] * _silu(gc)).astype(BF16))
    yield


def _mix_tile(proj_s, t, tt, sinks_ref, retg_ref, convw_ref, convb_ref,
              wr_ref, wi_ref, br_ref, bi_ref, lam_ref, qg_ref, kg_ref, dmask_ref, qdec_ref, kdec_ref,
              s_ref, mixed_s, xtail_s, kprev_s, vprev_s, hc_s):
    nch = tt // RET_CHUNK
    C = RET_CHUNK

    xb = proj_s[:, XB:XB + LRU_WIDTH]
    xfull = jnp.concatenate([xtail_s[...], xb], axis=0)
    xtail_s[...] = xb[tt - SUBLANES:tt, :]
    conv = convb_ref[...] + convw_ref[0:1, :] * pltpu.roll(xfull, 3, axis=0)[SUBLANES:]
    conv = conv + convw_ref[1:2, :] * pltpu.roll(xfull, 2, axis=0)[SUBLANES:]
    conv = conv + convw_ref[2:3, :] * pltpu.roll(xfull, 1, axis=0)[SUBLANES:]
    conv = conv + convw_ref[3:4, :] * xb
    yield

    a, mult, ig = _lru_gates(conv, wr_ref, wi_ref, br_ref, bi_ref, lam_ref)
    yield
    row = lax.broadcasted_iota(jnp.int32, (tt, LRU_WIDTH), 0)
    mult = jnp.where(row == jnp.where(t == 0, 0, -1), 1.0, mult)
    u = mult * ig * conv
    hseq = yield from _scan_rows(a, u, hc_s[...], tt)
    hc_s[...] = hseq[tt - 1:tt, :]
    mixed_s[:, GROUP_WIDTH:2 * GROUP_WIDTH] = (hseq * _silu(proj_s[:, GB:GB + LRU_WIDTH])).astype(BF16)
    yield

    for c in range(nch):
        r0 = c * C
        blk = t * nch + c
        yield from _zip_stages(
            [_retention_head(h, proj_s, r0, s_ref, retg_ref, dmask_ref, qdec_ref, kdec_ref, mixed_s)
             for h in range(RET_HEADS)]
            + [_swa_kv_head(kh, proj_s, r0, blk, sinks_ref, qg_ref, kg_ref, mixed_s, kprev_s, vprev_s)
               for kh in range(SWA_KV_HEADS)])


def _prompt_kernel(sinks_ref, xa_ref, xn_ref, ng_ref, win_ref, wout_ref, retg_ref, convw_ref, convb_ref,
                   wr_ref, wi_ref, br_ref, bi_ref, lam_ref, qg_ref, kg_ref,
                   dmask_ref, qdec_ref, kdec_ref, *rest, tt, npairs, layer, slot, fill):
    y_ref, s_ref, h_ref, conv_ref, kout_ref, vout_ref = rest[-14:-8]
    proj_a, proj_b, mixed_a, mixed_b, xtail_s, kprev_s, vprev_s, hc_s = rest[-8:]
    sinks_ref = sinks_ref.at[layer]
    b = pl.program_id(0)
    p = pl.program_id(1)

    @pl.when(p == 0)
    def _init():
        s_ref[...] = jnp.zeros_like(s_ref)
        xtail_s[...] = jnp.zeros_like(xtail_s)
        kprev_s[...] = jnp.zeros_like(kprev_s)
        vprev_s[...] = jnp.zeros_like(vprev_s)
        hc_s[...] = jnp.zeros_like(hc_s)

    xa0, xa1 = xa_ref.at[0, 0:tt, :], xa_ref.at[0, tt:2 * tt, :]
    ya0, ya1 = y_ref.at[0, 0:tt, :], y_ref.at[0, tt:2 * tt, :]

    @pl.when(jnp.logical_and(b == 0, p == 0))
    def _prologue():
        _run(_in_proj(xa0, ng_ref, win_ref, proj_a))

    common = (sinks_ref, retg_ref, convw_ref, convb_ref, wr_ref, wi_ref, br_ref, bi_ref,
              lam_ref, qg_ref, kg_ref, dmask_ref, qdec_ref, kdec_ref, s_ref.at[slot])
    state = (xtail_s, kprev_s, vprev_s, hc_s)

    n_main = _mix_stages(tt)
    _interleave(_mix_tile(proj_a, 2 * p, tt, *common, mixed_a, *state),
                [_in_proj(xa1, ng_ref, win_ref, proj_b)], n_main, IN_PROJ_STAGES)
    _interleave(_mix_tile(proj_b, 2 * p + 1, tt, *common, mixed_b, *state),
                [_out_proj(mixed_a, xa0, ya0, wout_ref), _in_proj(xn_ref.at[0], ng_ref, win_ref, proj_a)],
                n_main, OUT_PROJ_STAGES + IN_PROJ_STAGES)
    _run(_out_proj(mixed_b, xa1, ya1, wout_ref))

    @pl.when(p == npairs - 1)
    def _state_out():
        h_ref[slot, 0] = hc_s[...]
        conv_ref[slot, 0] = xtail_s[SUBLANES - (CONV_W - 1):SUBLANES, :]
        for kh in range(SWA_KV_HEADS):
            ks = slice(kh * 128, (kh + 1) * 128)
            kout_ref[slot, 0, pl.ds(kh, WINDOW, stride=SWA_KV_HEADS), :] = kprev_s[:, ks]
            vout_ref[slot, 0, pl.ds(kh, WINDOW, stride=SWA_KV_HEADS), :] = vprev_s[:, ks]
        if fill:
            for other in [o for o in range(DEPTH) if o != slot]:
                for ref in (h_ref, conv_ref, kout_ref, vout_ref):
                    ref[other] = jnp.zeros(ref.shape[1:], F32)


def _const_spec(shape):
    nd = len(shape)
    return pl.BlockSpec(shape, lambda *_: (0,) * nd)


def _resident_spec(shape):
    nd = len(shape)
    return pl.BlockSpec(shape, lambda *_: (0,) * nd, pipeline_mode=pl.Buffered(1))


def _layer_spec(a, layer, resident=False):
    nd = a.ndim
    kw = dict(pipeline_mode=pl.Buffered(1)) if resident else {}
    return pl.BlockSpec((None,) + a.shape[1:], lambda *_: (layer,) + (0,) * (nd - 1), **kw)


def _prompt_layer(layer, x, prm, tables, prev_out, tt=256):
    B, T, D = x.shape
    npairs = T // (2 * tt)
    layer_in = [prm[k] for k in ("ng", "win", "wout", "retg", "convw", "convb", "wr", "wi", "br", "bi",
                                 "lam", "qg", "kg")]
    vec_in = layer_in + list(tables)

    def next_tile(b, p):
        last = p == npairs - 1
        return (jnp.where(last, jnp.minimum(b + 1, B - 1), b), jnp.where(last, 0, 2 * p + 2), 0)

    in_specs = ([pl.BlockSpec(memory_space=pltpu.SMEM),
                 pl.BlockSpec((1, 2 * tt, D), lambda b, p: (b, p, 0)),
                 pl.BlockSpec((1, tt, D), next_tile)]
                + [_layer_spec(a, layer, resident=True) for a in layer_in]
                + [_resident_spec(a.shape) for a in tables]
                + [pl.BlockSpec(memory_space=pl.ANY) for _ in prev_out])
    out_shape = (
        jax.ShapeDtypeStruct((B, T, D), F32),
        jax.ShapeDtypeStruct((DEPTH, B, RET_HEADS, RET_DK, RET_DV), F32),
        jax.ShapeDtypeStruct((DEPTH, B, 1, LRU_WIDTH), F32),
        jax.ShapeDtypeStruct((DEPTH, B, CONV_W - 1, LRU_WIDTH), F32),
        jax.ShapeDtypeStruct((DEPTH, B, 2 * WINDOW, SWA_HEAD_DIM), F32),
        jax.ShapeDtypeStruct((DEPTH, B, 2 * WINDOW, SWA_HEAD_DIM), F32),
    )
    fill = not prev_out
    ld, li, slot = (DEPTH, 0, layer) if fill else (1, layer, 0)
    out_specs = (
        pl.BlockSpec((1, 2 * tt, D), lambda b, p: (b, p, 0)),
        pl.BlockSpec((ld, 1, RET_HEADS, RET_DK, RET_DV), lambda b, p: (li, b, 0, 0, 0)),
        pl.BlockSpec((ld, 1, 1, LRU_WIDTH), lambda b, p: (li, b, 0, 0)),
        pl.BlockSpec((ld, 1, CONV_W - 1, LRU_WIDTH), lambda b, p: (li, b, 0, 0)),
        pl.BlockSpec((ld, 1, 2 * WINDOW, SWA_HEAD_DIM), lambda b, p: (li, b, 0, 0)),
        pl.BlockSpec((ld, 1, 2 * WINDOW, SWA_HEAD_DIM), lambda b, p: (li, b, 0, 0)),
    )
    n_in = 3 + len(vec_in)
    aliases = {n_in + j: 1 + j for j in range(len(prev_out))}
    scratch = [
        pltpu.VMEM((tt, IN_WIDTH), F32),
        pltpu.VMEM((tt, IN_WIDTH), F32),
        pltpu.VMEM((tt, MIX_WIDTH), BF16),
        pltpu.VMEM((tt, MIX_WIDTH), BF16),
        pltpu.VMEM((SUBLANES, LRU_WIDTH), F32),
        pltpu.VMEM((WINDOW, 256), F32),
        pltpu.VMEM((WINDOW, 256), F32),
        pltpu.VMEM((1, LRU_WIDTH), F32),
    ]
    return pl.pallas_call(
        functools.partial(_prompt_kernel, tt=tt, npairs=npairs, layer=layer, slot=slot, fill=fill),
        grid=(B, npairs),
        in_specs=in_specs,
        out_specs=out_specs,
        out_shape=out_shape,
        scratch_shapes=scratch,
        input_output_aliases=aliases,
        compiler_params=pltpu.CompilerParams(
            dimension_semantics=("arbitrary", "arbitrary"), vmem_limit_bytes=VMEM_LIMIT),
        name="prompt_layer",
    )(prm["sinks"], x, x, *vec_in, *prev_out)


def _sproj_kernel(x_ref, ng_ref, win_ref, o_ref, hb_s):
    @pl.when(pl.program_id(0) == 0)
    def _norm_once():
        hb_s[...] = _rms(x_ref[...], ng_ref[...]).astype(BF16)

    o_ref[...] = _dot(hb_s[...], win_ref[...])


def _sample_proj(layer, xp, prm):
    R = xp.shape[0]
    nb = IN_WIDTH // 512
    ng, win = prm["ng"], prm["win"]
    return pl.pallas_call(
        _sproj_kernel,
        grid=(nb,),
        in_specs=[_const_spec(xp.shape), _layer_spec(ng, layer),
                  pl.BlockSpec((None, D_MODEL, 512), lambda j: (layer, 0, j))],
        out_specs=pl.BlockSpec((R, 512), lambda j: (0, j)),
        out_shape=jax.ShapeDtypeStruct((R, IN_WIDTH), F32),
        scratch_shapes=[pltpu.VMEM((R, D_MODEL), BF16)],
        compiler_params=pltpu.CompilerParams(
            dimension_semantics=("arbitrary",), vmem_limit_bytes=VMEM_LIMIT),
        name="sample_proj",
    )(xp, ng, win)


def _sample_mix_seq(sinks_ref, proj_ref, s0_ref, kbuf_ref, vbuf_ref, retg_ref, qg_ref, kg_ref,
                    cm_ref, qdec_ref, kdec_ref, mix_ref, snew_ref, knew_ref, vnew_ref):
    P = SEQ_PAD
    for h in range(RET_HEADS):
        cs = slice(h * 128, (h + 1) * 128)
        q = proj_ref[:, QA + h * 128:QA + (h + 1) * 128]
        k = proj_ref[:, KA + h * 128:KA + (h + 1) * 128] * (RET_DK ** -0.5)
        v = proj_ref[:, VA + h * 128:VA + (h + 1) * 128]
        intra = jnp.zeros((P, RET_DV), F32)
        for s in range(4):
            r_ = TOK0 + s
            w = jnp.sum(q * k[r_:r_ + 1, :], axis=-1, keepdims=True)
            intra = intra + (w * cm_ref[h, s]) * v[r_:r_ + 1, :]
        s_prev = s0_ref[h]
        cross = _dot((q * qdec_ref[h]).astype(BF16), s_prev.astype(BF16))
        kv = _dot_tn((k * kdec_ref[h]).astype(BF16), v.astype(BF16))
        yield
        snew_ref[h] = float(np.exp(np.float32(4.0) * _LOG_G[h])) * s_prev + kv
        o = _rms(intra + cross, retg_ref[:, cs])
        mix_ref[:, cs] = o * _silu(proj_ref[:, GA + h * 128:GA + (h + 1) * 128])
        yield

    r16 = lax.broadcasted_iota(jnp.int32, (2 * P, WINDOW), 0)
    j16 = lax.broadcasted_iota(jnp.int32, (2 * P, WINDOW), 1)
    rr = jnp.where(r16 >= P, r16 - P, r16)
    row_ok = jnp.logical_and(rr >= TOK0, rr < TOK0 + 4)
    dist = (rr - TOK0) + WINDOW - j16
    distf = dist.astype(F32)
    valid = jnp.logical_and(jnp.logical_and(dist >= 0, dist < WINDOW), row_ok)
    r16c = lax.broadcasted_iota(jnp.int32, (2 * P, 1), 0)
    rrc = jnp.where(r16c >= P, r16c - P, r16c)
    rowc_ok = jnp.logical_and(rrc >= TOK0, rrc < TOK0 + 4)
    knew_ref[0:2 * (WINDOW - 4), :] = kbuf_ref[2 * 4:2 * WINDOW, :]
    vnew_ref[0:2 * (WINDOW - 4), :] = vbuf_ref[2 * 4:2 * WINDOW, :]
    for kh in range(SWA_KV_HEADS):
        h0, h1 = kh * SWA_GROUP, kh * SWA_GROUP + 1
        kb = kbuf_ref[pl.ds(kh, WINDOW, stride=SWA_KV_HEADS), :]
        vb = vbuf_ref[pl.ds(kh, WINDOW, stride=SWA_KV_HEADS), :]
        kn = _rms(proj_ref[:, KC + kh * 128:KC + (kh + 1) * 128], kg_ref[...])
        vn = proj_ref[:, VC + kh * 128:VC + (kh + 1) * 128]
        q0 = _rms(proj_ref[:, QC + h0 * 128:QC + (h0 + 1) * 128], qg_ref[...])
        q1 = _rms(proj_ref[:, QC + h1 * 128:QC + (h1 + 1) * 128], qg_ref[...])
        qq = jnp.concatenate([q0, q1], axis=0)
        slope = jnp.where(r16 >= P, _SLOPES[h1], _SLOPES[h0])
        slopec = jnp.where(r16c >= P, _SLOPES[h1], _SLOPES[h0])
        sb = _dot_nt(qq.astype(BF16), kb.astype(BF16)) * (SWA_HEAD_DIM ** -0.5)
        sb = jnp.where(valid, sb - slope * distf, NEG_INF)
        yield
        sink = jnp.where(r16c >= P, sinks_ref[h1], sinks_ref[h0])
        m = jnp.maximum(jnp.max(sb, axis=-1, keepdims=True), sink)
        wn = []
        for s in range(4):
            r_ = TOK0 + s
            w = jnp.sum(qq * kn[r_:r_ + 1, :], axis=-1, keepdims=True) * (SWA_HEAD_DIM ** -0.5)
            dn = rrc - r_
            w = jnp.where(jnp.logical_and(dn >= 0, rowc_ok), w - slopec * dn.astype(F32), NEG_INF)
            wn.append(w)
            m = jnp.maximum(m, w)
        pb = jnp.exp(sb - m)
        denom = jnp.sum(pb, axis=-1, keepdims=True) + jnp.exp(sink - m)
        pn = [jnp.exp(w - m) for w in wn]
        for p_ in pn:
            denom = denom + p_
        yield
        o = _dot((pb / denom).astype(BF16), vb.astype(BF16))
        for s in range(4):
            r_ = TOK0 + s
            o = o + (pn[s] / denom) * vn[r_:r_ + 1, :]
        for g, hh in enumerate((h0, h1)):
            gc = proj_ref[:, GC + hh * 128:GC + (hh + 1) * 128]
            mix_ref[:, GROUP_WIDTH + hh * 128:GROUP_WIDTH + (hh + 1) * 128] = o[g * P:(g + 1) * P] * _silu(gc)
        for s in range(4):
            r_out = 2 * (WINDOW - 4 + s) + kh
            knew_ref[r_out:r_out + 1, :] = kn[TOK0 + s:TOK0 + s + 1, :]
            vnew_ref[r_out:r_out + 1, :] = vn[TOK0 + s:TOK0 + s + 1, :]
        yield


def _sample_mix_kernel(sinks_ref, proj_ref, s0_ref, kbuf_ref, vbuf_ref, retg_ref, qg_ref, kg_ref,
                       cm_ref, qdec_ref, kdec_ref, *rest, nseq, layer, slot, fill):
    mix_ref, snew_ref, knew_ref, vnew_ref = rest[-4:]
    sinks_ref = sinks_ref.at[layer]
    if fill:
        for other in [o for o in range(DEPTH) if o != slot]:
            for ref in (snew_ref, knew_ref, vnew_ref):
                ref[other] = jnp.zeros(ref.shape[1:], F32)
    gens = []
    for i in range(nseq):
        rows = slice(i * SEQ_PAD, (i + 1) * SEQ_PAD)
        gens.append(_sample_mix_seq(
            sinks_ref, proj_ref.at[rows, :], s0_ref.at[0, i], kbuf_ref.at[0, i], vbuf_ref.at[0, i],
            retg_ref, qg_ref, kg_ref, cm_ref, qdec_ref, kdec_ref,
            mix_ref.at[rows, :], snew_ref.at[slot, i], knew_ref.at[slot, i], vnew_ref.at[slot, i]))
    _run(_zip_stages(gens))


def _sample_mix(layer, proj, state_ret, cache_k, cache_v, prm, tables, prev_out, nseq=8):
    depth, B = state_ret.shape[:2]
    layer_in = [prm[k] for k in ("retg", "qg", "kg")]
    consts = layer_in + list(tables)
    st_spec = pl.BlockSpec((1, nseq, RET_HEADS, RET_DK, RET_DV), lambda i: (layer, i, 0, 0, 0))
    kv_spec = pl.BlockSpec((1, nseq, 2 * WINDOW, SWA_HEAD_DIM), lambda i: (layer, i, 0, 0))
    in_specs = ([pl.BlockSpec(memory_space=pltpu.SMEM),
                 pl.BlockSpec((nseq * SEQ_PAD, IN_WIDTH), lambda i: (i, 0)),
                 st_spec, kv_spec, kv_spec]
                + [_layer_spec(a, layer) for a in layer_in]
                + [_const_spec(a.shape) for a in tables]
                + [pl.BlockSpec(memory_space=pl.ANY) for _ in prev_out])
    out_shape = (
        jax.ShapeDtypeStruct((B * SEQ_PAD, 2 * GROUP_WIDTH), F32),
        jax.ShapeDtypeStruct(state_ret.shape, F32),
        jax.ShapeDtypeStruct(cache_k.shape, F32),
        jax.ShapeDtypeStruct(cache_v.shape, F32),
    )
    fill = not prev_out
    ld, li, slot = (DEPTH, 0, layer) if fill else (1, layer, 0)
    out_specs = (pl.BlockSpec((nseq * SEQ_PAD, 2 * GROUP_WIDTH), lambda i: (i, 0)),
                 pl.BlockSpec((ld, nseq, RET_HEADS, RET_DK, RET_DV), lambda i: (li, i, 0, 0, 0)),
                 pl.BlockSpec((ld, nseq, 2 * WINDOW, SWA_HEAD_DIM), lambda i: (li, i, 0, 0)),
                 pl.BlockSpec((ld, nseq, 2 * WINDOW, SWA_HEAD_DIM), lambda i: (li, i, 0, 0)))
    n_in = 5 + len(consts)
    aliases = {n_in + j: 1 + j for j in range(len(prev_out))}
    return pl.pallas_call(
        functools.partial(_sample_mix_kernel, nseq=nseq, layer=layer, slot=slot, fill=fill),
        grid=(B // nseq,),
        in_specs=in_specs,
        out_specs=out_specs,
        out_shape=out_shape,
        input_output_aliases=aliases,
        compiler_params=pltpu.CompilerParams(
            dimension_semantics=("arbitrary",), vmem_limit_bytes=VMEM_LIMIT),
        name="sample_mix",
    )(prm["sinks"], proj, state_ret, cache_k, cache_v, *consts, *prev_out)


def _sample_out_kernel(x_ref, xb_ref, gb_ref, mix_ref, conv8_ref, h8_ref, convw_ref, convb_ref,
                       wr_ref, wi_ref, br_ref, bi_ref, lam_ref, wout_ref,
                       y_ref, convo_ref, ho_ref):
    R = x_ref.shape[0]
    row = lax.broadcasted_iota(jnp.int32, (R, LRU_WIDTH), 0) & (SEQ_PAD - 1)
    xc = jnp.where(row < TOK0, conv8_ref[...], xb_ref[...])
    convo_ref[...] = xc
    conv = convb_ref[...] + convw_ref[0:1, :] * pltpu.roll(xc, 3, axis=0)
    conv = conv + convw_ref[1:2, :] * pltpu.roll(xc, 2, axis=0)
    conv = conv + convw_ref[2:3, :] * pltpu.roll(xc, 1, axis=0)
    conv = conv + convw_ref[3:4, :] * xc
    a, mult, ig = _lru_gates(conv, wr_ref, wi_ref, br_ref, bi_ref, lam_ref)
    u = mult * ig * conv
    h = h8_ref[...]
    for s in range(4):
        h = jnp.where(row == TOK0 + s, a * pltpu.roll(h, 1, axis=0) + u, h)
    ho_ref[...] = h
    ob = h * _silu(gb_ref[...])
    y = x_ref[...] + _dot(mix_ref[:, 0:GROUP_WIDTH].astype(BF16), wout_ref[0:GROUP_WIDTH, :])
    y = y + _dot(ob.astype(BF16), wout_ref[GROUP_WIDTH:2 * GROUP_WIDTH, :])
    y = y + _dot(mix_ref[:, GROUP_WIDTH:2 * GROUP_WIDTH].astype(BF16), wout_ref[2 * GROUP_WIDTH:, :])
    y_ref[...] = y


def _sample_out(layer, xp, proj, mix, conv8, h8, prm, rows=256):
    R = xp.shape[0]
    consts = [prm[k] for k in ("convw", "convb", "wr", "wi", "br", "bi", "lam", "wout")]
    in_specs = ([pl.BlockSpec((rows, D_MODEL), lambda i: (i, 0)),
                 pl.BlockSpec((rows, LRU_WIDTH), lambda i: (i, XB // LRU_WIDTH)),
                 pl.BlockSpec((rows, LRU_WIDTH), lambda i: (i, GB // LRU_WIDTH)),
                 pl.BlockSpec((rows, 2 * GROUP_WIDTH), lambda i: (i, 0)),
                 pl.BlockSpec((rows, LRU_WIDTH), lambda i: (i, 0)),
                 pl.BlockSpec((rows, LRU_WIDTH), lambda i: (i, 0))]
                + [_layer_spec(a, layer) for a in consts])
    out_shape = (
        jax.ShapeDtypeStruct((R, D_MODEL), F32),
        jax.ShapeDtypeStruct((R, LRU_WIDTH), F32),
        jax.ShapeDtypeStruct((R, LRU_WIDTH), F32),
    )
    out_specs = (
        pl.BlockSpec((rows, D_MODEL), lambda i: (i, 0)),
        pl.BlockSpec((rows, LRU_WIDTH), lambda i: (i, 0)),
        pl.BlockSpec((rows, LRU_WIDTH), lambda i: (i, 0)),
    )
    return pl.pallas_call(
        _sample_out_kernel,
        grid=(R // rows,),
        in_specs=in_specs,
        out_specs=out_specs,
        out_shape=out_shape,
        compiler_params=pltpu.CompilerParams(
            dimension_semantics=("arbitrary",), vmem_limit_bytes=VMEM_LIMIT),
        name="sample_out",
    )(xp, proj, proj, mix, conv8, h8, *consts)


def _prompt_tables():
    C = RET_CHUNK
    idx = np.arange(C, dtype=np.float32)
    diff = idx[:, None] - idx[None, :]
    causal = diff >= 0
    lg = _LOG_G[:, None, None]
    dmask = np.where(causal[None], np.exp(np.where(causal, diff, 0.0)[None] * lg), 0.0).astype(np.float32)
    qdec = np.exp((idx + 1.0)[None, :] * _LOG_G[:, None]).astype(np.float32)
    kdec = np.exp((C - 1 - idx)[None, :] * _LOG_G[:, None]).astype(np.float32)
    qdec = np.broadcast_to(qdec[:, :, None], (RET_HEADS, C, RET_DK)).copy()
    kdec = np.broadcast_to(kdec[:, :, None], (RET_HEADS, C, RET_DK)).copy()
    return jnp.asarray(dmask), jnp.asarray(qdec), jnp.asarray(kdec)


def _sample_tables():
    P = SEQ_PAD
    rows = np.arange(P, dtype=np.float32)
    i = rows - TOK0
    tok = (i >= 0) & (i < 4)
    cm = np.zeros((RET_HEADS, 4, P, RET_DV), np.float32)
    qdec = np.zeros((RET_HEADS, P, RET_DK), np.float32)
    kdec = np.zeros((RET_HEADS, P, RET_DK), np.float32)
    for h in range(RET_HEADS):
        for s in range(4):
            d = i - s
            col = np.where(tok & (d >= 0), np.exp(np.where(d >= 0, d, 0.0) * _LOG_G[h]), 0.0)
            cm[h, s] = col[:, None]
        qdec[h] = np.where(tok, np.exp((i + 1.0) * _LOG_G[h]), 0.0)[:, None]
        kdec[h] = np.where(tok, np.exp((3.0 - i) * _LOG_G[h]), 0.0)[:, None]
    return jnp.asarray(cm), jnp.asarray(qdec), jnp.asarray(kdec)


def _block_diag(w):
    L, n, d, _ = w.shape
    g = n // 2
    w = w.reshape(L, 2, g, d, d)
    eye = jnp.eye(g, dtype=w.dtype)
    return (eye[None, None, :, None, :, None] * w[:, :, :, :, None, :]).reshape(L, 2, g * d, g * d)


def kernel(x_prompt, x_sample, state_ret, state_lru, state_conv, cache_swa_k, cache_swa_v, norm_g, w_in, w_out, ret_norm_g, conv_w, conv_b, w_rgate, b_rgate, w_igate, b_igate, lru_lambda, q_norm_g, k_norm_g, attn_sinks):
    Bs, Ts, _ = x_sample.shape
    ptab = _prompt_tables()
    stab = _sample_tables()
    yp = x_prompt
    ys = jnp.pad(x_sample, ((0, 0), (TOK0, SEQ_PAD - TOK0 - Ts), (0, 0))).reshape(Bs * SEQ_PAD, D_MODEL)
    cache_k = cache_swa_k.reshape(DEPTH, Bs, 2 * WINDOW, SWA_HEAD_DIM)
    cache_v = cache_swa_v.reshape(DEPTH, Bs, 2 * WINDOW, SWA_HEAD_DIM)
    Bp = x_prompt.shape[0]
    p_state, s_state = (), ()
    s_h, s_conv = [], []
    row = lambda a: a.reshape(DEPTH, 1, -1)
    prm = dict(
        sinks=attn_sinks, ng=row(norm_g), win=w_in.astype(BF16), wout=w_out.astype(BF16), retg=row(ret_norm_g),
        convw=conv_w, convb=row(conv_b), wr=_block_diag(w_rgate).astype(BF16), wi=_block_diag(w_igate).astype(BF16),
        br=row(b_rgate), bi=row(b_igate), lam=row(lru_lambda), qg=row(q_norm_g), kg=row(k_norm_g))
    for l in range(DEPTH):
        yp, *p_state = _prompt_layer(l, yp, prm, ptab, tuple(p_state))

        proj = _sample_proj(l, ys, prm)
        mix, *s_state = _sample_mix(l, proj, state_ret, cache_k, cache_v, prm, stab, tuple(s_state))
        conv8 = jnp.pad(state_conv[l], ((0, 0), (0, SEQ_PAD - (CONV_W - 1)), (0, 0))).reshape(Bs * SEQ_PAD, LRU_WIDTH)
        h8 = jnp.pad(state_lru[l][:, None, :], ((0, 0), (TOK0 - 1, SEQ_PAD - TOK0), (0, 0))).reshape(Bs * SEQ_PAD, LRU_WIDTH)
        ys, convo, ho = _sample_out(l, ys, proj, mix, conv8, h8, prm)
        s_h.append(ho.reshape(Bs, SEQ_PAD, LRU_WIDTH)[:, TOK0 + Ts - 1])
        s_conv.append(convo.reshape(Bs, SEQ_PAD, LRU_WIDTH)[:, TOK0 + Ts - (CONV_W - 1):TOK0 + Ts])

    p_ret, p_h, p_conv, p_k, p_v = p_state
    s_ret, s_k, s_v = s_state
    kv5 = lambda a, n: a.reshape(DEPTH, n, WINDOW, SWA_KV_HEADS, SWA_HEAD_DIM)
    y_sample = ys.reshape(Bs, SEQ_PAD, D_MODEL)[:, TOK0:TOK0 + Ts]
    return (yp, y_sample,
            p_ret, p_h.reshape(DEPTH, Bp, LRU_WIDTH), p_conv, kv5(p_k, Bp), kv5(p_v, Bp),
            s_ret, jnp.stack(s_h), jnp.stack(s_conv), kv5(s_k, Bs), kv5(s_v, Bs))
```

```python
import functools
import math

import numpy as np
import jax
import jax.numpy as jnp
from jax import lax
from jax.experimental import pallas as pl
from jax.experimental.pallas import tpu as pltpu

D_MODEL = 1024
DEPTH = 2
PAST_LEN = 16384
GROUP_WIDTH = 512
RET_HEADS = 4
RET_DK = 128
RET_DV = 128
RET_CHUNK = 128
LRU_WIDTH = 512
LRU_BLOCKS = 8
LRU_C = 8.0
CONV_W = 4
SWA_HEADS = 4
SWA_KV_HEADS = 2
SWA_GROUP = 2
SWA_HEAD_DIM = 128
WINDOW = 128
NORM_EPS = 1e-6
NEG_INF = -1e30

IN_WIDTH = 4608
MIX_WIDTH = 1536
QA, KA, VA, GA, XB, GB, QC, KC, VC, GC = 0, 512, 1024, 1536, 2048, 2560, 3072, 3584, 3840, 4096

F32 = jnp.float32
BF16 = jnp.bfloat16

SUBLANES = 8
SEQ_PAD = 8
TOK0 = 3
VMEM_LIMIT = 56 * 1024 * 1024

_LOG_G = np.log1p(-np.power(np.float32(2.0), (-5.0 - np.arange(RET_HEADS)).astype(np.float32))).astype(np.float32)
_SLOPES = [2.0 ** (-8.0 * (h + 1) / SWA_HEADS) for h in range(SWA_HEADS)]


def _rms(x, g):
    ms = jnp.mean(x * x, axis=-1, keepdims=True)
    return x * lax.rsqrt(ms + NORM_EPS) * g


def _silu(x):
    return x * jax.nn.sigmoid(x)


def _softplus(x):
    return jnp.maximum(x, 0.0) + jnp.log1p(jnp.exp(-jnp.abs(x)))


def _dot(a, b):
    return jnp.dot(a, b, preferred_element_type=F32)


def _dot_nt(a, b):
    return lax.dot_general(a, b, (((1,), (1,)), ((), ())), preferred_element_type=F32)


def _dot_tn(a, b):
    return lax.dot_general(a, b, (((0,), (0,)), ((), ())), preferred_element_type=F32)


def _lru_gates(conv, wr_ref, wi_ref, br_ref, bi_ref, lam_ref):
    gin = conv.astype(BF16)
    half = LRU_WIDTH // 2
    lo, hi = gin[:, :half], gin[:, half:]
    r = jax.nn.sigmoid(jnp.concatenate([_dot(lo, wr_ref[0]), _dot(hi, wr_ref[1])], axis=1) + br_ref[...])
    i = jax.nn.sigmoid(jnp.concatenate([_dot(lo, wi_ref[0]), _dot(hi, wi_ref[1])], axis=1) + bi_ref[...])
    log_a = -LRU_C * r * _softplus(-lam_ref[...])
    a = jnp.exp(log_a)
    th = jnp.tanh(log_a)
    m2 = -2.0 * th / (1.0 - th)
    mult = jnp.where(m2 > 0.0, m2 * lax.rsqrt(m2), 0.0)
    return a, mult, i


_DONE = object()


def _run(gen):
    for _ in gen:
        pass


def _zip_stages(gens):
    gens = list(gens)
    while gens:
        gens = [g for g in gens if next(g, _DONE) is not _DONE]
        if gens:
            yield


def _chain(gens):
    for g in gens:
        yield from g


def _interleave(main, sides, n_main, n_side):
    side = _chain(sides)
    next(side)
    done_side = 1
    for i, _ in enumerate(main):
        want = ((i + 1) * n_side) // n_main
        while done_side < want and next(side, _DONE) is not _DONE:
            done_side += 1
    _run(side)


IN_PROJ_COLS = 512
IN_PROJ_STAGES = 1 + IN_WIDTH // IN_PROJ_COLS


def _in_proj(x_ref, ng_ref, win_ref, proj_ref):
    hb = _rms(x_ref[...], ng_ref[...]).astype(BF16)
    yield
    for j in range(IN_WIDTH // IN_PROJ_COLS):
        cols = slice(j * IN_PROJ_COLS, (j + 1) * IN_PROJ_COLS)
        proj_ref[:, cols] = _dot(hb, win_ref[:, cols])
        yield


OUT_PROJ_STAGES = D_MODEL // 256


def _out_proj(mixed_s, x_ref, y_ref, wout_ref):
    mixed = mixed_s[...]
    for j in range(OUT_PROJ_STAGES):
        cols = slice(j * 256, (j + 1) * 256)
        y_ref[:, cols] = x_ref[:, cols] + _dot(mixed, wout_ref[:, cols])
        yield


def _mix_stages(tt):
    return 4 + tt // SUBLANES // 8 + (tt // RET_CHUNK) * 5


def _scan_rows(a, u, h_in, tt):
    G = tt // SUBLANES
    W = a.shape[-1]
    a3 = a.reshape(G, SUBLANES, W)
    u3 = u.reshape(G, SUBLANES, W)
    r3 = lax.broadcasted_iota(jnp.int32, (G, SUBLANES, W), 1)
    sh = 1
    while sh < SUBLANES:
        keep = r3 >= sh
        a_sh = jnp.where(keep, pltpu.roll(a3, sh, axis=1), 1.0)
        u_sh = jnp.where(keep, pltpu.roll(u3, sh, axis=1), 0.0)
        u3 = a3 * u_sh + u3
        a3 = a3 * a_sh
        sh *= 2
    yield
    hs = []
    for g in range(G):
        hg = a3[g] * h_in + u3[g]
        hs.append(hg)
        h_in = hg[SUBLANES - 1:SUBLANES, :]
        if g % 8 == 7:
            yield
    return jnp.concatenate(hs, axis=0)


def _retention_head(h, proj_s, r0, s_ref, retg_ref, dmask_ref, qdec_ref, kdec_ref, mixed_s):
    C = RET_CHUNK
    cs = slice(h * RET_DK, (h + 1) * RET_DK)
    q = proj_s[r0:r0 + C, QA + h * 128:QA + (h + 1) * 128]
    k = proj_s[r0:r0 + C, KA + h * 128:KA + (h + 1) * 128] * (RET_DK ** -0.5)
    vb = proj_s[r0:r0 + C, VA + h * 128:VA + (h + 1) * 128].astype(BF16)
    sc = _dot_nt(q.astype(BF16), k.astype(BF16)) * dmask_ref[h]
    yield
    s_prev = s_ref[0, h]
    lhs = jnp.concatenate([sc.astype(BF16), (q * qdec_ref[h]).astype(BF16)], axis=1)
    rhs = jnp.concatenate([vb, s_prev.astype(BF16)], axis=0)
    o = _dot(lhs, rhs)
    yield
    kv = _dot_tn((k * kdec_ref[h]).astype(BF16), vb)
    s_ref[0, h] = float(np.exp(np.float32(C) * _LOG_G[h])) * s_prev + kv
    yield
    o = _rms(o, retg_ref[:, cs])
    mixed_s[r0:r0 + C, cs] = (o * _silu(proj_s[r0:r0 + C, GA + h * 128:GA + (h + 1) * 128])).astype(BF16)
    yield


def _swa_kv_head(kh, proj_s, r0, blk, sinks_ref, qg_ref, kg_ref, mixed_s, kprev_s, vprev_s):
    C = RET_CHUNK
    ks = slice(kh * 128, (kh + 1) * 128)
    h0, h1 = kh * SWA_GROUP, kh * SWA_GROUP + 1
    kn = _rms(proj_s[r0:r0 + C, KC + kh * 128:KC + (kh + 1) * 128], kg_ref[...])
    vv = proj_s[r0:r0 + C, VC + kh * 128:VC + (kh + 1) * 128]
    kband = jnp.concatenate([kprev_s[:, ks], kn], axis=0).astype(BF16)
    vband = jnp.concatenate([vprev_s[:, ks], vv], axis=0).astype(BF16)
    kprev_s[:, ks] = kn
    vprev_s[:, ks] = vv
    q0 = _rms(proj_s[r0:r0 + C, QC + h0 * 128:QC + (h0 + 1) * 128], qg_ref[...])
    q1 = _rms(proj_s[r0:r0 + C, QC + h1 * 128:QC + (h1 + 1) * 128], qg_ref[...])
    qq = jnp.concatenate([q0, q1], axis=0).astype(BF16)
    yield
    s = _dot_nt(qq, kband) * (SWA_HEAD_DIM ** -0.5)
    ii = lax.broadcasted_iota(jnp.int32, (2 * C, 2 * C), 0)
    jj = lax.broadcasted_iota(jnp.int32, (2 * C, 2 * C), 1)
    dist = jnp.where(ii >= C, ii - C, ii) + C - jj
    valid = jnp.logical_and(jnp.logical_and(dist >= 0, dist < WINDOW), jj >= jnp.where(blk > 0, 0, C))
    slope = jnp.where(ii >= C, _SLOPES[h1], _SLOPES[h0])
    s = jnp.where(valid, s - slope * dist.astype(F32), NEG_INF)
    yield
    rowc = lax.broadcasted_iota(jnp.int32, (2 * C, 1), 0)
    sink = jnp.where(rowc >= C, sinks_ref[h1], sinks_ref[h0])
    m = jnp.maximum(jnp.max(s, axis=-1, keepdims=True), sink)
    p = jnp.exp(s - m)
    denom = jnp.sum(p, axis=-1, keepdims=True) + jnp.exp(sink - m)
    yield
    o = _dot((p / denom).astype(BF16), vband)
    yield
    for g, hh in enumerate((h0, h1)):
        gc = proj_s[r0:r0 + C, GC + hh * 128:GC + (hh + 1) * 128]
        mixed_s[r0:r0 + C, 2 * GROUP_WIDTH + hh * 128:2 * GROUP_WIDTH + (hh + 1) * 128] = (
            (o[g * C:(g + 1) * C] * _silu(gc)).astype(BF16))
    yield


def _mix_tile(proj_s, t, tt, sinks_ref, retg_ref, convw_ref, convb_ref,
              wr_ref, wi_ref, br_ref, bi_ref, lam_ref, qg_ref, kg_ref, dmask_ref, qdec_ref, kdec_ref,
              s_ref, mixed_s, xtail_s, kprev_s, vprev_s, hc_s):
    nch = tt // RET_CHUNK
    C = RET_CHUNK

    xb = proj_s[:, XB:XB + LRU_WIDTH]
    xfull = jnp.concatenate([xtail_s[...], xb], axis=0)
    xtail_s[...] = xb[tt - SUBLANES:tt, :]
    conv = convb_ref[...] + convw_ref[0:1, :] * pltpu.roll(xfull, 3, axis=0)[SUBLANES:]
    conv = conv + convw_ref[1:2, :] * pltpu.roll(xfull, 2, axis=0)[SUBLANES:]
    conv = conv + convw_ref[2:3, :] * pltpu.roll(xfull, 1, axis=0)[SUBLANES:]
    conv = conv + convw_ref[3:4, :] * xb
    yield

    a, mult, ig = _lru_gates(conv, wr_ref, wi_ref, br_ref, bi_ref, lam_ref)
    yield
    row = lax.broadcasted_iota(jnp.int32, (tt, LRU_WIDTH), 0)
    mult = jnp.where(row == jnp.where(t == 0, 0, -1), 1.0, mult)
    u = mult * ig * conv
    hseq = yield from _scan_rows(a, u, hc_s[...], tt)
    hc_s[...] = hseq[tt - 1:tt, :]
    mixed_s[:, GROUP_WIDTH:2 * GROUP_WIDTH] = (hseq * _silu(proj_s[:, GB:GB + LRU_WIDTH])).astype(BF16)
    yield

    for c in range(nch):
        r0 = c * C
        blk = t * nch + c
        yield from _zip_stages(
            [_retention_head(h, proj_s, r0, s_ref, retg_ref, dmask_ref, qdec_ref, kdec_ref, mixed_s)
             for h in range(RET_HEADS)]
            + [_swa_kv_head(kh, proj_s, r0, blk, sinks_ref, qg_ref, kg_ref, mixed_s, kprev_s, vprev_s)
               for kh in range(SWA_KV_HEADS)])


def _prompt_kernel(sinks_ref, xa_ref, xn_ref, ng_ref, win_ref, wout_ref, retg_ref, convw_ref, convb_ref,
                   wr_ref, wi_ref, br_ref, bi_ref, lam_ref, qg_ref, kg_ref,
                   dmask_ref, qdec_ref, kdec_ref, *rest, tt, npairs, layer, slot, fill):
    y_ref, s_ref, h_ref, conv_ref, kout_ref, vout_ref = rest[-14:-8]
    proj_a, proj_b, mixed_a, mixed_b, xtail_s, kprev_s, vprev_s, hc_s = rest[-8:]
    sinks_ref = sinks_ref.at[layer]
    b = pl.program_id(0)
    p = pl.program_id(1)

    @pl.when(p == 0)
    def _init():
        s_ref[...] = jnp.zeros_like(s_ref)
        xtail_s[...] = jnp.zeros_like(xtail_s)
        kprev_s[...] = jnp.zeros_like(kprev_s)
        vprev_s[...] = jnp.zeros_like(vprev_s)
        hc_s[...] = jnp.zeros_like(hc_s)

    xa0, xa1 = xa_ref.at[0, 0:tt, :], xa_ref.at[0, tt:2 * tt, :]
    ya0, ya1 = y_ref.at[0, 0:tt, :], y_ref.at[0, tt:2 * tt, :]

    @pl.when(jnp.logical_and(b == 0, p == 0))
    def _prologue():
        _run(_in_proj(xa0, ng_ref, win_ref, proj_a))

    common = (sinks_ref, retg_ref, convw_ref, convb_ref, wr_ref, wi_ref, br_ref, bi_ref,
              lam_ref, qg_ref, kg_ref, dmask_ref, qdec_ref, kdec_ref, s_ref.at[slot])
    state = (xtail_s, kprev_s, vprev_s, hc_s)

    n_main = _mix_stages(tt)
    _interleave(_mix_tile(proj_a, 2 * p, tt, *common, mixed_a, *state),
                [_in_proj(xa1, ng_ref, win_ref, proj_b)], n_main, IN_PROJ_STAGES)
    _interleave(_mix_tile(proj_b, 2 * p + 1, tt, *common, mixed_b, *state),
                [_out_proj(mixed_a, xa0, ya0, wout_ref), _in_proj(xn_ref.at[0], ng_ref, win_ref, proj_a)],
                n_main, OUT_PROJ_STAGES + IN_PROJ_STAGES)
    _run(_out_proj(mixed_b, xa1, ya1, wout_ref))

    @pl.when(p == npairs - 1)
    def _state_out():
        h_ref[slot, 0] = hc_s[...]
        conv_ref[slot, 0] = xtail_s[SUBLANES - (CONV_W - 1):SUBLANES, :]
        for kh in range(SWA_KV_HEADS):
            ks = slice(kh * 128, (kh + 1) * 128)
            kout_ref[slot, 0, pl.ds(kh, WINDOW, stride=SWA_KV_HEADS), :] = kprev_s[:, ks]
            vout_ref[slot, 0, pl.ds(kh, WINDOW, stride=SWA_KV_HEADS), :] = vprev_s[:, ks]
        if fill:
            for other in [o for o in range(DEPTH) if o != slot]:
                for ref in (h_ref, conv_ref, kout_ref, vout_ref):
                    ref[other] = jnp.zeros(ref.shape[1:], F32)


def _const_spec(shape):
    nd = len(shape)
    return pl.BlockSpec(shape, lambda *_: (0,) * nd)


def _resident_spec(shape):
    nd = len(shape)
    return pl.BlockSpec(shape, lambda *_: (0,) * nd, pipeline_mode=pl.Buffered(1))


def _layer_spec(a, layer, resident=False):
    nd = a.ndim
    kw = dict(pipeline_mode=pl.Buffered(1)) if resident else {}
    return pl.BlockSpec((None,) + a.shape[1:], lambda *_: (layer,) + (0,) * (nd - 1), **kw)


def _prompt_layer(layer, x, prm, tables, prev_out, tt=256):
    B, T, D = x.shape
    npairs = T // (2 * tt)
    layer_in = [prm[k] for k in ("ng", "win", "wout", "retg", "convw", "convb", "wr", "wi", "br", "bi",
                                 "lam", "qg", "kg")]
    vec_in = layer_in + list(tables)

    def next_tile(b, p):
        last = p == npairs - 1
        return (jnp.where(last, jnp.minimum(b + 1, B - 1), b), jnp.where(last, 0, 2 * p + 2), 0)

    in_specs = ([pl.BlockSpec(memory_space=pltpu.SMEM),
                 pl.BlockSpec((1, 2 * tt, D), lambda b, p: (b, p, 0)),
                 pl.BlockSpec((1, tt, D), next_tile)]
                + [_layer_spec(a, layer, resident=True) for a in layer_in]
                + [_resident_spec(a.shape) for a in tables]
                + [pl.BlockSpec(memory_space=pl.ANY) for _ in prev_out])
    out_shape = (
        jax.ShapeDtypeStruct((B, T, D), F32),
        jax.ShapeDtypeStruct((DEPTH, B, RET_HEADS, RET_DK, RET_DV), F32),
        jax.ShapeDtypeStruct((DEPTH, B, 1, LRU_WIDTH), F32),
        jax.ShapeDtypeStruct((DEPTH, B, CONV_W - 1, LRU_WIDTH), F32),
        jax.ShapeDtypeStruct((DEPTH, B, 2 * WINDOW, SWA_HEAD_DIM), F32),
        jax.ShapeDtypeStruct((DEPTH, B, 2 * WINDOW, SWA_HEAD_DIM), F32),
    )
    fill = not prev_out
    ld, li, slot = (DEPTH, 0, layer) if fill else (1, layer, 0)
    out_specs = (
        pl.BlockSpec((1, 2 * tt, D), lambda b, p: (b, p, 0)),
        pl.BlockSpec((ld, 1, RET_HEADS, RET_DK, RET_DV), lambda b, p: (li, b, 0, 0, 0)),
        pl.BlockSpec((ld, 1, 1, LRU_WIDTH), lambda b, p: (li, b, 0, 0)),
        pl.BlockSpec((ld, 1, CONV_W - 1, LRU_WIDTH), lambda b, p: (li, b, 0, 0)),
        pl.BlockSpec((ld, 1, 2 * WINDOW, SWA_HEAD_DIM), lambda b, p: (li, b, 0, 0)),
        pl.BlockSpec((ld, 1, 2 * WINDOW, SWA_HEAD_DIM), lambda b, p: (li, b, 0, 0)),
    )
    n_in = 3 + len(vec_in)
    aliases = {n_in + j: 1 + j for j in range(len(prev_out))}
    scratch = [
        pltpu.VMEM((tt, IN_WIDTH), F32),
        pltpu.VMEM((tt, IN_WIDTH), F32),
        pltpu.VMEM((tt, MIX_WIDTH), BF16),
        pltpu.VMEM((tt, MIX_WIDTH), BF16),
        pltpu.VMEM((SUBLANES, LRU_WIDTH), F32),
        pltpu.VMEM((WINDOW, 256), F32),
        pltpu.VMEM((WINDOW, 256), F32),
        pltpu.VMEM((1, LRU_WIDTH), F32),
    ]
    return pl.pallas_call(
        functools.partial(_prompt_kernel, tt=tt, npairs=npairs, layer=layer, slot=slot, fill=fill),
        grid=(B, npairs),
        in_specs=in_specs,
        out_specs=out_specs,
        out_shape=out_shape,
        scratch_shapes=scratch,
        input_output_aliases=aliases,
        compiler_params=pltpu.CompilerParams(
            dimension_semantics=("arbitrary", "arbitrary"), vmem_limit_bytes=VMEM_LIMIT),
        name="prompt_layer",
    )(prm["sinks"], x, x, *vec_in, *prev_out)


def _sproj_kernel(x_ref, ng_ref, win_ref, o_ref, hb_s):
    @pl.when(pl.program_id(0) == 0)
    def _norm_once():
        hb_s[...] = _rms(x_ref[...], ng_ref[...]).astype(BF16)

    o_ref[...] = _dot(hb_s[...], win_ref[...])


def _sample_proj(layer, xp, prm):
    R = xp.shape[0]
    nb = IN_WIDTH // 512
    ng, win = prm["ng"], prm["win"]
    return pl.pallas_call(
        _sproj_kernel,
        grid=(nb,),
        in_specs=[_const_spec(xp.shape), _layer_spec(ng, layer),
                  pl.BlockSpec((None, D_MODEL, 512), lambda j: (layer, 0, j))],
        out_specs=pl.BlockSpec((R, 512), lambda j: (0, j)),
        out_shape=jax.ShapeDtypeStruct((R, IN_WIDTH), F32),
        scratch_shapes=[pltpu.VMEM((R, D_MODEL), BF16)],
        compiler_params=pltpu.CompilerParams(
            dimension_semantics=("arbitrary",), vmem_limit_bytes=VMEM_LIMIT),
        name="sample_proj",
    )(xp, ng, win)


def _sample_mix_seq(sinks_ref, proj_ref, s0_ref, kbuf_ref, vbuf_ref, retg_ref, qg_ref, kg_ref,
                    cm_ref, qdec_ref, kdec_ref, mix_ref, snew_ref, knew_ref, vnew_ref):
    P = SEQ_PAD
    for h in range(RET_HEADS):
        cs = slice(h * 128, (h + 1) * 128)
        q = proj_ref[:, QA + h * 128:QA + (h + 1) * 128]
        k = proj_ref[:, KA + h * 128:KA + (h + 1) * 128] * (RET_DK ** -0.5)
        v = proj_ref[:, VA + h * 128:VA + (h + 1) * 128]
        intra = jnp.zeros((P, RET_DV), F32)
        for s in range(4):
            r_ = TOK0 + s
            w = jnp.sum(q * k[r_:r_ + 1, :], axis=-1, keepdims=True)
            intra = intra + (w * cm_ref[h, s]) * v[r_:r_ + 1, :]
        s_prev = s0_ref[h]
        cross = _dot((q * qdec_ref[h]).astype(BF16), s_prev.astype(BF16))
        kv = _dot_tn((k * kdec_ref[h]).astype(BF16), v.astype(BF16))
        yield
        snew_ref[h] = float(np.exp(np.float32(4.0) * _LOG_G[h])) * s_prev + kv
        o = _rms(intra + cross, retg_ref[:, cs])
        mix_ref[:, cs] = o * _silu(proj_ref[:, GA + h * 128:GA + (h + 1) * 128])
        yield

    r16 = lax.broadcasted_iota(jnp.int32, (2 * P, WINDOW), 0)
    j16 = lax.broadcasted_iota(jnp.int32, (2 * P, WINDOW), 1)
    rr = jnp.where(r16 >= P, r16 - P, r16)
    row_ok = jnp.logical_and(rr >= TOK0, rr < TOK0 + 4)
    dist = (rr - TOK0) + WINDOW - j16
    distf = dist.astype(F32)
    valid = jnp.logical_and(jnp.logical_and(dist >= 0, dist < WINDOW), row_ok)
    r16c = lax.broadcasted_iota(jnp.int32, (2 * P, 1), 0)
    rrc = jnp.where(r16c >= P, r16c - P, r16c)
    rowc_ok = jnp.logical_and(rrc >= TOK0, rrc < TOK0 + 4)
    knew_ref[0:2 * (WINDOW - 4), :] = kbuf_ref[2 * 4:2 * WINDOW, :]
    vnew_ref[0:2 * (WINDOW - 4), :] = vbuf_ref[2 * 4:2 * WINDOW, :]
    for kh in range(SWA_KV_HEADS):
        h0, h1 = kh * SWA_GROUP, kh * SWA_GROUP + 1
        kb = kbuf_ref[pl.ds(kh, WINDOW, stride=SWA_KV_HEADS), :]
        vb = vbuf_ref[pl.ds(kh, WINDOW, stride=SWA_KV_HEADS), :]
        kn = _rms(proj_ref[:, KC + kh * 128:KC + (kh + 1) * 128], kg_ref[...])
        vn = proj_ref[:, VC + kh * 128:VC + (kh + 1) * 128]
        q0 = _rms(proj_ref[:, QC + h0 * 128:QC + (h0 + 1) * 128], qg_ref[...])
        q1 = _rms(proj_ref[:, QC + h1 * 128:QC + (h1 + 1) * 128], qg_ref[...])
        qq = jnp.concatenate([q0, q1], axis=0)
        slope = jnp.where(r16 >= P, _SLOPES[h1], _SLOPES[h0])
        slopec = jnp.where(r16c >= P, _SLOPES[h1], _SLOPES[h0])
        sb = _dot_nt(qq.astype(BF16), kb.astype(BF16)) * (SWA_HEAD_DIM ** -0.5)
        sb = jnp.where(valid, sb - slope * distf, NEG_INF)
        yield
        sink = jnp.where(r16c >= P, sinks_ref[h1], sinks_ref[h0])
        m = jnp.maximum(jnp.max(sb, axis=-1, keepdims=True), sink)
        wn = []
        for s in range(4):
            r_ = TOK0 + s
            w = jnp.sum(qq * kn[r_:r_ + 1, :], axis=-1, keepdims=True) * (SWA_HEAD_DIM ** -0.5)
            dn = rrc - r_
            w = jnp.where(jnp.logical_and(dn >= 0, rowc_ok), w - slopec * dn.astype(F32), NEG_INF)
            wn.append(w)
            m = jnp.maximum(m, w)
        pb = jnp.exp(sb - m)
        denom = jnp.sum(pb, axis=-1, keepdims=True) + jnp.exp(sink - m)
        pn = [jnp.exp(w - m) for w in wn]
        for p_ in pn:
            denom = denom + p_
        yield
        o = _dot((pb / denom).astype(BF16), vb.astype(BF16))
        for s in range(4):
            r_ = TOK0 + s
            o = o + (pn[s] / denom) * vn[r_:r_ + 1, :]
        for g, hh in enumerate((h0, h1)):
            gc = proj_ref[:, GC + hh * 128:GC + (hh + 1) * 128]
            mix_ref[:, GROUP_WIDTH + hh * 128:GROUP_WIDTH + (hh + 1) * 128] = o[g * P:(g + 1) * P] * _silu(gc)
        for s in range(4):
            r_out = 2 * (WINDOW - 4 + s) + kh
            knew_ref[r_out:r_out + 1, :] = kn[TOK0 + s:TOK0 + s + 1, :]
            vnew_ref[r_out:r_out + 1, :] = vn[TOK0 + s:TOK0 + s + 1, :]
        yield


def _sample_mix_kernel(sinks_ref, proj_ref, s0_ref, kbuf_ref, vbuf_ref, retg_ref, qg_ref, kg_ref,
                       cm_ref, qdec_ref, kdec_ref, *rest, nseq, layer, slot, fill):
    mix_ref, snew_ref, knew_ref, vnew_ref = rest[-4:]
    sinks_ref = sinks_ref.at[layer]
    if fill:
        for other in [o for o in range(DEPTH) if o != slot]:
            for ref in (snew_ref, knew_ref, vnew_ref):
                ref[other] = jnp.zeros(ref.shape[1:], F32)
    gens = []
    for i in range(nseq):
        rows = slice(i * SEQ_PAD, (i + 1) * SEQ_PAD)
        gens.append(_sample_mix_seq(
            sinks_ref, proj_ref.at[rows, :], s0_ref.at[0, i], kbuf_ref.at[0, i], vbuf_ref.at[0, i],
            retg_ref, qg_ref, kg_ref, cm_ref, qdec_ref, kdec_ref,
            mix_ref.at[rows, :], snew_ref.at[slot, i], knew_ref.at[slot, i], vnew_ref.at[slot, i]))
    _run(_zip_stages(gens))


def _sample_mix(layer, proj, state_ret, cache_k, cache_v, prm, tables, prev_out, nseq=8):
    depth, B = state_ret.shape[:2]
    layer_in = [prm[k] for k in ("retg", "qg", "kg")]
    consts = layer_in + list(tables)
    st_spec = pl.BlockSpec((1, nseq, RET_HEADS, RET_DK, RET_DV), lambda i: (layer, i, 0, 0, 0))
    kv_spec = pl.BlockSpec((1, nseq, 2 * WINDOW, SWA_HEAD_DIM), lambda i: (layer, i, 0, 0))
    in_specs = ([pl.BlockSpec(memory_space=pltpu.SMEM),
                 pl.BlockSpec((nseq * SEQ_PAD, IN_WIDTH), lambda i: (i, 0)),
                 st_spec, kv_spec, kv_spec]
                + [_layer_spec(a, layer) for a in layer_in]
                + [_const_spec(a.shape) for a in tables]
                + [pl.BlockSpec(memory_space=pl.ANY) for _ in prev_out])
    out_shape = (
        jax.ShapeDtypeStruct((B * SEQ_PAD, 2 * GROUP_WIDTH), F32),
        jax.ShapeDtypeStruct(state_ret.shape, F32),
        jax.ShapeDtypeStruct(cache_k.shape, F32),
        jax.ShapeDtypeStruct(cache_v.shape, F32),
    )
    fill = not prev_out
    ld, li, slot = (DEPTH, 0, layer) if fill else (1, layer, 0)
    out_specs = (pl.BlockSpec((nseq * SEQ_PAD, 2 * GROUP_WIDTH), lambda i: (i, 0)),
                 pl.BlockSpec((ld, nseq, RET_HEADS, RET_DK, RET_DV), lambda i: (li, i, 0, 0, 0)),
                 pl.BlockSpec((ld, nseq, 2 * WINDOW, SWA_HEAD_DIM), lambda i: (li, i, 0, 0)),
                 pl.BlockSpec((ld, nseq, 2 * WINDOW, SWA_HEAD_DIM), lambda i: (li, i, 0, 0)))
    n_in = 5 + len(consts)
    aliases = {n_in + j: 1 + j for j in range(len(prev_out))}
    return pl.pallas_call(
        functools.partial(_sample_mix_kernel, nseq=nseq, layer=layer, slot=slot, fill=fill),
        grid=(B // nseq,),
        in_specs=in_specs,
        out_specs=out_specs,
        out_shape=out_shape,
        input_output_aliases=aliases,
        compiler_params=pltpu.CompilerParams(
            dimension_semantics=("arbitrary",), vmem_limit_bytes=VMEM_LIMIT),
        name="sample_mix",
    )(prm["sinks"], proj, state_ret, cache_k, cache_v, *consts, *prev_out)


def _sample_out_kernel(x_ref, xb_ref, gb_ref, mix_ref, conv8_ref, h8_ref, convw_ref, convb_ref,
                       wr_ref, wi_ref, br_ref, bi_ref, lam_ref, wout_ref,
                       y_ref, convo_ref, ho_ref):
    R = x_ref.shape[0]
    row = lax.broadcasted_iota(jnp.int32, (R, LRU_WIDTH), 0) & (SEQ_PAD - 1)
    xc = jnp.where(row < TOK0, conv8_ref[...], xb_ref[...])
    convo_ref[...] = xc
    conv = convb_ref[...] + convw_ref[0:1, :] * pltpu.roll(xc, 3, axis=0)
    conv = conv + convw_ref[1:2, :] * pltpu.roll(xc, 2, axis=0)
    conv = conv + convw_ref[2:3, :] * pltpu.roll(xc, 1, axis=0)
    conv = conv + convw_ref[3:4, :] * xc
    a, mult, ig = _lru_gates(conv, wr_ref, wi_ref, br_ref, bi_ref, lam_ref)
    u = mult * ig * conv
    h = h8_ref[...]
    for s in range(4):
        h = jnp.where(row == TOK0 + s, a * pltpu.roll(h, 1, axis=0) + u, h)
    ho_ref[...] = h
    ob = h * _silu(gb_ref[...])
    y = x_ref[...] + _dot(mix_ref[:, 0:GROUP_WIDTH].astype(BF16), wout_ref[0:GROUP_WIDTH, :])
    y = y + _dot(ob.astype(BF16), wout_ref[GROUP_WIDTH:2 * GROUP_WIDTH, :])
    y = y + _dot(mix_ref[:, GROUP_WIDTH:2 * GROUP_WIDTH].astype(BF16), wout_ref[2 * GROUP_WIDTH:, :])
    y_ref[...] = y


def _sample_out(layer, xp, proj, mix, conv8, h8, prm, rows=256):
    R = xp.shape[0]
    consts = [prm[k] for k in ("convw", "convb", "wr", "wi", "br", "bi", "lam", "wout")]
    in_specs = ([pl.BlockSpec((rows, D_MODEL), lambda i: (i, 0)),
                 pl.BlockSpec((rows, LRU_WIDTH), lambda i: (i, XB // LRU_WIDTH)),
                 pl.BlockSpec((rows, LRU_WIDTH), lambda i: (i, GB // LRU_WIDTH)),
                 pl.BlockSpec((rows, 2 * GROUP_WIDTH), lambda i: (i, 0)),
                 pl.BlockSpec((rows, LRU_WIDTH), lambda i: (i, 0)),
                 pl.BlockSpec((rows, LRU_WIDTH), lambda i: (i, 0))]
                + [_layer_spec(a, layer) for a in consts])
    out_shape = (
        jax.ShapeDtypeStruct((R, D_MODEL), F32),
        jax.ShapeDtypeStruct((R, LRU_WIDTH), F32),
        jax.ShapeDtypeStruct((R, LRU_WIDTH), F32),
    )
    out_specs = (
        pl.BlockSpec((rows, D_MODEL), lambda i: (i, 0)),
        pl.BlockSpec((rows, LRU_WIDTH), lambda i: (i, 0)),
        pl.BlockSpec((rows, LRU_WIDTH), lambda i: (i, 0)),
    )
    return pl.pallas_call(
        _sample_out_kernel,
        grid=(R // rows,),
        in_specs=in_specs,
        out_specs=out_specs,
        out_shape=out_shape,
        compiler_params=pltpu.CompilerParams(
            dimension_semantics=("arbitrary",), vmem_limit_bytes=VMEM_LIMIT),
        name="sample_out",
    )(xp, proj, proj, mix, conv8, h8, *consts)


def _prompt_tables():
    C = RET_CHUNK
    idx = np.arange(C, dtype=np.float32)
    diff = idx[:, None] - idx[None, :]
    causal = diff >= 0
    lg = _LOG_G[:, None, None]
    dmask = np.where(causal[None], np.exp(np.where(causal, diff, 0.0)[None] * lg), 0.0).astype(np.float32)
    qdec = np.exp((idx + 1.0)[None, :] * _LOG_G[:, None]).astype(np.float32)
    kdec = np.exp((C - 1 - idx)[None, :] * _LOG_G[:, None]).astype(np.float32)
    qdec = np.broadcast_to(qdec[:, :, None], (RET_HEADS, C, RET_DK)).copy()
    kdec = np.broadcast_to(kdec[:, :, None], (RET_HEADS, C, RET_DK)).copy()
    return jnp.asarray(dmask), jnp.asarray(qdec), jnp.asarray(kdec)


def _sample_tables():
    P = SEQ_PAD
    rows = np.arange(P, dtype=np.float32)
    i = rows - TOK0
    tok = (i >= 0) & (i < 4)
    cm = np.zeros((RET_HEADS, 4, P, RET_DV), np.float32)
    qdec = np.zeros((RET_HEADS, P, RET_DK), np.float32)
    kdec = np.zeros((RET_HEADS, P, RET_DK), np.float32)
    for h in range(RET_HEADS):
        for s in range(4):
            d = i - s
            col = np.where(tok & (d >= 0), np.exp(np.where(d >= 0, d, 0.0) * _LOG_G[h]), 0.0)
            cm[h, s] = col[:, None]
        qdec[h] = np.where(tok, np.exp((i + 1.0) * _LOG_G[h]), 0.0)[:, None]
        kdec[h] = np.where(tok, np.exp((3.0 - i) * _LOG_G[h]), 0.0)[:, None]
    return jnp.asarray(cm), jnp.asarray(qdec), jnp.asarray(kdec)


def _block_diag(w):
    L, n, d, _ = w.shape
    g = n // 2
    w = w.reshape(L, 2, g, d, d)
    eye = jnp.eye(g, dtype=w.dtype)
    return (eye[None, None, :, None, :, None] * w[:, :, :, :, None, :]).reshape(L, 2, g * d, g * d)


def kernel(x_prompt, x_sample, state_ret, state_lru, state_conv, cache_swa_k, cache_swa_v, norm_g, w_in, w_out, ret_norm_g, conv_w, conv_b, w_rgate, b_rgate, w_igate, b_igate, lru_lambda, q_norm_g, k_norm_g, attn_sinks):
    Bs, Ts, _ = x_sample.shape
    ptab = _prompt_tables()
    stab = _sample_tables()
    yp = x_prompt
    ys = jnp.pad(x_sample, ((0, 0), (TOK0, SEQ_PAD - TOK0 - Ts), (0, 0))).reshape(Bs * SEQ_PAD, D_MODEL)
    cache_k = cache_swa_k.reshape(DEPTH, Bs, 2 * WINDOW, SWA_HEAD_DIM)
    cache_v = cache_swa_v.reshape(DEPTH, Bs, 2 * WINDOW, SWA_HEAD_DIM)
    Bp = x_prompt.shape[0]
    p_state, s_state = (), ()
    s_h, s_conv = [], []
    row = lambda a: a.reshape(DEPTH, 1, -1)
    prm = dict(
        sinks=attn_sinks, ng=row(norm_g), win=w_in.astype(BF16), wout=w_out.astype(BF16), retg=row(ret_norm_g),
        convw=conv_w, convb=row(conv_b), wr=_block_diag(w_rgate).astype(BF16), wi=_block_diag(w_igate).astype(BF16),
        br=row(b_rgate), bi=row(b_igate), lam=row(lru_lambda), qg=row(q_norm_g), kg=row(k_norm_g))
    for l in range(DEPTH):
        yp, *p_state = _prompt_layer(l, yp, prm, ptab, tuple(p_state))

        proj = _sample_proj(l, ys, prm)
        mix, *s_state = _sample_mix(l, proj, state_ret, cache_k, cache_v, prm, stab, tuple(s_state))
        conv8 = jnp.pad(state_conv[l], ((0, 0), (0, SEQ_PAD - (CONV_W - 1)), (0, 0))).reshape(Bs * SEQ_PAD, LRU_WIDTH)
        h8 = jnp.pad(state_lru[l][:, None, :], ((0, 0), (TOK0 - 1, SEQ_PAD - TOK0), (0, 0))).reshape(Bs * SEQ_PAD, LRU_WIDTH)
        ys, convo, ho = _sample_out(l, ys, proj, mix, conv8, h8, prm)
        s_h.append(ho.reshape(Bs, SEQ_PAD, LRU_WIDTH)[:, TOK0 + Ts - 1])
        s_conv.append(convo.reshape(Bs, SEQ_PAD, LRU_WIDTH)[:, TOK0 + Ts - (CONV_W - 1):TOK0 + Ts])

    p_ret, p_h, p_conv, p_k, p_v = p_state
    s_ret, s_k, s_v = s_state
    kv5 = lambda a, n: a.reshape(DEPTH, n, WINDOW, SWA_KV_HEADS, SWA_HEAD_DIM)
    y_sample = ys.reshape(Bs, SEQ_PAD, D_MODEL)[:, TOK0:TOK0 + Ts]
    return (yp, y_sample,
            p_ret, p_h.reshape(DEPTH, Bp, LRU_WIDTH), p_conv, kv5(p_k, Bp), kv5(p_v, Bp),
            s_ret, jnp.stack(s_h), jnp.stack(s_conv), kv5(s_k, Bs), kv5(s_v, Bs))
```

```python
import functools
import math

import numpy as np
import jax
import jax.numpy as jnp
from jax import lax
from jax.experimental import pallas as pl
from jax.experimental.pallas import tpu as pltpu

D_MODEL = 1024
DEPTH = 2
PAST_LEN = 16384
GROUP_WIDTH = 512
RET_HEADS = 4
RET_DK = 128
RET_DV = 128
RET_CHUNK = 128
LRU_WIDTH = 512
LRU_BLOCKS = 8
LRU_C = 8.0
CONV_W = 4
SWA_HEADS = 4
SWA_KV_HEADS = 2
SWA_GROUP = 2
SWA_HEAD_DIM = 128
WINDOW = 128
NORM_EPS = 1e-6
NEG_INF = -1e30

IN_WIDTH = 4608
MIX_WIDTH = 1536
QA, KA, VA, GA, XB, GB, QC, KC, VC, GC = 0, 512, 1024, 1536, 2048, 2560, 3072, 3584, 3840, 4096

F32 = jnp.float32
BF16 = jnp.bfloat16

SUBLANES = 8
SEQ_PAD = 8
TOK0 = 3
VMEM_LIMIT = 56 * 1024 * 1024

_LOG_G = np.log1p(-np.power(np.float32(2.0), (-5.0 - np.arange(RET_HEADS)).astype(np.float32))).astype(np.float32)
_SLOPES = [2.0 ** (-8.0 * (h + 1) / SWA_HEADS) for h in range(SWA_HEADS)]


def _rms(x, g):
    ms = jnp.mean(x * x, axis=-1, keepdims=True)
    return x * lax.rsqrt(ms + NORM_EPS) * g


def _silu(x):
    return x * jax.nn.sigmoid(x)


def _softplus(x):
    return jnp.maximum(x, 0.0) + jnp.log1p(jnp.exp(-jnp.abs(x)))


def _dot(a, b):
    return jnp.dot(a, b, preferred_element_type=F32)


def _dot_nt(a, b):
    return lax.dot_general(a, b, (((1,), (1,)), ((), ())), preferred_element_type=F32)


def _dot_tn(a, b):
    return lax.dot_general(a, b, (((0,), (0,)), ((), ())), preferred_element_type=F32)


def _lru_gates(conv, wr_ref, wi_ref, br_ref, bi_ref, lam_ref):
    gin = conv.astype(BF16)
    half = LRU_WIDTH // 2
    lo, hi = gin[:, :half], gin[:, half:]
    r = jax.nn.sigmoid(jnp.concatenate([_dot(lo, wr_ref[0]), _dot(hi, wr_ref[1])], axis=1) + br_ref[...])
    i = jax.nn.sigmoid(jnp.concatenate([_dot(lo, wi_ref[0]), _dot(hi, wi_ref[1])], axis=1) + bi_ref[...])
    log_a = -LRU_C * r * _softplus(-lam_ref[...])
    a = jnp.exp(log_a)
    m2 = -jnp.tanh(log_a) * (1.0 + a * a)
    mult = jnp.where(m2 > 0.0, m2 * lax.rsqrt(m2), 0.0)
    return a, mult, i


_DONE = object()


def _run(gen):
    for _ in gen:
        pass


def _zip_stages(gens):
    gens = list(gens)
    while gens:
        gens = [g for g in gens if next(g, _DONE) is not _DONE]
        if gens:
            yield


def _chain(gens):
    for g in gens:
        yield from g


def _interleave(main, sides, n_main, n_side):
    side = _chain(sides)
    next(side)
    done_side = 1
    for i, _ in enumerate(main):
        want = ((i + 1) * n_side) // n_main
        while done_side < want and next(side, _DONE) is not _DONE:
            done_side += 1
    _run(side)


IN_PROJ_COLS = 256
IN_PROJ_STAGES = 1 + IN_WIDTH // IN_PROJ_COLS


def _in_proj(x_ref, ng_ref, win_ref, proj_ref):
    hb = _rms(x_ref[...], ng_ref[...]).astype(BF16)
    yield
    for j in range(IN_WIDTH // IN_PROJ_COLS):
        cols = slice(j * IN_PROJ_COLS, (j + 1) * IN_PROJ_COLS)
        proj_ref[:, cols] = _dot(hb, win_ref[:, cols])
        yield


OUT_PROJ_STAGES = D_MODEL // 256


def _out_proj(mixed_s, x_ref, y_ref, wout_ref):
    mixed = mixed_s[...]
    for j in range(OUT_PROJ_STAGES):
        cols = slice(j * 256, (j + 1) * 256)
        y_ref[:, cols] = x_ref[:, cols] + _dot(mixed, wout_ref[:, cols])
        yield


def _mix_stages(tt):
    return 4 + tt // SUBLANES // 8 + (tt // RET_CHUNK) * 5


def _scan_rows(a, u, h_in, tt):
    G = tt // SUBLANES
    W = a.shape[-1]
    a3 = a.reshape(G, SUBLANES, W)
    u3 = u.reshape(G, SUBLANES, W)
    r3 = lax.broadcasted_iota(jnp.int32, (G, SUBLANES, W), 1)
    sh = 1
    while sh < SUBLANES:
        keep = r3 >= sh
        a_sh = jnp.where(keep, pltpu.roll(a3, sh, axis=1), 1.0)
        u_sh = jnp.where(keep, pltpu.roll(u3, sh, axis=1), 0.0)
        u3 = a3 * u_sh + u3
        a3 = a3 * a_sh
        sh *= 2
    yield
    hs = []
    for g in range(G):
        hg = a3[g] * h_in + u3[g]
        hs.append(hg)
        h_in = hg[SUBLANES - 1:SUBLANES, :]
        if g % 8 == 7:
            yield
    return jnp.concatenate(hs, axis=0)


def _retention_head(h, proj_s, r0, s_ref, retg_ref, dmask_ref, qdec_ref, kdec_ref, mixed_s):
    C = RET_CHUNK
    cs = slice(h * RET_DK, (h + 1) * RET_DK)
    q = proj_s[r0:r0 + C, QA + h * 128:QA + (h + 1) * 128]
    k = proj_s[r0:r0 + C, KA + h * 128:KA + (h + 1) * 128] * (RET_DK ** -0.5)
    vb = proj_s[r0:r0 + C, VA + h * 128:VA + (h + 1) * 128].astype(BF16)
    sc = _dot_nt(q.astype(BF16), k.astype(BF16)) * dmask_ref[h]
    yield
    s_prev = s_ref[0, h]
    lhs = jnp.concatenate([sc.astype(BF16), (q * qdec_ref[h]).astype(BF16)], axis=1)
    rhs = jnp.concatenate([vb, s_prev.astype(BF16)], axis=0)
    o = _dot(lhs, rhs)
    yield
    kv = _dot_tn((k * kdec_ref[h]).astype(BF16), vb)
    s_ref[0, h] = float(np.exp(np.float32(C) * _LOG_G[h])) * s_prev + kv
    yield
    o = _rms(o, retg_ref[:, cs])
    mixed_s[r0:r0 + C, cs] = (o * _silu(proj_s[r0:r0 + C, GA + h * 128:GA + (h + 1) * 128])).astype(BF16)
    yield


def _swa_kv_head(kh, proj_s, r0, blk, sinks_ref, qg_ref, kg_ref, mixed_s, kprev_s, vprev_s):
    C = RET_CHUNK
    ks = slice(kh * 128, (kh + 1) * 128)
    h0, h1 = kh * SWA_GROUP, kh * SWA_GROUP + 1
    kn = _rms(proj_s[r0:r0 + C, KC + kh * 128:KC + (kh + 1) * 128], kg_ref[...])
    vv = proj_s[r0:r0 + C, VC + kh * 128:VC + (kh + 1) * 128]
    kband = jnp.concatenate([kprev_s[:, ks], kn], axis=0).astype(BF16)
    vband = jnp.concatenate([vprev_s[:, ks], vv], axis=0).astype(BF16)
    kprev_s[:, ks] = kn
    vprev_s[:, ks] = vv
    q0 = _rms(proj_s[r0:r0 + C, QC + h0 * 128:QC + (h0 + 1) * 128], qg_ref[...])
    q1 = _rms(proj_s[r0:r0 + C, QC + h1 * 128:QC + (h1 + 1) * 128], qg_ref[...])
    qq = jnp.concatenate([q0, q1], axis=0).astype(BF16)
    yield
    s = _dot_nt(qq, kband) * (SWA_HEAD_DIM ** -0.5)
    ii = lax.broadcasted_iota(jnp.int32, (2 * C, 2 * C), 0)
    jj = lax.broadcasted_iota(jnp.int32, (2 * C, 2 * C), 1)
    dist = jnp.where(ii >= C, ii - C, ii) + C - jj
    valid = jnp.logical_and(jnp.logical_and(dist >= 0, dist < WINDOW), jj >= jnp.where(blk > 0, 0, C))
    slope = jnp.where(ii >= C, _SLOPES[h1], _SLOPES[h0])
    s = jnp.where(valid, s - slope * dist.astype(F32), NEG_INF)
    yield
    rowc = lax.broadcasted_iota(jnp.int32, (2 * C, 1), 0)
    sink = jnp.where(rowc >= C, sinks_ref[h1], sinks_ref[h0])
    m = jnp.maximum(jnp.max(s, axis=-1, keepdims=True), sink)
    p = jnp.exp(s - m)
    denom = jnp.sum(p, axis=-1, keepdims=True) + jnp.exp(sink - m)
    yield
    o = _dot((p / denom).astype(BF16), vband)
    yield
    for g, hh in enumerate((h0, h1)):
        gc = proj_s[r0:r0 + C, GC + hh * 128:GC + (hh + 1) * 128]
        mixed_s[r0:r0 + C, 2 * GROUP_WIDTH + hh * 128:2 * GROUP_WIDTH + (hh + 1) * 128] = (
            (o[g * C:(g + 1) * C] * _silu(gc)).astype(BF16))
    yield


def _mix_tile(proj_s, t, tt, sinks_ref, retg_ref, convw_ref, convb_ref,
              wr_ref, wi_ref, br_ref, bi_ref, lam_ref, qg_ref, kg_ref, dmask_ref, qdec_ref, kdec_ref,
              s_ref, mixed_s, xtail_s, kprev_s, vprev_s, hc_s):
    nch = tt // RET_CHUNK
    C = RET_CHUNK

    xb = proj_s[:, XB:XB + LRU_WIDTH]
    xfull = jnp.concatenate([xtail_s[...], xb], axis=0)
    xtail_s[...] = xb[tt - SUBLANES:tt, :]
    conv = convb_ref[...] + convw_ref[0:1, :] * pltpu.roll(xfull, 3, axis=0)[SUBLANES:]
    conv = conv + convw_ref[1:2, :] * pltpu.roll(xfull, 2, axis=0)[SUBLANES:]
    conv = conv + convw_ref[2:3, :] * pltpu.roll(xfull, 1, axis=0)[SUBLANES:]
    conv = conv + convw_ref[3:4, :] * xb
    yield

    a, mult, ig = _lru_gates(conv, wr_ref, wi_ref, br_ref, bi_ref, lam_ref)
    yield
    row = lax.broadcasted_iota(jnp.int32, (tt, LRU_WIDTH), 0)
    mult = jnp.where(row == jnp.where(t == 0, 0, -1), 1.0, mult)
    u = mult * ig * conv
    hseq = yield from _scan_rows(a, u, hc_s[...], tt)
    hc_s[...] = hseq[tt - 1:tt, :]
    mixed_s[:, GROUP_WIDTH:2 * GROUP_WIDTH] = (hseq * _silu(proj_s[:, GB:GB + LRU_WIDTH])).astype(BF16)
    yield

    for c in range(nch):
        r0 = c * C
        blk = t * nch + c
        yield from _zip_stages(
            [_retention_head(h, proj_s, r0, s_ref, retg_ref, dmask_ref, qdec_ref, kdec_ref, mixed_s)
             for h in range(RET_HEADS)]
            + [_swa_kv_head(kh, proj_s, r0, blk, sinks_ref, qg_ref, kg_ref, mixed_s, kprev_s, vprev_s)
               for kh in range(SWA_KV_HEADS)])


def _prompt_kernel(sinks_ref, xa_ref, xn_ref, ng_ref, win_ref, wout_ref, retg_ref, convw_ref, convb_ref,
                   wr_ref, wi_ref, br_ref, bi_ref, lam_ref, qg_ref, kg_ref,
                   dmask_ref, qdec_ref, kdec_ref, *rest, tt, npairs, layer, slot, fill):
    y_ref, s_ref, h_ref, conv_ref, kout_ref, vout_ref = rest[-14:-8]
    proj_a, proj_b, mixed_a, mixed_b, xtail_s, kprev_s, vprev_s, hc_s = rest[-8:]
    sinks_ref = sinks_ref.at[layer]
    b = pl.program_id(0)
    p = pl.program_id(1)

    @pl.when(p == 0)
    def _init():
        s_ref[...] = jnp.zeros_like(s_ref)
        xtail_s[...] = jnp.zeros_like(xtail_s)
        kprev_s[...] = jnp.zeros_like(kprev_s)
        vprev_s[...] = jnp.zeros_like(vprev_s)
        hc_s[...] = jnp.zeros_like(hc_s)

    xa0, xa1 = xa_ref.at[0, 0:tt, :], xa_ref.at[0, tt:2 * tt, :]
    ya0, ya1 = y_ref.at[0, 0:tt, :], y_ref.at[0, tt:2 * tt, :]

    @pl.when(jnp.logical_and(b == 0, p == 0))
    def _prologue():
        _run(_in_proj(xa0, ng_ref, win_ref, proj_a))

    common = (sinks_ref, retg_ref, convw_ref, convb_ref, wr_ref, wi_ref, br_ref, bi_ref,
              lam_ref, qg_ref, kg_ref, dmask_ref, qdec_ref, kdec_ref, s_ref.at[slot])
    state = (xtail_s, kprev_s, vprev_s, hc_s)

    n_main = _mix_stages(tt)
    _interleave(_mix_tile(proj_a, 2 * p, tt, *common, mixed_a, *state),
                [_in_proj(xa1, ng_ref, win_ref, proj_b)], n_main, IN_PROJ_STAGES)
    _interleave(_mix_tile(proj_b, 2 * p + 1, tt, *common, mixed_b, *state),
                [_out_proj(mixed_a, xa0, ya0, wout_ref), _in_proj(xn_ref.at[0], ng_ref, win_ref, proj_a)],
                n_main, OUT_PROJ_STAGES + IN_PROJ_STAGES)
    _run(_out_proj(mixed_b, xa1, ya1, wout_ref))

    @pl.when(p == npairs - 1)
    def _state_out():
        h_ref[slot, 0] = hc_s[...]
        conv_ref[slot, 0] = xtail_s[SUBLANES - (CONV_W - 1):SUBLANES, :]
        for kh in range(SWA_KV_HEADS):
            ks = slice(kh * 128, (kh + 1) * 128)
            kout_ref[slot, 0, pl.ds(kh, WINDOW, stride=SWA_KV_HEADS), :] = kprev_s[:, ks]
            vout_ref[slot, 0, pl.ds(kh, WINDOW, stride=SWA_KV_HEADS), :] = vprev_s[:, ks]
        if fill:
            for other in [o for o in range(DEPTH) if o != slot]:
                for ref in (h_ref, conv_ref, kout_ref, vout_ref):
                    ref[other] = jnp.zeros(ref.shape[1:], F32)


def _const_spec(shape):
    nd = len(shape)
    return pl.BlockSpec(shape, lambda *_: (0,) * nd)


def _resident_spec(shape):
    nd = len(shape)
    return pl.BlockSpec(shape, lambda *_: (0,) * nd, pipeline_mode=pl.Buffered(1))


def _layer_spec(a, layer, resident=False):
    nd = a.ndim
    kw = dict(pipeline_mode=pl.Buffered(1)) if resident else {}
    return pl.BlockSpec((None,) + a.shape[1:], lambda *_: (layer,) + (0,) * (nd - 1), **kw)


def _prompt_layer(layer, x, prm, tables, prev_out, tt=256):
    B, T, D = x.shape
    npairs = T // (2 * tt)
    layer_in = [prm[k] for k in ("ng", "win", "wout", "retg", "convw", "convb", "wr", "wi", "br", "bi",
                                 "lam", "qg", "kg")]
    vec_in = layer_in + list(tables)

    def next_tile(b, p):
        last = p == npairs - 1
        return (jnp.where(last, jnp.minimum(b + 1, B - 1), b), jnp.where(last, 0, 2 * p + 2), 0)

    in_specs = ([pl.BlockSpec(memory_space=pltpu.SMEM),
                 pl.BlockSpec((1, 2 * tt, D), lambda b, p: (b, p, 0)),
                 pl.BlockSpec((1, tt, D), next_tile)]
                + [_layer_spec(a, layer, resident=True) for a in layer_in]
                + [_resident_spec(a.shape) for a in tables]
                + [pl.BlockSpec(memory_space=pl.ANY) for _ in prev_out])
    out_shape = (
        jax.ShapeDtypeStruct((B, T, D), F32),
        jax.ShapeDtypeStruct((DEPTH, B, RET_HEADS, RET_DK, RET_DV), F32),
        jax.ShapeDtypeStruct((DEPTH, B, 1, LRU_WIDTH), F32),
        jax.ShapeDtypeStruct((DEPTH, B, CONV_W - 1, LRU_WIDTH), F32),
        jax.ShapeDtypeStruct((DEPTH, B, 2 * WINDOW, SWA_HEAD_DIM), F32),
        jax.ShapeDtypeStruct((DEPTH, B, 2 * WINDOW, SWA_HEAD_DIM), F32),
    )
    fill = not prev_out
    ld, li, slot = (DEPTH, 0, layer) if fill else (1, layer, 0)
    out_specs = (
        pl.BlockSpec((1, 2 * tt, D), lambda b, p: (b, p, 0)),
        pl.BlockSpec((ld, 1, RET_HEADS, RET_DK, RET_DV), lambda b, p: (li, b, 0, 0, 0)),
        pl.BlockSpec((ld, 1, 1, LRU_WIDTH), lambda b, p: (li, b, 0, 0)),
        pl.BlockSpec((ld, 1, CONV_W - 1, LRU_WIDTH), lambda b, p: (li, b, 0, 0)),
        pl.BlockSpec((ld, 1, 2 * WINDOW, SWA_HEAD_DIM), lambda b, p: (li, b, 0, 0)),
        pl.BlockSpec((ld, 1, 2 * WINDOW, SWA_HEAD_DIM), lambda b, p: (li, b, 0, 0)),
    )
    n_in = 3 + len(vec_in)
    aliases = {n_in + j: 1 + j for j in range(len(prev_out))}
    scratch = [
        pltpu.VMEM((tt, IN_WIDTH), F32),
        pltpu.VMEM((tt, IN_WIDTH), F32),
        pltpu.VMEM((tt, MIX_WIDTH), BF16),
        pltpu.VMEM((tt, MIX_WIDTH), BF16),
        pltpu.VMEM((SUBLANES, LRU_WIDTH), F32),
        pltpu.VMEM((WINDOW, 256), F32),
        pltpu.VMEM((WINDOW, 256), F32),
        pltpu.VMEM((1, LRU_WIDTH), F32),
    ]
    return pl.pallas_call(
        functools.partial(_prompt_kernel, tt=tt, npairs=npairs, layer=layer, slot=slot, fill=fill),
        grid=(B, npairs),
        in_specs=in_specs,
        out_specs=out_specs,
        out_shape=out_shape,
        scratch_shapes=scratch,
        input_output_aliases=aliases,
        compiler_params=pltpu.CompilerParams(
            dimension_semantics=("arbitrary", "arbitrary"), vmem_limit_bytes=VMEM_LIMIT),
        name="prompt_layer",
    )(prm["sinks"], x, x, *vec_in, *prev_out)


def _sproj_kernel(x_ref, ng_ref, win_ref, o_ref, hb_s):
    @pl.when(pl.program_id(0) == 0)
    def _norm_once():
        hb_s[...] = _rms(x_ref[...], ng_ref[...]).astype(BF16)

    o_ref[...] = _dot(hb_s[...], win_ref[...])


def _sample_proj(layer, xp, prm):
    R = xp.shape[0]
    nb = IN_WIDTH // 512
    ng, win = prm["ng"], prm["win"]
    return pl.pallas_call(
        _sproj_kernel,
        grid=(nb,),
        in_specs=[_const_spec(xp.shape), _layer_spec(ng, layer),
                  pl.BlockSpec((None, D_MODEL, 512), lambda j: (layer, 0, j))],
        out_specs=pl.BlockSpec((R, 512), lambda j: (0, j)),
        out_shape=jax.ShapeDtypeStruct((R, IN_WIDTH), F32),
        scratch_shapes=[pltpu.VMEM((R, D_MODEL), BF16)],
        compiler_params=pltpu.CompilerParams(
            dimension_semantics=("arbitrary",), vmem_limit_bytes=VMEM_LIMIT),
        name="sample_proj",
    )(xp, ng, win)


def _sample_mix_seq(sinks_ref, proj_ref, s0_ref, kbuf_ref, vbuf_ref, retg_ref, qg_ref, kg_ref,
                    cm_ref, qdec_ref, kdec_ref, mix_ref, snew_ref, knew_ref, vnew_ref):
    P = SEQ_PAD
    for h in range(RET_HEADS):
        cs = slice(h * 128, (h + 1) * 128)
        q = proj_ref[:, QA + h * 128:QA + (h + 1) * 128]
        k = proj_ref[:, KA + h * 128:KA + (h + 1) * 128] * (RET_DK ** -0.5)
        v = proj_ref[:, VA + h * 128:VA + (h + 1) * 128]
        intra = jnp.zeros((P, RET_DV), F32)
        for s in range(4):
            r_ = TOK0 + s
            w = jnp.sum(q * k[r_:r_ + 1, :], axis=-1, keepdims=True)
            intra = intra + (w * cm_ref[h, s]) * v[r_:r_ + 1, :]
        s_prev = s0_ref[h]
        cross = _dot((q * qdec_ref[h]).astype(BF16), s_prev.astype(BF16))
        kv = _dot_tn((k * kdec_ref[h]).astype(BF16), v.astype(BF16))
        yield
        snew_ref[h] = float(np.exp(np.float32(4.0) * _LOG_G[h])) * s_prev + kv
        o = _rms(intra + cross, retg_ref[:, cs])
        mix_ref[:, cs] = o * _silu(proj_ref[:, GA + h * 128:GA + (h + 1) * 128])
        yield

    r16 = lax.broadcasted_iota(jnp.int32, (2 * P, WINDOW), 0)
    j16 = lax.broadcasted_iota(jnp.int32, (2 * P, WINDOW), 1)
    rr = jnp.where(r16 >= P, r16 - P, r16)
    row_ok = jnp.logical_and(rr >= TOK0, rr < TOK0 + 4)
    dist = (rr - TOK0) + WINDOW - j16
    distf = dist.astype(F32)
    valid = jnp.logical_and(jnp.logical_and(dist >= 0, dist < WINDOW), row_ok)
    r16c = lax.broadcasted_iota(jnp.int32, (2 * P, 1), 0)
    rrc = jnp.where(r16c >= P, r16c - P, r16c)
    rowc_ok = jnp.logical_and(rrc >= TOK0, rrc < TOK0 + 4)
    knew_ref[0:2 * (WINDOW - 4), :] = kbuf_ref[2 * 4:2 * WINDOW, :]
    vnew_ref[0:2 * (WINDOW - 4), :] = vbuf_ref[2 * 4:2 * WINDOW, :]
    for kh in range(SWA_KV_HEADS):
        h0, h1 = kh * SWA_GROUP, kh * SWA_GROUP + 1
        kb = kbuf_ref[pl.ds(kh, WINDOW, stride=SWA_KV_HEADS), :]
        vb = vbuf_ref[pl.ds(kh, WINDOW, stride=SWA_KV_HEADS), :]
        kn = _rms(proj_ref[:, KC + kh * 128:KC + (kh + 1) * 128], kg_ref[...])
        vn = proj_ref[:, VC + kh * 128:VC + (kh + 1) * 128]
        q0 = _rms(proj_ref[:, QC + h0 * 128:QC + (h0 + 1) * 128], qg_ref[...])
        q1 = _rms(proj_ref[:, QC + h1 * 128:QC + (h1 + 1) * 128], qg_ref[...])
        qq = jnp.concatenate([q0, q1], axis=0)
        slope = jnp.where(r16 >= P, _SLOPES[h1], _SLOPES[h0])
        slopec = jnp.where(r16c >= P, _SLOPES[h1], _SLOPES[h0])
        sb = _dot_nt(qq.astype(BF16), kb.astype(BF16)) * (SWA_HEAD_DIM ** -0.5)
        sb = jnp.where(valid, sb - slope * distf, NEG_INF)
        yield
        sink = jnp.where(r16c >= P, sinks_ref[h1], sinks_ref[h0])
        m = jnp.maximum(jnp.max(sb, axis=-1, keepdims=True), sink)
        wn = []
        for s in range(4):
            r_ = TOK0 + s
            w = jnp.sum(qq * kn[r_:r_ + 1, :], axis=-1, keepdims=True) * (SWA_HEAD_DIM ** -0.5)
            dn = rrc - r_
            w = jnp.where(jnp.logical_and(dn >= 0, rowc_ok), w - slopec * dn.astype(F32), NEG_INF)
            wn.append(w)
            m = jnp.maximum(m, w)
        pb = jnp.exp(sb - m)
        denom = jnp.sum(pb, axis=-1, keepdims=True) + jnp.exp(sink - m)
        pn = [jnp.exp(w - m) for w in wn]
        for p_ in pn:
            denom = denom + p_
        yield
        o = _dot((pb / denom).astype(BF16), vb.astype(BF16))
        for s in range(4):
            r_ = TOK0 + s
            o = o + (pn[s] / denom) * vn[r_:r_ + 1, :]
        for g, hh in enumerate((h0, h1)):
            gc = proj_ref[:, GC + hh * 128:GC + (hh + 1) * 128]
            mix_ref[:, GROUP_WIDTH + hh * 128:GROUP_WIDTH + (hh + 1) * 128] = o[g * P:(g + 1) * P] * _silu(gc)
        for s in range(4):
            r_out = 2 * (WINDOW - 4 + s) + kh
            knew_ref[r_out:r_out + 1, :] = kn[TOK0 + s:TOK0 + s + 1, :]
            vnew_ref[r_out:r_out + 1, :] = vn[TOK0 + s:TOK0 + s + 1, :]
        yield


def _sample_mix_kernel(sinks_ref, proj_ref, s0_ref, kbuf_ref, vbuf_ref, retg_ref, qg_ref, kg_ref,
                       cm_ref, qdec_ref, kdec_ref, *rest, nseq, layer, slot, fill):
    mix_ref, snew_ref, knew_ref, vnew_ref = rest[-4:]
    sinks_ref = sinks_ref.at[layer]
    if fill:
        for other in [o for o in range(DEPTH) if o != slot]:
            for ref in (snew_ref, knew_ref, vnew_ref):
                ref[other] = jnp.zeros(ref.shape[1:], F32)
    gens = []
    for i in range(nseq):
        rows = slice(i * SEQ_PAD, (i + 1) * SEQ_PAD)
        gens.append(_sample_mix_seq(
            sinks_ref, proj_ref.at[rows, :], s0_ref.at[0, i], kbuf_ref.at[0, i], vbuf_ref.at[0, i],
            retg_ref, qg_ref, kg_ref, cm_ref, qdec_ref, kdec_ref,
            mix_ref.at[rows, :], snew_ref.at[slot, i], knew_ref.at[slot, i], vnew_ref.at[slot, i]))
    _run(_zip_stages(gens))


def _sample_mix(layer, proj, state_ret, cache_k, cache_v, prm, tables, prev_out, nseq=8):
    depth, B = state_ret.shape[:2]
    layer_in = [prm[k] for k in ("retg", "qg", "kg")]
    consts = layer_in + list(tables)
    st_spec = pl.BlockSpec((1, nseq, RET_HEADS, RET_DK, RET_DV), lambda i: (layer, i, 0, 0, 0))
    kv_spec = pl.BlockSpec((1, nseq, 2 * WINDOW, SWA_HEAD_DIM), lambda i: (layer, i, 0, 0))
    in_specs = ([pl.BlockSpec(memory_space=pltpu.SMEM),
                 pl.BlockSpec((nseq * SEQ_PAD, IN_WIDTH), lambda i: (i, 0)),
                 st_spec, kv_spec, kv_spec]
                + [_layer_spec(a, layer) for a in layer_in]
                + [_const_spec(a.shape) for a in tables]
                + [pl.BlockSpec(memory_space=pl.ANY) for _ in prev_out])
    out_shape = (
        jax.ShapeDtypeStruct((B * SEQ_PAD, 2 * GROUP_WIDTH), F32),
        jax.ShapeDtypeStruct(state_ret.shape, F32),
        jax.ShapeDtypeStruct(cache_k.shape, F32),
        jax.ShapeDtypeStruct(cache_v.shape, F32),
    )
    fill = not prev_out
    ld, li, slot = (DEPTH, 0, layer) if fill else (1, layer, 0)
    out_specs = (pl.BlockSpec((nseq * SEQ_PAD, 2 * GROUP_WIDTH), lambda i: (i, 0)),
                 pl.BlockSpec((ld, nseq, RET_HEADS, RET_DK, RET_DV), lambda i: (li, i, 0, 0, 0)),
                 pl.BlockSpec((ld, nseq, 2 * WINDOW, SWA_HEAD_DIM), lambda i: (li, i, 0, 0)),
                 pl.BlockSpec((ld, nseq, 2 * WINDOW, SWA_HEAD_DIM), lambda i: (li, i, 0, 0)))
    n_in = 5 + len(consts)
    aliases = {n_in + j: 1 + j for j in range(len(prev_out))}
    return pl.pallas_call(
        functools.partial(_sample_mix_kernel, nseq=nseq, layer=layer, slot=slot, fill=fill),
        grid=(B // nseq,),
        in_specs=in_specs,
        out_specs=out_specs,
        out_shape=out_shape,
        input_output_aliases=aliases,
        compiler_params=pltpu.CompilerParams(
            dimension_semantics=("arbitrary",), vmem_limit_bytes=VMEM_LIMIT),
        name="sample_mix",
    )(prm["sinks"], proj, state_ret, cache_k, cache_v, *consts, *prev_out)


def _sample_out_kernel(x_ref, xb_ref, gb_ref, mix_ref, conv8_ref, h8_ref, convw_ref, convb_ref,
                       wr_ref, wi_ref, br_ref, bi_ref, lam_ref, wout_ref,
                       y_ref, convo_ref, ho_ref):
    R = x_ref.shape[0]
    row = lax.broadcasted_iota(jnp.int32, (R, LRU_WIDTH), 0) & (SEQ_PAD - 1)
    xc = jnp.where(row < TOK0, conv8_ref[...], xb_ref[...])
    convo_ref[...] = xc
    conv = convb_ref[...] + convw_ref[0:1, :] * pltpu.roll(xc, 3, axis=0)
    conv = conv + convw_ref[1:2, :] * pltpu.roll(xc, 2, axis=0)
    conv = conv + convw_ref[2:3, :] * pltpu.roll(xc, 1, axis=0)
    conv = conv + convw_ref[3:4, :] * xc
    a, mult, ig = _lru_gates(conv, wr_ref, wi_ref, br_ref, bi_ref, lam_ref)
    u = mult * ig * conv
    h = h8_ref[...]
    for s in range(4):
        h = jnp.where(row == TOK0 + s, a * pltpu.roll(h, 1, axis=0) + u, h)
    ho_ref[...] = h
    ob = h * _silu(gb_ref[...])
    y = x_ref[...] + _dot(mix_ref[:, 0:GROUP_WIDTH].astype(BF16), wout_ref[0:GROUP_WIDTH, :])
    y = y + _dot(ob.astype(BF16), wout_ref[GROUP_WIDTH:2 * GROUP_WIDTH, :])
    y = y + _dot(mix_ref[:, GROUP_WIDTH:2 * GROUP_WIDTH].astype(BF16), wout_ref[2 * GROUP_WIDTH:, :])
    y_ref[...] = y


def _sample_out(layer, xp, proj, mix, conv8, h8, prm, rows=256):
    R = xp.shape[0]
    consts = [prm[k] for k in ("convw", "convb", "wr", "wi", "br", "bi", "lam", "wout")]
    in_specs = ([pl.BlockSpec((rows, D_MODEL), lambda i: (i, 0)),
                 pl.BlockSpec((rows, LRU_WIDTH), lambda i: (i, XB // LRU_WIDTH)),
                 pl.BlockSpec((rows, LRU_WIDTH), lambda i: (i, GB // LRU_WIDTH)),
                 pl.BlockSpec((rows, 2 * GROUP_WIDTH), lambda i: (i, 0)),
                 pl.BlockSpec((rows, LRU_WIDTH), lambda i: (i, 0)),
                 pl.BlockSpec((rows, LRU_WIDTH), lambda i: (i, 0))]
                + [_layer_spec(a, layer) for a in consts])
    out_shape = (
        jax.ShapeDtypeStruct((R, D_MODEL), F32),
        jax.ShapeDtypeStruct((R, LRU_WIDTH), F32),
        jax.ShapeDtypeStruct((R, LRU_WIDTH), F32),
    )
    out_specs = (
        pl.BlockSpec((rows, D_MODEL), lambda i: (i, 0)),
        pl.BlockSpec((rows, LRU_WIDTH), lambda i: (i, 0)),
        pl.BlockSpec((rows, LRU_WIDTH), lambda i: (i, 0)),
    )
    return pl.pallas_call(
        _sample_out_kernel,
        grid=(R // rows,),
        in_specs=in_specs,
        out_specs=out_specs,
        out_shape=out_shape,
        compiler_params=pltpu.CompilerParams(
            dimension_semantics=("arbitrary",), vmem_limit_bytes=VMEM_LIMIT),
        name="sample_out",
    )(xp, proj, proj, mix, conv8, h8, *consts)


def _prompt_tables():
    C = RET_CHUNK
    idx = np.arange(C, dtype=np.float32)
    diff = idx[:, None] - idx[None, :]
    causal = diff >= 0
    lg = _LOG_G[:, None, None]
    dmask = np.where(causal[None], np.exp(np.where(causal, diff, 0.0)[None] * lg), 0.0).astype(np.float32)
    qdec = np.exp((idx + 1.0)[None, :] * _LOG_G[:, None]).astype(np.float32)
    kdec = np.exp((C - 1 - idx)[None, :] * _LOG_G[:, None]).astype(np.float32)
    qdec = np.broadcast_to(qdec[:, :, None], (RET_HEADS, C, RET_DK)).copy()
    kdec = np.broadcast_to(kdec[:, :, None], (RET_HEADS, C, RET_DK)).copy()
    return jnp.asarray(dmask), jnp.asarray(qdec), jnp.asarray(kdec)


def _sample_tables():
    P = SEQ_PAD
    rows = np.arange(P, dtype=np.float32)
    i = rows - TOK0
    tok = (i >= 0) & (i < 4)
    cm = np.zeros((RET_HEADS, 4, P, RET_DV), np.float32)
    qdec = np.zeros((RET_HEADS, P, RET_DK), np.float32)
    kdec = np.zeros((RET_HEADS, P, RET_DK), np.float32)
    for h in range(RET_HEADS):
        for s in range(4):
            d = i - s
            col = np.where(tok & (d >= 0), np.exp(np.where(d >= 0, d, 0.0) * _LOG_G[h]), 0.0)
            cm[h, s] = col[:, None]
        qdec[h] = np.where(tok, np.exp((i + 1.0) * _LOG_G[h]), 0.0)[:, None]
        kdec[h] = np.where(tok, np.exp((3.0 - i) * _LOG_G[h]), 0.0)[:, None]
    return jnp.asarray(cm), jnp.asarray(qdec), jnp.asarray(kdec)


def _block_diag(w):
    L, n, d, _ = w.shape
    g = n // 2
    w = w.reshape(L, 2, g, d, d)
    eye = jnp.eye(g, dtype=w.dtype)
    return (eye[None, None, :, None, :, None] * w[:, :, :, :, None, :]).reshape(L, 2, g * d, g * d)


def kernel(x_prompt, x_sample, state_ret, state_lru, state_conv, cache_swa_k, cache_swa_v, norm_g, w_in, w_out, ret_norm_g, conv_w, conv_b, w_rgate, b_rgate, w_igate, b_igate, lru_lambda, q_norm_g, k_norm_g, attn_sinks):
    Bs, Ts, _ = x_sample.shape
    ptab = _prompt_tables()
    stab = _sample_tables()
    yp = x_prompt
    ys = jnp.pad(x_sample, ((0, 0), (TOK0, SEQ_PAD - TOK0 - Ts), (0, 0))).reshape(Bs * SEQ_PAD, D_MODEL)
    cache_k = cache_swa_k.reshape(DEPTH, Bs, 2 * WINDOW, SWA_HEAD_DIM)
    cache_v = cache_swa_v.reshape(DEPTH, Bs, 2 * WINDOW, SWA_HEAD_DIM)
    Bp = x_prompt.shape[0]
    p_state, s_state = (), ()
    s_h, s_conv = [], []
    row = lambda a: a.reshape(DEPTH, 1, -1)
    prm = dict(
        sinks=attn_sinks, ng=row(norm_g), win=w_in.astype(BF16), wout=w_out.astype(BF16), retg=row(ret_norm_g),
        convw=conv_w, convb=row(conv_b), wr=_block_diag(w_rgate).astype(BF16), wi=_block_diag(w_igate).astype(BF16),
        br=row(b_rgate), bi=row(b_igate), lam=row(lru_lambda), qg=row(q_norm_g), kg=row(k_norm_g))
    for l in range(DEPTH):
        yp, *p_state = _prompt_layer(l, yp, prm, ptab, tuple(p_state))

        proj = _sample_proj(l, ys, prm)
        mix, *s_state = _sample_mix(l, proj, state_ret, cache_k, cache_v, prm, stab, tuple(s_state))
        conv8 = jnp.pad(state_conv[l], ((0, 0), (0, SEQ_PAD - (CONV_W - 1)), (0, 0))).reshape(Bs * SEQ_PAD, LRU_WIDTH)
        h8 = jnp.pad(state_lru[l][:, None, :], ((0, 0), (TOK0 - 1, SEQ_PAD - TOK0), (0, 0))).reshape(Bs * SEQ_PAD, LRU_WIDTH)
        ys, convo, ho = _sample_out(l, ys, proj, mix, conv8, h8, prm)
        s_h.append(ho.reshape(Bs, SEQ_PAD, LRU_WIDTH)[:, TOK0 + Ts - 1])
        s_conv.append(convo.reshape(Bs, SEQ_PAD, LRU_WIDTH)[:, TOK0 + Ts - (CONV_W - 1):TOK0 + Ts])

    p_ret, p_h, p_conv, p_k, p_v = p_state
    s_ret, s_k, s_v = s_state
    kv5 = lambda a, n: a.reshape(DEPTH, n, WINDOW, SWA_KV_HEADS, SWA_HEAD_DIM)
    y_sample = ys.reshape(Bs, SEQ_PAD, D_MODEL)[:, TOK0:TOK0 + Ts]
    return (yp, y_sample,
            p_ret, p_h.reshape(DEPTH, Bp, LRU_WIDTH), p_conv, kv5(p_k, Bp), kv5(p_v, Bp),
            s_ret, jnp.stack(s_h), jnp.stack(s_conv), kv5(s_k, Bs), kv5(s_v, Bs))
```

```python
import functools
import math

import numpy as np
import jax
import jax.numpy as jnp
from jax import lax
from jax.experimental import pallas as pl
from jax.experimental.pallas import tpu as pltpu

D_MODEL = 1024
DEPTH = 2
PAST_LEN = 16384
GROUP_WIDTH = 512
RET_HEADS = 4
RET_DK = 128
RET_DV = 128
RET_CHUNK = 128
LRU_WIDTH = 512
LRU_BLOCKS = 8
LRU_C = 8.0
CONV_W = 4
SWA_HEADS = 4
SWA_KV_HEADS = 2
SWA_GROUP = 2
SWA_HEAD_DIM = 128
WINDOW = 128
NORM_EPS = 1e-6
NEG_INF = -1e30

IN_WIDTH = 4608
MIX_WIDTH = 1536
QA, KA, VA, GA, XB, GB, QC, KC, VC, GC = 0, 512, 1024, 1536, 2048, 2560, 3072, 3584, 3840, 4096

F32 = jnp.float32
BF16 = jnp.bfloat16

SUBLANES = 8
SEQ_PAD = 8
TOK0 = 3
VMEM_LIMIT = 56 * 1024 * 1024
PITCH_PAD = 128

_LOG_G = np.log1p(-np.power(np.float32(2.0), (-5.0 - np.arange(RET_HEADS)).astype(np.float32))).astype(np.float32)
_SLOPES = [2.0 ** (-8.0 * (h + 1) / SWA_HEADS) for h in range(SWA_HEADS)]


def _rms(x, g):
    ms = jnp.mean(x * x, axis=-1, keepdims=True)
    return x * lax.rsqrt(ms + NORM_EPS) * g


def _silu(x):
    return x * jax.nn.sigmoid(x)


def _softplus(x):
    return jnp.maximum(x, 0.0) + jnp.log1p(jnp.exp(-jnp.abs(x)))


def _dot(a, b):
    return jnp.dot(a, b, preferred_element_type=F32)


def _dot_nt(a, b):
    return lax.dot_general(a, b, (((1,), (1,)), ((), ())), preferred_element_type=F32)


def _dot_tn(a, b):
    return lax.dot_general(a, b, (((0,), (0,)), ((), ())), preferred_element_type=F32)


def _lru_gates(conv, wr_ref, wi_ref, br_ref, bi_ref, lam_ref):
    gin = conv.astype(BF16)
    half = LRU_WIDTH // 2
    lo, hi = gin[:, :half], gin[:, half:]
    r = jax.nn.sigmoid(jnp.concatenate([_dot(lo, wr_ref[0]), _dot(hi, wr_ref[1])], axis=1) + br_ref[...])
    i = jax.nn.sigmoid(jnp.concatenate([_dot(lo, wi_ref[0]), _dot(hi, wi_ref[1])], axis=1) + bi_ref[...])
    log_a = -LRU_C * r * _softplus(-lam_ref[...])
    a = jnp.exp(log_a)
    m2 = -jnp.tanh(log_a) * (1.0 + a * a)
    mult = jnp.where(m2 > 0.0, m2 * lax.rsqrt(m2), 0.0)
    return a, mult, i


_DONE = object()


def _run(gen):
    for _ in gen:
        pass


def _zip_stages(gens):
    gens = list(gens)
    while gens:
        gens = [g for g in gens if next(g, _DONE) is not _DONE]
        if gens:
            yield


def _chain(gens):
    for g in gens:
        yield from g


def _interleave(main, sides, n_main, n_side):
    side = _chain(sides)
    next(side)
    done_side = 1
    for i, _ in enumerate(main):
        want = ((i + 1) * n_side) // n_main
        while done_side < want and next(side, _DONE) is not _DONE:
            done_side += 1
    _run(side)


IN_PROJ_COLS = 256
IN_PROJ_STAGES = 1 + IN_WIDTH // IN_PROJ_COLS


def _in_proj(x_ref, ng_ref, win_ref, proj_ref):
    hb = _rms(x_ref[...], ng_ref[...]).astype(BF16)
    yield
    for j in range(IN_WIDTH // IN_PROJ_COLS):
        cols = slice(j * IN_PROJ_COLS, (j + 1) * IN_PROJ_COLS)
        proj_ref[:, cols] = _dot(hb, win_ref[:, cols])
        yield


OUT_PROJ_STAGES = D_MODEL // 256


def _out_proj(mixed_s, x_ref, y_ref, wout_ref):
    mixed = mixed_s[:, 0:MIX_WIDTH]
    for j in range(OUT_PROJ_STAGES):
        cols = slice(j * 256, (j + 1) * 256)
        y_ref[:, cols] = x_ref[:, cols] + _dot(mixed, wout_ref[:, cols])
        yield


def _mix_stages(tt):
    return 4 + tt // SUBLANES // 8 + (tt // RET_CHUNK) * 5


def _scan_rows(a, u, h_in, tt):
    G = tt // SUBLANES
    W = a.shape[-1]
    a3 = a.reshape(G, SUBLANES, W)
    u3 = u.reshape(G, SUBLANES, W)
    r3 = lax.broadcasted_iota(jnp.int32, (G, SUBLANES, W), 1)
    sh = 1
    while sh < SUBLANES:
        keep = r3 >= sh
        a_sh = jnp.where(keep, pltpu.roll(a3, sh, axis=1), 1.0)
        u_sh = jnp.where(keep, pltpu.roll(u3, sh, axis=1), 0.0)
        u3 = a3 * u_sh + u3
        a3 = a3 * a_sh
        sh *= 2
    yield
    hs = []
    for g in range(G):
        hg = a3[g] * h_in + u3[g]
        hs.append(hg)
        h_in = hg[SUBLANES - 1:SUBLANES, :]
        if g % 8 == 7:
            yield
    return jnp.concatenate(hs, axis=0)


def _retention_head(h, proj_s, r0, s_ref, retg_ref, dmask_ref, qdec_ref, kdec_ref, mixed_s):
    C = RET_CHUNK
    cs = slice(h * RET_DK, (h + 1) * RET_DK)
    q = proj_s[r0:r0 + C, QA + h * 128:QA + (h + 1) * 128]
    k = proj_s[r0:r0 + C, KA + h * 128:KA + (h + 1) * 128] * (RET_DK ** -0.5)
    vb = proj_s[r0:r0 + C, VA + h * 128:VA + (h + 1) * 128].astype(BF16)
    sc = _dot_nt(q.astype(BF16), k.astype(BF16)) * dmask_ref[h]
    yield
    s_prev = s_ref[0, h]
    lhs = jnp.concatenate([sc.astype(BF16), (q * qdec_ref[h]).astype(BF16)], axis=1)
    rhs = jnp.concatenate([vb, s_prev.astype(BF16)], axis=0)
    o = _dot(lhs, rhs)
    yield
    kv = _dot_tn((k * kdec_ref[h]).astype(BF16), vb)
    s_ref[0, h] = float(np.exp(np.float32(C) * _LOG_G[h])) * s_prev + kv
    yield
    o = _rms(o, retg_ref[:, cs])
    mixed_s[r0:r0 + C, cs] = (o * _silu(proj_s[r0:r0 + C, GA + h * 128:GA + (h + 1) * 128])).astype(BF16)
    yield


def _swa_kv_head(kh, proj_s, r0, blk, sinks_ref, qg_ref, kg_ref, mixed_s, kprev_s, vprev_s):
    C = RET_CHUNK
    ks = slice(kh * 128, (kh + 1) * 128)
    h0, h1 = kh * SWA_GROUP, kh * SWA_GROUP + 1
    kn = _rms(proj_s[r0:r0 + C, KC + kh * 128:KC + (kh + 1) * 128], kg_ref[...])
    vv = proj_s[r0:r0 + C, VC + kh * 128:VC + (kh + 1) * 128]
    kband = jnp.concatenate([kprev_s[:, ks], kn], axis=0).astype(BF16)
    vband = jnp.concatenate([vprev_s[:, ks], vv], axis=0).astype(BF16)
    kprev_s[:, ks] = kn
    vprev_s[:, ks] = vv
    q0 = _rms(proj_s[r0:r0 + C, QC + h0 * 128:QC + (h0 + 1) * 128], qg_ref[...])
    q1 = _rms(proj_s[r0:r0 + C, QC + h1 * 128:QC + (h1 + 1) * 128], qg_ref[...])
    qq = jnp.concatenate([q0, q1], axis=0).astype(BF16)
    yield
    s = _dot_nt(qq, kband) * (SWA_HEAD_DIM ** -0.5)
    ii = lax.broadcasted_iota(jnp.int32, (2 * C, 2 * C), 0)
    jj = lax.broadcasted_iota(jnp.int32, (2 * C, 2 * C), 1)
    dist = jnp.where(ii >= C, ii - C, ii) + C - jj
    valid = jnp.logical_and(jnp.logical_and(dist >= 0, dist < WINDOW), jj >= jnp.where(blk > 0, 0, C))
    slope = jnp.where(ii >= C, _SLOPES[h1], _SLOPES[h0])
    s = jnp.where(valid, s - slope * dist.astype(F32), NEG_INF)
    yield
    rowc = lax.broadcasted_iota(jnp.int32, (2 * C, 1), 0)
    sink = jnp.where(rowc >= C, sinks_ref[h1], sinks_ref[h0])
    m = jnp.maximum(jnp.max(s, axis=-1, keepdims=True), sink)
    p = jnp.exp(s - m)
    denom = jnp.sum(p, axis=-1, keepdims=True) + jnp.exp(sink - m)
    yield
    o = _dot((p / denom).astype(BF16), vband)
    yield
    for g, hh in enumerate((h0, h1)):
        gc = proj_s[r0:r0 + C, GC + hh * 128:GC + (hh + 1) * 128]
        mixed_s[r0:r0 + C, 2 * GROUP_WIDTH + hh * 128:2 * GROUP_WIDTH + (hh + 1) * 128] = (
            (o[g * C:(g + 1) * C] * _silu(gc)).astype(BF16))
    yield


def _mix_tile(proj_s, t, tt, sinks_ref, retg_ref, convw_ref, convb_ref,
              wr_ref, wi_ref, br_ref, bi_ref, lam_ref, qg_ref, kg_ref, dmask_ref, qdec_ref, kdec_ref,
              s_ref, mixed_s, xtail_s, kprev_s, vprev_s, hc_s):
    nch = tt // RET_CHUNK
    C = RET_CHUNK

    xb = proj_s[:, XB:XB + LRU_WIDTH]
    xfull = jnp.concatenate([xtail_s[...], xb], axis=0)
    xtail_s[...] = xb[tt - SUBLANES:tt, :]
    conv = convb_ref[...] + convw_ref[0:1, :] * pltpu.roll(xfull, 3, axis=0)[SUBLANES:]
    conv = conv + convw_ref[1:2, :] * pltpu.roll(xfull, 2, axis=0)[SUBLANES:]
    conv = conv + convw_ref[2:3, :] * pltpu.roll(xfull, 1, axis=0)[SUBLANES:]
    conv = conv + convw_ref[3:4, :] * xb
    yield

    a, mult, ig = _lru_gates(conv, wr_ref, wi_ref, br_ref, bi_ref, lam_ref)
    yield
    row = lax.broadcasted_iota(jnp.int32, (tt, LRU_WIDTH), 0)
    mult = jnp.where(row == jnp.where(t == 0, 0, -1), 1.0, mult)
    u = mult * ig * conv
    hseq = yield from _scan_rows(a, u, hc_s[...], tt)
    hc_s[...] = hseq[tt - 1:tt, :]
    mixed_s[:, GROUP_WIDTH:2 * GROUP_WIDTH] = (hseq * _silu(proj_s[:, GB:GB + LRU_WIDTH])).astype(BF16)
    yield

    for c in range(nch):
        r0 = c * C
        blk = t * nch + c
        yield from _zip_stages(
            [_retention_head(h, proj_s, r0, s_ref, retg_ref, dmask_ref, qdec_ref, kdec_ref, mixed_s)
             for h in range(RET_HEADS)]
            + [_swa_kv_head(kh, proj_s, r0, blk, sinks_ref, qg_ref, kg_ref, mixed_s, kprev_s, vprev_s)
               for kh in range(SWA_KV_HEADS)])


def _prompt_kernel(sinks_ref, xa_ref, xn_ref, ng_ref, win_ref, wout_ref, retg_ref, convw_ref, convb_ref,
                   wr_ref, wi_ref, br_ref, bi_ref, lam_ref, qg_ref, kg_ref,
                   dmask_ref, qdec_ref, kdec_ref, *rest, tt, npairs, layer, slot, fill):
    y_ref, s_ref, h_ref, conv_ref, kout_ref, vout_ref = rest[-14:-8]
    proj_a, proj_b, mixed_a, mixed_b, xtail_s, kprev_s, vprev_s, hc_s = rest[-8:]
    sinks_ref = sinks_ref.at[layer]
    b = pl.program_id(0)
    p = pl.program_id(1)

    @pl.when(p == 0)
    def _init():
        s_ref[...] = jnp.zeros_like(s_ref)
        xtail_s[...] = jnp.zeros_like(xtail_s)
        kprev_s[...] = jnp.zeros_like(kprev_s)
        vprev_s[...] = jnp.zeros_like(vprev_s)
        hc_s[...] = jnp.zeros_like(hc_s)

    xa0, xa1 = xa_ref.at[0, 0:tt, :], xa_ref.at[0, tt:2 * tt, :]
    ya0, ya1 = y_ref.at[0, 0:tt, :], y_ref.at[0, tt:2 * tt, :]

    @pl.when(jnp.logical_and(b == 0, p == 0))
    def _prologue():
        _run(_in_proj(xa0, ng_ref, win_ref, proj_a))

    common = (sinks_ref, retg_ref, convw_ref, convb_ref, wr_ref, wi_ref, br_ref, bi_ref,
              lam_ref, qg_ref, kg_ref, dmask_ref, qdec_ref, kdec_ref, s_ref.at[slot])
    state = (xtail_s, kprev_s, vprev_s, hc_s)

    n_main = _mix_stages(tt)
    _interleave(_mix_tile(proj_a, 2 * p, tt, *common, mixed_a, *state),
                [_in_proj(xa1, ng_ref, win_ref, proj_b)], n_main, IN_PROJ_STAGES)
    _interleave(_mix_tile(proj_b, 2 * p + 1, tt, *common, mixed_b, *state),
                [_out_proj(mixed_a, xa0, ya0, wout_ref), _in_proj(xn_ref.at[0], ng_ref, win_ref, proj_a)],
                n_main, OUT_PROJ_STAGES + IN_PROJ_STAGES)
    _run(_out_proj(mixed_b, xa1, ya1, wout_ref))

    @pl.when(p == npairs - 1)
    def _state_out():
        h_ref[slot, 0] = hc_s[...]
        conv_ref[slot, 0] = xtail_s[SUBLANES - (CONV_W - 1):SUBLANES, :]
        for kh in range(SWA_KV_HEADS):
            ks = slice(kh * 128, (kh + 1) * 128)
            kout_ref[slot, 0, pl.ds(kh, WINDOW, stride=SWA_KV_HEADS), :] = kprev_s[:, ks]
            vout_ref[slot, 0, pl.ds(kh, WINDOW, stride=SWA_KV_HEADS), :] = vprev_s[:, ks]
        if fill:
            for other in [o for o in range(DEPTH) if o != slot]:
                for ref in (h_ref, conv_ref, kout_ref, vout_ref):
                    ref[other] = jnp.zeros(ref.shape[1:], F32)


def _const_spec(shape):
    nd = len(shape)
    return pl.BlockSpec(shape, lambda *_: (0,) * nd)


def _resident_spec(shape):
    nd = len(shape)
    return pl.BlockSpec(shape, lambda *_: (0,) * nd, pipeline_mode=pl.Buffered(1))


def _layer_spec(a, layer, resident=False):
    nd = a.ndim
    kw = dict(pipeline_mode=pl.Buffered(1)) if resident else {}
    return pl.BlockSpec((None,) + a.shape[1:], lambda *_: (layer,) + (0,) * (nd - 1), **kw)


def _prompt_layer(layer, x, prm, tables, prev_out, tt=256):
    B, T, D = x.shape
    npairs = T // (2 * tt)
    layer_in = [prm[k] for k in ("ng", "win", "wout", "retg", "convw", "convb", "wr", "wi", "br", "bi",
                                 "lam", "qg", "kg")]
    vec_in = layer_in + list(tables)

    def next_tile(b, p):
        last = p == npairs - 1
        return (jnp.where(last, jnp.minimum(b + 1, B - 1), b), jnp.where(last, 0, 2 * p + 2), 0)

    in_specs = ([pl.BlockSpec(memory_space=pltpu.SMEM),
                 pl.BlockSpec((1, 2 * tt, D), lambda b, p: (b, p, 0)),
                 pl.BlockSpec((1, tt, D), next_tile)]
                + [_layer_spec(a, layer, resident=True) for a in layer_in]
                + [_resident_spec(a.shape) for a in tables]
                + [pl.BlockSpec(memory_space=pl.ANY) for _ in prev_out])
    out_shape = (
        jax.ShapeDtypeStruct((B, T, D), F32),
        jax.ShapeDtypeStruct((DEPTH, B, RET_HEADS, RET_DK, RET_DV), F32),
        jax.ShapeDtypeStruct((DEPTH, B, 1, LRU_WIDTH), F32),
        jax.ShapeDtypeStruct((DEPTH, B, CONV_W - 1, LRU_WIDTH), F32),
        jax.ShapeDtypeStruct((DEPTH, B, 2 * WINDOW, SWA_HEAD_DIM), F32),
        jax.ShapeDtypeStruct((DEPTH, B, 2 * WINDOW, SWA_HEAD_DIM), F32),
    )
    fill = not prev_out
    ld, li, slot = (DEPTH, 0, layer) if fill else (1, layer, 0)
    out_specs = (
        pl.BlockSpec((1, 2 * tt, D), lambda b, p: (b, p, 0)),
        pl.BlockSpec((ld, 1, RET_HEADS, RET_DK, RET_DV), lambda b, p: (li, b, 0, 0, 0)),
        pl.BlockSpec((ld, 1, 1, LRU_WIDTH), lambda b, p: (li, b, 0, 0)),
        pl.BlockSpec((ld, 1, CONV_W - 1, LRU_WIDTH), lambda b, p: (li, b, 0, 0)),
        pl.BlockSpec((ld, 1, 2 * WINDOW, SWA_HEAD_DIM), lambda b, p: (li, b, 0, 0)),
        pl.BlockSpec((ld, 1, 2 * WINDOW, SWA_HEAD_DIM), lambda b, p: (li, b, 0, 0)),
    )
    n_in = 3 + len(vec_in)
    aliases = {n_in + j: 1 + j for j in range(len(prev_out))}
    scratch = [
        pltpu.VMEM((tt, IN_WIDTH + PITCH_PAD), F32),
        pltpu.VMEM((tt, IN_WIDTH + PITCH_PAD), F32),
        pltpu.VMEM((tt, MIX_WIDTH + PITCH_PAD), BF16),
        pltpu.VMEM((tt, MIX_WIDTH + PITCH_PAD), BF16),
        pltpu.VMEM((SUBLANES, LRU_WIDTH), F32),
        pltpu.VMEM((WINDOW, 256), F32),
        pltpu.VMEM((WINDOW, 256), F32),
        pltpu.VMEM((1, LRU_WIDTH), F32),
    ]
    return pl.pallas_call(
        functools.partial(_prompt_kernel, tt=tt, npairs=npairs, layer=layer, slot=slot, fill=fill),
        grid=(B, npairs),
        in_specs=in_specs,
        out_specs=out_specs,
        out_shape=out_shape,
        scratch_shapes=scratch,
        input_output_aliases=aliases,
        compiler_params=pltpu.CompilerParams(
            dimension_semantics=("arbitrary", "arbitrary"), vmem_limit_bytes=VMEM_LIMIT),
        name="prompt_layer",
    )(prm["sinks"], x, x, *vec_in, *prev_out)


def _sproj_kernel(x_ref, ng_ref, win_ref, o_ref, hb_s):
    @pl.when(pl.program_id(0) == 0)
    def _norm_once():
        hb_s[...] = _rms(x_ref[...], ng_ref[...]).astype(BF16)

    o_ref[...] = _dot(hb_s[...], win_ref[...])


def _sample_proj(layer, xp, prm):
    R = xp.shape[0]
    nb = IN_WIDTH // 512
    ng, win = prm["ng"], prm["win"]
    return pl.pallas_call(
        _sproj_kernel,
        grid=(nb,),
        in_specs=[_const_spec(xp.shape), _layer_spec(ng, layer),
                  pl.BlockSpec((None, D_MODEL, 512), lambda j: (layer, 0, j))],
        out_specs=pl.BlockSpec((R, 512), lambda j: (0, j)),
        out_shape=jax.ShapeDtypeStruct((R, IN_WIDTH), F32),
        scratch_shapes=[pltpu.VMEM((R, D_MODEL), BF16)],
        compiler_params=pltpu.CompilerParams(
            dimension_semantics=("arbitrary",), vmem_limit_bytes=VMEM_LIMIT),
        name="sample_proj",
    )(xp, ng, win)


def _sample_mix_seq(sinks_ref, proj_ref, s0_ref, kbuf_ref, vbuf_ref, retg_ref, qg_ref, kg_ref,
                    cm_ref, qdec_ref, kdec_ref, mix_ref, snew_ref, knew_ref, vnew_ref):
    P = SEQ_PAD
    for h in range(RET_HEADS):
        cs = slice(h * 128, (h + 1) * 128)
        q = proj_ref[:, QA + h * 128:QA + (h + 1) * 128]
        k = proj_ref[:, KA + h * 128:KA + (h + 1) * 128] * (RET_DK ** -0.5)
        v = proj_ref[:, VA + h * 128:VA + (h + 1) * 128]
        intra = jnp.zeros((P, RET_DV), F32)
        for s in range(4):
            r_ = TOK0 + s
            w = jnp.sum(q * k[r_:r_ + 1, :], axis=-1, keepdims=True)
            intra = intra + (w * cm_ref[h, s]) * v[r_:r_ + 1, :]
        s_prev = s0_ref[h]
        cross = _dot((q * qdec_ref[h]).astype(BF16), s_prev.astype(BF16))
        kv = _dot_tn((k * kdec_ref[h]).astype(BF16), v.astype(BF16))
        yield
        snew_ref[h] = float(np.exp(np.float32(4.0) * _LOG_G[h])) * s_prev + kv
        o = _rms(intra + cross, retg_ref[:, cs])
        mix_ref[:, cs] = o * _silu(proj_ref[:, GA + h * 128:GA + (h + 1) * 128])
        yield

    r16 = lax.broadcasted_iota(jnp.int32, (2 * P, WINDOW), 0)
    j16 = lax.broadcasted_iota(jnp.int32, (2 * P, WINDOW), 1)
    rr = jnp.where(r16 >= P, r16 - P, r16)
    row_ok = jnp.logical_and(rr >= TOK0, rr < TOK0 + 4)
    dist = (rr - TOK0) + WINDOW - j16
    distf = dist.astype(F32)
    valid = jnp.logical_and(jnp.logical_and(dist >= 0, dist < WINDOW), row_ok)
    r16c = lax.broadcasted_iota(jnp.int32, (2 * P, 1), 0)
    rrc = jnp.where(r16c >= P, r16c - P, r16c)
    rowc_ok = jnp.logical_and(rrc >= TOK0, rrc < TOK0 + 4)
    knew_ref[0:2 * (WINDOW - 4), :] = kbuf_ref[2 * 4:2 * WINDOW, :]
    vnew_ref[0:2 * (WINDOW - 4), :] = vbuf_ref[2 * 4:2 * WINDOW, :]
    for kh in range(SWA_KV_HEADS):
        h0, h1 = kh * SWA_GROUP, kh * SWA_GROUP + 1
        kb = kbuf_ref[pl.ds(kh, WINDOW, stride=SWA_KV_HEADS), :]
        vb = vbuf_ref[pl.ds(kh, WINDOW, stride=SWA_KV_HEADS), :]
        kn = _rms(proj_ref[:, KC + kh * 128:KC + (kh + 1) * 128], kg_ref[...])
        vn = proj_ref[:, VC + kh * 128:VC + (kh + 1) * 128]
        q0 = _rms(proj_ref[:, QC + h0 * 128:QC + (h0 + 1) * 128], qg_ref[...])
        q1 = _rms(proj_ref[:, QC + h1 * 128:QC + (h1 + 1) * 128], qg_ref[...])
        qq = jnp.concatenate([q0, q1], axis=0)
        slope = jnp.where(r16 >= P, _SLOPES[h1], _SLOPES[h0])
        slopec = jnp.where(r16c >= P, _SLOPES[h1], _SLOPES[h0])
        sb = _dot_nt(qq.astype(BF16), kb.astype(BF16)) * (SWA_HEAD_DIM ** -0.5)
        sb = jnp.where(valid, sb - slope * distf, NEG_INF)
        yield
        sink = jnp.where(r16c >= P, sinks_ref[h1], sinks_ref[h0])
        m = jnp.maximum(jnp.max(sb, axis=-1, keepdims=True), sink)
        wn = []
        for s in range(4):
            r_ = TOK0 + s
            w = jnp.sum(qq * kn[r_:r_ + 1, :], axis=-1, keepdims=True) * (SWA_HEAD_DIM ** -0.5)
            dn = rrc - r_
            w = jnp.where(jnp.logical_and(dn >= 0, rowc_ok), w - slopec * dn.astype(F32), NEG_INF)
            wn.append(w)
            m = jnp.maximum(m, w)
        pb = jnp.exp(sb - m)
        denom = jnp.sum(pb, axis=-1, keepdims=True) + jnp.exp(sink - m)
        pn = [jnp.exp(w - m) for w in wn]
        for p_ in pn:
            denom = denom + p_
        yield
        o = _dot((pb / denom).astype(BF16), vb.astype(BF16))
        for s in range(4):
            r_ = TOK0 + s
            o = o + (pn[s] / denom) * vn[r_:r_ + 1, :]
        for g, hh in enumerate((h0, h1)):
            gc = proj_ref[:, GC + hh * 128:GC + (hh + 1) * 128]
            mix_ref[:, GROUP_WIDTH + hh * 128:GROUP_WIDTH + (hh + 1) * 128] = o[g * P:(g + 1) * P] * _silu(gc)
        for s in range(4):
            r_out = 2 * (WINDOW - 4 + s) + kh
            knew_ref[r_out:r_out + 1, :] = kn[TOK0 + s:TOK0 + s + 1, :]
            vnew_ref[r_out:r_out + 1, :] = vn[TOK0 + s:TOK0 + s + 1, :]
        yield


def _sample_mix_kernel(sinks_ref, proj_ref, s0_ref, kbuf_ref, vbuf_ref, retg_ref, qg_ref, kg_ref,
                       cm_ref, qdec_ref, kdec_ref, *rest, nseq, layer, slot, fill):
    mix_ref, snew_ref, knew_ref, vnew_ref = rest[-4:]
    sinks_ref = sinks_ref.at[layer]
    if fill:
        for other in [o for o in range(DEPTH) if o != slot]:
            for ref in (snew_ref, knew_ref, vnew_ref):
                ref[other] = jnp.zeros(ref.shape[1:], F32)
    gens = []
    for i in range(nseq):
        rows = slice(i * SEQ_PAD, (i + 1) * SEQ_PAD)
        gens.append(_sample_mix_seq(
            sinks_ref, proj_ref.at[rows, :], s0_ref.at[0, i], kbuf_ref.at[0, i], vbuf_ref.at[0, i],
            retg_ref, qg_ref, kg_ref, cm_ref, qdec_ref, kdec_ref,
            mix_ref.at[rows, :], snew_ref.at[slot, i], knew_ref.at[slot, i], vnew_ref.at[slot, i]))
    _run(_zip_stages(gens))


def _sample_mix(layer, proj, state_ret, cache_k, cache_v, prm, tables, prev_out, nseq=8):
    depth, B = state_ret.shape[:2]
    layer_in = [prm[k] for k in ("retg", "qg", "kg")]
    consts = layer_in + list(tables)
    st_spec = pl.BlockSpec((1, nseq, RET_HEADS, RET_DK, RET_DV), lambda i: (layer, i, 0, 0, 0))
    kv_spec = pl.BlockSpec((1, nseq, 2 * WINDOW, SWA_HEAD_DIM), lambda i: (layer, i, 0, 0))
    in_specs = ([pl.BlockSpec(memory_space=pltpu.SMEM),
                 pl.BlockSpec((nseq * SEQ_PAD, IN_WIDTH), lambda i: (i, 0)),
                 st_spec, kv_spec, kv_spec]
                + [_layer_spec(a, layer) for a in layer_in]
                + [_const_spec(a.shape) for a in tables]
                + [pl.BlockSpec(memory_space=pl.ANY) for _ in prev_out])
    out_shape = (
        jax.ShapeDtypeStruct((B * SEQ_PAD, 2 * GROUP_WIDTH), F32),
        jax.ShapeDtypeStruct(state_ret.shape, F32),
        jax.ShapeDtypeStruct(cache_k.shape, F32),
        jax.ShapeDtypeStruct(cache_v.shape, F32),
    )
    fill = not prev_out
    ld, li, slot = (DEPTH, 0, layer) if fill else (1, layer, 0)
    out_specs = (pl.BlockSpec((nseq * SEQ_PAD, 2 * GROUP_WIDTH), lambda i: (i, 0)),
                 pl.BlockSpec((ld, nseq, RET_HEADS, RET_DK, RET_DV), lambda i: (li, i, 0, 0, 0)),
                 pl.BlockSpec((ld, nseq, 2 * WINDOW, SWA_HEAD_DIM), lambda i: (li, i, 0, 0)),
                 pl.BlockSpec((ld, nseq, 2 * WINDOW, SWA_HEAD_DIM), lambda i: (li, i, 0, 0)))
    n_in = 5 + len(consts)
    aliases = {n_in + j: 1 + j for j in range(len(prev_out))}
    return pl.pallas_call(
        functools.partial(_sample_mix_kernel, nseq=nseq, layer=layer, slot=slot, fill=fill),
        grid=(B // nseq,),
        in_specs=in_specs,
        out_specs=out_specs,
        out_shape=out_shape,
        input_output_aliases=aliases,
        compiler_params=pltpu.CompilerParams(
            dimension_semantics=("arbitrary",), vmem_limit_bytes=VMEM_LIMIT),
        name="sample_mix",
    )(prm["sinks"], proj, state_ret, cache_k, cache_v, *consts, *prev_out)


def _sample_out_kernel(x_ref, xb_ref, gb_ref, mix_ref, conv8_ref, h8_ref, convw_ref, convb_ref,
                       wr_ref, wi_ref, br_ref, bi_ref, lam_ref, wout_ref,
                       y_ref, convo_ref, ho_ref):
    R = x_ref.shape[0]
    row = lax.broadcasted_iota(jnp.int32, (R, LRU_WIDTH), 0) & (SEQ_PAD - 1)
    xc = jnp.where(row < TOK0, conv8_ref[...], xb_ref[...])
    convo_ref[...] = xc
    conv = convb_ref[...] + convw_ref[0:1, :] * pltpu.roll(xc, 3, axis=0)
    conv = conv + convw_ref[1:2, :] * pltpu.roll(xc, 2, axis=0)
    conv = conv + convw_ref[2:3, :] * pltpu.roll(xc, 1, axis=0)
    conv = conv + convw_ref[3:4, :] * xc
    a, mult, ig = _lru_gates(conv, wr_ref, wi_ref, br_ref, bi_ref, lam_ref)
    u = mult * ig * conv
    h = h8_ref[...]
    for s in range(4):
        h = jnp.where(row == TOK0 + s, a * pltpu.roll(h, 1, axis=0) + u, h)
    ho_ref[...] = h
    ob = h * _silu(gb_ref[...])
    y = x_ref[...] + _dot(mix_ref[:, 0:GROUP_WIDTH].astype(BF16), wout_ref[0:GROUP_WIDTH, :])
    y = y + _dot(ob.astype(BF16), wout_ref[GROUP_WIDTH:2 * GROUP_WIDTH, :])
    y = y + _dot(mix_ref[:, GROUP_WIDTH:2 * GROUP_WIDTH].astype(BF16), wout_ref[2 * GROUP_WIDTH:, :])
    y_ref[...] = y


def _sample_out(layer, xp, proj, mix, conv8, h8, prm, rows=256):
    R = xp.shape[0]
    consts = [prm[k] for k in ("convw", "convb", "wr", "wi", "br", "bi", "lam", "wout")]
    in_specs = ([pl.BlockSpec((rows, D_MODEL), lambda i: (i, 0)),
                 pl.BlockSpec((rows, LRU_WIDTH), lambda i: (i, XB // LRU_WIDTH)),
                 pl.BlockSpec((rows, LRU_WIDTH), lambda i: (i, GB // LRU_WIDTH)),
                 pl.BlockSpec((rows, 2 * GROUP_WIDTH), lambda i: (i, 0)),
                 pl.BlockSpec((rows, LRU_WIDTH), lambda i: (i, 0)),
                 pl.BlockSpec((rows, LRU_WIDTH), lambda i: (i, 0))]
                + [_layer_spec(a, layer) for a in consts])
    out_shape = (
        jax.ShapeDtypeStruct((R, D_MODEL), F32),
        jax.ShapeDtypeStruct((R, LRU_WIDTH), F32),
        jax.ShapeDtypeStruct((R, LRU_WIDTH), F32),
    )
    out_specs = (
        pl.BlockSpec((rows, D_MODEL), lambda i: (i, 0)),
        pl.BlockSpec((rows, LRU_WIDTH), lambda i: (i, 0)),
        pl.BlockSpec((rows, LRU_WIDTH), lambda i: (i, 0)),
    )
    return pl.pallas_call(
        _sample_out_kernel,
        grid=(R // rows,),
        in_specs=in_specs,
        out_specs=out_specs,
        out_shape=out_shape,
        compiler_params=pltpu.CompilerParams(
            dimension_semantics=("arbitrary",), vmem_limit_bytes=VMEM_LIMIT),
        name="sample_out",
    )(xp, proj, proj, mix, conv8, h8, *consts)


def _prompt_tables():
    C = RET_CHUNK
    idx = np.arange(C, dtype=np.float32)
    diff = idx[:, None] - idx[None, :]
    causal = diff >= 0
    lg = _LOG_G[:, None, None]
    dmask = np.where(causal[None], np.exp(np.where(causal, diff, 0.0)[None] * lg), 0.0).astype(np.float32)
    qdec = np.exp((idx + 1.0)[None, :] * _LOG_G[:, None]).astype(np.float32)
    kdec = np.exp((C - 1 - idx)[None, :] * _LOG_G[:, None]).astype(np.float32)
    qdec = np.broadcast_to(qdec[:, :, None], (RET_HEADS, C, RET_DK)).copy()
    kdec = np.broadcast_to(kdec[:, :, None], (RET_HEADS, C, RET_DK)).copy()
    return jnp.asarray(dmask), jnp.asarray(qdec), jnp.asarray(kdec)


def _sample_tables():
    P = SEQ_PAD
    rows = np.arange(P, dtype=np.float32)
    i = rows - TOK0
    tok = (i >= 0) & (i < 4)
    cm = np.zeros((RET_HEADS, 4, P, RET_DV), np.float32)
    qdec = np.zeros((RET_HEADS, P, RET_DK), np.float32)
    kdec = np.zeros((RET_HEADS, P, RET_DK), np.float32)
    for h in range(RET_HEADS):
        for s in range(4):
            d = i - s
            col = np.where(tok & (d >= 0), np.exp(np.where(d >= 0, d, 0.0) * _LOG_G[h]), 0.0)
            cm[h, s] = col[:, None]
        qdec[h] = np.where(tok, np.exp((i + 1.0) * _LOG_G[h]), 0.0)[:, None]
        kdec[h] = np.where(tok, np.exp((3.0 - i) * _LOG_G[h]), 0.0)[:, None]
    return jnp.asarray(cm), jnp.asarray(qdec), jnp.asarray(kdec)


def _block_diag(w):
    L, n, d, _ = w.shape
    g = n // 2
    w = w.reshape(L, 2, g, d, d)
    eye = jnp.eye(g, dtype=w.dtype)
    return (eye[None, None, :, None, :, None] * w[:, :, :, :, None, :]).reshape(L, 2, g * d, g * d)


def kernel(x_prompt, x_sample, state_ret, state_lru, state_conv, cache_swa_k, cache_swa_v, norm_g, w_in, w_out, ret_norm_g, conv_w, conv_b, w_rgate, b_rgate, w_igate, b_igate, lru_lambda, q_norm_g, k_norm_g, attn_sinks):
    Bs, Ts, _ = x_sample.shape
    ptab = _prompt_tables()
    stab = _sample_tables()
    yp = x_prompt
    ys = jnp.pad(x_sample, ((0, 0), (TOK0, SEQ_PAD - TOK0 - Ts), (0, 0))).reshape(Bs * SEQ_PAD, D_MODEL)
    cache_k = cache_swa_k.reshape(DEPTH, Bs, 2 * WINDOW, SWA_HEAD_DIM)
    cache_v = cache_swa_v.reshape(DEPTH, Bs, 2 * WINDOW, SWA_HEAD_DIM)
    Bp = x_prompt.shape[0]
    p_state, s_state = (), ()
    s_h, s_conv = [], []
    row = lambda a: a.reshape(DEPTH, 1, -1)
    prm = dict(
        sinks=attn_sinks, ng=row(norm_g), win=w_in.astype(BF16), wout=w_out.astype(BF16), retg=row(ret_norm_g),
        convw=conv_w, convb=row(conv_b), wr=_block_diag(w_rgate).astype(BF16), wi=_block_diag(w_igate).astype(BF16),
        br=row(b_rgate), bi=row(b_igate), lam=row(lru_lambda), qg=row(q_norm_g), kg=row(k_norm_g))
    for l in range(DEPTH):
        yp, *p_state = _prompt_layer(l, yp, prm, ptab, tuple(p_state))

        proj = _sample_proj(l, ys, prm)
        mix, *s_state = _sample_mix(l, proj, state_ret, cache_k, cache_v, prm, stab, tuple(s_state))
        conv8 = jnp.pad(state_conv[l], ((0, 0), (0, SEQ_PAD - (CONV_W - 1)), (0, 0))).reshape(Bs * SEQ_PAD, LRU_WIDTH)
        h8 = jnp.pad(state_lru[l][:, None, :], ((0, 0), (TOK0 - 1, SEQ_PAD - TOK0), (0, 0))).reshape(Bs * SEQ_PAD, LRU_WIDTH)
        ys, convo, ho = _sample_out(l, ys, proj, mix, conv8, h8, prm)
        s_h.append(ho.reshape(Bs, SEQ_PAD, LRU_WIDTH)[:, TOK0 + Ts - 1])
        s_conv.append(convo.reshape(Bs, SEQ_PAD, LRU_WIDTH)[:, TOK0 + Ts - (CONV_W - 1):TOK0 + Ts])

    p_ret, p_h, p_conv, p_k, p_v = p_state
    s_ret, s_k, s_v = s_state
    kv5 = lambda a, n: a.reshape(DEPTH, n, WINDOW, SWA_KV_HEADS, SWA_HEAD_DIM)
    y_sample = ys.reshape(Bs, SEQ_PAD, D_MODEL)[:, TOK0:TOK0 + Ts]
    return (yp, y_sample,
            p_ret, p_h.reshape(DEPTH, Bp, LRU_WIDTH), p_conv, kv5(p_k, Bp), kv5(p_v, Bp),
            s_ret, jnp.stack(s_h), jnp.stack(s_conv), kv5(s_k, Bs), kv5(s_v, Bs))
```

```python
import functools
import math

import numpy as np
import jax
import jax.numpy as jnp
from jax import lax
from jax.experimental import pallas as pl
from jax.experimental.pallas import tpu as pltpu

D_MODEL = 1024
DEPTH = 2
PAST_LEN = 16384
GROUP_WIDTH = 512
RET_HEADS = 4
RET_DK = 128
RET_DV = 128
RET_CHUNK = 128
LRU_WIDTH = 512
LRU_BLOCKS = 8
LRU_C = 8.0
CONV_W = 4
SWA_HEADS = 4
SWA_KV_HEADS = 2
SWA_GROUP = 2
SWA_HEAD_DIM = 128
WINDOW = 128
NORM_EPS = 1e-6
NEG_INF = -1e30

IN_WIDTH = 4608
MIX_WIDTH = 1536
QA, KA, VA, GA, XB, GB, QC, KC, VC, GC = 0, 512, 1024, 1536, 2048, 2560, 3072, 3584, 3840, 4096

F32 = jnp.float32
BF16 = jnp.bfloat16

SUBLANES = 8
SEQ_PAD = 8
TOK0 = 3
VMEM_LIMIT = 56 * 1024 * 1024

_LOG_G = np.log1p(-np.power(np.float32(2.0), (-5.0 - np.arange(RET_HEADS)).astype(np.float32))).astype(np.float32)
_SLOPES = [2.0 ** (-8.0 * (h + 1) / SWA_HEADS) for h in range(SWA_HEADS)]


def _rms(x, g):
    ms = jnp.mean(x * x, axis=-1, keepdims=True)
    return x * lax.rsqrt(ms + NORM_EPS) * g


def _silu(x):
    return x * jax.nn.sigmoid(x)


def _softplus(x):
    return jnp.maximum(x, 0.0) + jnp.log1p(jnp.exp(-jnp.abs(x)))


def _dot(a, b):
    return jnp.dot(a, b, preferred_element_type=F32)


def _dot_nt(a, b):
    return lax.dot_general(a, b, (((1,), (1,)), ((), ())), preferred_element_type=F32)


def _dot_tn(a, b):
    return lax.dot_general(a, b, (((0,), (0,)), ((), ())), preferred_element_type=F32)


def _lru_gates(conv, wr_ref, wi_ref, br_ref, bi_ref, lam_ref):
    gin = conv.astype(BF16)
    half = LRU_WIDTH // 2
    lo, hi = gin[:, :half], gin[:, half:]
    r = jax.nn.sigmoid(jnp.concatenate([_dot(lo, wr_ref[0]), _dot(hi, wr_ref[1])], axis=1) + br_ref[...])
    i = jax.nn.sigmoid(jnp.concatenate([_dot(lo, wi_ref[0]), _dot(hi, wi_ref[1])], axis=1) + bi_ref[...])
    log_a = -LRU_C * r * _softplus(-lam_ref[...])
    a = jnp.exp(log_a)
    m2 = -jnp.tanh(log_a) * (1.0 + a * a)
    mult = jnp.where(m2 > 0.0, m2 * lax.rsqrt(m2), 0.0)
    return a, mult, i


_DONE = object()


def _run(gen):
    for _ in gen:
        pass


def _zip_stages(gens):
    gens = list(gens)
    while gens:
        gens = [g for g in gens if next(g, _DONE) is not _DONE]
        if gens:
            yield


def _chain(gens):
    for g in gens:
        yield from g


def _interleave(main, sides, n_main, n_side):
    side = _chain(sides)
    next(side)
    done_side = 1
    for i, _ in enumerate(main):
        want = ((i + 1) * n_side) // n_main
        while done_side < want and next(side, _DONE) is not _DONE:
            done_side += 1
    _run(side)


IN_PROJ_COLS = 256
IN_PROJ_STAGES = 1 + IN_WIDTH // IN_PROJ_COLS


def _in_proj(x_ref, ng_ref, win_ref, proj_ref):
    hb = _rms(x_ref[...], ng_ref[...]).astype(BF16)
    yield
    for j in range(IN_WIDTH // IN_PROJ_COLS):
        cols = slice(j * IN_PROJ_COLS, (j + 1) * IN_PROJ_COLS)
        proj_ref[:, cols] = _dot(hb, win_ref[:, cols])
        yield


OUT_PROJ_STAGES = D_MODEL // 256


def _out_proj(mixed_s, x_ref, y_ref, wout_ref):
    mixed = mixed_s[...]
    for j in range(OUT_PROJ_STAGES):
        cols = slice(j * 256, (j + 1) * 256)
        y_ref[:, cols] = x_ref[:, cols] + _dot(mixed, wout_ref[:, cols])
        yield


def _mix_stages(tt):
    return 4 + tt // SUBLANES // 8 + (tt // RET_CHUNK) * 5


def _scan_rows(a, u, h_in, tt):
    G = tt // SUBLANES
    W = a.shape[-1]
    a3 = a.reshape(G, SUBLANES, W)
    u3 = u.reshape(G, SUBLANES, W)
    r3 = lax.broadcasted_iota(jnp.int32, (G, SUBLANES, W), 1)
    sh = 1
    while sh < SUBLANES:
        keep = r3 >= sh
        a_sh = jnp.where(keep, pltpu.roll(a3, sh, axis=1), 1.0)
        u_sh = jnp.where(keep, pltpu.roll(u3, sh, axis=1), 0.0)
        u3 = a3 * u_sh + u3
        a3 = a3 * a_sh
        sh *= 2
    yield
    hs = []
    for g in range(G):
        hg = a3[g] * h_in + u3[g]
        hs.append(hg)
        h_in = hg[SUBLANES - 1:SUBLANES, :]
        if g % 8 == 7:
            yield
    return jnp.concatenate(hs, axis=0)


def _retention_head(h, proj_s, r0, s_ref, retg_ref, dmask_ref, qdec_ref, kdec_ref, mixed_s):
    C = RET_CHUNK
    cs = slice(h * RET_DK, (h + 1) * RET_DK)
    q = proj_s[r0:r0 + C, QA + h * 128:QA + (h + 1) * 128]
    k = proj_s[r0:r0 + C, KA + h * 128:KA + (h + 1) * 128] * (RET_DK ** -0.5)
    vb = proj_s[r0:r0 + C, VA + h * 128:VA + (h + 1) * 128].astype(BF16)
    sc = _dot_nt(q.astype(BF16), k.astype(BF16)) * dmask_ref[h]
    yield
    s_prev = s_ref[0, h]
    lhs = jnp.concatenate([sc.astype(BF16), (q * qdec_ref[h]).astype(BF16)], axis=1)
    rhs = jnp.concatenate([vb, s_prev.astype(BF16)], axis=0)
    o = _dot(lhs, rhs)
    yield
    kv = _dot_tn((k * kdec_ref[h]).astype(BF16), vb)
    s_ref[0, h] = float(np.exp(np.float32(C) * _LOG_G[h])) * s_prev + kv
    yield
    o = _rms(o, retg_ref[:, cs])
    mixed_s[r0:r0 + C, cs] = (o * _silu(proj_s[r0:r0 + C, GA + h * 128:GA + (h + 1) * 128])).astype(BF16)
    yield


def _swa_kv_head(kh, proj_s, r0, blk, sinks_ref, qg_ref, kg_ref, mixed_s, kprev_s, vprev_s):
    C = RET_CHUNK
    ks = slice(kh * 128, (kh + 1) * 128)
    h0, h1 = kh * SWA_GROUP, kh * SWA_GROUP + 1
    kn = _rms(proj_s[r0:r0 + C, KC + kh * 128:KC + (kh + 1) * 128], kg_ref[...])
    vv = proj_s[r0:r0 + C, VC + kh * 128:VC + (kh + 1) * 128]
    kband = jnp.concatenate([kprev_s[:, ks], kn], axis=0).astype(BF16)
    vband = jnp.concatenate([vprev_s[:, ks], vv], axis=0).astype(BF16)
    kprev_s[:, ks] = kn
    vprev_s[:, ks] = vv
    q0 = _rms(proj_s[r0:r0 + C, QC + h0 * 128:QC + (h0 + 1) * 128], qg_ref[...])
    q1 = _rms(proj_s[r0:r0 + C, QC + h1 * 128:QC + (h1 + 1) * 128], qg_ref[...])
    qq = jnp.concatenate([q0, q1], axis=0).astype(BF16)
    yield
    s = _dot_nt(qq, kband) * (SWA_HEAD_DIM ** -0.5)
    ii = lax.broadcasted_iota(jnp.int32, (2 * C, 2 * C), 0)
    jj = lax.broadcasted_iota(jnp.int32, (2 * C, 2 * C), 1)
    dist = jnp.where(ii >= C, ii - C, ii) + C - jj
    valid = jnp.logical_and(jnp.logical_and(dist >= 0, dist < WINDOW), jj >= jnp.where(blk > 0, 0, C))
    slope = jnp.where(ii >= C, _SLOPES[h1], _SLOPES[h0])
    s = jnp.where(valid, s - slope * dist.astype(F32), NEG_INF)
    yield
    rowc = lax.broadcasted_iota(jnp.int32, (2 * C, 1), 0)
    sink = jnp.where(rowc >= C, sinks_ref[h1], sinks_ref[h0])
    m = jnp.maximum(jnp.max(s, axis=-1, keepdims=True), sink)
    p = jnp.exp(s - m)
    denom = jnp.sum(p, axis=-1, keepdims=True) + jnp.exp(sink - m)
    yield
    o = _dot((p / denom).astype(BF16), vband)
    yield
    for g, hh in enumerate((h0, h1)):
        gc = proj_s[r0:r0 + C, GC + hh * 128:GC + (hh + 1) * 128]
        mixed_s[r0:r0 + C, 2 * GROUP_WIDTH + hh * 128:2 * GROUP_WIDTH + (hh + 1) * 128] = (
            (o[g * C:(g + 1) * C] * _silu(gc)).astype(BF16))
    yield


def _mix_tile(proj_s, t, tt, sinks_ref, retg_ref, convw_ref, convb_ref,
              wr_ref, wi_ref, br_ref, bi_ref, lam_ref, qg_ref, kg_ref, dmask_ref, qdec_ref, kdec_ref,
              s_ref, mixed_s, xtail_s, kprev_s, vprev_s, hc_s):
    nch = tt // RET_CHUNK
    C = RET_CHUNK

    xb = proj_s[:, XB:XB + LRU_WIDTH]
    xfull = jnp.concatenate([xtail_s[...], xb], axis=0)
    xtail_s[...] = xb[tt - SUBLANES:tt, :]
    conv = convb_ref[...] + convw_ref[0:1, :] * pltpu.roll(xfull, 3, axis=0)[SUBLANES:]
    conv = conv + convw_ref[1:2, :] * pltpu.roll(xfull, 2, axis=0)[SUBLANES:]
    conv = conv + convw_ref[2:3, :] * pltpu.roll(xfull, 1, axis=0)[SUBLANES:]
    conv = conv + convw_ref[3:4, :] * xb
    yield

    a, mult, ig = _lru_gates(conv, wr_ref, wi_ref, br_ref, bi_ref, lam_ref)
    yield
    row = lax.broadcasted_iota(jnp.int32, (tt, LRU_WIDTH), 0)
    mult = jnp.where(row == jnp.where(t == 0, 0, -1), 1.0, mult)
    u = mult * ig * conv
    hseq = yield from _scan_rows(a, u, hc_s[...], tt)
    hc_s[...] = hseq[tt - 1:tt, :]
    mixed_s[:, GROUP_WIDTH:2 * GROUP_WIDTH] = (hseq * _silu(proj_s[:, GB:GB + LRU_WIDTH])).astype(BF16)
    yield

    for c in range(nch):
        r0 = c * C
        blk = t * nch + c
        yield from _zip_stages(
            [_retention_head(h, proj_s, r0, s_ref, retg_ref, dmask_ref, qdec_ref, kdec_ref, mixed_s)
             for h in range(RET_HEADS)]
            + [_swa_kv_head(kh, proj_s, r0, blk, sinks_ref, qg_ref, kg_ref, mixed_s, kprev_s, vprev_s)
               for kh in range(SWA_KV_HEADS)])


def _prompt_kernel(sinks_ref, xa_ref, xn_ref, ng_ref, win_ref, wout_ref, retg_ref, convw_ref, convb_ref,
                   wr_ref, wi_ref, br_ref, bi_ref, lam_ref, qg_ref, kg_ref,
                   dmask_ref, qdec_ref, kdec_ref, *rest, tt, npairs, layer, slot, fill):
    y_ref, s_ref, h_ref, conv_ref, kout_ref, vout_ref = rest[-14:-8]
    proj_a, proj_b, mixed_a, mixed_b, xtail_s, kprev_s, vprev_s, hc_s = rest[-8:]
    sinks_ref = sinks_ref.at[layer]
    b = pl.program_id(0)
    p = pl.program_id(1)

    @pl.when(p == 0)
    def _init():
        s_ref[...] = jnp.zeros_like(s_ref)
        xtail_s[...] = jnp.zeros_like(xtail_s)
        kprev_s[...] = jnp.zeros_like(kprev_s)
        vprev_s[...] = jnp.zeros_like(vprev_s)
        hc_s[...] = jnp.zeros_like(hc_s)

    xa0, xa1 = xa_ref.at[0, 0:tt, :], xa_ref.at[0, tt:2 * tt, :]
    ya0, ya1 = y_ref.at[0, 0:tt, :], y_ref.at[0, tt:2 * tt, :]

    @pl.when(jnp.logical_and(b == 0, p == 0))
    def _prologue():
        _run(_in_proj(xa0, ng_ref, win_ref, proj_a))

    common = (sinks_ref, retg_ref, convw_ref, convb_ref, wr_ref, wi_ref, br_ref, bi_ref,
              lam_ref, qg_ref, kg_ref, dmask_ref, qdec_ref, kdec_ref, s_ref.at[slot])
    state = (xtail_s, kprev_s, vprev_s, hc_s)

    n_main = _mix_stages(tt)
    _interleave(_mix_tile(proj_a, 2 * p, tt, *common, mixed_a, *state),
                [_in_proj(xa1, ng_ref, win_ref, proj_b)], n_main, IN_PROJ_STAGES)
    _interleave(_mix_tile(proj_b, 2 * p + 1, tt, *common, mixed_b, *state),
                [_out_proj(mixed_a, xa0, ya0, wout_ref), _in_proj(xn_ref.at[0], ng_ref, win_ref, proj_a)],
                n_main, OUT_PROJ_STAGES + IN_PROJ_STAGES)
    _run(_out_proj(mixed_b, xa1, ya1, wout_ref))

    @pl.when(p == npairs - 1)
    def _state_out():
        h_ref[slot, 0] = hc_s[...]
        conv_ref[slot, 0] = xtail_s[SUBLANES - (CONV_W - 1):SUBLANES, :]
        for kh in range(SWA_KV_HEADS):
            ks = slice(kh * 128, (kh + 1) * 128)
            kout_ref[slot, 0, pl.ds(kh, WINDOW, stride=SWA_KV_HEADS), :] = kprev_s[:, ks]
            vout_ref[slot, 0, pl.ds(kh, WINDOW, stride=SWA_KV_HEADS), :] = vprev_s[:, ks]
        if fill:
            for other in [o for o in range(DEPTH) if o != slot]:
                for ref in (h_ref, conv_ref, kout_ref, vout_ref):
                    ref[other] = jnp.zeros(ref.shape[1:], F32)


def _const_spec(shape):
    nd = len(shape)
    return pl.BlockSpec(shape, lambda *_: (0,) * nd)


def _resident_spec(shape):
    nd = len(shape)
    return pl.BlockSpec(shape, lambda *_: (0,) * nd, pipeline_mode=pl.Buffered(1))


def _layer_spec(a, layer, resident=False):
    nd = a.ndim
    kw = dict(pipeline_mode=pl.Buffered(1)) if resident else {}
    return pl.BlockSpec((None,) + a.shape[1:], lambda *_: (layer,) + (0,) * (nd - 1), **kw)


def _prompt_layer(layer, x, prm, winb, woutb, tables, prev_out, tt=256):
    B, T, D = x.shape
    npairs = T // (2 * tt)
    layer_in = [prm[k] for k in ("retg", "convw", "convb", "wr", "wi", "br", "bi", "lam", "qg", "kg")]
    vec_in = [prm["ng"], winb, woutb] + layer_in + list(tables)

    def next_tile(b, p):
        last = p == npairs - 1
        return (jnp.where(last, jnp.minimum(b + 1, B - 1), b), jnp.where(last, 0, 2 * p + 2), 0)

    in_specs = ([pl.BlockSpec(memory_space=pltpu.SMEM),
                 pl.BlockSpec((1, 2 * tt, D), lambda b, p: (b, p, 0)),
                 pl.BlockSpec((1, tt, D), next_tile)]
                + [_layer_spec(prm["ng"], layer, resident=True), _resident_spec(winb.shape),
                   _resident_spec(woutb.shape)]
                + [_layer_spec(a, layer, resident=True) for a in layer_in]
                + [_resident_spec(a.shape) for a in tables]
                + [pl.BlockSpec(memory_space=pl.ANY) for _ in prev_out])
    out_shape = (
        jax.ShapeDtypeStruct((B, T, D), F32),
        jax.ShapeDtypeStruct((DEPTH, B, RET_HEADS, RET_DK, RET_DV), F32),
        jax.ShapeDtypeStruct((DEPTH, B, 1, LRU_WIDTH), F32),
        jax.ShapeDtypeStruct((DEPTH, B, CONV_W - 1, LRU_WIDTH), F32),
        jax.ShapeDtypeStruct((DEPTH, B, 2 * WINDOW, SWA_HEAD_DIM), F32),
        jax.ShapeDtypeStruct((DEPTH, B, 2 * WINDOW, SWA_HEAD_DIM), F32),
    )
    fill = not prev_out
    ld, li, slot = (DEPTH, 0, layer) if fill else (1, layer, 0)
    out_specs = (
        pl.BlockSpec((1, 2 * tt, D), lambda b, p: (b, p, 0)),
        pl.BlockSpec((ld, 1, RET_HEADS, RET_DK, RET_DV), lambda b, p: (li, b, 0, 0, 0)),
        pl.BlockSpec((ld, 1, 1, LRU_WIDTH), lambda b, p: (li, b, 0, 0)),
        pl.BlockSpec((ld, 1, CONV_W - 1, LRU_WIDTH), lambda b, p: (li, b, 0, 0)),
        pl.BlockSpec((ld, 1, 2 * WINDOW, SWA_HEAD_DIM), lambda b, p: (li, b, 0, 0)),
        pl.BlockSpec((ld, 1, 2 * WINDOW, SWA_HEAD_DIM), lambda b, p: (li, b, 0, 0)),
    )
    n_in = 3 + len(vec_in)
    aliases = {n_in + j: 1 + j for j in range(len(prev_out))}
    scratch = [
        pltpu.VMEM((tt, IN_WIDTH), F32),
        pltpu.VMEM((tt, IN_WIDTH), F32),
        pltpu.VMEM((tt, MIX_WIDTH), BF16),
        pltpu.VMEM((tt, MIX_WIDTH), BF16),
        pltpu.VMEM((SUBLANES, LRU_WIDTH), F32),
        pltpu.VMEM((WINDOW, 256), F32),
        pltpu.VMEM((WINDOW, 256), F32),
        pltpu.VMEM((1, LRU_WIDTH), F32),
    ]
    return pl.pallas_call(
        functools.partial(_prompt_kernel, tt=tt, npairs=npairs, layer=layer, slot=slot, fill=fill),
        grid=(B, npairs),
        in_specs=in_specs,
        out_specs=out_specs,
        out_shape=out_shape,
        scratch_shapes=scratch,
        input_output_aliases=aliases,
        compiler_params=pltpu.CompilerParams(
            dimension_semantics=("arbitrary", "arbitrary"), vmem_limit_bytes=VMEM_LIMIT),
        name="prompt_layer",
    )(prm["sinks"], x, x, *vec_in, *prev_out)


def _sproj_kernel(x_ref, ng_ref, win_ref, wout_ref, o_ref, winb_ref, woutb_ref, hb_s):
    @pl.when(pl.program_id(0) == 0)
    def _norm_once():
        hb_s[...] = _rms(x_ref[...], ng_ref[...]).astype(BF16)

    wb = win_ref[...].astype(BF16)
    winb_ref[...] = wb
    woutb_ref[...] = wout_ref[...].astype(BF16)
    o_ref[...] = _dot(hb_s[...], wb)


BF16_SUBLANES = 2 * SUBLANES


def _sample_proj(layer, xp, ng, w_in, w_out):
    R = xp.shape[0]
    nb = IN_WIDTH // 512
    wo_blocks = max(n for n in range(1, nb + 1) if MIX_WIDTH % (n * BF16_SUBLANES) == 0)
    wo_rows = MIX_WIDTH // wo_blocks

    def wo_block(j):
        return jnp.minimum(j, wo_blocks - 1)

    return pl.pallas_call(
        _sproj_kernel,
        grid=(nb,),
        in_specs=[_const_spec(xp.shape), _layer_spec(ng, layer),
                  pl.BlockSpec((None, D_MODEL, 512), lambda j: (layer, 0, j)),
                  pl.BlockSpec((None, wo_rows, D_MODEL), lambda j: (layer, wo_block(j), 0))],
        out_specs=(pl.BlockSpec((R, 512), lambda j: (0, j)),
                   pl.BlockSpec((D_MODEL, 512), lambda j: (0, j)),
                   pl.BlockSpec((wo_rows, D_MODEL), lambda j: (wo_block(j), 0))),
        out_shape=(jax.ShapeDtypeStruct((R, IN_WIDTH), F32),
                   jax.ShapeDtypeStruct((D_MODEL, IN_WIDTH), BF16),
                   jax.ShapeDtypeStruct((MIX_WIDTH, D_MODEL), BF16)),
        scratch_shapes=[pltpu.VMEM((R, D_MODEL), BF16)],
        compiler_params=pltpu.CompilerParams(
            dimension_semantics=("arbitrary",), vmem_limit_bytes=VMEM_LIMIT),
        name="sample_proj",
    )(xp, ng, w_in, w_out)


def _sample_mix_seq(sinks_ref, proj_ref, s0_ref, kbuf_ref, vbuf_ref, retg_ref, qg_ref, kg_ref,
                    cm_ref, qdec_ref, kdec_ref, mix_ref, snew_ref, knew_ref, vnew_ref):
    P = SEQ_PAD
    for h in range(RET_HEADS):
        cs = slice(h * 128, (h + 1) * 128)
        q = proj_ref[:, QA + h * 128:QA + (h + 1) * 128]
        k = proj_ref[:, KA + h * 128:KA + (h + 1) * 128] * (RET_DK ** -0.5)
        v = proj_ref[:, VA + h * 128:VA + (h + 1) * 128]
        intra = jnp.zeros((P, RET_DV), F32)
        for s in range(4):
            r_ = TOK0 + s
            w = jnp.sum(q * k[r_:r_ + 1, :], axis=-1, keepdims=True)
            intra = intra + (w * cm_ref[h, s]) * v[r_:r_ + 1, :]
        s_prev = s0_ref[h]
        cross = _dot((q * qdec_ref[h]).astype(BF16), s_prev.astype(BF16))
        kv = _dot_tn((k * kdec_ref[h]).astype(BF16), v.astype(BF16))
        yield
        snew_ref[h] = float(np.exp(np.float32(4.0) * _LOG_G[h])) * s_prev + kv
        o = _rms(intra + cross, retg_ref[:, cs])
        mix_ref[:, cs] = o * _silu(proj_ref[:, GA + h * 128:GA + (h + 1) * 128])
        yield

    r16 = lax.broadcasted_iota(jnp.int32, (2 * P, WINDOW), 0)
    j16 = lax.broadcasted_iota(jnp.int32, (2 * P, WINDOW), 1)
    rr = jnp.where(r16 >= P, r16 - P, r16)
    row_ok = jnp.logical_and(rr >= TOK0, rr < TOK0 + 4)
    dist = (rr - TOK0) + WINDOW - j16
    distf = dist.astype(F32)
    valid = jnp.logical_and(jnp.logical_and(dist >= 0, dist < WINDOW), row_ok)
    r16c = lax.broadcasted_iota(jnp.int32, (2 * P, 1), 0)
    rrc = jnp.where(r16c >= P, r16c - P, r16c)
    rowc_ok = jnp.logical_and(rrc >= TOK0, rrc < TOK0 + 4)
    knew_ref[0:2 * (WINDOW - 4), :] = kbuf_ref[2 * 4:2 * WINDOW, :]
    vnew_ref[0:2 * (WINDOW - 4), :] = vbuf_ref[2 * 4:2 * WINDOW, :]
    for kh in range(SWA_KV_HEADS):
        h0, h1 = kh * SWA_GROUP, kh * SWA_GROUP + 1
        kb = kbuf_ref[pl.ds(kh, WINDOW, stride=SWA_KV_HEADS), :]
        vb = vbuf_ref[pl.ds(kh, WINDOW, stride=SWA_KV_HEADS), :]
        kn = _rms(proj_ref[:, KC + kh * 128:KC + (kh + 1) * 128], kg_ref[...])
        vn = proj_ref[:, VC + kh * 128:VC + (kh + 1) * 128]
        q0 = _rms(proj_ref[:, QC + h0 * 128:QC + (h0 + 1) * 128], qg_ref[...])
        q1 = _rms(proj_ref[:, QC + h1 * 128:QC + (h1 + 1) * 128], qg_ref[...])
        qq = jnp.concatenate([q0, q1], axis=0)
        slope = jnp.where(r16 >= P, _SLOPES[h1], _SLOPES[h0])
        slopec = jnp.where(r16c >= P, _SLOPES[h1], _SLOPES[h0])
        sb = _dot_nt(qq.astype(BF16), kb.astype(BF16)) * (SWA_HEAD_DIM ** -0.5)
        sb = jnp.where(valid, sb - slope * distf, NEG_INF)
        yield
        sink = jnp.where(r16c >= P, sinks_ref[h1], sinks_ref[h0])
        m = jnp.maximum(jnp.max(sb, axis=-1, keepdims=True), sink)
        wn = []
        for s in range(4):
            r_ = TOK0 + s
            w = jnp.sum(qq * kn[r_:r_ + 1, :], axis=-1, keepdims=True) * (SWA_HEAD_DIM ** -0.5)
            dn = rrc - r_
            w = jnp.where(jnp.logical_and(dn >= 0, rowc_ok), w - slopec * dn.astype(F32), NEG_INF)
            wn.append(w)
            m = jnp.maximum(m, w)
        pb = jnp.exp(sb - m)
        denom = jnp.sum(pb, axis=-1, keepdims=True) + jnp.exp(sink - m)
        pn = [jnp.exp(w - m) for w in wn]
        for p_ in pn:
            denom = denom + p_
        yield
        o = _dot((pb / denom).astype(BF16), vb.astype(BF16))
        for s in range(4):
            r_ = TOK0 + s
            o = o + (pn[s] / denom) * vn[r_:r_ + 1, :]
        for g, hh in enumerate((h0, h1)):
            gc = proj_ref[:, GC + hh * 128:GC + (hh + 1) * 128]
            mix_ref[:, GROUP_WIDTH + hh * 128:GROUP_WIDTH + (hh + 1) * 128] = o[g * P:(g + 1) * P] * _silu(gc)
        for s in range(4):
            r_out = 2 * (WINDOW - 4 + s) + kh
            knew_ref[r_out:r_out + 1, :] = kn[TOK0 + s:TOK0 + s + 1, :]
            vnew_ref[r_out:r_out + 1, :] = vn[TOK0 + s:TOK0 + s + 1, :]
        yield


def _sample_mix_kernel(sinks_ref, proj_ref, s0_ref, kbuf_ref, vbuf_ref, retg_ref, qg_ref, kg_ref,
                       cm_ref, qdec_ref, kdec_ref, *rest, nseq, layer, slot, fill):
    mix_ref, snew_ref, knew_ref, vnew_ref = rest[-4:]
    sinks_ref = sinks_ref.at[layer]
    if fill:
        for other in [o for o in range(DEPTH) if o != slot]:
            for ref in (snew_ref, knew_ref, vnew_ref):
                ref[other] = jnp.zeros(ref.shape[1:], F32)
    gens = []
    for i in range(nseq):
        rows = slice(i * SEQ_PAD, (i + 1) * SEQ_PAD)
        gens.append(_sample_mix_seq(
            sinks_ref, proj_ref.at[rows, :], s0_ref.at[0, i], kbuf_ref.at[0, i], vbuf_ref.at[0, i],
            retg_ref, qg_ref, kg_ref, cm_ref, qdec_ref, kdec_ref,
            mix_ref.at[rows, :], snew_ref.at[slot, i], knew_ref.at[slot, i], vnew_ref.at[slot, i]))
    _run(_zip_stages(gens))


def _sample_mix(layer, proj, state_ret, cache_k, cache_v, prm, tables, prev_out, nseq=8):
    depth, B = state_ret.shape[:2]
    layer_in = [prm[k] for k in ("retg", "qg", "kg")]
    consts = layer_in + list(tables)
    st_spec = pl.BlockSpec((1, nseq, RET_HEADS, RET_DK, RET_DV), lambda i: (layer, i, 0, 0, 0))
    kv_spec = pl.BlockSpec((1, nseq, 2 * WINDOW, SWA_HEAD_DIM), lambda i: (layer, i, 0, 0))
    in_specs = ([pl.BlockSpec(memory_space=pltpu.SMEM),
                 pl.BlockSpec((nseq * SEQ_PAD, IN_WIDTH), lambda i: (i, 0)),
                 st_spec, kv_spec, kv_spec]
                + [_layer_spec(a, layer) for a in layer_in]
                + [_const_spec(a.shape) for a in tables]
                + [pl.BlockSpec(memory_space=pl.ANY) for _ in prev_out])
    out_shape = (
        jax.ShapeDtypeStruct((B * SEQ_PAD, 2 * GROUP_WIDTH), F32),
        jax.ShapeDtypeStruct(state_ret.shape, F32),
        jax.ShapeDtypeStruct(cache_k.shape, F32),
        jax.ShapeDtypeStruct(cache_v.shape, F32),
    )
    fill = not prev_out
    ld, li, slot = (DEPTH, 0, layer) if fill else (1, layer, 0)
    out_specs = (pl.BlockSpec((nseq * SEQ_PAD, 2 * GROUP_WIDTH), lambda i: (i, 0)),
                 pl.BlockSpec((ld, nseq, RET_HEADS, RET_DK, RET_DV), lambda i: (li, i, 0, 0, 0)),
                 pl.BlockSpec((ld, nseq, 2 * WINDOW, SWA_HEAD_DIM), lambda i: (li, i, 0, 0)),
                 pl.BlockSpec((ld, nseq, 2 * WINDOW, SWA_HEAD_DIM), lambda i: (li, i, 0, 0)))
    n_in = 5 + len(consts)
    aliases = {n_in + j: 1 + j for j in range(len(prev_out))}
    return pl.pallas_call(
        functools.partial(_sample_mix_kernel, nseq=nseq, layer=layer, slot=slot, fill=fill),
        grid=(B // nseq,),
        in_specs=in_specs,
        out_specs=out_specs,
        out_shape=out_shape,
        input_output_aliases=aliases,
        compiler_params=pltpu.CompilerParams(
            dimension_semantics=("arbitrary",), vmem_limit_bytes=VMEM_LIMIT),
        name="sample_mix",
    )(prm["sinks"], proj, state_ret, cache_k, cache_v, *consts, *prev_out)


def _sample_out_kernel(x_ref, xb_ref, gb_ref, mix_ref, conv8_ref, h8_ref, convw_ref, convb_ref,
                       wr_ref, wi_ref, br_ref, bi_ref, lam_ref, wout_ref,
                       y_ref, convo_ref, ho_ref):
    R = x_ref.shape[0]
    row = lax.broadcasted_iota(jnp.int32, (R, LRU_WIDTH), 0) & (SEQ_PAD - 1)
    xc = jnp.where(row < TOK0, conv8_ref[...], xb_ref[...])
    convo_ref[...] = xc
    conv = convb_ref[...] + convw_ref[0:1, :] * pltpu.roll(xc, 3, axis=0)
    conv = conv + convw_ref[1:2, :] * pltpu.roll(xc, 2, axis=0)
    conv = conv + convw_ref[2:3, :] * pltpu.roll(xc, 1, axis=0)
    conv = conv + convw_ref[3:4, :] * xc
    a, mult, ig = _lru_gates(conv, wr_ref, wi_ref, br_ref, bi_ref, lam_ref)
    u = mult * ig * conv
    h = h8_ref[...]
    for s in range(4):
        h = jnp.where(row == TOK0 + s, a * pltpu.roll(h, 1, axis=0) + u, h)
    ho_ref[...] = h
    ob = h * _silu(gb_ref[...])
    y = x_ref[...] + _dot(mix_ref[:, 0:GROUP_WIDTH].astype(BF16), wout_ref[0:GROUP_WIDTH, :])
    y = y + _dot(ob.astype(BF16), wout_ref[GROUP_WIDTH:2 * GROUP_WIDTH, :])
    y = y + _dot(mix_ref[:, GROUP_WIDTH:2 * GROUP_WIDTH].astype(BF16), wout_ref[2 * GROUP_WIDTH:, :])
    y_ref[...] = y


def _sample_out(layer, xp, proj, mix, conv8, h8, prm, woutb, rows=256):
    R = xp.shape[0]
    consts = [prm[k] for k in ("convw", "convb", "wr", "wi", "br", "bi", "lam")]
    in_specs = ([pl.BlockSpec((rows, D_MODEL), lambda i: (i, 0)),
                 pl.BlockSpec((rows, LRU_WIDTH), lambda i: (i, XB // LRU_WIDTH)),
                 pl.BlockSpec((rows, LRU_WIDTH), lambda i: (i, GB // LRU_WIDTH)),
                 pl.BlockSpec((rows, 2 * GROUP_WIDTH), lambda i: (i, 0)),
                 pl.BlockSpec((rows, LRU_WIDTH), lambda i: (i, 0)),
                 pl.BlockSpec((rows, LRU_WIDTH), lambda i: (i, 0))]
                + [_layer_spec(a, layer) for a in consts] + [_const_spec(woutb.shape)])
    out_shape = (
        jax.ShapeDtypeStruct((R, D_MODEL), F32),
        jax.ShapeDtypeStruct((R, LRU_WIDTH), F32),
        jax.ShapeDtypeStruct((R, LRU_WIDTH), F32),
    )
    out_specs = (
        pl.BlockSpec((rows, D_MODEL), lambda i: (i, 0)),
        pl.BlockSpec((rows, LRU_WIDTH), lambda i: (i, 0)),
        pl.BlockSpec((rows, LRU_WIDTH), lambda i: (i, 0)),
    )
    return pl.pallas_call(
        _sample_out_kernel,
        grid=(R // rows,),
        in_specs=in_specs,
        out_specs=out_specs,
        out_shape=out_shape,
        compiler_params=pltpu.CompilerParams(
            dimension_semantics=("arbitrary",), vmem_limit_bytes=VMEM_LIMIT),
        name="sample_out",
    )(xp, proj, proj, mix, conv8, h8, *consts, woutb)


def _prompt_tables():
    C = RET_CHUNK
    idx = np.arange(C, dtype=np.float32)
    diff = idx[:, None] - idx[None, :]
    causal = diff >= 0
    lg = _LOG_G[:, None, None]
    dmask = np.where(causal[None], np.exp(np.where(causal, diff, 0.0)[None] * lg), 0.0).astype(np.float32)
    qdec = np.exp((idx + 1.0)[None, :] * _LOG_G[:, None]).astype(np.float32)
    kdec = np.exp((C - 1 - idx)[None, :] * _LOG_G[:, None]).astype(np.float32)
    qdec = np.broadcast_to(qdec[:, :, None], (RET_HEADS, C, RET_DK)).copy()
    kdec = np.broadcast_to(kdec[:, :, None], (RET_HEADS, C, RET_DK)).copy()
    return jnp.asarray(dmask), jnp.asarray(qdec), jnp.asarray(kdec)


def _sample_tables():
    P = SEQ_PAD
    rows = np.arange(P, dtype=np.float32)
    i = rows - TOK0
    tok = (i >= 0) & (i < 4)
    cm = np.zeros((RET_HEADS, 4, P, RET_DV), np.float32)
    qdec = np.zeros((RET_HEADS, P, RET_DK), np.float32)
    kdec = np.zeros((RET_HEADS, P, RET_DK), np.float32)
    for h in range(RET_HEADS):
        for s in range(4):
            d = i - s
            col = np.where(tok & (d >= 0), np.exp(np.where(d >= 0, d, 0.0) * _LOG_G[h]), 0.0)
            cm[h, s] = col[:, None]
        qdec[h] = np.where(tok, np.exp((i + 1.0) * _LOG_G[h]), 0.0)[:, None]
        kdec[h] = np.where(tok, np.exp((3.0 - i) * _LOG_G[h]), 0.0)[:, None]
    return jnp.asarray(cm), jnp.asarray(qdec), jnp.asarray(kdec)


def _block_diag(w):
    L, n, d, _ = w.shape
    g = n // 2
    w = w.reshape(L, 2, g, d, d)
    eye = jnp.eye(g, dtype=w.dtype)
    return (eye[None, None, :, None, :, None] * w[:, :, :, :, None, :]).reshape(L, 2, g * d, g * d)


def kernel(x_prompt, x_sample, state_ret, state_lru, state_conv, cache_swa_k, cache_swa_v, norm_g, w_in, w_out, ret_norm_g, conv_w, conv_b, w_rgate, b_rgate, w_igate, b_igate, lru_lambda, q_norm_g, k_norm_g, attn_sinks):
    Bs, Ts, _ = x_sample.shape
    ptab = _prompt_tables()
    stab = _sample_tables()
    yp = x_prompt
    ys = jnp.pad(x_sample, ((0, 0), (TOK0, SEQ_PAD - TOK0 - Ts), (0, 0))).reshape(Bs * SEQ_PAD, D_MODEL)
    cache_k = cache_swa_k.reshape(DEPTH, Bs, 2 * WINDOW, SWA_HEAD_DIM)
    cache_v = cache_swa_v.reshape(DEPTH, Bs, 2 * WINDOW, SWA_HEAD_DIM)
    Bp = x_prompt.shape[0]
    p_state, s_state = (), ()
    s_h, s_conv = [], []
    row = lambda a: a.reshape(DEPTH, 1, -1)
    prm = dict(
        sinks=attn_sinks, ng=row(norm_g), retg=row(ret_norm_g),
        convw=conv_w, convb=row(conv_b), wr=_block_diag(w_rgate).astype(BF16), wi=_block_diag(w_igate).astype(BF16),
        br=row(b_rgate), bi=row(b_igate), lam=row(lru_lambda), qg=row(q_norm_g), kg=row(k_norm_g))
    for l in range(DEPTH):
        proj, winb, woutb = _sample_proj(l, ys, prm["ng"], w_in, w_out)
        yp, *p_state = _prompt_layer(l, yp, prm, winb, woutb, ptab, tuple(p_state))
        mix, *s_state = _sample_mix(l, proj, state_ret, cache_k, cache_v, prm, stab, tuple(s_state))
        conv8 = jnp.pad(state_conv[l], ((0, 0), (0, SEQ_PAD - (CONV_W - 1)), (0, 0))).reshape(Bs * SEQ_PAD, LRU_WIDTH)
        h8 = jnp.pad(state_lru[l][:, None, :], ((0, 0), (TOK0 - 1, SEQ_PAD - TOK0), (0, 0))).reshape(Bs * SEQ_PAD, LRU_WIDTH)
        ys, convo, ho = _sample_out(l, ys, proj, mix, conv8, h8, prm, woutb)
        s_h.append(ho.reshape(Bs, SEQ_PAD, LRU_WIDTH)[:, TOK0 + Ts - 1])
        s_conv.append(convo.reshape(Bs, SEQ_PAD, LRU_WIDTH)[:, TOK0 + Ts - (CONV_W - 1):TOK0 + Ts])

    p_ret, p_h, p_conv, p_k, p_v = p_state
    s_ret, s_k, s_v = s_state
    kv5 = lambda a, n: a.reshape(DEPTH, n, WINDOW, SWA_KV_HEADS, SWA_HEAD_DIM)
    y_sample = ys.reshape(Bs, SEQ_PAD, D_MODEL)[:, TOK0:TOK0 + Ts]
    return (yp, y_sample,
            p_ret, p_h.reshape(DEPTH, Bp, LRU_WIDTH), p_conv, kv5(p_k, Bp), kv5(p_v, Bp),
            s_ret, jnp.stack(s_h), jnp.stack(s_conv), kv5(s_k, Bs), kv5(s_v, Bs))
```

```python
import functools
import math

import numpy as np
import jax
import jax.numpy as jnp
from jax import lax
from jax.experimental import pallas as pl
from jax.experimental.pallas import tpu as pltpu

D_MODEL = 1024
DEPTH = 2
PAST_LEN = 16384
GROUP_WIDTH = 512
RET_HEADS = 4
RET_DK = 128
RET_DV = 128
RET_CHUNK = 128
LRU_WIDTH = 512
LRU_BLOCKS = 8
LRU_C = 8.0
CONV_W = 4
SWA_HEADS = 4
SWA_KV_HEADS = 2
SWA_GROUP = 2
SWA_HEAD_DIM = 128
WINDOW = 128
NORM_EPS = 1e-6
NEG_INF = -1e30

IN_WIDTH = 4608
MIX_WIDTH = 1536
QA, KA, VA, GA, XB, GB, QC, KC, VC, GC = 0, 512, 1024, 1536, 2048, 2560, 3072, 3584, 3840, 4096

F32 = jnp.float32
BF16 = jnp.bfloat16

SUBLANES = 8
SEQ_PAD = 8
TOK0 = 3
VMEM_LIMIT = 56 * 1024 * 1024

_LOG_G = np.log1p(-np.power(np.float32(2.0), (-5.0 - np.arange(RET_HEADS)).astype(np.float32))).astype(np.float32)
_SLOPES = [2.0 ** (-8.0 * (h + 1) / SWA_HEADS) for h in range(SWA_HEADS)]


def _rms(x, g):
    ms = jnp.mean(x * x, axis=-1, keepdims=True)
    return x * lax.rsqrt(ms + NORM_EPS) * g


def _silu(x):
    return x * jax.nn.sigmoid(x)


def _softplus(x):
    return jnp.maximum(x, 0.0) + jnp.log1p(jnp.exp(-jnp.abs(x)))


def _dot(a, b):
    return jnp.dot(a, b, preferred_element_type=F32)


def _dot_nt(a, b):
    return lax.dot_general(a, b, (((1,), (1,)), ((), ())), preferred_element_type=F32)


def _dot_tn(a, b):
    return lax.dot_general(a, b, (((0,), (0,)), ((), ())), preferred_element_type=F32)


def _lru_gates(conv, wr_ref, wi_ref, br_ref, bi_ref, lam_ref):
    gin = conv.astype(BF16)
    half = LRU_WIDTH // 2
    lo, hi = gin[:, :half], gin[:, half:]
    r = jax.nn.sigmoid(jnp.concatenate([_dot(lo, wr_ref[0]), _dot(hi, wr_ref[1])], axis=1) + br_ref[...])
    i = jax.nn.sigmoid(jnp.concatenate([_dot(lo, wi_ref[0]), _dot(hi, wi_ref[1])], axis=1) + bi_ref[...])
    log_a = -LRU_C * r * _softplus(-lam_ref[...])
    a = jnp.exp(log_a)
    m2 = -jnp.tanh(log_a) * (1.0 + a * a)
    mult = jnp.where(m2 > 0.0, m2 * lax.rsqrt(m2), 0.0)
    return a, mult, i


_DONE = object()


def _run(gen):
    for _ in gen:
        pass


def _zip_stages(gens):
    gens = list(gens)
    while gens:
        gens = [g for g in gens if next(g, _DONE) is not _DONE]
        if gens:
            yield


def _chain(gens):
    for g in gens:
        yield from g


def _interleave(main, sides, n_main, n_side):
    side = _chain(sides)
    next(side)
    done_side = 1
    for i, _ in enumerate(main):
        want = ((i + 1) * n_side) // n_main
        while done_side < want and next(side, _DONE) is not _DONE:
            done_side += 1
    _run(side)


IN_PROJ_COLS = 256
IN_PROJ_STAGES = 1 + IN_WIDTH // IN_PROJ_COLS


def _in_proj(x_ref, ng_ref, win_ref, proj_ref):
    hb = _rms(x_ref[...], ng_ref[...]).astype(BF16)
    yield
    for j in range(IN_WIDTH // IN_PROJ_COLS):
        cols = slice(j * IN_PROJ_COLS, (j + 1) * IN_PROJ_COLS)
        proj_ref[:, cols] = _dot(hb, win_ref[:, cols])
        yield


OUT_PROJ_STAGES = D_MODEL // 256


def _out_proj(mixed_s, x_ref, y_ref, wout_ref):
    mixed = mixed_s[...]
    for j in range(OUT_PROJ_STAGES):
        cols = slice(j * 256, (j + 1) * 256)
        y_ref[:, cols] = x_ref[:, cols] + _dot(mixed, wout_ref[:, cols])
        yield


def _mix_stages(tt):
    return 4 + tt // SUBLANES // 8 + (tt // RET_CHUNK) * 5


def _scan_rows(a, u, h_in, tt):
    G = tt // SUBLANES
    W = a.shape[-1]
    a3 = a.reshape(G, SUBLANES, W)
    u3 = u.reshape(G, SUBLANES, W)
    r3 = lax.broadcasted_iota(jnp.int32, (G, SUBLANES, W), 1)
    sh = 1
    while sh < SUBLANES:
        keep = r3 >= sh
        a_sh = jnp.where(keep, pltpu.roll(a3, sh, axis=1), 1.0)
        u_sh = jnp.where(keep, pltpu.roll(u3, sh, axis=1), 0.0)
        u3 = a3 * u_sh + u3
        a3 = a3 * a_sh
        sh *= 2
    yield
    hs = []
    for g in range(G):
        hg = a3[g] * h_in + u3[g]
        hs.append(hg)
        h_in = hg[SUBLANES - 1:SUBLANES, :]
        if g % 8 == 7:
            yield
    return jnp.concatenate(hs, axis=0)


def _retention_head(h, proj_s, r0, s_ref, retg_ref, dmask_ref, qdec_ref, kdec_ref, mixed_s):
    C = RET_CHUNK
    cs = slice(h * RET_DK, (h + 1) * RET_DK)
    q = proj_s[r0:r0 + C, QA + h * 128:QA + (h + 1) * 128]
    k = proj_s[r0:r0 + C, KA + h * 128:KA + (h + 1) * 128] * (RET_DK ** -0.5)
    vb = proj_s[r0:r0 + C, VA + h * 128:VA + (h + 1) * 128].astype(BF16)
    sc = _dot_nt(q.astype(BF16), k.astype(BF16)) * dmask_ref[h]
    yield
    s_prev = s_ref[0, h]
    lhs = jnp.concatenate([sc.astype(BF16), (q * qdec_ref[h]).astype(BF16)], axis=1)
    rhs = jnp.concatenate([vb, s_prev.astype(BF16)], axis=0)
    o = _dot(lhs, rhs)
    yield
    kv = _dot_tn((k * kdec_ref[h]).astype(BF16), vb)
    s_ref[0, h] = float(np.exp(np.float32(C) * _LOG_G[h])) * s_prev + kv
    yield
    o = _rms(o, retg_ref[:, cs])
    mixed_s[r0:r0 + C, cs] = (o * _silu(proj_s[r0:r0 + C, GA + h * 128:GA + (h + 1) * 128])).astype(BF16)
    yield


def _swa_kv_head(kh, proj_s, r0, blk, sinks_ref, qg_ref, kg_ref, mixed_s, kprev_s, vprev_s):
    C = RET_CHUNK
    ks = slice(kh * 128, (kh + 1) * 128)
    h0, h1 = kh * SWA_GROUP, kh * SWA_GROUP + 1
    kn = _rms(proj_s[r0:r0 + C, KC + kh * 128:KC + (kh + 1) * 128], kg_ref[...])
    vv = proj_s[r0:r0 + C, VC + kh * 128:VC + (kh + 1) * 128]
    kband = jnp.concatenate([kprev_s[:, ks], kn], axis=0).astype(BF16)
    vband = jnp.concatenate([vprev_s[:, ks], vv], axis=0).astype(BF16)
    kprev_s[:, ks] = kn
    vprev_s[:, ks] = vv
    q0 = _rms(proj_s[r0:r0 + C, QC + h0 * 128:QC + (h0 + 1) * 128], qg_ref[...])
    q1 = _rms(proj_s[r0:r0 + C, QC + h1 * 128:QC + (h1 + 1) * 128], qg_ref[...])
    qq = jnp.concatenate([q0, q1], axis=0).astype(BF16)
    yield
    s = _dot_nt(qq, kband) * (SWA_HEAD_DIM ** -0.5)
    ii = lax.broadcasted_iota(jnp.int32, (2 * C, 2 * C), 0)
    jj = lax.broadcasted_iota(jnp.int32, (2 * C, 2 * C), 1)
    dist = jnp.where(ii >= C, ii - C, ii) + C - jj
    valid = jnp.logical_and(jnp.logical_and(dist >= 0, dist < WINDOW), jj >= jnp.where(blk > 0, 0, C))
    slope = jnp.where(ii >= C, _SLOPES[h1], _SLOPES[h0])
    s = jnp.where(valid, s - slope * dist.astype(F32), NEG_INF)
    yield
    rowc = lax.broadcasted_iota(jnp.int32, (2 * C, 1), 0)
    sink = jnp.where(rowc >= C, sinks_ref[h1], sinks_ref[h0])
    m = jnp.maximum(jnp.max(s, axis=-1, keepdims=True), sink)
    p = jnp.exp(s - m)
    denom = jnp.sum(p, axis=-1, keepdims=True) + jnp.exp(sink - m)
    yield
    o = _dot((p / denom).astype(BF16), vband)
    yield
    for g, hh in enumerate((h0, h1)):
        gc = proj_s[r0:r0 + C, GC + hh * 128:GC + (hh + 1) * 128]
        mixed_s[r0:r0 + C, 2 * GROUP_WIDTH + hh * 128:2 * GROUP_WIDTH + (hh + 1) * 128] = (
            (o[g * C:(g + 1) * C] * _silu(gc)).astype(BF16))
    yield


def _mix_tile(proj_s, t, tt, sinks_ref, retg_ref, convw_ref, convb_ref,
              wr_ref, wi_ref, br_ref, bi_ref, lam_ref, qg_ref, kg_ref, dmask_ref, qdec_ref, kdec_ref,
              s_ref, mixed_s, xtail_s, kprev_s, vprev_s, hc_s):
    nch = tt // RET_CHUNK
    C = RET_CHUNK

    xb = proj_s[:, XB:XB + LRU_WIDTH]
    xfull = jnp.concatenate([xtail_s[...], xb], axis=0)
    xtail_s[...] = xb[tt - SUBLANES:tt, :]
    conv = convb_ref[...] + convw_ref[0:1, :] * pltpu.roll(xfull, 3, axis=0)[SUBLANES:]
    conv = conv + convw_ref[1:2, :] * pltpu.roll(xfull, 2, axis=0)[SUBLANES:]
    conv = conv + convw_ref[2:3, :] * pltpu.roll(xfull, 1, axis=0)[SUBLANES:]
    conv = conv + convw_ref[3:4, :] * xb
    yield

    a, mult, ig = _lru_gates(conv, wr_ref, wi_ref, br_ref, bi_ref, lam_ref)
    yield
    row = lax.broadcasted_iota(jnp.int32, (tt, LRU_WIDTH), 0)
    mult = jnp.where(row == jnp.where(t == 0, 0, -1), 1.0, mult)
    u = mult * ig * conv
    hseq = yield from _scan_rows(a, u, hc_s[...], tt)
    hc_s[...] = hseq[tt - 1:tt, :]
    mixed_s[:, GROUP_WIDTH:2 * GROUP_WIDTH] = (hseq * _silu(proj_s[:, GB:GB + LRU_WIDTH])).astype(BF16)
    yield

    for c in range(nch):
        r0 = c * C
        blk = t * nch + c
        yield from _zip_stages(
            [_retention_head(h, proj_s, r0, s_ref, retg_ref, dmask_ref, qdec_ref, kdec_ref, mixed_s)
             for h in range(RET_HEADS)]
            + [_swa_kv_head(kh, proj_s, r0, blk, sinks_ref, qg_ref, kg_ref, mixed_s, kprev_s, vprev_s)
               for kh in range(SWA_KV_HEADS)])


def _prompt_kernel(sinks_ref, xa_ref, xn_ref, ng_ref, win_ref, wout_ref, retg_ref, convw_ref, convb_ref,
                   wr_ref, wi_ref, br_ref, bi_ref, lam_ref, qg_ref, kg_ref,
                   dmask_ref, qdec_ref, kdec_ref, *rest, tt, npairs, layer, slot, fill):
    y_ref, s_ref, h_ref, conv_ref, kout_ref, vout_ref = rest[-14:-8]
    proj_a, proj_b, mixed_a, mixed_b, xtail_s, kprev_s, vprev_s, hc_s = rest[-8:]
    sinks_ref = sinks_ref.at[layer]
    b = pl.program_id(0)
    p = pl.program_id(1)

    @pl.when(p == 0)
    def _init():
        s_ref[...] = jnp.zeros_like(s_ref)
        xtail_s[...] = jnp.zeros_like(xtail_s)
        kprev_s[...] = jnp.zeros_like(kprev_s)
        vprev_s[...] = jnp.zeros_like(vprev_s)
        hc_s[...] = jnp.zeros_like(hc_s)

    xa0, xa1 = xa_ref.at[0, 0:tt, :], xa_ref.at[0, tt:2 * tt, :]
    ya0, ya1 = y_ref.at[0, 0:tt, :], y_ref.at[0, tt:2 * tt, :]

    @pl.when(jnp.logical_and(b == 0, p == 0))
    def _prologue():
        _run(_in_proj(xa0, ng_ref, win_ref, proj_a))

    common = (sinks_ref, retg_ref, convw_ref, convb_ref, wr_ref, wi_ref, br_ref, bi_ref,
              lam_ref, qg_ref, kg_ref, dmask_ref, qdec_ref, kdec_ref, s_ref.at[slot])
    state = (xtail_s, kprev_s, vprev_s, hc_s)

    n_main = _mix_stages(tt)
    _interleave(_mix_tile(proj_a, 2 * p, tt, *common, mixed_a, *state),
                [_in_proj(xa1, ng_ref, win_ref, proj_b)], n_main, IN_PROJ_STAGES)
    _interleave(_mix_tile(proj_b, 2 * p + 1, tt, *common, mixed_b, *state),
                [_out_proj(mixed_a, xa0, ya0, wout_ref), _in_proj(xn_ref.at[0], ng_ref, win_ref, proj_a)],
                n_main, OUT_PROJ_STAGES + IN_PROJ_STAGES)
    _run(_out_proj(mixed_b, xa1, ya1, wout_ref))

    @pl.when(p == npairs - 1)
    def _state_out():
        h_ref[slot, 0] = hc_s[...]
        conv_ref[slot, 0] = xtail_s[SUBLANES - (CONV_W - 1):SUBLANES, :]
        for kh in range(SWA_KV_HEADS):
            ks = slice(kh * 128, (kh + 1) * 128)
            kout_ref[slot, 0, pl.ds(kh, WINDOW, stride=SWA_KV_HEADS), :] = kprev_s[:, ks]
            vout_ref[slot, 0, pl.ds(kh, WINDOW, stride=SWA_KV_HEADS), :] = vprev_s[:, ks]
        if fill:
            for other in [o for o in range(DEPTH) if o != slot]:
                for ref in (h_ref, conv_ref, kout_ref, vout_ref):
                    ref[other] = jnp.zeros(ref.shape[1:], F32)


def _const_spec(shape):
    nd = len(shape)
    return pl.BlockSpec(shape, lambda *_: (0,) * nd)


def _resident_spec(shape):
    nd = len(shape)
    return pl.BlockSpec(shape, lambda *_: (0,) * nd, pipeline_mode=pl.Buffered(1))


def _layer_spec(a, layer, resident=False):
    nd = a.ndim
    kw = dict(pipeline_mode=pl.Buffered(1)) if resident else {}
    return pl.BlockSpec((None,) + a.shape[1:], lambda *_: (layer,) + (0,) * (nd - 1), **kw)


def _prompt_layer(layer, x, prm, tables, prev_out, tt=256):
    B, T, D = x.shape
    npairs = T // (2 * tt)
    layer_in = [prm[k] for k in ("ng", "win", "wout", "retg", "convw", "convb", "wr", "wi", "br", "bi",
                                 "lam", "qg", "kg")]
    vec_in = layer_in + list(tables)

    def next_tile(b, p):
        last = p == npairs - 1
        return (jnp.where(last, jnp.minimum(b + 1, B - 1), b), jnp.where(last, 0, 2 * p + 2), 0)

    in_specs = ([pl.BlockSpec(memory_space=pltpu.SMEM),
                 pl.BlockSpec((1, 2 * tt, D), lambda b, p: (b, p, 0)),
                 pl.BlockSpec((1, tt, D), next_tile)]
                + [_layer_spec(a, layer, resident=True) for a in layer_in]
                + [_resident_spec(a.shape) for a in tables]
                + [pl.BlockSpec(memory_space=pl.ANY) for _ in prev_out])
    out_shape = (
        jax.ShapeDtypeStruct((B, T, D), F32),
        jax.ShapeDtypeStruct((DEPTH, B, RET_HEADS, RET_DK, RET_DV), F32),
        jax.ShapeDtypeStruct((DEPTH, B, 1, LRU_WIDTH), F32),
        jax.ShapeDtypeStruct((DEPTH, B, CONV_W - 1, LRU_WIDTH), F32),
        jax.ShapeDtypeStruct((DEPTH, B, 2 * WINDOW, SWA_HEAD_DIM), F32),
        jax.ShapeDtypeStruct((DEPTH, B, 2 * WINDOW, SWA_HEAD_DIM), F32),
    )
    fill = not prev_out
    ld, li, slot = (DEPTH, 0, layer) if fill else (1, layer, 0)
    out_specs = (
        pl.BlockSpec((1, 2 * tt, D), lambda b, p: (b, p, 0)),
        pl.BlockSpec((ld, 1, RET_HEADS, RET_DK, RET_DV), lambda b, p: (li, b, 0, 0, 0)),
        pl.BlockSpec((ld, 1, 1, LRU_WIDTH), lambda b, p: (li, b, 0, 0)),
        pl.BlockSpec((ld, 1, CONV_W - 1, LRU_WIDTH), lambda b, p: (li, b, 0, 0)),
        pl.BlockSpec((ld, 1, 2 * WINDOW, SWA_HEAD_DIM), lambda b, p: (li, b, 0, 0)),
        pl.BlockSpec((ld, 1, 2 * WINDOW, SWA_HEAD_DIM), lambda b, p: (li, b, 0, 0)),
    )
    n_in = 3 + len(vec_in)
    aliases = {n_in + j: 1 + j for j in range(len(prev_out))}
    scratch = [
        pltpu.VMEM((tt, IN_WIDTH), F32),
        pltpu.VMEM((tt, IN_WIDTH), F32),
        pltpu.VMEM((tt, MIX_WIDTH), BF16),
        pltpu.VMEM((tt, MIX_WIDTH), BF16),
        pltpu.VMEM((SUBLANES, LRU_WIDTH), F32),
        pltpu.VMEM((WINDOW, 256), F32),
        pltpu.VMEM((WINDOW, 256), F32),
        pltpu.VMEM((1, LRU_WIDTH), F32),
    ]
    return pl.pallas_call(
        functools.partial(_prompt_kernel, tt=tt, npairs=npairs, layer=layer, slot=slot, fill=fill),
        grid=(B, npairs),
        in_specs=in_specs,
        out_specs=out_specs,
        out_shape=out_shape,
        scratch_shapes=scratch,
        input_output_aliases=aliases,
        compiler_params=pltpu.CompilerParams(
            dimension_semantics=("arbitrary", "arbitrary"), vmem_limit_bytes=VMEM_LIMIT),
        name="prompt_layer",
    )(prm["sinks"], x, x, *vec_in, *prev_out)


def _sproj_kernel(x_ref, ng_ref, win_ref, *rest, layer, cast_weights):
    if cast_weights:
        wout_ref, o_ref, winb_ref, woutb_ref, hb_s = rest
        winb_ref[...] = win_ref[...].astype(BF16)
        woutb_ref[...] = wout_ref[...].astype(BF16)
        win_ref = winb_ref.at[layer]
    else:
        o_ref, hb_s = rest

    @pl.when(pl.program_id(0) == 0)
    def _norm_once():
        hb_s[...] = _rms(x_ref[...], ng_ref[...]).astype(BF16)

    o_ref[...] = _dot(hb_s[...], win_ref[...])


BF16_SUBLANES = 2 * SUBLANES


def _sample_proj(layer, xp, ng, win, wout=None):
    R = xp.shape[0]
    nb = IN_WIDTH // 512
    cast_weights = wout is not None
    in_specs = [_const_spec(xp.shape), _layer_spec(ng, layer)]
    out_specs = [pl.BlockSpec((R, 512), lambda j: (0, j))]
    out_shape = [jax.ShapeDtypeStruct((R, IN_WIDTH), F32)]
    if cast_weights:
        wo_blocks = max(n for n in range(1, nb + 1) if MIX_WIDTH % (n * BF16_SUBLANES) == 0)
        wo_rows = MIX_WIDTH // wo_blocks
        win_spec = pl.BlockSpec((DEPTH, D_MODEL, 512), lambda j: (0, 0, j))
        wout_spec = pl.BlockSpec((DEPTH, wo_rows, D_MODEL), lambda j: (0, jnp.minimum(j, wo_blocks - 1), 0))
        in_specs += [win_spec, wout_spec]
        out_specs += [win_spec, wout_spec]
        out_shape += [jax.ShapeDtypeStruct(win.shape, BF16), jax.ShapeDtypeStruct(wout.shape, BF16)]
    else:
        in_specs += [pl.BlockSpec((None, D_MODEL, 512), lambda j: (layer, 0, j))]
    out = pl.pallas_call(
        functools.partial(_sproj_kernel, layer=layer, cast_weights=cast_weights),
        grid=(nb,),
        in_specs=in_specs,
        out_specs=out_specs,
        out_shape=out_shape,
        scratch_shapes=[pltpu.VMEM((R, D_MODEL), BF16)],
        compiler_params=pltpu.CompilerParams(
            dimension_semantics=("arbitrary",), vmem_limit_bytes=VMEM_LIMIT),
        name="sample_proj",
    )(xp, ng, win, *((wout,) if cast_weights else ()))
    return out if cast_weights else out[0]


def _sample_mix_seq(sinks_ref, proj_ref, s0_ref, kbuf_ref, vbuf_ref, retg_ref, qg_ref, kg_ref,
                    cm_ref, qdec_ref, kdec_ref, mix_ref, snew_ref, knew_ref, vnew_ref):
    P = SEQ_PAD
    for h in range(RET_HEADS):
        cs = slice(h * 128, (h + 1) * 128)
        q = proj_ref[:, QA + h * 128:QA + (h + 1) * 128]
        k = proj_ref[:, KA + h * 128:KA + (h + 1) * 128] * (RET_DK ** -0.5)
        v = proj_ref[:, VA + h * 128:VA + (h + 1) * 128]
        intra = jnp.zeros((P, RET_DV), F32)
        for s in range(4):
            r_ = TOK0 + s
            w = jnp.sum(q * k[r_:r_ + 1, :], axis=-1, keepdims=True)
            intra = intra + (w * cm_ref[h, s]) * v[r_:r_ + 1, :]
        s_prev = s0_ref[h]
        cross = _dot((q * qdec_ref[h]).astype(BF16), s_prev.astype(BF16))
        kv = _dot_tn((k * kdec_ref[h]).astype(BF16), v.astype(BF16))
        yield
        snew_ref[h] = float(np.exp(np.float32(4.0) * _LOG_G[h])) * s_prev + kv
        o = _rms(intra + cross, retg_ref[:, cs])
        mix_ref[:, cs] = o * _silu(proj_ref[:, GA + h * 128:GA + (h + 1) * 128])
        yield

    r16 = lax.broadcasted_iota(jnp.int32, (2 * P, WINDOW), 0)
    j16 = lax.broadcasted_iota(jnp.int32, (2 * P, WINDOW), 1)
    rr = jnp.where(r16 >= P, r16 - P, r16)
    row_ok = jnp.logical_and(rr >= TOK0, rr < TOK0 + 4)
    dist = (rr - TOK0) + WINDOW - j16
    distf = dist.astype(F32)
    valid = jnp.logical_and(jnp.logical_and(dist >= 0, dist < WINDOW), row_ok)
    r16c = lax.broadcasted_iota(jnp.int32, (2 * P, 1), 0)
    rrc = jnp.where(r16c >= P, r16c - P, r16c)
    rowc_ok = jnp.logical_and(rrc >= TOK0, rrc < TOK0 + 4)
    knew_ref[0:2 * (WINDOW - 4), :] = kbuf_ref[2 * 4:2 * WINDOW, :]
    vnew_ref[0:2 * (WINDOW - 4), :] = vbuf_ref[2 * 4:2 * WINDOW, :]
    for kh in range(SWA_KV_HEADS):
        h0, h1 = kh * SWA_GROUP, kh * SWA_GROUP + 1
        kb = kbuf_ref[pl.ds(kh, WINDOW, stride=SWA_KV_HEADS), :]
        vb = vbuf_ref[pl.ds(kh, WINDOW, stride=SWA_KV_HEADS), :]
        kn = _rms(proj_ref[:, KC + kh * 128:KC + (kh + 1) * 128], kg_ref[...])
        vn = proj_ref[:, VC + kh * 128:VC + (kh + 1) * 128]
        q0 = _rms(proj_ref[:, QC + h0 * 128:QC + (h0 + 1) * 128], qg_ref[...])
        q1 = _rms(proj_ref[:, QC + h1 * 128:QC + (h1 + 1) * 128], qg_ref[...])
        qq = jnp.concatenate([q0, q1], axis=0)
        slope = jnp.where(r16 >= P, _SLOPES[h1], _SLOPES[h0])
        slopec = jnp.where(r16c >= P, _SLOPES[h1], _SLOPES[h0])
        sb = _dot_nt(qq.astype(BF16), kb.astype(BF16)) * (SWA_HEAD_DIM ** -0.5)
        sb = jnp.where(valid, sb - slope * distf, NEG_INF)
        yield
        sink = jnp.where(r16c >= P, sinks_ref[h1], sinks_ref[h0])
        m = jnp.maximum(jnp.max(sb, axis=-1, keepdims=True), sink)
        wn = []
        for s in range(4):
            r_ = TOK0 + s
            w = jnp.sum(qq * kn[r_:r_ + 1, :], axis=-1, keepdims=True) * (SWA_HEAD_DIM ** -0.5)
            dn = rrc - r_
            w = jnp.where(jnp.logical_and(dn >= 0, rowc_ok), w - slopec * dn.astype(F32), NEG_INF)
            wn.append(w)
            m = jnp.maximum(m, w)
        pb = jnp.exp(sb - m)
        denom = jnp.sum(pb, axis=-1, keepdims=True) + jnp.exp(sink - m)
        pn = [jnp.exp(w - m) for w in wn]
        for p_ in pn:
            denom = denom + p_
        yield
        o = _dot((pb / denom).astype(BF16), vb.astype(BF16))
        for s in range(4):
            r_ = TOK0 + s
            o = o + (pn[s] / denom) * vn[r_:r_ + 1, :]
        for g, hh in enumerate((h0, h1)):
            gc = proj_ref[:, GC + hh * 128:GC + (hh + 1) * 128]
            mix_ref[:, GROUP_WIDTH + hh * 128:GROUP_WIDTH + (hh + 1) * 128] = o[g * P:(g + 1) * P] * _silu(gc)
        for s in range(4):
            r_out = 2 * (WINDOW - 4 + s) + kh
            knew_ref[r_out:r_out + 1, :] = kn[TOK0 + s:TOK0 + s + 1, :]
            vnew_ref[r_out:r_out + 1, :] = vn[TOK0 + s:TOK0 + s + 1, :]
        yield


def _sample_mix_kernel(sinks_ref, proj_ref, s0_ref, kbuf_ref, vbuf_ref, retg_ref, qg_ref, kg_ref,
                       cm_ref, qdec_ref, kdec_ref, *rest, nseq, layer, slot, fill):
    mix_ref, snew_ref, knew_ref, vnew_ref = rest[-4:]
    sinks_ref = sinks_ref.at[layer]
    if fill:
        for other in [o for o in range(DEPTH) if o != slot]:
            for ref in (snew_ref, knew_ref, vnew_ref):
                ref[other] = jnp.zeros(ref.shape[1:], F32)
    gens = []
    for i in range(nseq):
        rows = slice(i * SEQ_PAD, (i + 1) * SEQ_PAD)
        gens.append(_sample_mix_seq(
            sinks_ref, proj_ref.at[rows, :], s0_ref.at[0, i], kbuf_ref.at[0, i], vbuf_ref.at[0, i],
            retg_ref, qg_ref, kg_ref, cm_ref, qdec_ref, kdec_ref,
            mix_ref.at[rows, :], snew_ref.at[slot, i], knew_ref.at[slot, i], vnew_ref.at[slot, i]))
    _run(_zip_stages(gens))


def _sample_mix(layer, proj, state_ret, cache_k, cache_v, prm, tables, prev_out, nseq=8):
    depth, B = state_ret.shape[:2]
    layer_in = [prm[k] for k in ("retg", "qg", "kg")]
    consts = layer_in + list(tables)
    st_spec = pl.BlockSpec((1, nseq, RET_HEADS, RET_DK, RET_DV), lambda i: (layer, i, 0, 0, 0))
    kv_spec = pl.BlockSpec((1, nseq, 2 * WINDOW, SWA_HEAD_DIM), lambda i: (layer, i, 0, 0))
    in_specs = ([pl.BlockSpec(memory_space=pltpu.SMEM),
                 pl.BlockSpec((nseq * SEQ_PAD, IN_WIDTH), lambda i: (i, 0)),
                 st_spec, kv_spec, kv_spec]
                + [_layer_spec(a, layer) for a in layer_in]
                + [_const_spec(a.shape) for a in tables]
                + [pl.BlockSpec(memory_space=pl.ANY) for _ in prev_out])
    out_shape = (
        jax.ShapeDtypeStruct((B * SEQ_PAD, 2 * GROUP_WIDTH), F32),
        jax.ShapeDtypeStruct(state_ret.shape, F32),
        jax.ShapeDtypeStruct(cache_k.shape, F32),
        jax.ShapeDtypeStruct(cache_v.shape, F32),
    )
    fill = not prev_out
    ld, li, slot = (DEPTH, 0, layer) if fill else (1, layer, 0)
    out_specs = (pl.BlockSpec((nseq * SEQ_PAD, 2 * GROUP_WIDTH), lambda i: (i, 0)),
                 pl.BlockSpec((ld, nseq, RET_HEADS, RET_DK, RET_DV), lambda i: (li, i, 0, 0, 0)),
                 pl.BlockSpec((ld, nseq, 2 * WINDOW, SWA_HEAD_DIM), lambda i: (li, i, 0, 0)),
                 pl.BlockSpec((ld, nseq, 2 * WINDOW, SWA_HEAD_DIM), lambda i: (li, i, 0, 0)))
    n_in = 5 + len(consts)
    aliases = {n_in + j: 1 + j for j in range(len(prev_out))}
    return pl.pallas_call(
        functools.partial(_sample_mix_kernel, nseq=nseq, layer=layer, slot=slot, fill=fill),
        grid=(B // nseq,),
        in_specs=in_specs,
        out_specs=out_specs,
        out_shape=out_shape,
        input_output_aliases=aliases,
        compiler_params=pltpu.CompilerParams(
            dimension_semantics=("arbitrary",), vmem_limit_bytes=VMEM_LIMIT),
        name="sample_mix",
    )(prm["sinks"], proj, state_ret, cache_k, cache_v, *consts, *prev_out)


def _sample_out_kernel(x_ref, xb_ref, gb_ref, mix_ref, conv8_ref, h8_ref, convw_ref, convb_ref,
                       wr_ref, wi_ref, br_ref, bi_ref, lam_ref, wout_ref,
                       y_ref, convo_ref, ho_ref):
    R = x_ref.shape[0]
    row = lax.broadcasted_iota(jnp.int32, (R, LRU_WIDTH), 0) & (SEQ_PAD - 1)
    xc = jnp.where(row < TOK0, conv8_ref[...], xb_ref[...])
    convo_ref[...] = xc
    conv = convb_ref[...] + convw_ref[0:1, :] * pltpu.roll(xc, 3, axis=0)
    conv = conv + convw_ref[1:2, :] * pltpu.roll(xc, 2, axis=0)
    conv = conv + convw_ref[2:3, :] * pltpu.roll(xc, 1, axis=0)
    conv = conv + convw_ref[3:4, :] * xc
    a, mult, ig = _lru_gates(conv, wr_ref, wi_ref, br_ref, bi_ref, lam_ref)
    u = mult * ig * conv
    h = h8_ref[...]
    for s in range(4):
        h = jnp.where(row == TOK0 + s, a * pltpu.roll(h, 1, axis=0) + u, h)
    ho_ref[...] = h
    ob = h * _silu(gb_ref[...])
    y = x_ref[...] + _dot(mix_ref[:, 0:GROUP_WIDTH].astype(BF16), wout_ref[0:GROUP_WIDTH, :])
    y = y + _dot(ob.astype(BF16), wout_ref[GROUP_WIDTH:2 * GROUP_WIDTH, :])
    y = y + _dot(mix_ref[:, GROUP_WIDTH:2 * GROUP_WIDTH].astype(BF16), wout_ref[2 * GROUP_WIDTH:, :])
    y_ref[...] = y


def _sample_out(layer, xp, proj, mix, conv8, h8, prm, rows=256):
    R = xp.shape[0]
    consts = [prm[k] for k in ("convw", "convb", "wr", "wi", "br", "bi", "lam", "wout")]
    in_specs = ([pl.BlockSpec((rows, D_MODEL), lambda i: (i, 0)),
                 pl.BlockSpec((rows, LRU_WIDTH), lambda i: (i, XB // LRU_WIDTH)),
                 pl.BlockSpec((rows, LRU_WIDTH), lambda i: (i, GB // LRU_WIDTH)),
                 pl.BlockSpec((rows, 2 * GROUP_WIDTH), lambda i: (i, 0)),
                 pl.BlockSpec((rows, LRU_WIDTH), lambda i: (i, 0)),
                 pl.BlockSpec((rows, LRU_WIDTH), lambda i: (i, 0))]
                + [_layer_spec(a, layer) for a in consts])
    out_shape = (
        jax.ShapeDtypeStruct((R, D_MODEL), F32),
        jax.ShapeDtypeStruct((R, LRU_WIDTH), F32),
        jax.ShapeDtypeStruct((R, LRU_WIDTH), F32),
    )
    out_specs = (
        pl.BlockSpec((rows, D_MODEL), lambda i: (i, 0)),
        pl.BlockSpec((rows, LRU_WIDTH), lambda i: (i, 0)),
        pl.BlockSpec((rows, LRU_WIDTH), lambda i: (i, 0)),
    )
    return pl.pallas_call(
        _sample_out_kernel,
        grid=(R // rows,),
        in_specs=in_specs,
        out_specs=out_specs,
        out_shape=out_shape,
        compiler_params=pltpu.CompilerParams(
            dimension_semantics=("arbitrary",), vmem_limit_bytes=VMEM_LIMIT),
        name="sample_out",
    )(xp, proj, proj, mix, conv8, h8, *consts)


def _prompt_tables():
    C = RET_CHUNK
    idx = np.arange(C, dtype=np.float32)
    diff = idx[:, None] - idx[None, :]
    causal = diff >= 0
    lg = _LOG_G[:, None, None]
    dmask = np.where(causal[None], np.exp(np.where(causal, diff, 0.0)[None] * lg), 0.0).astype(np.float32)
    qdec = np.exp((idx + 1.0)[None, :] * _LOG_G[:, None]).astype(np.float32)
    kdec = np.exp((C - 1 - idx)[None, :] * _LOG_G[:, None]).astype(np.float32)
    qdec = np.broadcast_to(qdec[:, :, None], (RET_HEADS, C, RET_DK)).copy()
    kdec = np.broadcast_to(kdec[:, :, None], (RET_HEADS, C, RET_DK)).copy()
    return jnp.asarray(dmask), jnp.asarray(qdec), jnp.asarray(kdec)


def _sample_tables():
    P = SEQ_PAD
    rows = np.arange(P, dtype=np.float32)
    i = rows - TOK0
    tok = (i >= 0) & (i < 4)
    cm = np.zeros((RET_HEADS, 4, P, RET_DV), np.float32)
    qdec = np.zeros((RET_HEADS, P, RET_DK), np.float32)
    kdec = np.zeros((RET_HEADS, P, RET_DK), np.float32)
    for h in range(RET_HEADS):
        for s in range(4):
            d = i - s
            col = np.where(tok & (d >= 0), np.exp(np.where(d >= 0, d, 0.0) * _LOG_G[h]), 0.0)
            cm[h, s] = col[:, None]
        qdec[h] = np.where(tok, np.exp((i + 1.0) * _LOG_G[h]), 0.0)[:, None]
        kdec[h] = np.where(tok, np.exp((3.0 - i) * _LOG_G[h]), 0.0)[:, None]
    return jnp.asarray(cm), jnp.asarray(qdec), jnp.asarray(kdec)


def _block_diag(w):
    L, n, d, _ = w.shape
    g = n // 2
    w = w.reshape(L, 2, g, d, d)
    eye = jnp.eye(g, dtype=w.dtype)
    return (eye[None, None, :, None, :, None] * w[:, :, :, :, None, :]).reshape(L, 2, g * d, g * d)


def kernel(x_prompt, x_sample, state_ret, state_lru, state_conv, cache_swa_k, cache_swa_v, norm_g, w_in, w_out, ret_norm_g, conv_w, conv_b, w_rgate, b_rgate, w_igate, b_igate, lru_lambda, q_norm_g, k_norm_g, attn_sinks):
    Bs, Ts, _ = x_sample.shape
    ptab = _prompt_tables()
    stab = _sample_tables()
    yp = x_prompt
    ys = jnp.pad(x_sample, ((0, 0), (TOK0, SEQ_PAD - TOK0 - Ts), (0, 0))).reshape(Bs * SEQ_PAD, D_MODEL)
    cache_k = cache_swa_k.reshape(DEPTH, Bs, 2 * WINDOW, SWA_HEAD_DIM)
    cache_v = cache_swa_v.reshape(DEPTH, Bs, 2 * WINDOW, SWA_HEAD_DIM)
    Bp = x_prompt.shape[0]
    p_state, s_state = (), ()
    s_h, s_conv = [], []
    row = lambda a: a.reshape(DEPTH, 1, -1)
    prm = dict(
        sinks=attn_sinks, ng=row(norm_g), retg=row(ret_norm_g),
        convw=conv_w, convb=row(conv_b), wr=_block_diag(w_rgate).astype(BF16), wi=_block_diag(w_igate).astype(BF16),
        br=row(b_rgate), bi=row(b_igate), lam=row(lru_lambda), qg=row(q_norm_g), kg=row(k_norm_g))
    proj, prm["win"], prm["wout"] = _sample_proj(0, ys, prm["ng"], w_in, w_out)
    for l in range(DEPTH):
        yp, *p_state = _prompt_layer(l, yp, prm, ptab, tuple(p_state))
        if l > 0:
            proj = _sample_proj(l, ys, prm["ng"], prm["win"])
        mix, *s_state = _sample_mix(l, proj, state_ret, cache_k, cache_v, prm, stab, tuple(s_state))
        conv8 = jnp.pad(state_conv[l], ((0, 0), (0, SEQ_PAD - (CONV_W - 1)), (0, 0))).reshape(Bs * SEQ_PAD, LRU_WIDTH)
        h8 = jnp.pad(state_lru[l][:, None, :], ((0, 0), (TOK0 - 1, SEQ_PAD - TOK0), (0, 0))).reshape(Bs * SEQ_PAD, LRU_WIDTH)
        ys, convo, ho = _sample_out(l, ys, proj, mix, conv8, h8, prm)
        s_h.append(ho.reshape(Bs, SEQ_PAD, LRU_WIDTH)[:, TOK0 + Ts - 1])
        s_conv.append(convo.reshape(Bs, SEQ_PAD, LRU_WIDTH)[:, TOK0 + Ts - (CONV_W - 1):TOK0 + Ts])

    p_ret, p_h, p_conv, p_k, p_v = p_state
    s_ret, s_k, s_v = s_state
    kv5 = lambda a, n: a.reshape(DEPTH, n, WINDOW, SWA_KV_HEADS, SWA_HEAD_DIM)
    y_sample = ys.reshape(Bs, SEQ_PAD, D_MODEL)[:, TOK0:TOK0 + Ts]
    return (yp, y_sample,
            p_ret, p_h.reshape(DEPTH, Bp, LRU_WIDTH), p_conv, kv5(p_k, Bp), kv5(p_v, Bp),
            s_ret, jnp.stack(s_h), jnp.stack(s_conv), kv5(s_k, Bs), kv5(s_v, Bs))
```

```python
import functools
import math

import numpy as np
import jax
import jax.numpy as jnp
from jax import lax
from jax.experimental import pallas as pl
from jax.experimental.pallas import tpu as pltpu

D_MODEL = 1024
DEPTH = 2
PAST_LEN = 16384
GROUP_WIDTH = 512
RET_HEADS = 4
RET_DK = 128
RET_DV = 128
RET_CHUNK = 128
LRU_WIDTH = 512
LRU_BLOCKS = 8
LRU_C = 8.0
CONV_W = 4
SWA_HEADS = 4
SWA_KV_HEADS = 2
SWA_GROUP = 2
SWA_HEAD_DIM = 128
WINDOW = 128
NORM_EPS = 1e-6
NEG_INF = -1e30

IN_WIDTH = 4608
MIX_WIDTH = 1536
QA, KA, VA, GA, XB, GB, QC, KC, VC, GC = 0, 512, 1024, 1536, 2048, 2560, 3072, 3584, 3840, 4096

F32 = jnp.float32
BF16 = jnp.bfloat16

SUBLANES = 8
SEQ_PAD = 8
TOK0 = 3
VMEM_LIMIT = 56 * 1024 * 1024

_LOG_G = np.log1p(-np.power(np.float32(2.0), (-5.0 - np.arange(RET_HEADS)).astype(np.float32))).astype(np.float32)
_SLOPES = [2.0 ** (-8.0 * (h + 1) / SWA_HEADS) for h in range(SWA_HEADS)]


def _rms(x, g):
    ms = jnp.mean(x * x, axis=-1, keepdims=True)
    return x * lax.rsqrt(ms + NORM_EPS) * g


def _silu(x):
    return x * jax.nn.sigmoid(x)


def _softplus(x):
    return jnp.maximum(x, 0.0) + jnp.log1p(jnp.exp(-jnp.abs(x)))


def _dot(a, b):
    return jnp.dot(a, b, preferred_element_type=F32)


def _dot_nt(a, b):
    return lax.dot_general(a, b, (((1,), (1,)), ((), ())), preferred_element_type=F32)


def _dot_tn(a, b):
    return lax.dot_general(a, b, (((0,), (0,)), ((), ())), preferred_element_type=F32)


def _lru_gates(conv, wr_ref, wi_ref, br_ref, bi_ref, lam_ref):
    gin = conv.astype(BF16)
    half = LRU_WIDTH // 2
    lo, hi = gin[:, :half], gin[:, half:]
    r = jax.nn.sigmoid(jnp.concatenate([_dot(lo, wr_ref[0]), _dot(hi, wr_ref[1])], axis=1) + br_ref[...])
    i = jax.nn.sigmoid(jnp.concatenate([_dot(lo, wi_ref[0]), _dot(hi, wi_ref[1])], axis=1) + bi_ref[...])
    log_a = -LRU_C * r * _softplus(-lam_ref[...])
    a = jnp.exp(log_a)
    m2 = -jnp.tanh(log_a) * (1.0 + a * a)
    mult = jnp.where(m2 > 0.0, m2 * lax.rsqrt(m2), 0.0)
    return a, mult, i


_DONE = object()


def _run(gen):
    for _ in gen:
        pass


def _zip_stages(gens):
    gens = list(gens)
    while gens:
        gens = [g for g in gens if next(g, _DONE) is not _DONE]
        if gens:
            yield


def _chain(gens):
    for g in gens:
        yield from g


def _interleave(main, sides, n_main, n_side):
    side = _chain(sides)
    next(side)
    done_side = 1
    for i, _ in enumerate(main):
        want = ((i + 1) * n_side) // n_main
        while done_side < want and next(side, _DONE) is not _DONE:
            done_side += 1
    _run(side)


IN_PROJ_COLS = 256
IN_PROJ_STAGES = 1 + IN_WIDTH // IN_PROJ_COLS


def _in_proj(x_ref, ng_ref, win_ref, proj_ref):
    hb = _rms(x_ref[...], ng_ref[...]).astype(BF16)
    yield
    for j in range(IN_WIDTH // IN_PROJ_COLS):
        cols = slice(j * IN_PROJ_COLS, (j + 1) * IN_PROJ_COLS)
        proj_ref[:, cols] = _dot(hb, win_ref[:, cols])
        yield


OUT_PROJ_STAGES = D_MODEL // 256


def _out_proj(mixed_s, x_ref, y_ref, wout_ref):
    mixed = mixed_s[...]
    for j in range(OUT_PROJ_STAGES):
        cols = slice(j * 256, (j + 1) * 256)
        y_ref[:, cols] = x_ref[:, cols] + _dot(mixed, wout_ref[:, cols])
        yield


def _mix_stages(tt):
    return 4 + tt // SUBLANES // 8 + (tt // RET_CHUNK) * 5


def _scan_rows(a, u, h_in, tt):
    G = tt // SUBLANES
    W = a.shape[-1]
    a3 = a.reshape(G, SUBLANES, W)
    u3 = u.reshape(G, SUBLANES, W)
    r3 = lax.broadcasted_iota(jnp.int32, (G, SUBLANES, W), 1)
    sh = 1
    while sh < SUBLANES:
        keep = r3 >= sh
        a_sh = jnp.where(keep, pltpu.roll(a3, sh, axis=1), 1.0)
        u_sh = jnp.where(keep, pltpu.roll(u3, sh, axis=1), 0.0)
        u3 = a3 * u_sh + u3
        a3 = a3 * a_sh
        sh *= 2
    yield
    hs = []
    for g in range(G):
        hg = a3[g] * h_in + u3[g]
        hs.append(hg)
        h_in = hg[SUBLANES - 1:SUBLANES, :]
        if g % 8 == 7:
            yield
    return jnp.concatenate(hs, axis=0)


def _retention_head(h, proj_s, r0, s_ref, retg_ref, dmask_ref, qdec_ref, kdec_ref, mixed_s):
    C = RET_CHUNK
    cs = slice(h * RET_DK, (h + 1) * RET_DK)
    q = proj_s[r0:r0 + C, QA + h * 128:QA + (h + 1) * 128]
    k = proj_s[r0:r0 + C, KA + h * 128:KA + (h + 1) * 128] * (RET_DK ** -0.5)
    vb = proj_s[r0:r0 + C, VA + h * 128:VA + (h + 1) * 128].astype(BF16)
    sc = _dot_nt(q.astype(BF16), k.astype(BF16)) * dmask_ref[h]
    yield
    s_prev = s_ref[0, h]
    lhs = jnp.concatenate([sc.astype(BF16), (q * qdec_ref[h]).astype(BF16)], axis=1)
    rhs = jnp.concatenate([vb, s_prev.astype(BF16)], axis=0)
    o = _dot(lhs, rhs)
    yield
    kv = _dot_tn((k * kdec_ref[h]).astype(BF16), vb)
    s_ref[0, h] = float(np.exp(np.float32(C) * _LOG_G[h])) * s_prev + kv
    yield
    o = _rms(o, retg_ref[:, cs])
    mixed_s[r0:r0 + C, cs] = (o * _silu(proj_s[r0:r0 + C, GA + h * 128:GA + (h + 1) * 128])).astype(BF16)
    yield


def _swa_kv_head(kh, proj_s, r0, blk, sinks_ref, qg_ref, kg_ref, mixed_s, kprev_s, vprev_s):
    C = RET_CHUNK
    ks = slice(kh * 128, (kh + 1) * 128)
    h0, h1 = kh * SWA_GROUP, kh * SWA_GROUP + 1
    kn = _rms(proj_s[r0:r0 + C, KC + kh * 128:KC + (kh + 1) * 128], kg_ref[...])
    vv = proj_s[r0:r0 + C, VC + kh * 128:VC + (kh + 1) * 128]
    kband = jnp.concatenate([kprev_s[:, ks], kn], axis=0).astype(BF16)
    vband = jnp.concatenate([vprev_s[:, ks], vv], axis=0).astype(BF16)
    kprev_s[:, ks] = kn
    vprev_s[:, ks] = vv
    q0 = _rms(proj_s[r0:r0 + C, QC + h0 * 128:QC + (h0 + 1) * 128], qg_ref[...])
    q1 = _rms(proj_s[r0:r0 + C, QC + h1 * 128:QC + (h1 + 1) * 128], qg_ref[...])
    qq = jnp.concatenate([q0, q1], axis=0).astype(BF16)
    yield
    s = _dot_nt(qq, kband) * (SWA_HEAD_DIM ** -0.5)
    ii = lax.broadcasted_iota(jnp.int32, (2 * C, 2 * C), 0)
    jj = lax.broadcasted_iota(jnp.int32, (2 * C, 2 * C), 1)
    dist = jnp.where(ii >= C, ii - C, ii) + C - jj
    valid = jnp.logical_and(jnp.logical_and(dist >= 0, dist < WINDOW), jj >= jnp.where(blk > 0, 0, C))
    slope = jnp.where(ii >= C, _SLOPES[h1], _SLOPES[h0])
    s = jnp.where(valid, s - slope * dist.astype(F32), NEG_INF)
    yield
    rowc = lax.broadcasted_iota(jnp.int32, (2 * C, 1), 0)
    sink = jnp.where(rowc >= C, sinks_ref[h1], sinks_ref[h0])
    m = jnp.maximum(jnp.max(s, axis=-1, keepdims=True), sink)
    p = jnp.exp(s - m)
    denom = jnp.sum(p, axis=-1, keepdims=True) + jnp.exp(sink - m)
    yield
    o = _dot((p / denom).astype(BF16), vband)
    yield
    for g, hh in enumerate((h0, h1)):
        gc = proj_s[r0:r0 + C, GC + hh * 128:GC + (hh + 1) * 128]
        mixed_s[r0:r0 + C, 2 * GROUP_WIDTH + hh * 128:2 * GROUP_WIDTH + (hh + 1) * 128] = (
            (o[g * C:(g + 1) * C] * _silu(gc)).astype(BF16))
    yield


def _mix_tile(proj_s, t, tt, sinks_ref, retg_ref, convw_ref, convb_ref,
              wr_ref, wi_ref, br_ref, bi_ref, lam_ref, qg_ref, kg_ref, dmask_ref, qdec_ref, kdec_ref,
              s_ref, mixed_s, xtail_s, kprev_s, vprev_s, hc_s):
    nch = tt // RET_CHUNK
    C = RET_CHUNK

    xb = proj_s[:, XB:XB + LRU_WIDTH]
    xfull = jnp.concatenate([xtail_s[...], xb], axis=0)
    xtail_s[...] = xb[tt - SUBLANES:tt, :]
    conv = convb_ref[...] + convw_ref[0:1, :] * pltpu.roll(xfull, 3, axis=0)[SUBLANES:]
    conv = conv + convw_ref[1:2, :] * pltpu.roll(xfull, 2, axis=0)[SUBLANES:]
    conv = conv + convw_ref[2:3, :] * pltpu.roll(xfull, 1, axis=0)[SUBLANES:]
    conv = conv + convw_ref[3:4, :] * xb
    yield

    a, mult, ig = _lru_gates(conv, wr_ref, wi_ref, br_ref, bi_ref, lam_ref)
    yield
    row = lax.broadcasted_iota(jnp.int32, (tt, LRU_WIDTH), 0)
    mult = jnp.where(row == jnp.where(t == 0, 0, -1), 1.0, mult)
    u = mult * ig * conv
    hseq = yield from _scan_rows(a, u, hc_s[...], tt)
    hc_s[...] = hseq[tt - 1:tt, :]
    mixed_s[:, GROUP_WIDTH:2 * GROUP_WIDTH] = (hseq * _silu(proj_s[:, GB:GB + LRU_WIDTH])).astype(BF16)
    yield

    for c in range(nch):
        r0 = c * C
        blk = t * nch + c
        yield from _zip_stages(
            [_retention_head(h, proj_s, r0, s_ref, retg_ref, dmask_ref, qdec_ref, kdec_ref, mixed_s)
             for h in range(RET_HEADS)]
            + [_swa_kv_head(kh, proj_s, r0, blk, sinks_ref, qg_ref, kg_ref, mixed_s, kprev_s, vprev_s)
               for kh in range(SWA_KV_HEADS)])


def _prompt_kernel(sinks_ref, xa_ref, xn_ref, ng_ref, win_ref, wout_ref, retg_ref, convw_ref, convb_ref,
                   wr_ref, wi_ref, br_ref, bi_ref, lam_ref, qg_ref, kg_ref,
                   dmask_ref, qdec_ref, kdec_ref, *rest, tt, npairs, layer, slot, fill):
    y_ref, s_ref, h_ref, conv_ref, kout_ref, vout_ref = rest[-14:-8]
    proj_a, proj_b, mixed_a, mixed_b, xtail_s, kprev_s, vprev_s, hc_s = rest[-8:]
    sinks_ref = sinks_ref.at[layer]
    b = pl.program_id(0)
    p = pl.program_id(1)

    @pl.when(p == 0)
    def _init():
        s_ref[...] = jnp.zeros_like(s_ref)
        xtail_s[...] = jnp.zeros_like(xtail_s)
        kprev_s[...] = jnp.zeros_like(kprev_s)
        vprev_s[...] = jnp.zeros_like(vprev_s)
        hc_s[...] = jnp.zeros_like(hc_s)

    xa0, xa1 = xa_ref.at[0, 0:tt, :], xa_ref.at[0, tt:2 * tt, :]
    ya0, ya1 = y_ref.at[0, 0:tt, :], y_ref.at[0, tt:2 * tt, :]

    @pl.when(jnp.logical_and(b == 0, p == 0))
    def _prologue():
        _run(_in_proj(xa0, ng_ref, win_ref, proj_a))

    common = (sinks_ref, retg_ref, convw_ref, convb_ref, wr_ref, wi_ref, br_ref, bi_ref,
              lam_ref, qg_ref, kg_ref, dmask_ref, qdec_ref, kdec_ref, s_ref.at[slot])
    state = (xtail_s, kprev_s, vprev_s, hc_s)

    n_main = _mix_stages(tt)
    _interleave(_mix_tile(proj_a, 2 * p, tt, *common, mixed_a, *state),
                [_in_proj(xa1, ng_ref, win_ref, proj_b)], n_main, IN_PROJ_STAGES)
    _interleave(_mix_tile(proj_b, 2 * p + 1, tt, *common, mixed_b, *state),
                [_out_proj(mixed_a, xa0, ya0, wout_ref), _in_proj(xn_ref.at[0], ng_ref, win_ref, proj_a)],
                n_main, OUT_PROJ_STAGES + IN_PROJ_STAGES)
    _run(_out_proj(mixed_b, xa1, ya1, wout_ref))

    @pl.when(p == npairs - 1)
    def _state_out():
        h_ref[slot, 0] = hc_s[...]
        conv_ref[slot, 0] = xtail_s[SUBLANES - (CONV_W - 1):SUBLANES, :]
        for kh in range(SWA_KV_HEADS):
            ks = slice(kh * 128, (kh + 1) * 128)
            kout_ref[slot, 0, pl.ds(kh, WINDOW, stride=SWA_KV_HEADS), :] = kprev_s[:, ks]
            vout_ref[slot, 0, pl.ds(kh, WINDOW, stride=SWA_KV_HEADS), :] = vprev_s[:, ks]
        if fill:
            for other in [o for o in range(DEPTH) if o != slot]:
                for ref in (h_ref, conv_ref, kout_ref, vout_ref):
                    ref[other] = jnp.zeros(ref.shape[1:], F32)


def _const_spec(shape):
    nd = len(shape)
    return pl.BlockSpec(shape, lambda *_: (0,) * nd)


def _resident_spec(shape):
    nd = len(shape)
    return pl.BlockSpec(shape, lambda *_: (0,) * nd, pipeline_mode=pl.Buffered(1))


def _layer_spec(a, layer, resident=False):
    nd = a.ndim
    kw = dict(pipeline_mode=pl.Buffered(1)) if resident else {}
    return pl.BlockSpec((None,) + a.shape[1:], lambda *_: (layer,) + (0,) * (nd - 1), **kw)


def _prompt_layer(layer, x, prm, tables, prev_out, tt=256):
    B, T, D = x.shape
    npairs = T // (2 * tt)
    layer_in = [prm[k] for k in ("ng", "win", "wout", "retg", "convw", "convb", "wr", "wi", "br", "bi",
                                 "lam", "qg", "kg")]
    vec_in = layer_in + list(tables)

    def next_tile(b, p):
        last = p == npairs - 1
        return (jnp.where(last, jnp.minimum(b + 1, B - 1), b), jnp.where(last, 0, 2 * p + 2), 0)

    in_specs = ([pl.BlockSpec(memory_space=pltpu.SMEM),
                 pl.BlockSpec((1, 2 * tt, D), lambda b, p: (b, p, 0)),
                 pl.BlockSpec((1, tt, D), next_tile)]
                + [_layer_spec(a, layer, resident=True) for a in layer_in]
                + [_resident_spec(a.shape) for a in tables]
                + [pl.BlockSpec(memory_space=pl.ANY) for _ in prev_out])
    out_shape = (
        jax.ShapeDtypeStruct((B, T, D), F32),
        jax.ShapeDtypeStruct((DEPTH, B, RET_HEADS, RET_DK, RET_DV), F32),
        jax.ShapeDtypeStruct((DEPTH, B, 1, LRU_WIDTH), F32),
        jax.ShapeDtypeStruct((DEPTH, B, CONV_W - 1, LRU_WIDTH), F32),
        jax.ShapeDtypeStruct((DEPTH, B, 2 * WINDOW, SWA_HEAD_DIM), F32),
        jax.ShapeDtypeStruct((DEPTH, B, 2 * WINDOW, SWA_HEAD_DIM), F32),
    )
    fill = not prev_out
    ld, li, slot = (DEPTH, 0, layer) if fill else (1, layer, 0)
    out_specs = (
        pl.BlockSpec((1, 2 * tt, D), lambda b, p: (b, p, 0)),
        pl.BlockSpec((ld, 1, RET_HEADS, RET_DK, RET_DV), lambda b, p: (li, b, 0, 0, 0)),
        pl.BlockSpec((ld, 1, 1, LRU_WIDTH), lambda b, p: (li, b, 0, 0)),
        pl.BlockSpec((ld, 1, CONV_W - 1, LRU_WIDTH), lambda b, p: (li, b, 0, 0)),
        pl.BlockSpec((ld, 1, 2 * WINDOW, SWA_HEAD_DIM), lambda b, p: (li, b, 0, 0)),
        pl.BlockSpec((ld, 1, 2 * WINDOW, SWA_HEAD_DIM), lambda b, p: (li, b, 0, 0)),
    )
    n_in = 3 + len(vec_in)
    aliases = {n_in + j: 1 + j for j in range(len(prev_out))}
    scratch = [
        pltpu.VMEM((tt, IN_WIDTH), F32),
        pltpu.VMEM((tt, IN_WIDTH), F32),
        pltpu.VMEM((tt, MIX_WIDTH), BF16),
        pltpu.VMEM((tt, MIX_WIDTH), BF16),
        pltpu.VMEM((SUBLANES, LRU_WIDTH), F32),
        pltpu.VMEM((WINDOW, 256), F32),
        pltpu.VMEM((WINDOW, 256), F32),
        pltpu.VMEM((1, LRU_WIDTH), F32),
    ]
    return pl.pallas_call(
        functools.partial(_prompt_kernel, tt=tt, npairs=npairs, layer=layer, slot=slot, fill=fill),
        grid=(B, npairs),
        in_specs=in_specs,
        out_specs=out_specs,
        out_shape=out_shape,
        scratch_shapes=scratch,
        input_output_aliases=aliases,
        compiler_params=pltpu.CompilerParams(
            dimension_semantics=("arbitrary", "arbitrary"), vmem_limit_bytes=VMEM_LIMIT),
        name="prompt_layer",
    )(prm["sinks"], x, x, *vec_in, *prev_out)


def _sproj_kernel(x_ref, ng_ref, win_ref, o_ref, hb_s):
    @pl.when(pl.program_id(0) == 0)
    def _norm_once():
        hb_s[...] = _rms(x_ref[...], ng_ref[...]).astype(BF16)

    o_ref[...] = _dot(hb_s[...], win_ref[...])


def _sample_proj(layer, xp, prm):
    R = xp.shape[0]
    nb = IN_WIDTH // 512
    ng, win = prm["ng"], prm["win"]
    return pl.pallas_call(
        _sproj_kernel,
        grid=(nb,),
        in_specs=[_const_spec(xp.shape), _layer_spec(ng, layer),
                  pl.BlockSpec((None, D_MODEL, 512), lambda j: (layer, 0, j))],
        out_specs=pl.BlockSpec((R, 512), lambda j: (0, j)),
        out_shape=jax.ShapeDtypeStruct((R, IN_WIDTH), F32),
        scratch_shapes=[pltpu.VMEM((R, D_MODEL), BF16)],
        compiler_params=pltpu.CompilerParams(
            dimension_semantics=("arbitrary",), vmem_limit_bytes=VMEM_LIMIT),
        name="sample_proj",
    )(xp, ng, win)


def _sample_mix_seq(sinks_ref, pa_ref, pc_ref, s0_ref, kbuf_ref, vbuf_ref, retg_ref, qg_ref, kg_ref,
                    cm_ref, qdec_ref, kdec_ref, mix_ref, snew_ref, knew_ref, vnew_ref):
    P = SEQ_PAD
    for h in range(RET_HEADS):
        cs = slice(h * 128, (h + 1) * 128)
        q = pa_ref[:, QA + h * 128:QA + (h + 1) * 128]
        k = pa_ref[:, KA + h * 128:KA + (h + 1) * 128] * (RET_DK ** -0.5)
        v = pa_ref[:, VA + h * 128:VA + (h + 1) * 128]
        intra = jnp.zeros((P, RET_DV), F32)
        for s in range(4):
            r_ = TOK0 + s
            w = jnp.sum(q * k[r_:r_ + 1, :], axis=-1, keepdims=True)
            intra = intra + (w * cm_ref[h, s]) * v[r_:r_ + 1, :]
        s_prev = s0_ref[h]
        cross = _dot((q * qdec_ref[h]).astype(BF16), s_prev.astype(BF16))
        kv = _dot_tn((k * kdec_ref[h]).astype(BF16), v.astype(BF16))
        yield
        snew_ref[h] = float(np.exp(np.float32(4.0) * _LOG_G[h])) * s_prev + kv
        o = _rms(intra + cross, retg_ref[:, cs])
        mix_ref[:, cs] = o * _silu(pa_ref[:, GA + h * 128:GA + (h + 1) * 128])
        yield

    r16 = lax.broadcasted_iota(jnp.int32, (2 * P, WINDOW), 0)
    j16 = lax.broadcasted_iota(jnp.int32, (2 * P, WINDOW), 1)
    rr = jnp.where(r16 >= P, r16 - P, r16)
    row_ok = jnp.logical_and(rr >= TOK0, rr < TOK0 + 4)
    dist = (rr - TOK0) + WINDOW - j16
    distf = dist.astype(F32)
    valid = jnp.logical_and(jnp.logical_and(dist >= 0, dist < WINDOW), row_ok)
    r16c = lax.broadcasted_iota(jnp.int32, (2 * P, 1), 0)
    rrc = jnp.where(r16c >= P, r16c - P, r16c)
    rowc_ok = jnp.logical_and(rrc >= TOK0, rrc < TOK0 + 4)
    knew_ref[0:2 * (WINDOW - 4), :] = kbuf_ref[2 * 4:2 * WINDOW, :]
    vnew_ref[0:2 * (WINDOW - 4), :] = vbuf_ref[2 * 4:2 * WINDOW, :]
    for kh in range(SWA_KV_HEADS):
        h0, h1 = kh * SWA_GROUP, kh * SWA_GROUP + 1
        kb = kbuf_ref[pl.ds(kh, WINDOW, stride=SWA_KV_HEADS), :]
        vb = vbuf_ref[pl.ds(kh, WINDOW, stride=SWA_KV_HEADS), :]
        kc, vc = KC - QC, VC - QC
        kn = _rms(pc_ref[:, kc + kh * 128:kc + (kh + 1) * 128], kg_ref[...])
        vn = pc_ref[:, vc + kh * 128:vc + (kh + 1) * 128]
        q0 = _rms(pc_ref[:, h0 * 128:(h0 + 1) * 128], qg_ref[...])
        q1 = _rms(pc_ref[:, h1 * 128:(h1 + 1) * 128], qg_ref[...])
        qq = jnp.concatenate([q0, q1], axis=0)
        slope = jnp.where(r16 >= P, _SLOPES[h1], _SLOPES[h0])
        slopec = jnp.where(r16c >= P, _SLOPES[h1], _SLOPES[h0])
        sb = _dot_nt(qq.astype(BF16), kb.astype(BF16)) * (SWA_HEAD_DIM ** -0.5)
        sb = jnp.where(valid, sb - slope * distf, NEG_INF)
        yield
        sink = jnp.where(r16c >= P, sinks_ref[h1], sinks_ref[h0])
        m = jnp.maximum(jnp.max(sb, axis=-1, keepdims=True), sink)
        wn = []
        for s in range(4):
            r_ = TOK0 + s
            w = jnp.sum(qq * kn[r_:r_ + 1, :], axis=-1, keepdims=True) * (SWA_HEAD_DIM ** -0.5)
            dn = rrc - r_
            w = jnp.where(jnp.logical_and(dn >= 0, rowc_ok), w - slopec * dn.astype(F32), NEG_INF)
            wn.append(w)
            m = jnp.maximum(m, w)
        pb = jnp.exp(sb - m)
        denom = jnp.sum(pb, axis=-1, keepdims=True) + jnp.exp(sink - m)
        pn = [jnp.exp(w - m) for w in wn]
        for p_ in pn:
            denom = denom + p_
        yield
        o = _dot((pb / denom).astype(BF16), vb.astype(BF16))
        for s in range(4):
            r_ = TOK0 + s
            o = o + (pn[s] / denom) * vn[r_:r_ + 1, :]
        for g, hh in enumerate((h0, h1)):
            gc = pc_ref[:, GC - QC + hh * 128:GC - QC + (hh + 1) * 128]
            mix_ref[:, GROUP_WIDTH + hh * 128:GROUP_WIDTH + (hh + 1) * 128] = o[g * P:(g + 1) * P] * _silu(gc)
        for s in range(4):
            r_out = 2 * (WINDOW - 4 + s) + kh
            knew_ref[r_out:r_out + 1, :] = kn[TOK0 + s:TOK0 + s + 1, :]
            vnew_ref[r_out:r_out + 1, :] = vn[TOK0 + s:TOK0 + s + 1, :]
        yield


def _sample_mix_kernel(sinks_ref, pa_ref, pc_ref, s0_ref, kbuf_ref, vbuf_ref, retg_ref, qg_ref, kg_ref,
                       cm_ref, qdec_ref, kdec_ref, *rest, nseq, layer, slot, fill):
    mix_ref, snew_ref, knew_ref, vnew_ref = rest[-4:]
    sinks_ref = sinks_ref.at[layer]
    if fill:
        for other in [o for o in range(DEPTH) if o != slot]:
            for ref in (snew_ref, knew_ref, vnew_ref):
                ref[other] = jnp.zeros(ref.shape[1:], F32)
    gens = []
    for i in range(nseq):
        rows = slice(i * SEQ_PAD, (i + 1) * SEQ_PAD)
        gens.append(_sample_mix_seq(
            sinks_ref, pa_ref.at[rows, :], pc_ref.at[rows, :], s0_ref.at[0, i], kbuf_ref.at[0, i], vbuf_ref.at[0, i],
            retg_ref, qg_ref, kg_ref, cm_ref, qdec_ref, kdec_ref,
            mix_ref.at[rows, :], snew_ref.at[slot, i], knew_ref.at[slot, i], vnew_ref.at[slot, i]))
    _run(_zip_stages(gens))


def _sample_mix(layer, proj, state_ret, cache_k, cache_v, prm, tables, prev_out, nseq=8):
    depth, B = state_ret.shape[:2]
    layer_in = [prm[k] for k in ("retg", "qg", "kg")]
    consts = layer_in + list(tables)
    st_spec = pl.BlockSpec((1, nseq, RET_HEADS, RET_DK, RET_DV), lambda i: (layer, i, 0, 0, 0))
    kv_spec = pl.BlockSpec((1, nseq, 2 * WINDOW, SWA_HEAD_DIM), lambda i: (layer, i, 0, 0))
    in_specs = ([pl.BlockSpec(memory_space=pltpu.SMEM),
                 pl.BlockSpec((nseq * SEQ_PAD, XB), lambda i: (i, 0)),
                 pl.BlockSpec((nseq * SEQ_PAD, IN_WIDTH - QC), lambda i: (i, QC // (IN_WIDTH - QC))),
                 st_spec, kv_spec, kv_spec]
                + [_layer_spec(a, layer) for a in layer_in]
                + [_const_spec(a.shape) for a in tables]
                + [pl.BlockSpec(memory_space=pl.ANY) for _ in prev_out])
    out_shape = (
        jax.ShapeDtypeStruct((B * SEQ_PAD, 2 * GROUP_WIDTH), F32),
        jax.ShapeDtypeStruct(state_ret.shape, F32),
        jax.ShapeDtypeStruct(cache_k.shape, F32),
        jax.ShapeDtypeStruct(cache_v.shape, F32),
    )
    fill = not prev_out
    ld, li, slot = (DEPTH, 0, layer) if fill else (1, layer, 0)
    out_specs = (pl.BlockSpec((nseq * SEQ_PAD, 2 * GROUP_WIDTH), lambda i: (i, 0)),
                 pl.BlockSpec((ld, nseq, RET_HEADS, RET_DK, RET_DV), lambda i: (li, i, 0, 0, 0)),
                 pl.BlockSpec((ld, nseq, 2 * WINDOW, SWA_HEAD_DIM), lambda i: (li, i, 0, 0)),
                 pl.BlockSpec((ld, nseq, 2 * WINDOW, SWA_HEAD_DIM), lambda i: (li, i, 0, 0)))
    assert QC % (IN_WIDTH - QC) == 0
    n_in = 6 + len(consts)
    aliases = {n_in + j: 1 + j for j in range(len(prev_out))}
    return pl.pallas_call(
        functools.partial(_sample_mix_kernel, nseq=nseq, layer=layer, slot=slot, fill=fill),
        grid=(B // nseq,),
        in_specs=in_specs,
        out_specs=out_specs,
        out_shape=out_shape,
        input_output_aliases=aliases,
        compiler_params=pltpu.CompilerParams(
            dimension_semantics=("arbitrary",), vmem_limit_bytes=VMEM_LIMIT),
        name="sample_mix",
    )(prm["sinks"], proj, proj, state_ret, cache_k, cache_v, *consts, *prev_out)


def _sample_out_kernel(x_ref, xb_ref, gb_ref, mix_ref, conv8_ref, h8_ref, convw_ref, convb_ref,
                       wr_ref, wi_ref, br_ref, bi_ref, lam_ref, wout_ref,
                       y_ref, convo_ref, ho_ref):
    R = x_ref.shape[0]
    row = lax.broadcasted_iota(jnp.int32, (R, LRU_WIDTH), 0) & (SEQ_PAD - 1)
    xc = jnp.where(row < TOK0, conv8_ref[...], xb_ref[...])
    convo_ref[...] = xc
    conv = convb_ref[...] + convw_ref[0:1, :] * pltpu.roll(xc, 3, axis=0)
    conv = conv + convw_ref[1:2, :] * pltpu.roll(xc, 2, axis=0)
    conv = conv + convw_ref[2:3, :] * pltpu.roll(xc, 1, axis=0)
    conv = conv + convw_ref[3:4, :] * xc
    a, mult, ig = _lru_gates(conv, wr_ref, wi_ref, br_ref, bi_ref, lam_ref)
    u = mult * ig * conv
    h = h8_ref[...]
    for s in range(4):
        h = jnp.where(row == TOK0 + s, a * pltpu.roll(h, 1, axis=0) + u, h)
    ho_ref[...] = h
    ob = h * _silu(gb_ref[...])
    y = x_ref[...] + _dot(mix_ref[:, 0:GROUP_WIDTH].astype(BF16), wout_ref[0:GROUP_WIDTH, :])
    y = y + _dot(ob.astype(BF16), wout_ref[GROUP_WIDTH:2 * GROUP_WIDTH, :])
    y = y + _dot(mix_ref[:, GROUP_WIDTH:2 * GROUP_WIDTH].astype(BF16), wout_ref[2 * GROUP_WIDTH:, :])
    y_ref[...] = y


def _sample_out(layer, xp, proj, mix, conv8, h8, prm, rows=256):
    R = xp.shape[0]
    consts = [prm[k] for k in ("convw", "convb", "wr", "wi", "br", "bi", "lam", "wout")]
    in_specs = ([pl.BlockSpec((rows, D_MODEL), lambda i: (i, 0)),
                 pl.BlockSpec((rows, LRU_WIDTH), lambda i: (i, XB // LRU_WIDTH)),
                 pl.BlockSpec((rows, LRU_WIDTH), lambda i: (i, GB // LRU_WIDTH)),
                 pl.BlockSpec((rows, 2 * GROUP_WIDTH), lambda i: (i, 0)),
                 pl.BlockSpec((rows, LRU_WIDTH), lambda i: (i, 0)),
                 pl.BlockSpec((rows, LRU_WIDTH), lambda i: (i, 0))]
                + [_layer_spec(a, layer) for a in consts])
    out_shape = (
        jax.ShapeDtypeStruct((R, D_MODEL), F32),
        jax.ShapeDtypeStruct((R, LRU_WIDTH), F32),
        jax.ShapeDtypeStruct((R, LRU_WIDTH), F32),
    )
    out_specs = (
        pl.BlockSpec((rows, D_MODEL), lambda i: (i, 0)),
        pl.BlockSpec((rows, LRU_WIDTH), lambda i: (i, 0)),
        pl.BlockSpec((rows, LRU_WIDTH), lambda i: (i, 0)),
    )
    return pl.pallas_call(
        _sample_out_kernel,
        grid=(R // rows,),
        in_specs=in_specs,
        out_specs=out_specs,
        out_shape=out_shape,
        compiler_params=pltpu.CompilerParams(
            dimension_semantics=("arbitrary",), vmem_limit_bytes=VMEM_LIMIT),
        name="sample_out",
    )(xp, proj, proj, mix, conv8, h8, *consts)


def _prompt_tables():
    C = RET_CHUNK
    idx = np.arange(C, dtype=np.float32)
    diff = idx[:, None] - idx[None, :]
    causal = diff >= 0
    lg = _LOG_G[:, None, None]
    dmask = np.where(causal[None], np.exp(np.where(causal, diff, 0.0)[None] * lg), 0.0).astype(np.float32)
    qdec = np.exp((idx + 1.0)[None, :] * _LOG_G[:, None]).astype(np.float32)
    kdec = np.exp((C - 1 - idx)[None, :] * _LOG_G[:, None]).astype(np.float32)
    qdec = np.broadcast_to(qdec[:, :, None], (RET_HEADS, C, RET_DK)).copy()
    kdec = np.broadcast_to(kdec[:, :, None], (RET_HEADS, C, RET_DK)).copy()
    return jnp.asarray(dmask), jnp.asarray(qdec), jnp.asarray(kdec)


def _sample_tables():
    P = SEQ_PAD
    rows = np.arange(P, dtype=np.float32)
    i = rows - TOK0
    tok = (i >= 0) & (i < 4)
    cm = np.zeros((RET_HEADS, 4, P, RET_DV), np.float32)
    qdec = np.zeros((RET_HEADS, P, RET_DK), np.float32)
    kdec = np.zeros((RET_HEADS, P, RET_DK), np.float32)
    for h in range(RET_HEADS):
        for s in range(4):
            d = i - s
            col = np.where(tok & (d >= 0), np.exp(np.where(d >= 0, d, 0.0) * _LOG_G[h]), 0.0)
            cm[h, s] = col[:, None]
        qdec[h] = np.where(tok, np.exp((i + 1.0) * _LOG_G[h]), 0.0)[:, None]
        kdec[h] = np.where(tok, np.exp((3.0 - i) * _LOG_G[h]), 0.0)[:, None]
    return jnp.asarray(cm), jnp.asarray(qdec), jnp.asarray(kdec)


def _block_diag(w):
    L, n, d, _ = w.shape
    g = n // 2
    w = w.reshape(L, 2, g, d, d)
    eye = jnp.eye(g, dtype=w.dtype)
    return (eye[None, None, :, None, :, None] * w[:, :, :, :, None, :]).reshape(L, 2, g * d, g * d)


def kernel(x_prompt, x_sample, state_ret, state_lru, state_conv, cache_swa_k, cache_swa_v, norm_g, w_in, w_out, ret_norm_g, conv_w, conv_b, w_rgate, b_rgate, w_igate, b_igate, lru_lambda, q_norm_g, k_norm_g, attn_sinks):
    Bs, Ts, _ = x_sample.shape
    ptab = _prompt_tables()
    stab = _sample_tables()
    yp = x_prompt
    ys = jnp.pad(x_sample, ((0, 0), (TOK0, SEQ_PAD - TOK0 - Ts), (0, 0))).reshape(Bs * SEQ_PAD, D_MODEL)
    cache_k = cache_swa_k.reshape(DEPTH, Bs, 2 * WINDOW, SWA_HEAD_DIM)
    cache_v = cache_swa_v.reshape(DEPTH, Bs, 2 * WINDOW, SWA_HEAD_DIM)
    Bp = x_prompt.shape[0]
    p_state, s_state = (), ()
    s_h, s_conv = [], []
    row = lambda a: a.reshape(DEPTH, 1, -1)
    prm = dict(
        sinks=attn_sinks, ng=row(norm_g), win=w_in.astype(BF16), wout=w_out.astype(BF16), retg=row(ret_norm_g),
        convw=conv_w, convb=row(conv_b), wr=_block_diag(w_rgate).astype(BF16), wi=_block_diag(w_igate).astype(BF16),
        br=row(b_rgate), bi=row(b_igate), lam=row(lru_lambda), qg=row(q_norm_g), kg=row(k_norm_g))
    for l in range(DEPTH):
        yp, *p_state = _prompt_layer(l, yp, prm, ptab, tuple(p_state))

        proj = _sample_proj(l, ys, prm)
        mix, *s_state = _sample_mix(l, proj, state_ret, cache_k, cache_v, prm, stab, tuple(s_state))
        conv8 = jnp.pad(state_conv[l], ((0, 0), (0, SEQ_PAD - (CONV_W - 1)), (0, 0))).reshape(Bs * SEQ_PAD, LRU_WIDTH)
        h8 = jnp.pad(state_lru[l][:, None, :], ((0, 0), (TOK0 - 1, SEQ_PAD - TOK0), (0, 0))).reshape(Bs * SEQ_PAD, LRU_WIDTH)
        ys, convo, ho = _sample_out(l, ys, proj, mix, conv8, h8, prm)
        s_h.append(ho.reshape(Bs, SEQ_PAD, LRU_WIDTH)[:, TOK0 + Ts - 1])
        s_conv.append(convo.reshape(Bs, SEQ_PAD, LRU_WIDTH)[:, TOK0 + Ts - (CONV_W - 1):TOK0 + Ts])

    p_ret, p_h, p_conv, p_k, p_v = p_state
    s_ret, s_k, s_v = s_state
    kv5 = lambda a, n: a.reshape(DEPTH, n, WINDOW, SWA_KV_HEADS, SWA_HEAD_DIM)
    y_sample = ys.reshape(Bs, SEQ_PAD, D_MODEL)[:, TOK0:TOK0 + Ts]
    return (yp, y_sample,
            p_ret, p_h.reshape(DEPTH, Bp, LRU_WIDTH), p_conv, kv5(p_k, Bp), kv5(p_v, Bp),
            s_ret, jnp.stack(s_h), jnp.stack(s_conv), kv5(s_k, Bs), kv5(s_v, Bs))
```

```python
import functools
import math

import numpy as np
import jax
import jax.numpy as jnp
from jax import lax
from jax.experimental import pallas as pl
from jax.experimental.pallas import tpu as pltpu

D_MODEL = 1024
DEPTH = 2
PAST_LEN = 16384
GROUP_WIDTH = 512
RET_HEADS = 4
RET_DK = 128
RET_DV = 128
RET_CHUNK = 128
LRU_WIDTH = 512
LRU_BLOCKS = 8
LRU_C = 8.0
CONV_W = 4
SWA_HEADS = 4
SWA_KV_HEADS = 2
SWA_GROUP = 2
SWA_HEAD_DIM = 128
WINDOW = 128
NORM_EPS = 1e-6
NEG_INF = -1e30

IN_WIDTH = 4608
MIX_WIDTH = 1536
QA, KA, VA, GA, XB, GB, QC, KC, VC, GC = 0, 512, 1024, 1536, 2048, 2560, 3072, 3584, 3840, 4096

F32 = jnp.float32
BF16 = jnp.bfloat16

SUBLANES = 8
SEQ_PAD = 8
TOK0 = 3
VMEM_LIMIT = 56 * 1024 * 1024

_LOG_G = np.log1p(-np.power(np.float32(2.0), (-5.0 - np.arange(RET_HEADS)).astype(np.float32))).astype(np.float32)
_SLOPES = [2.0 ** (-8.0 * (h + 1) / SWA_HEADS) for h in range(SWA_HEADS)]


def _rms(x, g):
    ms = jnp.mean(x * x, axis=-1, keepdims=True)
    return x * lax.rsqrt(ms + NORM_EPS) * g


def _silu(x):
    return x * jax.nn.sigmoid(x)


def _softplus(x):
    return jnp.maximum(x, 0.0) + jnp.log1p(jnp.exp(-jnp.abs(x)))


def _dot(a, b):
    return jnp.dot(a, b, preferred_element_type=F32)


def _dot_nt(a, b):
    return lax.dot_general(a, b, (((1,), (1,)), ((), ())), preferred_element_type=F32)


def _dot_tn(a, b):
    return lax.dot_general(a, b, (((0,), (0,)), ((), ())), preferred_element_type=F32)


def _lru_gates(conv, wr_ref, wi_ref, br_ref, bi_ref, lam_ref):
    gin = conv.astype(BF16)
    half = LRU_WIDTH // 2
    lo, hi = gin[:, :half], gin[:, half:]
    r = jax.nn.sigmoid(jnp.concatenate([_dot(lo, wr_ref[0]), _dot(hi, wr_ref[1])], axis=1) + br_ref[...])
    i = jax.nn.sigmoid(jnp.concatenate([_dot(lo, wi_ref[0]), _dot(hi, wi_ref[1])], axis=1) + bi_ref[...])
    log_a = -LRU_C * r * _softplus(-lam_ref[...])
    a = jnp.exp(log_a)
    m2 = -jnp.tanh(log_a) * (1.0 + a * a)
    mult = jnp.where(m2 > 0.0, m2 * lax.rsqrt(m2), 0.0)
    return a, mult, i


_DONE = object()


def _run(gen):
    for _ in gen:
        pass


def _zip_stages(gens):
    gens = list(gens)
    while gens:
        gens = [g for g in gens if next(g, _DONE) is not _DONE]
        if gens:
            yield


def _chain(gens):
    for g in gens:
        yield from g


def _interleave(main, sides, n_main, n_side):
    side = _chain(sides)
    next(side)
    done_side = 1
    for i, _ in enumerate(main):
        want = ((i + 1) * n_side) // n_main
        while done_side < want and next(side, _DONE) is not _DONE:
            done_side += 1
    _run(side)


IN_PROJ_COLS = 256
IN_PROJ_STAGES = 1 + IN_WIDTH // IN_PROJ_COLS


def _in_proj(x_ref, ng_ref, win_ref, proj_ref):
    hb = _rms(x_ref[...], ng_ref[...]).astype(BF16)
    yield
    for j in range(IN_WIDTH // IN_PROJ_COLS):
        cols = slice(j * IN_PROJ_COLS, (j + 1) * IN_PROJ_COLS)
        proj_ref[:, cols] = _dot(hb, win_ref[:, cols])
        yield


OUT_PROJ_STAGES = D_MODEL // 256


def _out_proj(mixed_s, x_ref, y_ref, wout_ref):
    mixed = mixed_s[...]
    for j in range(OUT_PROJ_STAGES):
        cols = slice(j * 256, (j + 1) * 256)
        y_ref[:, cols] = x_ref[:, cols] + _dot(mixed, wout_ref[:, cols])
        yield


def _mix_stages(tt):
    return 4 + tt // SUBLANES // 8 + (tt // RET_CHUNK) * 5


def _scan_rows(a, u, h_in, tt):
    G = tt // SUBLANES
    W = a.shape[-1]
    a3 = a.reshape(G, SUBLANES, W)
    u3 = u.reshape(G, SUBLANES, W)
    r3 = lax.broadcasted_iota(jnp.int32, (G, SUBLANES, W), 1)
    sh = 1
    while sh < SUBLANES:
        keep = r3 >= sh
        a_sh = jnp.where(keep, pltpu.roll(a3, sh, axis=1), 1.0)
        u_sh = jnp.where(keep, pltpu.roll(u3, sh, axis=1), 0.0)
        u3 = a3 * u_sh + u3
        a3 = a3 * a_sh
        sh *= 2
    yield
    hs = []
    for g in range(G):
        hg = a3[g] * h_in + u3[g]
        hs.append(hg)
        h_in = hg[SUBLANES - 1:SUBLANES, :]
        if g % 8 == 7:
            yield
    return jnp.concatenate(hs, axis=0)


def _retention_head(h, proj_s, r0, s_ref, retg_ref, dmask_ref, qdec_ref, kdec_ref, mixed_s):
    C = RET_CHUNK
    cs = slice(h * RET_DK, (h + 1) * RET_DK)
    q = proj_s[r0:r0 + C, QA + h * 128:QA + (h + 1) * 128]
    k = proj_s[r0:r0 + C, KA + h * 128:KA + (h + 1) * 128] * (RET_DK ** -0.5)
    vb = proj_s[r0:r0 + C, VA + h * 128:VA + (h + 1) * 128].astype(BF16)
    sc = _dot_nt(q.astype(BF16), k.astype(BF16)) * dmask_ref[h]
    yield
    s_prev = s_ref[0, h]
    lhs = jnp.concatenate([sc.astype(BF16), (q * qdec_ref[h]).astype(BF16)], axis=1)
    rhs = jnp.concatenate([vb, s_prev.astype(BF16)], axis=0)
    o = _dot(lhs, rhs)
    yield
    kv = _dot_tn((k * kdec_ref[h]).astype(BF16), vb)
    s_ref[0, h] = float(np.exp(np.float32(C) * _LOG_G[h])) * s_prev + kv
    yield
    o = _rms(o, retg_ref[:, cs])
    mixed_s[r0:r0 + C, cs] = (o * _silu(proj_s[r0:r0 + C, GA + h * 128:GA + (h + 1) * 128])).astype(BF16)
    yield


def _swa_kv_head(kh, proj_s, r0, blk, sinks_ref, qg_ref, kg_ref, mixed_s, kprev_s, vprev_s):
    C = RET_CHUNK
    ks = slice(kh * 128, (kh + 1) * 128)
    h0, h1 = kh * SWA_GROUP, kh * SWA_GROUP + 1
    kn = _rms(proj_s[r0:r0 + C, KC + kh * 128:KC + (kh + 1) * 128], kg_ref[...])
    vv = proj_s[r0:r0 + C, VC + kh * 128:VC + (kh + 1) * 128]
    kband = jnp.concatenate([kprev_s[:, ks], kn], axis=0).astype(BF16)
    vband = jnp.concatenate([vprev_s[:, ks], vv], axis=0).astype(BF16)
    kprev_s[:, ks] = kn
    vprev_s[:, ks] = vv
    q0 = _rms(proj_s[r0:r0 + C, QC + h0 * 128:QC + (h0 + 1) * 128], qg_ref[...])
    q1 = _rms(proj_s[r0:r0 + C, QC + h1 * 128:QC + (h1 + 1) * 128], qg_ref[...])
    qq = jnp.concatenate([q0, q1], axis=0).astype(BF16)
    yield
    s = _dot_nt(qq, kband) * (SWA_HEAD_DIM ** -0.5)
    ii = lax.broadcasted_iota(jnp.int32, (2 * C, 2 * C), 0)
    jj = lax.broadcasted_iota(jnp.int32, (2 * C, 2 * C), 1)
    dist = jnp.where(ii >= C, ii - C, ii) + C - jj
    valid = jnp.logical_and(jnp.logical_and(dist >= 0, dist < WINDOW), jj >= jnp.where(blk > 0, 0, C))
    slope = jnp.where(ii >= C, _SLOPES[h1], _SLOPES[h0])
    s = jnp.where(valid, s - slope * dist.astype(F32), NEG_INF)
    yield
    rowc = lax.broadcasted_iota(jnp.int32, (2 * C, 1), 0)
    sink = jnp.where(rowc >= C, sinks_ref[h1], sinks_ref[h0])
    m = jnp.maximum(jnp.max(s, axis=-1, keepdims=True), sink)
    p = jnp.exp(s - m)
    denom = jnp.sum(p, axis=-1, keepdims=True) + jnp.exp(sink - m)
    yield
    o = _dot((p / denom).astype(BF16), vband)
    yield
    for g, hh in enumerate((h0, h1)):
        gc = proj_s[r0:r0 + C, GC + hh * 128:GC + (hh + 1) * 128]
        mixed_s[r0:r0 + C, 2 * GROUP_WIDTH + hh * 128:2 * GROUP_WIDTH + (hh + 1) * 128] = (
            (o[g * C:(g + 1) * C] * _silu(gc)).astype(BF16))
    yield


def _mix_tile(proj_s, t, tt, sinks_ref, retg_ref, convw_ref, convb_ref,
              wr_ref, wi_ref, br_ref, bi_ref, lam_ref, qg_ref, kg_ref, dmask_ref, qdec_ref, kdec_ref,
              s_ref, mixed_s, xtail_s, kprev_s, vprev_s, hc_s):
    nch = tt // RET_CHUNK
    C = RET_CHUNK

    xb = proj_s[:, XB:XB + LRU_WIDTH]
    xfull = jnp.concatenate([xtail_s[...], xb], axis=0)
    xtail_s[...] = xb[tt - SUBLANES:tt, :]
    conv = convb_ref[...] + convw_ref[0:1, :] * pltpu.roll(xfull, 3, axis=0)[SUBLANES:]
    conv = conv + convw_ref[1:2, :] * pltpu.roll(xfull, 2, axis=0)[SUBLANES:]
    conv = conv + convw_ref[2:3, :] * pltpu.roll(xfull, 1, axis=0)[SUBLANES:]
    conv = conv + convw_ref[3:4, :] * xb
    yield

    a, mult, ig = _lru_gates(conv, wr_ref, wi_ref, br_ref, bi_ref, lam_ref)
    yield
    row = lax.broadcasted_iota(jnp.int32, (tt, LRU_WIDTH), 0)
    mult = jnp.where(row == jnp.where(t == 0, 0, -1), 1.0, mult)
    u = mult * ig * conv
    hseq = yield from _scan_rows(a, u, hc_s[...], tt)
    hc_s[...] = hseq[tt - 1:tt, :]
    mixed_s[:, GROUP_WIDTH:2 * GROUP_WIDTH] = (hseq * _silu(proj_s[:, GB:GB + LRU_WIDTH])).astype(BF16)
    yield

    for c in range(nch):
        r0 = c * C
        blk = t * nch + c
        yield from _zip_stages(
            [_retention_head(h, proj_s, r0, s_ref, retg_ref, dmask_ref, qdec_ref, kdec_ref, mixed_s)
             for h in range(RET_HEADS)]
            + [_swa_kv_head(kh, proj_s, r0, blk, sinks_ref, qg_ref, kg_ref, mixed_s, kprev_s, vprev_s)
               for kh in range(SWA_KV_HEADS)])


def _prompt_kernel(sinks_ref, xa_ref, xn_ref, ng_ref, win_ref, wout_ref, retg_ref, convw_ref, convb_ref,
                   wr_ref, wi_ref, br_ref, bi_ref, lam_ref, qg_ref, kg_ref,
                   dmask_ref, qdec_ref, kdec_ref, *rest, tt, npairs, layer, slot, fill):
    y_ref, s_ref, h_ref, conv_ref, kout_ref, vout_ref = rest[-14:-8]
    proj_a, proj_b, mixed_a, mixed_b, xtail_s, kprev_s, vprev_s, hc_s = rest[-8:]
    sinks_ref = sinks_ref.at[layer]
    b = pl.program_id(0)
    p = pl.program_id(1)

    @pl.when(p == 0)
    def _init():
        s_ref[...] = jnp.zeros_like(s_ref)
        xtail_s[...] = jnp.zeros_like(xtail_s)
        kprev_s[...] = jnp.zeros_like(kprev_s)
        vprev_s[...] = jnp.zeros_like(vprev_s)
        hc_s[...] = jnp.zeros_like(hc_s)

    xa0, xa1 = xa_ref.at[0, 0:tt, :], xa_ref.at[0, tt:2 * tt, :]
    ya0, ya1 = y_ref.at[0, 0:tt, :], y_ref.at[0, tt:2 * tt, :]

    @pl.when(jnp.logical_and(b == 0, p == 0))
    def _prologue():
        _run(_in_proj(xa0, ng_ref, win_ref, proj_a))

    common = (sinks_ref, retg_ref, convw_ref, convb_ref, wr_ref, wi_ref, br_ref, bi_ref,
              lam_ref, qg_ref, kg_ref, dmask_ref, qdec_ref, kdec_ref, s_ref.at[slot])
    state = (xtail_s, kprev_s, vprev_s, hc_s)

    n_main = _mix_stages(tt)
    _interleave(_mix_tile(proj_a, 2 * p, tt, *common, mixed_a, *state),
                [_in_proj(xa1, ng_ref, win_ref, proj_b)], n_main, IN_PROJ_STAGES)
    _interleave(_mix_tile(proj_b, 2 * p + 1, tt, *common, mixed_b, *state),
                [_out_proj(mixed_a, xa0, ya0, wout_ref), _in_proj(xn_ref.at[0], ng_ref, win_ref, proj_a)],
                n_main, OUT_PROJ_STAGES + IN_PROJ_STAGES)
    _run(_out_proj(mixed_b, xa1, ya1, wout_ref))

    @pl.when(p == npairs - 1)
    def _state_out():
        h_ref[slot, 0] = hc_s[...]
        conv_ref[slot, 0] = xtail_s[SUBLANES - (CONV_W - 1):SUBLANES, :]
        for kh in range(SWA_KV_HEADS):
            ks = slice(kh * 128, (kh + 1) * 128)
            kout_ref[slot, 0, pl.ds(kh, WINDOW, stride=SWA_KV_HEADS), :] = kprev_s[:, ks]
            vout_ref[slot, 0, pl.ds(kh, WINDOW, stride=SWA_KV_HEADS), :] = vprev_s[:, ks]
        if fill:
            for other in [o for o in range(DEPTH) if o != slot]:
                for ref in (h_ref, conv_ref, kout_ref, vout_ref):
                    ref[other] = jnp.zeros(ref.shape[1:], F32)


def _const_spec(shape):
    nd = len(shape)
    return pl.BlockSpec(shape, lambda *_: (0,) * nd)


def _resident_spec(shape):
    nd = len(shape)
    return pl.BlockSpec(shape, lambda *_: (0,) * nd, pipeline_mode=pl.Buffered(1))


def _layer_spec(a, layer, resident=False):
    nd = a.ndim
    kw = dict(pipeline_mode=pl.Buffered(1)) if resident else {}
    return pl.BlockSpec((None,) + a.shape[1:], lambda *_: (layer,) + (0,) * (nd - 1), **kw)


def _prompt_layer(layer, x, prm, tables, prev_out, tt=256):
    B, T, D = x.shape
    npairs = T // (2 * tt)
    layer_in = [prm[k] for k in ("ng", "win", "wout", "retg", "convw", "convb", "wr", "wi", "br", "bi",
                                 "lam", "qg", "kg")]
    vec_in = layer_in + list(tables)

    def next_tile(b, p):
        last = p == npairs - 1
        return (jnp.where(last, jnp.minimum(b + 1, B - 1), b), jnp.where(last, 0, 2 * p + 2), 0)

    in_specs = ([pl.BlockSpec(memory_space=pltpu.SMEM),
                 pl.BlockSpec((1, 2 * tt, D), lambda b, p: (b, p, 0)),
                 pl.BlockSpec((1, tt, D), next_tile)]
                + [_layer_spec(a, layer, resident=True) for a in layer_in]
                + [_resident_spec(a.shape) for a in tables]
                + [pl.BlockSpec(memory_space=pl.ANY) for _ in prev_out])
    out_shape = (
        jax.ShapeDtypeStruct((B, T, D), F32),
        jax.ShapeDtypeStruct((DEPTH, B, RET_HEADS, RET_DK, RET_DV), F32),
        jax.ShapeDtypeStruct((DEPTH, B, 1, LRU_WIDTH), F32),
        jax.ShapeDtypeStruct((DEPTH, B, CONV_W - 1, LRU_WIDTH), F32),
        jax.ShapeDtypeStruct((DEPTH, B, 2 * WINDOW, SWA_HEAD_DIM), F32),
        jax.ShapeDtypeStruct((DEPTH, B, 2 * WINDOW, SWA_HEAD_DIM), F32),
    )
    fill = not prev_out
    ld, li, slot = (DEPTH, 0, layer) if fill else (1, layer, 0)
    out_specs = (
        pl.BlockSpec((1, 2 * tt, D), lambda b, p: (b, p, 0)),
        pl.BlockSpec((ld, 1, RET_HEADS, RET_DK, RET_DV), lambda b, p: (li, b, 0, 0, 0)),
        pl.BlockSpec((ld, 1, 1, LRU_WIDTH), lambda b, p: (li, b, 0, 0)),
        pl.BlockSpec((ld, 1, CONV_W - 1, LRU_WIDTH), lambda b, p: (li, b, 0, 0)),
        pl.BlockSpec((ld, 1, 2 * WINDOW, SWA_HEAD_DIM), lambda b, p: (li, b, 0, 0)),
        pl.BlockSpec((ld, 1, 2 * WINDOW, SWA_HEAD_DIM), lambda b, p: (li, b, 0, 0)),
    )
    n_in = 3 + len(vec_in)
    aliases = {n_in + j: 1 + j for j in range(len(prev_out))}
    scratch = [
        pltpu.VMEM((tt, IN_WIDTH), F32),
        pltpu.VMEM((tt, IN_WIDTH), F32),
        pltpu.VMEM((tt, MIX_WIDTH), BF16),
        pltpu.VMEM((tt, MIX_WIDTH), BF16),
        pltpu.VMEM((SUBLANES, LRU_WIDTH), F32),
        pltpu.VMEM((WINDOW, 256), F32),
        pltpu.VMEM((WINDOW, 256), F32),
        pltpu.VMEM((1, LRU_WIDTH), F32),
    ]
    return pl.pallas_call(
        functools.partial(_prompt_kernel, tt=tt, npairs=npairs, layer=layer, slot=slot, fill=fill),
        grid=(B, npairs),
        in_specs=in_specs,
        out_specs=out_specs,
        out_shape=out_shape,
        scratch_shapes=scratch,
        input_output_aliases=aliases,
        compiler_params=pltpu.CompilerParams(
            dimension_semantics=("arbitrary", "arbitrary"), vmem_limit_bytes=VMEM_LIMIT),
        name="prompt_layer",
    )(prm["sinks"], x, x, *vec_in, *prev_out)


def _sproj_kernel(x_ref, ng_ref, win_ref, *rest, layer, cast_weights):
    if cast_weights:
        wout_ref, o_ref, winb_ref, woutb_ref, hb_s = rest
        winb_ref[...] = win_ref[...].astype(BF16)
        woutb_ref[...] = wout_ref[...].astype(BF16)
        win_ref = winb_ref.at[layer]
    else:
        o_ref, hb_s = rest

    @pl.when(pl.program_id(0) == 0)
    def _norm_once():
        hb_s[...] = _rms(x_ref[...], ng_ref[...]).astype(BF16)

    o_ref[...] = _dot(hb_s[...], win_ref[...])


BF16_SUBLANES = 2 * SUBLANES


def _sample_proj(layer, xp, ng, win, wout=None):
    R = xp.shape[0]
    nb = IN_WIDTH // 512
    cast_weights = wout is not None
    in_specs = [_const_spec(xp.shape), _layer_spec(ng, layer)]
    out_specs = [pl.BlockSpec((R, 512), lambda j: (0, j))]
    out_shape = [jax.ShapeDtypeStruct((R, IN_WIDTH), F32)]
    if cast_weights:
        wo_blocks = max(n for n in range(1, nb + 1) if MIX_WIDTH % (n * BF16_SUBLANES) == 0)
        wo_rows = MIX_WIDTH // wo_blocks
        win_spec = pl.BlockSpec((DEPTH, D_MODEL, 512), lambda j: (0, 0, j))
        wout_spec = pl.BlockSpec((DEPTH, wo_rows, D_MODEL), lambda j: (0, jnp.minimum(j, wo_blocks - 1), 0))
        in_specs += [win_spec, wout_spec]
        out_specs += [win_spec, wout_spec]
        out_shape += [jax.ShapeDtypeStruct(win.shape, BF16), jax.ShapeDtypeStruct(wout.shape, BF16)]
    else:
        in_specs += [pl.BlockSpec((None, D_MODEL, 512), lambda j: (layer, 0, j))]
    out = pl.pallas_call(
        functools.partial(_sproj_kernel, layer=layer, cast_weights=cast_weights),
        grid=(nb,),
        in_specs=in_specs,
        out_specs=out_specs,
        out_shape=out_shape,
        scratch_shapes=[pltpu.VMEM((R, D_MODEL), BF16)],
        compiler_params=pltpu.CompilerParams(
            dimension_semantics=("arbitrary",), vmem_limit_bytes=VMEM_LIMIT),
        name="sample_proj",
    )(xp, ng, win, *((wout,) if cast_weights else ()))
    return out if cast_weights else out[0]


def _sample_mix_seq(sinks_ref, pa_ref, pc_ref, s0_ref, kbuf_ref, vbuf_ref, retg_ref, qg_ref, kg_ref,
                    cm_ref, qdec_ref, kdec_ref, mix_ref, snew_ref, knew_ref, vnew_ref):
    P = SEQ_PAD
    for h in range(RET_HEADS):
        cs = slice(h * 128, (h + 1) * 128)
        q = pa_ref[:, QA + h * 128:QA + (h + 1) * 128]
        k = pa_ref[:, KA + h * 128:KA + (h + 1) * 128] * (RET_DK ** -0.5)
        v = pa_ref[:, VA + h * 128:VA + (h + 1) * 128]
        intra = jnp.zeros((P, RET_DV), F32)
        for s in range(4):
            r_ = TOK0 + s
            w = jnp.sum(q * k[r_:r_ + 1, :], axis=-1, keepdims=True)
            intra = intra + (w * cm_ref[h, s]) * v[r_:r_ + 1, :]
        s_prev = s0_ref[h]
        cross = _dot((q * qdec_ref[h]).astype(BF16), s_prev.astype(BF16))
        kv = _dot_tn((k * kdec_ref[h]).astype(BF16), v.astype(BF16))
        yield
        snew_ref[h] = float(np.exp(np.float32(4.0) * _LOG_G[h])) * s_prev + kv
        o = _rms(intra + cross, retg_ref[:, cs])
        mix_ref[:, cs] = o * _silu(pa_ref[:, GA + h * 128:GA + (h + 1) * 128])
        yield

    r16 = lax.broadcasted_iota(jnp.int32, (2 * P, WINDOW), 0)
    j16 = lax.broadcasted_iota(jnp.int32, (2 * P, WINDOW), 1)
    rr = jnp.where(r16 >= P, r16 - P, r16)
    row_ok = jnp.logical_and(rr >= TOK0, rr < TOK0 + 4)
    dist = (rr - TOK0) + WINDOW - j16
    distf = dist.astype(F32)
    valid = jnp.logical_and(jnp.logical_and(dist >= 0, dist < WINDOW), row_ok)
    r16c = lax.broadcasted_iota(jnp.int32, (2 * P, 1), 0)
    rrc = jnp.where(r16c >= P, r16c - P, r16c)
    rowc_ok = jnp.logical_and(rrc >= TOK0, rrc < TOK0 + 4)
    knew_ref[0:2 * (WINDOW - 4), :] = kbuf_ref[2 * 4:2 * WINDOW, :]
    vnew_ref[0:2 * (WINDOW - 4), :] = vbuf_ref[2 * 4:2 * WINDOW, :]
    for kh in range(SWA_KV_HEADS):
        h0, h1 = kh * SWA_GROUP, kh * SWA_GROUP + 1
        kb = kbuf_ref[pl.ds(kh, WINDOW, stride=SWA_KV_HEADS), :]
        vb = vbuf_ref[pl.ds(kh, WINDOW, stride=SWA_KV_HEADS), :]
        kc, vc = KC - QC, VC - QC
        kn = _rms(pc_ref[:, kc + kh * 128:kc + (kh + 1) * 128], kg_ref[...])
        vn = pc_ref[:, vc + kh * 128:vc + (kh + 1) * 128]
        q0 = _rms(pc_ref[:, h0 * 128:(h0 + 1) * 128], qg_ref[...])
        q1 = _rms(pc_ref[:, h1 * 128:(h1 + 1) * 128], qg_ref[...])
        qq = jnp.concatenate([q0, q1], axis=0)
        slope = jnp.where(r16 >= P, _SLOPES[h1], _SLOPES[h0])
        slopec = jnp.where(r16c >= P, _SLOPES[h1], _SLOPES[h0])
        sb = _dot_nt(qq.astype(BF16), kb.astype(BF16)) * (SWA_HEAD_DIM ** -0.5)
        sb = jnp.where(valid, sb - slope * distf, NEG_INF)
        yield
        sink = jnp.where(r16c >= P, sinks_ref[h1], sinks_ref[h0])
        m = jnp.maximum(jnp.max(sb, axis=-1, keepdims=True), sink)
        wn = []
        for s in range(4):
            r_ = TOK0 + s
            w = jnp.sum(qq * kn[r_:r_ + 1, :], axis=-1, keepdims=True) * (SWA_HEAD_DIM ** -0.5)
            dn = rrc - r_
            w = jnp.where(jnp.logical_and(dn >= 0, rowc_ok), w - slopec * dn.astype(F32), NEG_INF)
            wn.append(w)
            m = jnp.maximum(m, w)
        pb = jnp.exp(sb - m)
        denom = jnp.sum(pb, axis=-1, keepdims=True) + jnp.exp(sink - m)
        pn = [jnp.exp(w - m) for w in wn]
        for p_ in pn:
            denom = denom + p_
        yield
        o = _dot((pb / denom).astype(BF16), vb.astype(BF16))
        for s in range(4):
            r_ = TOK0 + s
            o = o + (pn[s] / denom) * vn[r_:r_ + 1, :]
        for g, hh in enumerate((h0, h1)):
            gc = pc_ref[:, GC - QC + hh * 128:GC - QC + (hh + 1) * 128]
            mix_ref[:, GROUP_WIDTH + hh * 128:GROUP_WIDTH + (hh + 1) * 128] = o[g * P:(g + 1) * P] * _silu(gc)
        for s in range(4):
            r_out = 2 * (WINDOW - 4 + s) + kh
            knew_ref[r_out:r_out + 1, :] = kn[TOK0 + s:TOK0 + s + 1, :]
            vnew_ref[r_out:r_out + 1, :] = vn[TOK0 + s:TOK0 + s + 1, :]
        yield


def _sample_mix_kernel(sinks_ref, pa_ref, pc_ref, s0_ref, kbuf_ref, vbuf_ref, retg_ref, qg_ref, kg_ref,
                       cm_ref, qdec_ref, kdec_ref, *rest, nseq, layer, slot, fill):
    mix_ref, snew_ref, knew_ref, vnew_ref = rest[-4:]
    sinks_ref = sinks_ref.at[layer]
    if fill:
        for other in [o for o in range(DEPTH) if o != slot]:
            for ref in (snew_ref, knew_ref, vnew_ref):
                ref[other] = jnp.zeros(ref.shape[1:], F32)
    gens = []
    for i in range(nseq):
        rows = slice(i * SEQ_PAD, (i + 1) * SEQ_PAD)
        gens.append(_sample_mix_seq(
            sinks_ref, pa_ref.at[rows, :], pc_ref.at[rows, :], s0_ref.at[0, i], kbuf_ref.at[0, i], vbuf_ref.at[0, i],
            retg_ref, qg_ref, kg_ref, cm_ref, qdec_ref, kdec_ref,
            mix_ref.at[rows, :], snew_ref.at[slot, i], knew_ref.at[slot, i], vnew_ref.at[slot, i]))
    _run(_zip_stages(gens))


def _sample_mix(layer, proj, state_ret, cache_k, cache_v, prm, tables, prev_out, nseq=8):
    depth, B = state_ret.shape[:2]
    layer_in = [prm[k] for k in ("retg", "qg", "kg")]
    consts = layer_in + list(tables)
    st_spec = pl.BlockSpec((1, nseq, RET_HEADS, RET_DK, RET_DV), lambda i: (layer, i, 0, 0, 0))
    kv_spec = pl.BlockSpec((1, nseq, 2 * WINDOW, SWA_HEAD_DIM), lambda i: (layer, i, 0, 0))
    in_specs = ([pl.BlockSpec(memory_space=pltpu.SMEM),
                 pl.BlockSpec((nseq * SEQ_PAD, XB), lambda i: (i, 0)),
                 pl.BlockSpec((nseq * SEQ_PAD, IN_WIDTH - QC), lambda i: (i, QC // (IN_WIDTH - QC))),
                 st_spec, kv_spec, kv_spec]
                + [_layer_spec(a, layer) for a in layer_in]
                + [_const_spec(a.shape) for a in tables]
                + [pl.BlockSpec(memory_space=pl.ANY) for _ in prev_out])
    out_shape = (
        jax.ShapeDtypeStruct((B * SEQ_PAD, 2 * GROUP_WIDTH), F32),
        jax.ShapeDtypeStruct(state_ret.shape, F32),
        jax.ShapeDtypeStruct(cache_k.shape, F32),
        jax.ShapeDtypeStruct(cache_v.shape, F32),
    )
    fill = not prev_out
    ld, li, slot = (DEPTH, 0, layer) if fill else (1, layer, 0)
    out_specs = (pl.BlockSpec((nseq * SEQ_PAD, 2 * GROUP_WIDTH), lambda i: (i, 0)),
                 pl.BlockSpec((ld, nseq, RET_HEADS, RET_DK, RET_DV), lambda i: (li, i, 0, 0, 0)),
                 pl.BlockSpec((ld, nseq, 2 * WINDOW, SWA_HEAD_DIM), lambda i: (li, i, 0, 0)),
                 pl.BlockSpec((ld, nseq, 2 * WINDOW, SWA_HEAD_DIM), lambda i: (li, i, 0, 0)))
    assert QC % (IN_WIDTH - QC) == 0
    n_in = 6 + len(consts)
    aliases = {n_in + j: 1 + j for j in range(len(prev_out))}
    return pl.pallas_call(
        functools.partial(_sample_mix_kernel, nseq=nseq, layer=layer, slot=slot, fill=fill),
        grid=(B // nseq,),
        in_specs=in_specs,
        out_specs=out_specs,
        out_shape=out_shape,
        input_output_aliases=aliases,
        compiler_params=pltpu.CompilerParams(
            dimension_semantics=("arbitrary",), vmem_limit_bytes=VMEM_LIMIT),
        name="sample_mix",
    )(prm["sinks"], proj, proj, state_ret, cache_k, cache_v, *consts, *prev_out)


def _sample_out_kernel(x_ref, xb_ref, gb_ref, mix_ref, conv8_ref, h8_ref, convw_ref, convb_ref,
                       wr_ref, wi_ref, br_ref, bi_ref, lam_ref, wout_ref,
                       y_ref, convo_ref, ho_ref):
    R = x_ref.shape[0]
    row = lax.broadcasted_iota(jnp.int32, (R, LRU_WIDTH), 0) & (SEQ_PAD - 1)
    xc = jnp.where(row < TOK0, conv8_ref[...], xb_ref[...])
    convo_ref[...] = xc
    conv = convb_ref[...] + convw_ref[0:1, :] * pltpu.roll(xc, 3, axis=0)
    conv = conv + convw_ref[1:2, :] * pltpu.roll(xc, 2, axis=0)
    conv = conv + convw_ref[2:3, :] * pltpu.roll(xc, 1, axis=0)
    conv = conv + convw_ref[3:4, :] * xc
    a, mult, ig = _lru_gates(conv, wr_ref, wi_ref, br_ref, bi_ref, lam_ref)
    u = mult * ig * conv
    h = h8_ref[...]
    for s in range(4):
        h = jnp.where(row == TOK0 + s, a * pltpu.roll(h, 1, axis=0) + u, h)
    ho_ref[...] = h
    ob = h * _silu(gb_ref[...])
    y = x_ref[...] + _dot(mix_ref[:, 0:GROUP_WIDTH].astype(BF16), wout_ref[0:GROUP_WIDTH, :])
    y = y + _dot(ob.astype(BF16), wout_ref[GROUP_WIDTH:2 * GROUP_WIDTH, :])
    y = y + _dot(mix_ref[:, GROUP_WIDTH:2 * GROUP_WIDTH].astype(BF16), wout_ref[2 * GROUP_WIDTH:, :])
    y_ref[...] = y


def _sample_out(layer, xp, proj, mix, conv8, h8, prm, rows=256):
    R = xp.shape[0]
    consts = [prm[k] for k in ("convw", "convb", "wr", "wi", "br", "bi", "lam", "wout")]
    in_specs = ([pl.BlockSpec((rows, D_MODEL), lambda i: (i, 0)),
                 pl.BlockSpec((rows, LRU_WIDTH), lambda i: (i, XB // LRU_WIDTH)),
                 pl.BlockSpec((rows, LRU_WIDTH), lambda i: (i, GB // LRU_WIDTH)),
                 pl.BlockSpec((rows, 2 * GROUP_WIDTH), lambda i: (i, 0)),
                 pl.BlockSpec((rows, LRU_WIDTH), lambda i: (i, 0)),
                 pl.BlockSpec((rows, LRU_WIDTH), lambda i: (i, 0))]
                + [_layer_spec(a, layer) for a in consts])
    out_shape = (
        jax.ShapeDtypeStruct((R, D_MODEL), F32),
        jax.ShapeDtypeStruct((R, LRU_WIDTH), F32),
        jax.ShapeDtypeStruct((R, LRU_WIDTH), F32),
    )
    out_specs = (
        pl.BlockSpec((rows, D_MODEL), lambda i: (i, 0)),
        pl.BlockSpec((rows, LRU_WIDTH), lambda i: (i, 0)),
        pl.BlockSpec((rows, LRU_WIDTH), lambda i: (i, 0)),
    )
    return pl.pallas_call(
        _sample_out_kernel,
        grid=(R // rows,),
        in_specs=in_specs,
        out_specs=out_specs,
        out_shape=out_shape,
        compiler_params=pltpu.CompilerParams(
            dimension_semantics=("arbitrary",), vmem_limit_bytes=VMEM_LIMIT),
        name="sample_out",
    )(xp, proj, proj, mix, conv8, h8, *consts)


def _prompt_tables():
    C = RET_CHUNK
    idx = np.arange(C, dtype=np.float32)
    diff = idx[:, None] - idx[None, :]
    causal = diff >= 0
    lg = _LOG_G[:, None, None]
    dmask = np.where(causal[None], np.exp(np.where(causal, diff, 0.0)[None] * lg), 0.0).astype(np.float32)
    qdec = np.exp((idx + 1.0)[None, :] * _LOG_G[:, None]).astype(np.float32)
    kdec = np.exp((C - 1 - idx)[None, :] * _LOG_G[:, None]).astype(np.float32)
    qdec = np.broadcast_to(qdec[:, :, None], (RET_HEADS, C, RET_DK)).copy()
    kdec = np.broadcast_to(kdec[:, :, None], (RET_HEADS, C, RET_DK)).copy()
    return jnp.asarray(dmask), jnp.asarray(qdec), jnp.asarray(kdec)


def _sample_tables():
    P = SEQ_PAD
    rows = np.arange(P, dtype=np.float32)
    i = rows - TOK0
    tok = (i >= 0) & (i < 4)
    cm = np.zeros((RET_HEADS, 4, P, RET_DV), np.float32)
    qdec = np.zeros((RET_HEADS, P, RET_DK), np.float32)
    kdec = np.zeros((RET_HEADS, P, RET_DK), np.float32)
    for h in range(RET_HEADS):
        for s in range(4):
            d = i - s
            col = np.where(tok & (d >= 0), np.exp(np.where(d >= 0, d, 0.0) * _LOG_G[h]), 0.0)
            cm[h, s] = col[:, None]
        qdec[h] = np.where(tok, np.exp((i + 1.0) * _LOG_G[h]), 0.0)[:, None]
        kdec[h] = np.where(tok, np.exp((3.0 - i) * _LOG_G[h]), 0.0)[:, None]
    return jnp.asarray(cm), jnp.asarray(qdec), jnp.asarray(kdec)


def _block_diag(w):
    L, n, d, _ = w.shape
    g = n // 2
    w = w.reshape(L, 2, g, d, d)
    eye = jnp.eye(g, dtype=w.dtype)
    return (eye[None, None, :, None, :, None] * w[:, :, :, :, None, :]).reshape(L, 2, g * d, g * d)


def kernel(x_prompt, x_sample, state_ret, state_lru, state_conv, cache_swa_k, cache_swa_v, norm_g, w_in, w_out, ret_norm_g, conv_w, conv_b, w_rgate, b_rgate, w_igate, b_igate, lru_lambda, q_norm_g, k_norm_g, attn_sinks):
    Bs, Ts, _ = x_sample.shape
    ptab = _prompt_tables()
    stab = _sample_tables()
    yp = x_prompt
    ys = jnp.pad(x_sample, ((0, 0), (TOK0, SEQ_PAD - TOK0 - Ts), (0, 0))).reshape(Bs * SEQ_PAD, D_MODEL)
    cache_k = cache_swa_k.reshape(DEPTH, Bs, 2 * WINDOW, SWA_HEAD_DIM)
    cache_v = cache_swa_v.reshape(DEPTH, Bs, 2 * WINDOW, SWA_HEAD_DIM)
    Bp = x_prompt.shape[0]
    p_state, s_state = (), ()
    s_h, s_conv = [], []
    row = lambda a: a.reshape(DEPTH, 1, -1)
    prm = dict(
        sinks=attn_sinks, ng=row(norm_g), retg=row(ret_norm_g),
        convw=conv_w, convb=row(conv_b), wr=_block_diag(w_rgate).astype(BF16), wi=_block_diag(w_igate).astype(BF16),
        br=row(b_rgate), bi=row(b_igate), lam=row(lru_lambda), qg=row(q_norm_g), kg=row(k_norm_g))
    proj, prm["win"], prm["wout"] = _sample_proj(0, ys, prm["ng"], w_in, w_out)
    for l in range(DEPTH):
        yp, *p_state = _prompt_layer(l, yp, prm, ptab, tuple(p_state))
        if l > 0:
            proj = _sample_proj(l, ys, prm["ng"], prm["win"])
        mix, *s_state = _sample_mix(l, proj, state_ret, cache_k, cache_v, prm, stab, tuple(s_state))
        conv8 = jnp.pad(state_conv[l], ((0, 0), (0, SEQ_PAD - (CONV_W - 1)), (0, 0))).reshape(Bs * SEQ_PAD, LRU_WIDTH)
        h8 = jnp.pad(state_lru[l][:, None, :], ((0, 0), (TOK0 - 1, SEQ_PAD - TOK0), (0, 0))).reshape(Bs * SEQ_PAD, LRU_WIDTH)
        ys, convo, ho = _sample_out(l, ys, proj, mix, conv8, h8, prm)
        s_h.append(ho.reshape(Bs, SEQ_PAD, LRU_WIDTH)[:, TOK0 + Ts - 1])
        s_conv.append(convo.reshape(Bs, SEQ_PAD, LRU_WIDTH)[:, TOK0 + Ts - (CONV_W - 1):TOK0 + Ts])

    p_ret, p_h, p_conv, p_k, p_v = p_state
    s_ret, s_k, s_v = s_state
    kv5 = lambda a, n: a.reshape(DEPTH, n, WINDOW, SWA_KV_HEADS, SWA_HEAD_DIM)
    y_sample = ys.reshape(Bs, SEQ_PAD, D_MODEL)[:, TOK0:TOK0 + Ts]
    return (yp, y_sample,
            p_ret, p_h.reshape(DEPTH, Bp, LRU_WIDTH), p_conv, kv5(p_k, Bp), kv5(p_v, Bp),
            s_ret, jnp.stack(s_h), jnp.stack(s_conv), kv5(s_k, Bs), kv5(s_v, Bs))
```

```python
import functools
import math

import numpy as np
import jax
import jax.numpy as jnp
from jax import lax
from jax.experimental import pallas as pl
from jax.experimental.pallas import tpu as pltpu

D_MODEL = 1024
DEPTH = 2
PAST_LEN = 16384
GROUP_WIDTH = 512
RET_HEADS = 4
RET_DK = 128
RET_DV = 128
RET_CHUNK = 128
LRU_WIDTH = 512
LRU_BLOCKS = 8
LRU_C = 8.0
CONV_W = 4
SWA_HEADS = 4
SWA_KV_HEADS = 2
SWA_GROUP = 2
SWA_HEAD_DIM = 128
WINDOW = 128
NORM_EPS = 1e-6
NEG_INF = -1e30

IN_WIDTH = 4608
MIX_WIDTH = 1536
QA, KA, VA, GA, XB, GB, QC, KC, VC, GC = 0, 512, 1024, 1536, 2048, 2560, 3072, 3584, 3840, 4096

F32 = jnp.float32
BF16 = jnp.bfloat16

SUBLANES = 8
SEQ_PAD = 8
TOK0 = 3
VMEM_LIMIT = 56 * 1024 * 1024

_LOG_G = np.log1p(-np.power(np.float32(2.0), (-5.0 - np.arange(RET_HEADS)).astype(np.float32))).astype(np.float32)
_SLOPES = [2.0 ** (-8.0 * (h + 1) / SWA_HEADS) for h in range(SWA_HEADS)]


def _rms(x, g):
    ms = jnp.mean(x * x, axis=-1, keepdims=True)
    return x * lax.rsqrt(ms + NORM_EPS) * g


def _silu(x):
    return x * jax.nn.sigmoid(x)


def _softplus(x):
    return jnp.maximum(x, 0.0) + jnp.log1p(jnp.exp(-jnp.abs(x)))


def _dot(a, b):
    return jnp.dot(a, b, preferred_element_type=F32)


def _dot_nt(a, b):
    return lax.dot_general(a, b, (((1,), (1,)), ((), ())), preferred_element_type=F32)


def _dot_tn(a, b):
    return lax.dot_general(a, b, (((0,), (0,)), ((), ())), preferred_element_type=F32)


LRU_HALF = LRU_WIDTH // 2


def _lru_gates(conv, half, wr_ref, wi_ref, br_ref, bi_ref, lam_ref):
    cs = slice(half * LRU_HALF, (half + 1) * LRU_HALF)
    gin = conv.astype(BF16)
    r = jax.nn.sigmoid(_dot(gin, wr_ref[half]) + br_ref[:, cs])
    i = jax.nn.sigmoid(_dot(gin, wi_ref[half]) + bi_ref[:, cs])
    log_a = -LRU_C * r * _softplus(-lam_ref[:, cs])
    a = jnp.exp(log_a)
    m2 = -jnp.tanh(log_a) * (1.0 + a * a)
    mult = jnp.where(m2 > 0.0, m2 * lax.rsqrt(m2), 0.0)
    return a, mult, i


_DONE = object()


def _run(gen):
    for _ in gen:
        pass


def _zip_stages(gens):
    gens = list(gens)
    while gens:
        gens = [g for g in gens if next(g, _DONE) is not _DONE]
        if gens:
            yield


def _chain(gens):
    for g in gens:
        yield from g


def _interleave(main, sides, n_main, n_side):
    side = _chain(sides)
    next(side)
    done_side = 1
    for i, _ in enumerate(main):
        want = ((i + 1) * n_side) // n_main
        while done_side < want and next(side, _DONE) is not _DONE:
            done_side += 1
    _run(side)


IN_PROJ_COLS = 256
IN_PROJ_STAGES = 1 + IN_WIDTH // IN_PROJ_COLS


def _in_proj(x_ref, ng_ref, win_ref, proj_ref):
    hb = _rms(x_ref[...], ng_ref[...]).astype(BF16)
    yield
    for j in range(IN_WIDTH // IN_PROJ_COLS):
        cols = slice(j * IN_PROJ_COLS, (j + 1) * IN_PROJ_COLS)
        proj_ref[:, cols] = _dot(hb, win_ref[:, cols])
        yield


OUT_PROJ_STAGES = D_MODEL // 256


def _out_proj(mixed_s, x_ref, y_ref, wout_ref):
    mixed = mixed_s[...]
    for j in range(OUT_PROJ_STAGES):
        cols = slice(j * 256, (j + 1) * 256)
        y_ref[:, cols] = x_ref[:, cols] + _dot(mixed, wout_ref[:, cols])
        yield


def _mix_stages(tt):
    return 2 * (4 + tt // SUBLANES // 8) + (tt // RET_CHUNK) * 5


def _scan_rows(a, u, h_in, tt):
    G = tt // SUBLANES
    W = a.shape[-1]
    a3 = a.reshape(G, SUBLANES, W)
    u3 = u.reshape(G, SUBLANES, W)
    r3 = lax.broadcasted_iota(jnp.int32, (G, SUBLANES, W), 1)
    sh = 1
    while sh < SUBLANES:
        keep = r3 >= sh
        a_sh = jnp.where(keep, pltpu.roll(a3, sh, axis=1), 1.0)
        u_sh = jnp.where(keep, pltpu.roll(u3, sh, axis=1), 0.0)
        u3 = a3 * u_sh + u3
        a3 = a3 * a_sh
        sh *= 2
    yield
    hs = []
    for g in range(G):
        hg = a3[g] * h_in + u3[g]
        hs.append(hg)
        h_in = hg[SUBLANES - 1:SUBLANES, :]
        if g % 8 == 7:
            yield
    return jnp.concatenate(hs, axis=0)


def _retention_head(h, proj_s, r0, s_ref, retg_ref, dmask_ref, qdec_ref, kdec_ref, mixed_s):
    C = RET_CHUNK
    cs = slice(h * RET_DK, (h + 1) * RET_DK)
    q = proj_s[r0:r0 + C, QA + h * 128:QA + (h + 1) * 128]
    k = proj_s[r0:r0 + C, KA + h * 128:KA + (h + 1) * 128] * (RET_DK ** -0.5)
    vb = proj_s[r0:r0 + C, VA + h * 128:VA + (h + 1) * 128].astype(BF16)
    sc = _dot_nt(q.astype(BF16), k.astype(BF16)) * dmask_ref[h]
    yield
    s_prev = s_ref[0, h]
    lhs = jnp.concatenate([sc.astype(BF16), (q * qdec_ref[h]).astype(BF16)], axis=1)
    rhs = jnp.concatenate([vb, s_prev.astype(BF16)], axis=0)
    o = _dot(lhs, rhs)
    yield
    kv = _dot_tn((k * kdec_ref[h]).astype(BF16), vb)
    s_ref[0, h] = float(np.exp(np.float32(C) * _LOG_G[h])) * s_prev + kv
    yield
    o = _rms(o, retg_ref[:, cs])
    mixed_s[r0:r0 + C, cs] = (o * _silu(proj_s[r0:r0 + C, GA + h * 128:GA + (h + 1) * 128])).astype(BF16)
    yield


def _swa_kv_head(kh, proj_s, r0, blk, sinks_ref, qg_ref, kg_ref, mixed_s, kprev_s, vprev_s):
    C = RET_CHUNK
    ks = slice(kh * 128, (kh + 1) * 128)
    h0, h1 = kh * SWA_GROUP, kh * SWA_GROUP + 1
    kn = _rms(proj_s[r0:r0 + C, KC + kh * 128:KC + (kh + 1) * 128], kg_ref[...])
    vv = proj_s[r0:r0 + C, VC + kh * 128:VC + (kh + 1) * 128]
    kband = jnp.concatenate([kprev_s[:, ks], kn], axis=0).astype(BF16)
    vband = jnp.concatenate([vprev_s[:, ks], vv], axis=0).astype(BF16)
    kprev_s[:, ks] = kn
    vprev_s[:, ks] = vv
    q0 = _rms(proj_s[r0:r0 + C, QC + h0 * 128:QC + (h0 + 1) * 128], qg_ref[...])
    q1 = _rms(proj_s[r0:r0 + C, QC + h1 * 128:QC + (h1 + 1) * 128], qg_ref[...])
    qq = jnp.concatenate([q0, q1], axis=0).astype(BF16)
    yield
    s = _dot_nt(qq, kband) * (SWA_HEAD_DIM ** -0.5)
    ii = lax.broadcasted_iota(jnp.int32, (2 * C, 2 * C), 0)
    jj = lax.broadcasted_iota(jnp.int32, (2 * C, 2 * C), 1)
    dist = jnp.where(ii >= C, ii - C, ii) + C - jj
    valid = jnp.logical_and(jnp.logical_and(dist >= 0, dist < WINDOW), jj >= jnp.where(blk > 0, 0, C))
    slope = jnp.where(ii >= C, _SLOPES[h1], _SLOPES[h0])
    s = jnp.where(valid, s - slope * dist.astype(F32), NEG_INF)
    yield
    rowc = lax.broadcasted_iota(jnp.int32, (2 * C, 1), 0)
    sink = jnp.where(rowc >= C, sinks_ref[h1], sinks_ref[h0])
    m = jnp.maximum(jnp.max(s, axis=-1, keepdims=True), sink)
    p = jnp.exp(s - m)
    denom = jnp.sum(p, axis=-1, keepdims=True) + jnp.exp(sink - m)
    yield
    o = _dot((p / denom).astype(BF16), vband)
    yield
    for g, hh in enumerate((h0, h1)):
        gc = proj_s[r0:r0 + C, GC + hh * 128:GC + (hh + 1) * 128]
        mixed_s[r0:r0 + C, 2 * GROUP_WIDTH + hh * 128:2 * GROUP_WIDTH + (hh + 1) * 128] = (
            (o[g * C:(g + 1) * C] * _silu(gc)).astype(BF16))
    yield


def _mix_tile(proj_s, t, tt, sinks_ref, retg_ref, convw_ref, convb_ref,
              wr_ref, wi_ref, br_ref, bi_ref, lam_ref, qg_ref, kg_ref, dmask_ref, qdec_ref, kdec_ref,
              s_ref, mixed_s, xtail_s, kprev_s, vprev_s, hc_s):
    nch = tt // RET_CHUNK
    C = RET_CHUNK

    for half in range(2):
        lo = half * LRU_HALF
        cs = slice(lo, lo + LRU_HALF)
        xb = proj_s[:, XB + lo:XB + lo + LRU_HALF]
        xfull = jnp.concatenate([xtail_s[:, cs], xb], axis=0)
        xtail_s[:, cs] = xb[tt - SUBLANES:tt, :]
        conv = convb_ref[:, cs] + convw_ref[0:1, cs] * pltpu.roll(xfull, 3, axis=0)[SUBLANES:]
        conv = conv + convw_ref[1:2, cs] * pltpu.roll(xfull, 2, axis=0)[SUBLANES:]
        conv = conv + convw_ref[2:3, cs] * pltpu.roll(xfull, 1, axis=0)[SUBLANES:]
        conv = conv + convw_ref[3:4, cs] * xb
        yield

        a, mult, ig = _lru_gates(conv, half, wr_ref, wi_ref, br_ref, bi_ref, lam_ref)
        yield
        row = lax.broadcasted_iota(jnp.int32, (tt, LRU_HALF), 0)
        mult = jnp.where(row == jnp.where(t == 0, 0, -1), 1.0, mult)
        u = mult * ig * conv
        hseq = yield from _scan_rows(a, u, hc_s[:, cs], tt)
        hc_s[:, cs] = hseq[tt - 1:tt, :]
        gate = _silu(proj_s[:, GB + lo:GB + lo + LRU_HALF])
        mixed_s[:, GROUP_WIDTH + lo:GROUP_WIDTH + lo + LRU_HALF] = (hseq * gate).astype(BF16)
        yield

    for c in range(nch):
        r0 = c * C
        blk = t * nch + c
        yield from _zip_stages(
            [_retention_head(h, proj_s, r0, s_ref, retg_ref, dmask_ref, qdec_ref, kdec_ref, mixed_s)
             for h in range(RET_HEADS)]
            + [_swa_kv_head(kh, proj_s, r0, blk, sinks_ref, qg_ref, kg_ref, mixed_s, kprev_s, vprev_s)
               for kh in range(SWA_KV_HEADS)])


def _prompt_kernel(sinks_ref, xa_ref, xn_ref, ng_ref, win_ref, wout_ref, retg_ref, convw_ref, convb_ref,
                   wr_ref, wi_ref, br_ref, bi_ref, lam_ref, qg_ref, kg_ref,
                   dmask_ref, qdec_ref, kdec_ref, *rest, tt, npairs, layer, slot, fill):
    y_ref, s_ref, h_ref, conv_ref, kout_ref, vout_ref = rest[-14:-8]
    proj_a, proj_b, mixed_a, mixed_b, xtail_s, kprev_s, vprev_s, hc_s = rest[-8:]
    sinks_ref = sinks_ref.at[layer]
    b = pl.program_id(0)
    p = pl.program_id(1)

    @pl.when(p == 0)
    def _init():
        s_ref[...] = jnp.zeros_like(s_ref)
        xtail_s[...] = jnp.zeros_like(xtail_s)
        kprev_s[...] = jnp.zeros_like(kprev_s)
        vprev_s[...] = jnp.zeros_like(vprev_s)
        hc_s[...] = jnp.zeros_like(hc_s)

    xa0, xa1 = xa_ref.at[0, 0:tt, :], xa_ref.at[0, tt:2 * tt, :]
    ya0, ya1 = y_ref.at[0, 0:tt, :], y_ref.at[0, tt:2 * tt, :]

    @pl.when(jnp.logical_and(b == 0, p == 0))
    def _prologue():
        _run(_in_proj(xa0, ng_ref, win_ref, proj_a))

    common = (sinks_ref, retg_ref, convw_ref, convb_ref, wr_ref, wi_ref, br_ref, bi_ref,
              lam_ref, qg_ref, kg_ref, dmask_ref, qdec_ref, kdec_ref, s_ref.at[slot])
    state = (xtail_s, kprev_s, vprev_s, hc_s)

    n_main = _mix_stages(tt)
    _interleave(_mix_tile(proj_a, 2 * p, tt, *common, mixed_a, *state),
                [_in_proj(xa1, ng_ref, win_ref, proj_b)], n_main, IN_PROJ_STAGES)
    _interleave(_mix_tile(proj_b, 2 * p + 1, tt, *common, mixed_b, *state),
                [_out_proj(mixed_a, xa0, ya0, wout_ref), _in_proj(xn_ref.at[0], ng_ref, win_ref, proj_a)],
                n_main, OUT_PROJ_STAGES + IN_PROJ_STAGES)
    _run(_out_proj(mixed_b, xa1, ya1, wout_ref))

    @pl.when(p == npairs - 1)
    def _state_out():
        h_ref[slot, 0] = hc_s[...]
        conv_ref[slot, 0] = xtail_s[SUBLANES - (CONV_W - 1):SUBLANES, :]
        for kh in range(SWA_KV_HEADS):
            ks = slice(kh * 128, (kh + 1) * 128)
            kout_ref[slot, 0, pl.ds(kh, WINDOW, stride=SWA_KV_HEADS), :] = kprev_s[:, ks]
            vout_ref[slot, 0, pl.ds(kh, WINDOW, stride=SWA_KV_HEADS), :] = vprev_s[:, ks]
        if fill:
            for other in [o for o in range(DEPTH) if o != slot]:
                for ref in (h_ref, conv_ref, kout_ref, vout_ref):
                    ref[other] = jnp.zeros(ref.shape[1:], F32)


def _const_spec(shape):
    nd = len(shape)
    return pl.BlockSpec(shape, lambda *_: (0,) * nd)


def _resident_spec(shape):
    nd = len(shape)
    return pl.BlockSpec(shape, lambda *_: (0,) * nd, pipeline_mode=pl.Buffered(1))


def _layer_spec(a, layer, resident=False):
    nd = a.ndim
    kw = dict(pipeline_mode=pl.Buffered(1)) if resident else {}
    return pl.BlockSpec((None,) + a.shape[1:], lambda *_: (layer,) + (0,) * (nd - 1), **kw)


def _prompt_layer(layer, x, prm, tables, prev_out, tt=256):
    B, T, D = x.shape
    npairs = T // (2 * tt)
    layer_in = [prm[k] for k in ("ng", "win", "wout", "retg", "convw", "convb", "wr", "wi", "br", "bi",
                                 "lam", "qg", "kg")]
    vec_in = layer_in + list(tables)

    def next_tile(b, p):
        last = p == npairs - 1
        return (jnp.where(last, jnp.minimum(b + 1, B - 1), b), jnp.where(last, 0, 2 * p + 2), 0)

    in_specs = ([pl.BlockSpec(memory_space=pltpu.SMEM),
                 pl.BlockSpec((1, 2 * tt, D), lambda b, p: (b, p, 0)),
                 pl.BlockSpec((1, tt, D), next_tile)]
                + [_layer_spec(a, layer, resident=True) for a in layer_in]
                + [_resident_spec(a.shape) for a in tables]
                + [pl.BlockSpec(memory_space=pl.ANY) for _ in prev_out])
    out_shape = (
        jax.ShapeDtypeStruct((B, T, D), F32),
        jax.ShapeDtypeStruct((DEPTH, B, RET_HEADS, RET_DK, RET_DV), F32),
        jax.ShapeDtypeStruct((DEPTH, B, 1, LRU_WIDTH), F32),
        jax.ShapeDtypeStruct((DEPTH, B, CONV_W - 1, LRU_WIDTH), F32),
        jax.ShapeDtypeStruct((DEPTH, B, 2 * WINDOW, SWA_HEAD_DIM), F32),
        jax.ShapeDtypeStruct((DEPTH, B, 2 * WINDOW, SWA_HEAD_DIM), F32),
    )
    fill = not prev_out
    ld, li, slot = (DEPTH, 0, layer) if fill else (1, layer, 0)
    out_specs = (
        pl.BlockSpec((1, 2 * tt, D), lambda b, p: (b, p, 0)),
        pl.BlockSpec((ld, 1, RET_HEADS, RET_DK, RET_DV), lambda b, p: (li, b, 0, 0, 0)),
        pl.BlockSpec((ld, 1, 1, LRU_WIDTH), lambda b, p: (li, b, 0, 0)),
        pl.BlockSpec((ld, 1, CONV_W - 1, LRU_WIDTH), lambda b, p: (li, b, 0, 0)),
        pl.BlockSpec((ld, 1, 2 * WINDOW, SWA_HEAD_DIM), lambda b, p: (li, b, 0, 0)),
        pl.BlockSpec((ld, 1, 2 * WINDOW, SWA_HEAD_DIM), lambda b, p: (li, b, 0, 0)),
    )
    n_in = 3 + len(vec_in)
    aliases = {n_in + j: 1 + j for j in range(len(prev_out))}
    scratch = [
        pltpu.VMEM((tt, IN_WIDTH), F32),
        pltpu.VMEM((tt, IN_WIDTH), F32),
        pltpu.VMEM((tt, MIX_WIDTH), BF16),
        pltpu.VMEM((tt, MIX_WIDTH), BF16),
        pltpu.VMEM((SUBLANES, LRU_WIDTH), F32),
        pltpu.VMEM((WINDOW, 256), F32),
        pltpu.VMEM((WINDOW, 256), F32),
        pltpu.VMEM((1, LRU_WIDTH), F32),
    ]
    return pl.pallas_call(
        functools.partial(_prompt_kernel, tt=tt, npairs=npairs, layer=layer, slot=slot, fill=fill),
        grid=(B, npairs),
        in_specs=in_specs,
        out_specs=out_specs,
        out_shape=out_shape,
        scratch_shapes=scratch,
        input_output_aliases=aliases,
        compiler_params=pltpu.CompilerParams(
            dimension_semantics=("arbitrary", "arbitrary"), vmem_limit_bytes=VMEM_LIMIT),
        name="prompt_layer",
    )(prm["sinks"], x, x, *vec_in, *prev_out)


def _sproj_kernel(x_ref, ng_ref, win_ref, *rest, layer, cast_weights):
    if cast_weights:
        wout_ref, o_ref, winb_ref, woutb_ref, hb_s = rest
        winb_ref[...] = win_ref[...].astype(BF16)
        woutb_ref[...] = wout_ref[...].astype(BF16)
        win_ref = winb_ref.at[layer]
    else:
        o_ref, hb_s = rest

    @pl.when(pl.program_id(0) == 0)
    def _norm_once():
        hb_s[...] = _rms(x_ref[...], ng_ref[...]).astype(BF16)

    o_ref[...] = _dot(hb_s[...], win_ref[...])


BF16_SUBLANES = 2 * SUBLANES


def _sample_proj(layer, xp, ng, win, wout=None):
    R = xp.shape[0]
    nb = IN_WIDTH // 512
    cast_weights = wout is not None
    in_specs = [_const_spec(xp.shape), _layer_spec(ng, layer)]
    out_specs = [pl.BlockSpec((R, 512), lambda j: (0, j))]
    out_shape = [jax.ShapeDtypeStruct((R, IN_WIDTH), F32)]
    if cast_weights:
        wo_blocks = max(n for n in range(1, nb + 1) if MIX_WIDTH % (n * BF16_SUBLANES) == 0)
        wo_rows = MIX_WIDTH // wo_blocks
        win_spec = pl.BlockSpec((DEPTH, D_MODEL, 512), lambda j: (0, 0, j))
        wout_spec = pl.BlockSpec((DEPTH, wo_rows, D_MODEL), lambda j: (0, jnp.minimum(j, wo_blocks - 1), 0))
        in_specs += [win_spec, wout_spec]
        out_specs += [win_spec, wout_spec]
        out_shape += [jax.ShapeDtypeStruct(win.shape, BF16), jax.ShapeDtypeStruct(wout.shape, BF16)]
    else:
        in_specs += [pl.BlockSpec((None, D_MODEL, 512), lambda j: (layer, 0, j))]
    out = pl.pallas_call(
        functools.partial(_sproj_kernel, layer=layer, cast_weights=cast_weights),
        grid=(nb,),
        in_specs=in_specs,
        out_specs=out_specs,
        out_shape=out_shape,
        scratch_shapes=[pltpu.VMEM((R, D_MODEL), BF16)],
        compiler_params=pltpu.CompilerParams(
            dimension_semantics=("arbitrary",), vmem_limit_bytes=VMEM_LIMIT),
        name="sample_proj",
    )(xp, ng, win, *((wout,) if cast_weights else ()))
    return out if cast_weights else out[0]


def _sample_mix_seq(sinks_ref, pa_ref, pc_ref, s0_ref, kbuf_ref, vbuf_ref, retg_ref, qg_ref, kg_ref,
                    cm_ref, qdec_ref, kdec_ref, mix_ref, snew_ref, knew_ref, vnew_ref):
    P = SEQ_PAD
    for h in range(RET_HEADS):
        cs = slice(h * 128, (h + 1) * 128)
        q = pa_ref[:, QA + h * 128:QA + (h + 1) * 128]
        k = pa_ref[:, KA + h * 128:KA + (h + 1) * 128] * (RET_DK ** -0.5)
        v = pa_ref[:, VA + h * 128:VA + (h + 1) * 128]
        intra = jnp.zeros((P, RET_DV), F32)
        for s in range(4):
            r_ = TOK0 + s
            w = jnp.sum(q * k[r_:r_ + 1, :], axis=-1, keepdims=True)
            intra = intra + (w * cm_ref[h, s]) * v[r_:r_ + 1, :]
        s_prev = s0_ref[h]
        cross = _dot((q * qdec_ref[h]).astype(BF16), s_prev.astype(BF16))
        kv = _dot_tn((k * kdec_ref[h]).astype(BF16), v.astype(BF16))
        yield
        snew_ref[h] = float(np.exp(np.float32(4.0) * _LOG_G[h])) * s_prev + kv
        o = _rms(intra + cross, retg_ref[:, cs])
        mix_ref[:, cs] = o * _silu(pa_ref[:, GA + h * 128:GA + (h + 1) * 128])
        yield

    r16 = lax.broadcasted_iota(jnp.int32, (2 * P, WINDOW), 0)
    j16 = lax.broadcasted_iota(jnp.int32, (2 * P, WINDOW), 1)
    rr = jnp.where(r16 >= P, r16 - P, r16)
    row_ok = jnp.logical_and(rr >= TOK0, rr < TOK0 + 4)
    dist = (rr - TOK0) + WINDOW - j16
    distf = dist.astype(F32)
    valid = jnp.logical_and(jnp.logical_and(dist >= 0, dist < WINDOW), row_ok)
    r16c = lax.broadcasted_iota(jnp.int32, (2 * P, 1), 0)
    rrc = jnp.where(r16c >= P, r16c - P, r16c)
    rowc_ok = jnp.logical_and(rrc >= TOK0, rrc < TOK0 + 4)
    knew_ref[0:2 * (WINDOW - 4), :] = kbuf_ref[2 * 4:2 * WINDOW, :]
    vnew_ref[0:2 * (WINDOW - 4), :] = vbuf_ref[2 * 4:2 * WINDOW, :]
    for kh in range(SWA_KV_HEADS):
        h0, h1 = kh * SWA_GROUP, kh * SWA_GROUP + 1
        kb = kbuf_ref[pl.ds(kh, WINDOW, stride=SWA_KV_HEADS), :]
        vb = vbuf_ref[pl.ds(kh, WINDOW, stride=SWA_KV_HEADS), :]
        kc, vc = KC - QC, VC - QC
        kn = _rms(pc_ref[:, kc + kh * 128:kc + (kh + 1) * 128], kg_ref[...])
        vn = pc_ref[:, vc + kh * 128:vc + (kh + 1) * 128]
        q0 = _rms(pc_ref[:, h0 * 128:(h0 + 1) * 128], qg_ref[...])
        q1 = _rms(pc_ref[:, h1 * 128:(h1 + 1) * 128], qg_ref[...])
        qq = jnp.concatenate([q0, q1], axis=0)
        slope = jnp.where(r16 >= P, _SLOPES[h1], _SLOPES[h0])
        slopec = jnp.where(r16c >= P, _SLOPES[h1], _SLOPES[h0])
        sb = _dot_nt(qq.astype(BF16), kb.astype(BF16)) * (SWA_HEAD_DIM ** -0.5)
        sb = jnp.where(valid, sb - slope * distf, NEG_INF)
        yield
        sink = jnp.where(r16c >= P, sinks_ref[h1], sinks_ref[h0])
        m = jnp.maximum(jnp.max(sb, axis=-1, keepdims=True), sink)
        wn = []
        for s in range(4):
            r_ = TOK0 + s
            w = jnp.sum(qq * kn[r_:r_ + 1, :], axis=-1, keepdims=True) * (SWA_HEAD_DIM ** -0.5)
            dn = rrc - r_
            w = jnp.where(jnp.logical_and(dn >= 0, rowc_ok), w - slopec * dn.astype(F32), NEG_INF)
            wn.append(w)
            m = jnp.maximum(m, w)
        pb = jnp.exp(sb - m)
        denom = jnp.sum(pb, axis=-1, keepdims=True) + jnp.exp(sink - m)
        pn = [jnp.exp(w - m) for w in wn]
        for p_ in pn:
            denom = denom + p_
        yield
        o = _dot((pb / denom).astype(BF16), vb.astype(BF16))
        for s in range(4):
            r_ = TOK0 + s
            o = o + (pn[s] / denom) * vn[r_:r_ + 1, :]
        for g, hh in enumerate((h0, h1)):
            gc = pc_ref[:, GC - QC + hh * 128:GC - QC + (hh + 1) * 128]
            mix_ref[:, GROUP_WIDTH + hh * 128:GROUP_WIDTH + (hh + 1) * 128] = o[g * P:(g + 1) * P] * _silu(gc)
        for s in range(4):
            r_out = 2 * (WINDOW - 4 + s) + kh
            knew_ref[r_out:r_out + 1, :] = kn[TOK0 + s:TOK0 + s + 1, :]
            vnew_ref[r_out:r_out + 1, :] = vn[TOK0 + s:TOK0 + s + 1, :]
        yield


def _sample_mix_kernel(sinks_ref, pa_ref, pc_ref, s0_ref, kbuf_ref, vbuf_ref, retg_ref, qg_ref, kg_ref,
                       cm_ref, qdec_ref, kdec_ref, *rest, nseq, layer, slot, fill):
    mix_ref, snew_ref, knew_ref, vnew_ref = rest[-4:]
    sinks_ref = sinks_ref.at[layer]
    if fill:
        for other in [o for o in range(DEPTH) if o != slot]:
            for ref in (snew_ref, knew_ref, vnew_ref):
                ref[other] = jnp.zeros(ref.shape[1:], F32)
    gens = []
    for i in range(nseq):
        rows = slice(i * SEQ_PAD, (i + 1) * SEQ_PAD)
        gens.append(_sample_mix_seq(
            sinks_ref, pa_ref.at[rows, :], pc_ref.at[rows, :], s0_ref.at[0, i], kbuf_ref.at[0, i], vbuf_ref.at[0, i],
            retg_ref, qg_ref, kg_ref, cm_ref, qdec_ref, kdec_ref,
            mix_ref.at[rows, :], snew_ref.at[slot, i], knew_ref.at[slot, i], vnew_ref.at[slot, i]))
    _run(_zip_stages(gens))


def _sample_mix(layer, proj, state_ret, cache_k, cache_v, prm, tables, prev_out, nseq=8):
    depth, B = state_ret.shape[:2]
    layer_in = [prm[k] for k in ("retg", "qg", "kg")]
    consts = layer_in + list(tables)
    st_spec = pl.BlockSpec((1, nseq, RET_HEADS, RET_DK, RET_DV), lambda i: (layer, i, 0, 0, 0))
    kv_spec = pl.BlockSpec((1, nseq, 2 * WINDOW, SWA_HEAD_DIM), lambda i: (layer, i, 0, 0))
    in_specs = ([pl.BlockSpec(memory_space=pltpu.SMEM),
                 pl.BlockSpec((nseq * SEQ_PAD, XB), lambda i: (i, 0)),
                 pl.BlockSpec((nseq * SEQ_PAD, IN_WIDTH - QC), lambda i: (i, QC // (IN_WIDTH - QC))),
                 st_spec, kv_spec, kv_spec]
                + [_layer_spec(a, layer) for a in layer_in]
                + [_const_spec(a.shape) for a in tables]
                + [pl.BlockSpec(memory_space=pl.ANY) for _ in prev_out])
    out_shape = (
        jax.ShapeDtypeStruct((B * SEQ_PAD, 2 * GROUP_WIDTH), F32),
        jax.ShapeDtypeStruct(state_ret.shape, F32),
        jax.ShapeDtypeStruct(cache_k.shape, F32),
        jax.ShapeDtypeStruct(cache_v.shape, F32),
    )
    fill = not prev_out
    ld, li, slot = (DEPTH, 0, layer) if fill else (1, layer, 0)
    out_specs = (pl.BlockSpec((nseq * SEQ_PAD, 2 * GROUP_WIDTH), lambda i: (i, 0)),
                 pl.BlockSpec((ld, nseq, RET_HEADS, RET_DK, RET_DV), lambda i: (li, i, 0, 0, 0)),
                 pl.BlockSpec((ld, nseq, 2 * WINDOW, SWA_HEAD_DIM), lambda i: (li, i, 0, 0)),
                 pl.BlockSpec((ld, nseq, 2 * WINDOW, SWA_HEAD_DIM), lambda i: (li, i, 0, 0)))
    assert QC % (IN_WIDTH - QC) == 0
    n_in = 6 + len(consts)
    aliases = {n_in + j: 1 + j for j in range(len(prev_out))}
    return pl.pallas_call(
        functools.partial(_sample_mix_kernel, nseq=nseq, layer=layer, slot=slot, fill=fill),
        grid=(B // nseq,),
        in_specs=in_specs,
        out_specs=out_specs,
        out_shape=out_shape,
        input_output_aliases=aliases,
        compiler_params=pltpu.CompilerParams(
            dimension_semantics=("arbitrary",), vmem_limit_bytes=VMEM_LIMIT),
        name="sample_mix",
    )(prm["sinks"], proj, proj, state_ret, cache_k, cache_v, *consts, *prev_out)


def _sample_out_kernel(x_ref, xb_ref, gb_ref, mix_ref, conv8_ref, h8_ref, convw_ref, convb_ref,
                       wr_ref, wi_ref, br_ref, bi_ref, lam_ref, wout_ref,
                       y_ref, convo_ref, ho_ref):
    R = x_ref.shape[0]
    row = lax.broadcasted_iota(jnp.int32, (R, LRU_WIDTH), 0) & (SEQ_PAD - 1)
    xc = jnp.where(row < TOK0, conv8_ref[...], xb_ref[...])
    convo_ref[...] = xc
    conv = convb_ref[...] + convw_ref[0:1, :] * pltpu.roll(xc, 3, axis=0)
    conv = conv + convw_ref[1:2, :] * pltpu.roll(xc, 2, axis=0)
    conv = conv + convw_ref[2:3, :] * pltpu.roll(xc, 1, axis=0)
    conv = conv + convw_ref[3:4, :] * xc
    halves = [_lru_gates(conv[:, half * LRU_HALF:(half + 1) * LRU_HALF], half, wr_ref, wi_ref, br_ref, bi_ref,
                         lam_ref) for half in range(2)]
    a, mult, ig = (jnp.concatenate(parts, axis=1) for parts in zip(*halves))
    u = mult * ig * conv
    h = h8_ref[...]
    for s in range(4):
        h = jnp.where(row == TOK0 + s, a * pltpu.roll(h, 1, axis=0) + u, h)
    ho_ref[...] = h
    ob = h * _silu(gb_ref[...])
    y = x_ref[...] + _dot(mix_ref[:, 0:GROUP_WIDTH].astype(BF16), wout_ref[0:GROUP_WIDTH, :])
    y = y + _dot(ob.astype(BF16), wout_ref[GROUP_WIDTH:2 * GROUP_WIDTH, :])
    y = y + _dot(mix_ref[:, GROUP_WIDTH:2 * GROUP_WIDTH].astype(BF16), wout_ref[2 * GROUP_WIDTH:, :])
    y_ref[...] = y


def _sample_out(layer, xp, proj, mix, conv8, h8, prm, rows=256):
    R = xp.shape[0]
    consts = [prm[k] for k in ("convw", "convb", "wr", "wi", "br", "bi", "lam", "wout")]
    in_specs = ([pl.BlockSpec((rows, D_MODEL), lambda i: (i, 0)),
                 pl.BlockSpec((rows, LRU_WIDTH), lambda i: (i, XB // LRU_WIDTH)),
                 pl.BlockSpec((rows, LRU_WIDTH), lambda i: (i, GB // LRU_WIDTH)),
                 pl.BlockSpec((rows, 2 * GROUP_WIDTH), lambda i: (i, 0)),
                 pl.BlockSpec((rows, LRU_WIDTH), lambda i: (i, 0)),
                 pl.BlockSpec((rows, LRU_WIDTH), lambda i: (i, 0))]
                + [_layer_spec(a, layer) for a in consts])
    out_shape = (
        jax.ShapeDtypeStruct((R, D_MODEL), F32),
        jax.ShapeDtypeStruct((R, LRU_WIDTH), F32),
        jax.ShapeDtypeStruct((R, LRU_WIDTH), F32),
    )
    out_specs = (
        pl.BlockSpec((rows, D_MODEL), lambda i: (i, 0)),
        pl.BlockSpec((rows, LRU_WIDTH), lambda i: (i, 0)),
        pl.BlockSpec((rows, LRU_WIDTH), lambda i: (i, 0)),
    )
    return pl.pallas_call(
        _sample_out_kernel,
        grid=(R // rows,),
        in_specs=in_specs,
        out_specs=out_specs,
        out_shape=out_shape,
        compiler_params=pltpu.CompilerParams(
            dimension_semantics=("arbitrary",), vmem_limit_bytes=VMEM_LIMIT),
        name="sample_out",
    )(xp, proj, proj, mix, conv8, h8, *consts)


def _prompt_tables():
    C = RET_CHUNK
    idx = np.arange(C, dtype=np.float32)
    diff = idx[:, None] - idx[None, :]
    causal = diff >= 0
    lg = _LOG_G[:, None, None]
    dmask = np.where(causal[None], np.exp(np.where(causal, diff, 0.0)[None] * lg), 0.0).astype(np.float32)
    qdec = np.exp((idx + 1.0)[None, :] * _LOG_G[:, None]).astype(np.float32)
    kdec = np.exp((C - 1 - idx)[None, :] * _LOG_G[:, None]).astype(np.float32)
    qdec = np.broadcast_to(qdec[:, :, None], (RET_HEADS, C, RET_DK)).copy()
    kdec = np.broadcast_to(kdec[:, :, None], (RET_HEADS, C, RET_DK)).copy()
    return jnp.asarray(dmask), jnp.asarray(qdec), jnp.asarray(kdec)


def _sample_tables():
    P = SEQ_PAD
    rows = np.arange(P, dtype=np.float32)
    i = rows - TOK0
    tok = (i >= 0) & (i < 4)
    cm = np.zeros((RET_HEADS, 4, P, RET_DV), np.float32)
    qdec = np.zeros((RET_HEADS, P, RET_DK), np.float32)
    kdec = np.zeros((RET_HEADS, P, RET_DK), np.float32)
    for h in range(RET_HEADS):
        for s in range(4):
            d = i - s
            col = np.where(tok & (d >= 0), np.exp(np.where(d >= 0, d, 0.0) * _LOG_G[h]), 0.0)
            cm[h, s] = col[:, None]
        qdec[h] = np.where(tok, np.exp((i + 1.0) * _LOG_G[h]), 0.0)[:, None]
        kdec[h] = np.where(tok, np.exp((3.0 - i) * _LOG_G[h]), 0.0)[:, None]
    return jnp.asarray(cm), jnp.asarray(qdec), jnp.asarray(kdec)


def _block_diag(w):
    L, n, d, _ = w.shape
    g = n // 2
    w = w.reshape(L, 2, g, d, d)
    eye = jnp.eye(g, dtype=w.dtype)
    return (eye[None, None, :, None, :, None] * w[:, :, :, :, None, :]).reshape(L, 2, g * d, g * d)


def kernel(x_prompt, x_sample, state_ret, state_lru, state_conv, cache_swa_k, cache_swa_v, norm_g, w_in, w_out, ret_norm_g, conv_w, conv_b, w_rgate, b_rgate, w_igate, b_igate, lru_lambda, q_norm_g, k_norm_g, attn_sinks):
    Bs, Ts, _ = x_sample.shape
    ptab = _prompt_tables()
    stab = _sample_tables()
    yp = x_prompt
    ys = jnp.pad(x_sample, ((0, 0), (TOK0, SEQ_PAD - TOK0 - Ts), (0, 0))).reshape(Bs * SEQ_PAD, D_MODEL)
    cache_k = cache_swa_k.reshape(DEPTH, Bs, 2 * WINDOW, SWA_HEAD_DIM)
    cache_v = cache_swa_v.reshape(DEPTH, Bs, 2 * WINDOW, SWA_HEAD_DIM)
    Bp = x_prompt.shape[0]
    p_state, s_state = (), ()
    s_h, s_conv = [], []
    row = lambda a: a.reshape(DEPTH, 1, -1)
    prm = dict(
        sinks=attn_sinks, ng=row(norm_g), retg=row(ret_norm_g),
        convw=conv_w, convb=row(conv_b), wr=_block_diag(w_rgate).astype(BF16), wi=_block_diag(w_igate).astype(BF16),
        br=row(b_rgate), bi=row(b_igate), lam=row(lru_lambda), qg=row(q_norm_g), kg=row(k_norm_g))
    proj, prm["win"], prm["wout"] = _sample_proj(0, ys, prm["ng"], w_in, w_out)
    for l in range(DEPTH):
        yp, *p_state = _prompt_layer(l, yp, prm, ptab, tuple(p_state))
        if l > 0:
            proj = _sample_proj(l, ys, prm["ng"], prm["win"])
        mix, *s_state = _sample_mix(l, proj, state_ret, cache_k, cache_v, prm, stab, tuple(s_state))
        conv8 = jnp.pad(state_conv[l], ((0, 0), (0, SEQ_PAD - (CONV_W - 1)), (0, 0))).reshape(Bs * SEQ_PAD, LRU_WIDTH)
        h8 = jnp.pad(state_lru[l][:, None, :], ((0, 0), (TOK0 - 1, SEQ_PAD - TOK0), (0, 0))).reshape(Bs * SEQ_PAD, LRU_WIDTH)
        ys, convo, ho = _sample_out(l, ys, proj, mix, conv8, h8, prm)
        s_h.append(ho.reshape(Bs, SEQ_PAD, LRU_WIDTH)[:, TOK0 + Ts - 1])
        s_conv.append(convo.reshape(Bs, SEQ_PAD, LRU_WIDTH)[:, TOK0 + Ts - (CONV_W - 1):TOK0 + Ts])

    p_ret, p_h, p_conv, p_k, p_v = p_state
    s_ret, s_k, s_v = s_state
    kv5 = lambda a, n: a.reshape(DEPTH, n, WINDOW, SWA_KV_HEADS, SWA_HEAD_DIM)
    y_sample = ys.reshape(Bs, SEQ_PAD, D_MODEL)[:, TOK0:TOK0 + Ts]
    return (yp, y_sample,
            p_ret, p_h.reshape(DEPTH, Bp, LRU_WIDTH), p_conv, kv5(p_k, Bp), kv5(p_v, Bp),
            s_ret, jnp.stack(s_h), jnp.stack(s_conv), kv5(s_k, Bs), kv5(s_v, Bs))
```

```python
import functools
import math

import numpy as np
import jax
import jax.numpy as jnp
from jax import lax
from jax.experimental import pallas as pl
from jax.experimental.pallas import tpu as pltpu

D_MODEL = 1024
DEPTH = 2
PAST_LEN = 16384
GROUP_WIDTH = 512
RET_HEADS = 4
RET_DK = 128
RET_DV = 128
RET_CHUNK = 128
LRU_WIDTH = 512
LRU_BLOCKS = 8
LRU_C = 8.0
CONV_W = 4
SWA_HEADS = 4
SWA_KV_HEADS = 2
SWA_GROUP = 2
SWA_HEAD_DIM = 128
WINDOW = 128
NORM_EPS = 1e-6
NEG_INF = -1e30

IN_WIDTH = 4608
MIX_WIDTH = 1536
QA, KA, VA, GA, XB, GB, QC, KC, VC, GC = 0, 512, 1024, 1536, 2048, 2560, 3072, 3584, 3840, 4096

F32 = jnp.float32
BF16 = jnp.bfloat16

SUBLANES = 8
SEQ_PAD = 8
TOK0 = 3
VMEM_LIMIT = 56 * 1024 * 1024

_LOG_G = np.log1p(-np.power(np.float32(2.0), (-5.0 - np.arange(RET_HEADS)).astype(np.float32))).astype(np.float32)
_SLOPES = [2.0 ** (-8.0 * (h + 1) / SWA_HEADS) for h in range(SWA_HEADS)]


def _rms(x, g):
    ms = jnp.mean(x * x, axis=-1, keepdims=True)
    return x * lax.rsqrt(ms + NORM_EPS) * g


def _silu(x):
    return x * jax.nn.sigmoid(x)


def _softplus(x):
    return jnp.maximum(x, 0.0) + jnp.log1p(jnp.exp(-jnp.abs(x)))


def _dot(a, b):
    return jnp.dot(a, b, preferred_element_type=F32)


def _dot_nt(a, b):
    return lax.dot_general(a, b, (((1,), (1,)), ((), ())), preferred_element_type=F32)


def _dot_tn(a, b):
    return lax.dot_general(a, b, (((0,), (0,)), ((), ())), preferred_element_type=F32)


LRU_HALF = LRU_WIDTH // 2


def _lru_gates(conv, half, wr_ref, wi_ref, br_ref, bi_ref, lam_ref):
    cs = slice(half * LRU_HALF, (half + 1) * LRU_HALF)
    gin = conv.astype(BF16)
    r = jax.nn.sigmoid(_dot(gin, wr_ref[half]) + br_ref[:, cs])
    i = jax.nn.sigmoid(_dot(gin, wi_ref[half]) + bi_ref[:, cs])
    log_a = -LRU_C * r * _softplus(-lam_ref[:, cs])
    a = jnp.exp(log_a)
    m2 = -jnp.tanh(log_a) * (1.0 + a * a)
    mult = jnp.where(m2 > 0.0, m2 * lax.rsqrt(m2), 0.0)
    return a, mult, i


_DONE = object()


def _run(gen):
    for _ in gen:
        pass


def _zip_stages(gens):
    gens = list(gens)
    while gens:
        gens = [g for g in gens if next(g, _DONE) is not _DONE]
        if gens:
            yield


def _chain(gens):
    for g in gens:
        yield from g


def _interleave(main, sides, n_main, n_side):
    side = _chain(sides)
    next(side)
    done_side = 1
    for i, _ in enumerate(main):
        want = ((i + 1) * n_side) // n_main
        while done_side < want and next(side, _DONE) is not _DONE:
            done_side += 1
    _run(side)


IN_PROJ_COLS = 256
IN_PROJ_STAGES = 1 + IN_WIDTH // IN_PROJ_COLS


def _in_proj(x_ref, ng_ref, win_ref, proj_ref):
    hb = _rms(x_ref[...], ng_ref[...]).astype(BF16)
    yield
    for j in range(IN_WIDTH // IN_PROJ_COLS):
        cols = slice(j * IN_PROJ_COLS, (j + 1) * IN_PROJ_COLS)
        proj_ref[:, cols] = _dot(hb, win_ref[:, cols])
        yield


OUT_PROJ_STAGES = D_MODEL // 256


def _out_proj(mixed_s, x_ref, y_ref, wout_ref):
    mixed = mixed_s[...]
    for j in range(OUT_PROJ_STAGES):
        cols = slice(j * 256, (j + 1) * 256)
        y_ref[:, cols] = x_ref[:, cols] + _dot(mixed, wout_ref[:, cols])
        yield


def _mix_stages(tt):
    return 4 + tt // SUBLANES // 8 + (tt // RET_CHUNK) * 5


def _scan_rows(a, u, h_in, tt):
    G = tt // SUBLANES
    W = a.shape[-1]
    a3 = a.reshape(G, SUBLANES, W)
    u3 = u.reshape(G, SUBLANES, W)
    r3 = lax.broadcasted_iota(jnp.int32, (G, SUBLANES, W), 1)
    sh = 1
    while sh < SUBLANES:
        keep = r3 >= sh
        a_sh = jnp.where(keep, pltpu.roll(a3, sh, axis=1), 1.0)
        u_sh = jnp.where(keep, pltpu.roll(u3, sh, axis=1), 0.0)
        u3 = a3 * u_sh + u3
        a3 = a3 * a_sh
        sh *= 2
    yield
    hs = []
    for g in range(G):
        hg = a3[g] * h_in + u3[g]
        hs.append(hg)
        h_in = hg[SUBLANES - 1:SUBLANES, :]
        if g % 8 == 7:
            yield
    return jnp.concatenate(hs, axis=0)


def _retention_head(h, proj_s, r0, s_ref, retg_ref, dmask_ref, qdec_ref, kdec_ref, mixed_s):
    C = RET_CHUNK
    cs = slice(h * RET_DK, (h + 1) * RET_DK)
    q = proj_s[r0:r0 + C, QA + h * 128:QA + (h + 1) * 128]
    k = proj_s[r0:r0 + C, KA + h * 128:KA + (h + 1) * 128] * (RET_DK ** -0.5)
    vb = proj_s[r0:r0 + C, VA + h * 128:VA + (h + 1) * 128].astype(BF16)
    sc = _dot_nt(q.astype(BF16), k.astype(BF16)) * dmask_ref[h]
    yield
    s_prev = s_ref[0, h]
    lhs = jnp.concatenate([sc.astype(BF16), (q * qdec_ref[h]).astype(BF16)], axis=1)
    rhs = jnp.concatenate([vb, s_prev.astype(BF16)], axis=0)
    o = _dot(lhs, rhs)
    yield
    kv = _dot_tn((k * kdec_ref[h]).astype(BF16), vb)
    s_ref[0, h] = float(np.exp(np.float32(C) * _LOG_G[h])) * s_prev + kv
    yield
    o = _rms(o, retg_ref[:, cs])
    mixed_s[r0:r0 + C, cs] = (o * _silu(proj_s[r0:r0 + C, GA + h * 128:GA + (h + 1) * 128])).astype(BF16)
    yield


def _swa_kv_head(kh, proj_s, r0, blk, sinks_ref, qg_ref, kg_ref, mixed_s, kprev_s, vprev_s):
    C = RET_CHUNK
    ks = slice(kh * 128, (kh + 1) * 128)
    h0, h1 = kh * SWA_GROUP, kh * SWA_GROUP + 1
    kn = _rms(proj_s[r0:r0 + C, KC + kh * 128:KC + (kh + 1) * 128], kg_ref[...])
    vv = proj_s[r0:r0 + C, VC + kh * 128:VC + (kh + 1) * 128]
    kband = jnp.concatenate([kprev_s[:, ks], kn], axis=0).astype(BF16)
    vband = jnp.concatenate([vprev_s[:, ks], vv], axis=0).astype(BF16)
    kprev_s[:, ks] = kn
    vprev_s[:, ks] = vv
    q0 = _rms(proj_s[r0:r0 + C, QC + h0 * 128:QC + (h0 + 1) * 128], qg_ref[...])
    q1 = _rms(proj_s[r0:r0 + C, QC + h1 * 128:QC + (h1 + 1) * 128], qg_ref[...])
    qq = jnp.concatenate([q0, q1], axis=0).astype(BF16)
    yield
    s = _dot_nt(qq, kband) * (SWA_HEAD_DIM ** -0.5)
    ii = lax.broadcasted_iota(jnp.int32, (2 * C, 2 * C), 0)
    jj = lax.broadcasted_iota(jnp.int32, (2 * C, 2 * C), 1)
    dist = jnp.where(ii >= C, ii - C, ii) + C - jj
    valid = jnp.logical_and(jnp.logical_and(dist >= 0, dist < WINDOW), jj >= jnp.where(blk > 0, 0, C))
    slope = jnp.where(ii >= C, _SLOPES[h1], _SLOPES[h0])
    s = jnp.where(valid, s - slope * dist.astype(F32), NEG_INF)
    yield
    rowc = lax.broadcasted_iota(jnp.int32, (2 * C, 1), 0)
    sink = jnp.where(rowc >= C, sinks_ref[h1], sinks_ref[h0])
    m = jnp.maximum(jnp.max(s, axis=-1, keepdims=True), sink)
    p = jnp.exp(s - m)
    denom = jnp.sum(p, axis=-1, keepdims=True) + jnp.exp(sink - m)
    yield
    o = _dot((p / denom).astype(BF16), vband)
    yield
    for g, hh in enumerate((h0, h1)):
        gc = proj_s[r0:r0 + C, GC + hh * 128:GC + (hh + 1) * 128]
        mixed_s[r0:r0 + C, 2 * GROUP_WIDTH + hh * 128:2 * GROUP_WIDTH + (hh + 1) * 128] = (
            (o[g * C:(g + 1) * C] * _silu(gc)).astype(BF16))
    yield


def _mix_tile(proj_s, t, tt, sinks_ref, retg_ref, convw_ref, convb_ref,
              wr_ref, wi_ref, br_ref, bi_ref, lam_ref, qg_ref, kg_ref, dmask_ref, qdec_ref, kdec_ref,
              s_ref, mixed_s, xtail_s, kprev_s, vprev_s, hc_s):
    nch = tt // RET_CHUNK
    C = RET_CHUNK

    def lru_half(half):
        lo = half * LRU_HALF
        cs = slice(lo, lo + LRU_HALF)
        xb = proj_s[:, XB + lo:XB + lo + LRU_HALF]
        xfull = jnp.concatenate([xtail_s[:, cs], xb], axis=0)
        xtail_s[:, cs] = xb[tt - SUBLANES:tt, :]
        conv = convb_ref[:, cs] + convw_ref[0:1, cs] * pltpu.roll(xfull, 3, axis=0)[SUBLANES:]
        conv = conv + convw_ref[1:2, cs] * pltpu.roll(xfull, 2, axis=0)[SUBLANES:]
        conv = conv + convw_ref[2:3, cs] * pltpu.roll(xfull, 1, axis=0)[SUBLANES:]
        conv = conv + convw_ref[3:4, cs] * xb
        yield

        a, mult, ig = _lru_gates(conv, half, wr_ref, wi_ref, br_ref, bi_ref, lam_ref)
        yield
        row = lax.broadcasted_iota(jnp.int32, (tt, LRU_HALF), 0)
        mult = jnp.where(row == jnp.where(t == 0, 0, -1), 1.0, mult)
        u = mult * ig * conv
        hseq = yield from _scan_rows(a, u, hc_s[:, cs], tt)
        hc_s[:, cs] = hseq[tt - 1:tt, :]
        gate = _silu(proj_s[:, GB + lo:GB + lo + LRU_HALF])
        mixed_s[:, GROUP_WIDTH + lo:GROUP_WIDTH + lo + LRU_HALF] = (hseq * gate).astype(BF16)
        yield

    yield from _zip_stages([lru_half(0), lru_half(1)])

    for c in range(nch):
        r0 = c * C
        blk = t * nch + c
        yield from _zip_stages(
            [_retention_head(h, proj_s, r0, s_ref, retg_ref, dmask_ref, qdec_ref, kdec_ref, mixed_s)
             for h in range(RET_HEADS)]
            + [_swa_kv_head(kh, proj_s, r0, blk, sinks_ref, qg_ref, kg_ref, mixed_s, kprev_s, vprev_s)
               for kh in range(SWA_KV_HEADS)])


def _prompt_kernel(sinks_ref, xa_ref, xn_ref, ng_ref, win_ref, wout_ref, retg_ref, convw_ref, convb_ref,
                   wr_ref, wi_ref, br_ref, bi_ref, lam_ref, qg_ref, kg_ref,
                   dmask_ref, qdec_ref, kdec_ref, *rest, tt, npairs, layer, slot, fill):
    y_ref, s_ref, h_ref, conv_ref, kout_ref, vout_ref = rest[-14:-8]
    proj_a, proj_b, mixed_a, mixed_b, xtail_s, kprev_s, vprev_s, hc_s = rest[-8:]
    sinks_ref = sinks_ref.at[layer]
    b = pl.program_id(0)
    p = pl.program_id(1)

    @pl.when(p == 0)
    def _init():
        s_ref[...] = jnp.zeros_like(s_ref)
        xtail_s[...] = jnp.zeros_like(xtail_s)
        kprev_s[...] = jnp.zeros_like(kprev_s)
        vprev_s[...] = jnp.zeros_like(vprev_s)
        hc_s[...] = jnp.zeros_like(hc_s)

    xa0, xa1 = xa_ref.at[0, 0:tt, :], xa_ref.at[0, tt:2 * tt, :]
    ya0, ya1 = y_ref.at[0, 0:tt, :], y_ref.at[0, tt:2 * tt, :]

    @pl.when(jnp.logical_and(b == 0, p == 0))
    def _prologue():
        _run(_in_proj(xa0, ng_ref, win_ref, proj_a))

    common = (sinks_ref, retg_ref, convw_ref, convb_ref, wr_ref, wi_ref, br_ref, bi_ref,
              lam_ref, qg_ref, kg_ref, dmask_ref, qdec_ref, kdec_ref, s_ref.at[slot])
    state = (xtail_s, kprev_s, vprev_s, hc_s)

    n_main = _mix_stages(tt)
    _interleave(_mix_tile(proj_a, 2 * p, tt, *common, mixed_a, *state),
                [_in_proj(xa1, ng_ref, win_ref, proj_b)], n_main, IN_PROJ_STAGES)
    _interleave(_mix_tile(proj_b, 2 * p + 1, tt, *common, mixed_b, *state),
                [_out_proj(mixed_a, xa0, ya0, wout_ref), _in_proj(xn_ref.at[0], ng_ref, win_ref, proj_a)],
                n_main, OUT_PROJ_STAGES + IN_PROJ_STAGES)
    _run(_out_proj(mixed_b, xa1, ya1, wout_ref))

    @pl.when(p == npairs - 1)
    def _state_out():
        h_ref[slot, 0] = hc_s[...]
        conv_ref[slot, 0] = xtail_s[SUBLANES - (CONV_W - 1):SUBLANES, :]
        for kh in range(SWA_KV_HEADS):
            ks = slice(kh * 128, (kh + 1) * 128)
            kout_ref[slot, 0, pl.ds(kh, WINDOW, stride=SWA_KV_HEADS), :] = kprev_s[:, ks]
            vout_ref[slot, 0, pl.ds(kh, WINDOW, stride=SWA_KV_HEADS), :] = vprev_s[:, ks]
        if fill:
            for other in [o for o in range(DEPTH) if o != slot]:
                for ref in (h_ref, conv_ref, kout_ref, vout_ref):
                    ref[other] = jnp.zeros(ref.shape[1:], F32)


def _const_spec(shape):
    nd = len(shape)
    return pl.BlockSpec(shape, lambda *_: (0,) * nd)


def _resident_spec(shape):
    nd = len(shape)
    return pl.BlockSpec(shape, lambda *_: (0,) * nd, pipeline_mode=pl.Buffered(1))


def _layer_spec(a, layer, resident=False):
    nd = a.ndim
    kw = dict(pipeline_mode=pl.Buffered(1)) if resident else {}
    return pl.BlockSpec((None,) + a.shape[1:], lambda *_: (layer,) + (0,) * (nd - 1), **kw)


def _prompt_layer(layer, x, prm, tables, prev_out, tt=256):
    B, T, D = x.shape
    npairs = T // (2 * tt)
    layer_in = [prm[k] for k in ("ng", "win", "wout", "retg", "convw", "convb", "wr", "wi", "br", "bi",
                                 "lam", "qg", "kg")]
    vec_in = layer_in + list(tables)

    def next_tile(b, p):
        last = p == npairs - 1
        return (jnp.where(last, jnp.minimum(b + 1, B - 1), b), jnp.where(last, 0, 2 * p + 2), 0)

    in_specs = ([pl.BlockSpec(memory_space=pltpu.SMEM),
                 pl.BlockSpec((1, 2 * tt, D), lambda b, p: (b, p, 0)),
                 pl.BlockSpec((1, tt, D), next_tile)]
                + [_layer_spec(a, layer, resident=True) for a in layer_in]
                + [_resident_spec(a.shape) for a in tables]
                + [pl.BlockSpec(memory_space=pl.ANY) for _ in prev_out])
    out_shape = (
        jax.ShapeDtypeStruct((B, T, D), F32),
        jax.ShapeDtypeStruct((DEPTH, B, RET_HEADS, RET_DK, RET_DV), F32),
        jax.ShapeDtypeStruct((DEPTH, B, 1, LRU_WIDTH), F32),
        jax.ShapeDtypeStruct((DEPTH, B, CONV_W - 1, LRU_WIDTH), F32),
        jax.ShapeDtypeStruct((DEPTH, B, 2 * WINDOW, SWA_HEAD_DIM), F32),
        jax.ShapeDtypeStruct((DEPTH, B, 2 * WINDOW, SWA_HEAD_DIM), F32),
    )
    fill = not prev_out
    ld, li, slot = (DEPTH, 0, layer) if fill else (1, layer, 0)
    out_specs = (
        pl.BlockSpec((1, 2 * tt, D), lambda b, p: (b, p, 0)),
        pl.BlockSpec((ld, 1, RET_HEADS, RET_DK, RET_DV), lambda b, p: (li, b, 0, 0, 0)),
        pl.BlockSpec((ld, 1, 1, LRU_WIDTH), lambda b, p: (li, b, 0, 0)),
        pl.BlockSpec((ld, 1, CONV_W - 1, LRU_WIDTH), lambda b, p: (li, b, 0, 0)),
        pl.BlockSpec((ld, 1, 2 * WINDOW, SWA_HEAD_DIM), lambda b, p: (li, b, 0, 0)),
        pl.BlockSpec((ld, 1, 2 * WINDOW, SWA_HEAD_DIM), lambda b, p: (li, b, 0, 0)),
    )
    n_in = 3 + len(vec_in)
    aliases = {n_in + j: 1 + j for j in range(len(prev_out))}
    scratch = [
        pltpu.VMEM((tt, IN_WIDTH), F32),
        pltpu.VMEM((tt, IN_WIDTH), F32),
        pltpu.VMEM((tt, MIX_WIDTH), BF16),
        pltpu.VMEM((tt, MIX_WIDTH), BF16),
        pltpu.VMEM((SUBLANES, LRU_WIDTH), F32),
        pltpu.VMEM((WINDOW, 256), F32),
        pltpu.VMEM((WINDOW, 256), F32),
        pltpu.VMEM((1, LRU_WIDTH), F32),
    ]
    return pl.pallas_call(
        functools.partial(_prompt_kernel, tt=tt, npairs=npairs, layer=layer, slot=slot, fill=fill),
        grid=(B, npairs),
        in_specs=in_specs,
        out_specs=out_specs,
        out_shape=out_shape,
        scratch_shapes=scratch,
        input_output_aliases=aliases,
        compiler_params=pltpu.CompilerParams(
            dimension_semantics=("arbitrary", "arbitrary"), vmem_limit_bytes=VMEM_LIMIT),
        name="prompt_layer",
    )(prm["sinks"], x, x, *vec_in, *prev_out)


def _sproj_kernel(x_ref, ng_ref, win_ref, *rest, layer, cast_weights):
    if cast_weights:
        wout_ref, o_ref, winb_ref, woutb_ref, hb_s = rest
        winb_ref[...] = win_ref[...].astype(BF16)
        woutb_ref[...] = wout_ref[...].astype(BF16)
        win_ref = winb_ref.at[layer]
    else:
        o_ref, hb_s = rest

    @pl.when(pl.program_id(0) == 0)
    def _norm_once():
        hb_s[...] = _rms(x_ref[...], ng_ref[...]).astype(BF16)

    o_ref[...] = _dot(hb_s[...], win_ref[...])


BF16_SUBLANES = 2 * SUBLANES


def _sample_proj(layer, xp, ng, win, wout=None):
    R = xp.shape[0]
    nb = IN_WIDTH // 512
    cast_weights = wout is not None
    in_specs = [_const_spec(xp.shape), _layer_spec(ng, layer)]
    out_specs = [pl.BlockSpec((R, 512), lambda j: (0, j))]
    out_shape = [jax.ShapeDtypeStruct((R, IN_WIDTH), F32)]
    if cast_weights:
        wo_blocks = max(n for n in range(1, nb + 1) if MIX_WIDTH % (n * BF16_SUBLANES) == 0)
        wo_rows = MIX_WIDTH // wo_blocks
        win_spec = pl.BlockSpec((DEPTH, D_MODEL, 512), lambda j: (0, 0, j))
        wout_spec = pl.BlockSpec((DEPTH, wo_rows, D_MODEL), lambda j: (0, jnp.minimum(j, wo_blocks - 1), 0))
        in_specs += [win_spec, wout_spec]
        out_specs += [win_spec, wout_spec]
        out_shape += [jax.ShapeDtypeStruct(win.shape, BF16), jax.ShapeDtypeStruct(wout.shape, BF16)]
    else:
        in_specs += [pl.BlockSpec((None, D_MODEL, 512), lambda j: (layer, 0, j))]
    out = pl.pallas_call(
        functools.partial(_sproj_kernel, layer=layer, cast_weights=cast_weights),
        grid=(nb,),
        in_specs=in_specs,
        out_specs=out_specs,
        out_shape=out_shape,
        scratch_shapes=[pltpu.VMEM((R, D_MODEL), BF16)],
        compiler_params=pltpu.CompilerParams(
            dimension_semantics=("arbitrary",), vmem_limit_bytes=VMEM_LIMIT),
        name="sample_proj",
    )(xp, ng, win, *((wout,) if cast_weights else ()))
    return out if cast_weights else out[0]


def _sample_mix_seq(sinks_ref, pa_ref, pc_ref, s0_ref, kbuf_ref, vbuf_ref, retg_ref, qg_ref, kg_ref,
                    cm_ref, qdec_ref, kdec_ref, mix_ref, snew_ref, knew_ref, vnew_ref):
    P = SEQ_PAD
    for h in range(RET_HEADS):
        cs = slice(h * 128, (h + 1) * 128)
        q = pa_ref[:, QA + h * 128:QA + (h + 1) * 128]
        k = pa_ref[:, KA + h * 128:KA + (h + 1) * 128] * (RET_DK ** -0.5)
        v = pa_ref[:, VA + h * 128:VA + (h + 1) * 128]
        intra = jnp.zeros((P, RET_DV), F32)
        for s in range(4):
            r_ = TOK0 + s
            w = jnp.sum(q * k[r_:r_ + 1, :], axis=-1, keepdims=True)
            intra = intra + (w * cm_ref[h, s]) * v[r_:r_ + 1, :]
        s_prev = s0_ref[h]
        cross = _dot((q * qdec_ref[h]).astype(BF16), s_prev.astype(BF16))
        kv = _dot_tn((k * kdec_ref[h]).astype(BF16), v.astype(BF16))
        yield
        snew_ref[h] = float(np.exp(np.float32(4.0) * _LOG_G[h])) * s_prev + kv
        o = _rms(intra + cross, retg_ref[:, cs])
        mix_ref[:, cs] = o * _silu(pa_ref[:, GA + h * 128:GA + (h + 1) * 128])
        yield

    r16 = lax.broadcasted_iota(jnp.int32, (2 * P, WINDOW), 0)
    j16 = lax.broadcasted_iota(jnp.int32, (2 * P, WINDOW), 1)
    rr = jnp.where(r16 >= P, r16 - P, r16)
    row_ok = jnp.logical_and(rr >= TOK0, rr < TOK0 + 4)
    dist = (rr - TOK0) + WINDOW - j16
    distf = dist.astype(F32)
    valid = jnp.logical_and(jnp.logical_and(dist >= 0, dist < WINDOW), row_ok)
    r16c = lax.broadcasted_iota(jnp.int32, (2 * P, 1), 0)
    rrc = jnp.where(r16c >= P, r16c - P, r16c)
    rowc_ok = jnp.logical_and(rrc >= TOK0, rrc < TOK0 + 4)
    knew_ref[0:2 * (WINDOW - 4), :] = kbuf_ref[2 * 4:2 * WINDOW, :]
    vnew_ref[0:2 * (WINDOW - 4), :] = vbuf_ref[2 * 4:2 * WINDOW, :]
    for kh in range(SWA_KV_HEADS):
        h0, h1 = kh * SWA_GROUP, kh * SWA_GROUP + 1
        kb = kbuf_ref[pl.ds(kh, WINDOW, stride=SWA_KV_HEADS), :]
        vb = vbuf_ref[pl.ds(kh, WINDOW, stride=SWA_KV_HEADS), :]
        kc, vc = KC - QC, VC - QC
        kn = _rms(pc_ref[:, kc + kh * 128:kc + (kh + 1) * 128], kg_ref[...])
        vn = pc_ref[:, vc + kh * 128:vc + (kh + 1) * 128]
        q0 = _rms(pc_ref[:, h0 * 128:(h0 + 1) * 128], qg_ref[...])
        q1 = _rms(pc_ref[:, h1 * 128:(h1 + 1) * 128], qg_ref[...])
        qq = jnp.concatenate([q0, q1], axis=0)
        slope = jnp.where(r16 >= P, _SLOPES[h1], _SLOPES[h0])
        slopec = jnp.where(r16c >= P, _SLOPES[h1], _SLOPES[h0])
        sb = _dot_nt(qq.astype(BF16), kb.astype(BF16)) * (SWA_HEAD_DIM ** -0.5)
        sb = jnp.where(valid, sb - slope * distf, NEG_INF)
        yield
        sink = jnp.where(r16c >= P, sinks_ref[h1], sinks_ref[h0])
        m = jnp.maximum(jnp.max(sb, axis=-1, keepdims=True), sink)
        wn = []
        for s in range(4):
            r_ = TOK0 + s
            w = jnp.sum(qq * kn[r_:r_ + 1, :], axis=-1, keepdims=True) * (SWA_HEAD_DIM ** -0.5)
            dn = rrc - r_
            w = jnp.where(jnp.logical_and(dn >= 0, rowc_ok), w - slopec * dn.astype(F32), NEG_INF)
            wn.append(w)
            m = jnp.maximum(m, w)
        pb = jnp.exp(sb - m)
        denom = jnp.sum(pb, axis=-1, keepdims=True) + jnp.exp(sink - m)
        pn = [jnp.exp(w - m) for w in wn]
        for p_ in pn:
            denom = denom + p_
        yield
        o = _dot((pb / denom).astype(BF16), vb.astype(BF16))
        for s in range(4):
            r_ = TOK0 + s
            o = o + (pn[s] / denom) * vn[r_:r_ + 1, :]
        for g, hh in enumerate((h0, h1)):
            gc = pc_ref[:, GC - QC + hh * 128:GC - QC + (hh + 1) * 128]
            mix_ref[:, GROUP_WIDTH + hh * 128:GROUP_WIDTH + (hh + 1) * 128] = o[g * P:(g + 1) * P] * _silu(gc)
        for s in range(4):
            r_out = 2 * (WINDOW - 4 + s) + kh
            knew_ref[r_out:r_out + 1, :] = kn[TOK0 + s:TOK0 + s + 1, :]
            vnew_ref[r_out:r_out + 1, :] = vn[TOK0 + s:TOK0 + s + 1, :]
        yield


def _sample_mix_kernel(sinks_ref, pa_ref, pc_ref, s0_ref, kbuf_ref, vbuf_ref, retg_ref, qg_ref, kg_ref,
                       cm_ref, qdec_ref, kdec_ref, *rest, nseq, layer, slot, fill):
    mix_ref, snew_ref, knew_ref, vnew_ref = rest[-4:]
    sinks_ref = sinks_ref.at[layer]
    if fill:
        for other in [o for o in range(DEPTH) if o != slot]:
            for ref in (snew_ref, knew_ref, vnew_ref):
                ref[other] = jnp.zeros(ref.shape[1:], F32)
    gens = []
    for i in range(nseq):
        rows = slice(i * SEQ_PAD, (i + 1) * SEQ_PAD)
        gens.append(_sample_mix_seq(
            sinks_ref, pa_ref.at[rows, :], pc_ref.at[rows, :], s0_ref.at[0, i], kbuf_ref.at[0, i], vbuf_ref.at[0, i],
            retg_ref, qg_ref, kg_ref, cm_ref, qdec_ref, kdec_ref,
            mix_ref.at[rows, :], snew_ref.at[slot, i], knew_ref.at[slot, i], vnew_ref.at[slot, i]))
    _run(_zip_stages(gens))


def _sample_mix(layer, proj, state_ret, cache_k, cache_v, prm, tables, prev_out, nseq=8):
    depth, B = state_ret.shape[:2]
    layer_in = [prm[k] for k in ("retg", "qg", "kg")]
    consts = layer_in + list(tables)
    st_spec = pl.BlockSpec((1, nseq, RET_HEADS, RET_DK, RET_DV), lambda i: (layer, i, 0, 0, 0))
    kv_spec = pl.BlockSpec((1, nseq, 2 * WINDOW, SWA_HEAD_DIM), lambda i: (layer, i, 0, 0))
    in_specs = ([pl.BlockSpec(memory_space=pltpu.SMEM),
                 pl.BlockSpec((nseq * SEQ_PAD, XB), lambda i: (i, 0)),
                 pl.BlockSpec((nseq * SEQ_PAD, IN_WIDTH - QC), lambda i: (i, QC // (IN_WIDTH - QC))),
                 st_spec, kv_spec, kv_spec]
                + [_layer_spec(a, layer) for a in layer_in]
                + [_const_spec(a.shape) for a in tables]
                + [pl.BlockSpec(memory_space=pl.ANY) for _ in prev_out])
    out_shape = (
        jax.ShapeDtypeStruct((B * SEQ_PAD, 2 * GROUP_WIDTH), F32),
        jax.ShapeDtypeStruct(state_ret.shape, F32),
        jax.ShapeDtypeStruct(cache_k.shape, F32),
        jax.ShapeDtypeStruct(cache_v.shape, F32),
    )
    fill = not prev_out
    ld, li, slot = (DEPTH, 0, layer) if fill else (1, layer, 0)
    out_specs = (pl.BlockSpec((nseq * SEQ_PAD, 2 * GROUP_WIDTH), lambda i: (i, 0)),
                 pl.BlockSpec((ld, nseq, RET_HEADS, RET_DK, RET_DV), lambda i: (li, i, 0, 0, 0)),
                 pl.BlockSpec((ld, nseq, 2 * WINDOW, SWA_HEAD_DIM), lambda i: (li, i, 0, 0)),
                 pl.BlockSpec((ld, nseq, 2 * WINDOW, SWA_HEAD_DIM), lambda i: (li, i, 0, 0)))
    assert QC % (IN_WIDTH - QC) == 0
    n_in = 6 + len(consts)
    aliases = {n_in + j: 1 + j for j in range(len(prev_out))}
    return pl.pallas_call(
        functools.partial(_sample_mix_kernel, nseq=nseq, layer=layer, slot=slot, fill=fill),
        grid=(B // nseq,),
        in_specs=in_specs,
        out_specs=out_specs,
        out_shape=out_shape,
        input_output_aliases=aliases,
        compiler_params=pltpu.CompilerParams(
            dimension_semantics=("arbitrary",), vmem_limit_bytes=VMEM_LIMIT),
        name="sample_mix",
    )(prm["sinks"], proj, proj, state_ret, cache_k, cache_v, *consts, *prev_out)


def _sample_out_kernel(x_ref, xb_ref, gb_ref, mix_ref, conv8_ref, h8_ref, convw_ref, convb_ref,
                       wr_ref, wi_ref, br_ref, bi_ref, lam_ref, wout_ref,
                       y_ref, convo_ref, ho_ref):
    R = x_ref.shape[0]
    row = lax.broadcasted_iota(jnp.int32, (R, LRU_WIDTH), 0) & (SEQ_PAD - 1)
    xc = jnp.where(row < TOK0, conv8_ref[...], xb_ref[...])
    convo_ref[...] = xc
    conv = convb_ref[...] + convw_ref[0:1, :] * pltpu.roll(xc, 3, axis=0)
    conv = conv + convw_ref[1:2, :] * pltpu.roll(xc, 2, axis=0)
    conv = conv + convw_ref[2:3, :] * pltpu.roll(xc, 1, axis=0)
    conv = conv + convw_ref[3:4, :] * xc
    halves = [_lru_gates(conv[:, half * LRU_HALF:(half + 1) * LRU_HALF], half, wr_ref, wi_ref, br_ref, bi_ref,
                         lam_ref) for half in range(2)]
    a, mult, ig = (jnp.concatenate(parts, axis=1) for parts in zip(*halves))
    u = mult * ig * conv
    h = h8_ref[...]
    for s in range(4):
        h = jnp.where(row == TOK0 + s, a * pltpu.roll(h, 1, axis=0) + u, h)
    ho_ref[...] = h
    ob = h * _silu(gb_ref[...])
    y = x_ref[...] + _dot(mix_ref[:, 0:GROUP_WIDTH].astype(BF16), wout_ref[0:GROUP_WIDTH, :])
    y = y + _dot(ob.astype(BF16), wout_ref[GROUP_WIDTH:2 * GROUP_WIDTH, :])
    y = y + _dot(mix_ref[:, GROUP_WIDTH:2 * GROUP_WIDTH].astype(BF16), wout_ref[2 * GROUP_WIDTH:, :])
    y_ref[...] = y


def _sample_out(layer, xp, proj, mix, conv8, h8, prm, rows=256):
    R = xp.shape[0]
    consts = [prm[k] for k in ("convw", "convb", "wr", "wi", "br", "bi", "lam", "wout")]
    in_specs = ([pl.BlockSpec((rows, D_MODEL), lambda i: (i, 0)),
                 pl.BlockSpec((rows, LRU_WIDTH), lambda i: (i, XB // LRU_WIDTH)),
                 pl.BlockSpec((rows, LRU_WIDTH), lambda i: (i, GB // LRU_WIDTH)),
                 pl.BlockSpec((rows, 2 * GROUP_WIDTH), lambda i: (i, 0)),
                 pl.BlockSpec((rows, LRU_WIDTH), lambda i: (i, 0)),
                 pl.BlockSpec((rows, LRU_WIDTH), lambda i: (i, 0))]
                + [_layer_spec(a, layer) for a in consts])
    out_shape = (
        jax.ShapeDtypeStruct((R, D_MODEL), F32),
        jax.ShapeDtypeStruct((R, LRU_WIDTH), F32),
        jax.ShapeDtypeStruct((R, LRU_WIDTH), F32),
    )
    out_specs = (
        pl.BlockSpec((rows, D_MODEL), lambda i: (i, 0)),
        pl.BlockSpec((rows, LRU_WIDTH), lambda i: (i, 0)),
        pl.BlockSpec((rows, LRU_WIDTH), lambda i: (i, 0)),
    )
    return pl.pallas_call(
        _sample_out_kernel,
        grid=(R // rows,),
        in_specs=in_specs,
        out_specs=out_specs,
        out_shape=out_shape,
        compiler_params=pltpu.CompilerParams(
            dimension_semantics=("arbitrary",), vmem_limit_bytes=VMEM_LIMIT),
        name="sample_out",
    )(xp, proj, proj, mix, conv8, h8, *consts)


def _prompt_tables():
    C = RET_CHUNK
    idx = np.arange(C, dtype=np.float32)
    diff = idx[:, None] - idx[None, :]
    causal = diff >= 0
    lg = _LOG_G[:, None, None]
    dmask = np.where(causal[None], np.exp(np.where(causal, diff, 0.0)[None] * lg), 0.0).astype(np.float32)
    qdec = np.exp((idx + 1.0)[None, :] * _LOG_G[:, None]).astype(np.float32)
    kdec = np.exp((C - 1 - idx)[None, :] * _LOG_G[:, None]).astype(np.float32)
    qdec = np.broadcast_to(qdec[:, :, None], (RET_HEADS, C, RET_DK)).copy()
    kdec = np.broadcast_to(kdec[:, :, None], (RET_HEADS, C, RET_DK)).copy()
    return jnp.asarray(dmask), jnp.asarray(qdec), jnp.asarray(kdec)


def _sample_tables():
    P = SEQ_PAD
    rows = np.arange(P, dtype=np.float32)
    i = rows - TOK0
    tok = (i >= 0) & (i < 4)
    cm = np.zeros((RET_HEADS, 4, P, RET_DV), np.float32)
    qdec = np.zeros((RET_HEADS, P, RET_DK), np.float32)
    kdec = np.zeros((RET_HEADS, P, RET_DK), np.float32)
    for h in range(RET_HEADS):
        for s in range(4):
            d = i - s
            col = np.where(tok & (d >= 0), np.exp(np.where(d >= 0, d, 0.0) * _LOG_G[h]), 0.0)
            cm[h, s] = col[:, None]
        qdec[h] = np.where(tok, np.exp((i + 1.0) * _LOG_G[h]), 0.0)[:, None]
        kdec[h] = np.where(tok, np.exp((3.0 - i) * _LOG_G[h]), 0.0)[:, None]
    return jnp.asarray(cm), jnp.asarray(qdec), jnp.asarray(kdec)


def _block_diag(w):
    L, n, d, _ = w.shape
    g = n // 2
    w = w.reshape(L, 2, g, d, d)
    eye = jnp.eye(g, dtype=w.dtype)
    return (eye[None, None, :, None, :, None] * w[:, :, :, :, None, :]).reshape(L, 2, g * d, g * d)


def kernel(x_prompt, x_sample, state_ret, state_lru, state_conv, cache_swa_k, cache_swa_v, norm_g, w_in, w_out, ret_norm_g, conv_w, conv_b, w_rgate, b_rgate, w_igate, b_igate, lru_lambda, q_norm_g, k_norm_g, attn_sinks):
    Bs, Ts, _ = x_sample.shape
    ptab = _prompt_tables()
    stab = _sample_tables()
    yp = x_prompt
    ys = jnp.pad(x_sample, ((0, 0), (TOK0, SEQ_PAD - TOK0 - Ts), (0, 0))).reshape(Bs * SEQ_PAD, D_MODEL)
    cache_k = cache_swa_k.reshape(DEPTH, Bs, 2 * WINDOW, SWA_HEAD_DIM)
    cache_v = cache_swa_v.reshape(DEPTH, Bs, 2 * WINDOW, SWA_HEAD_DIM)
    Bp = x_prompt.shape[0]
    p_state, s_state = (), ()
    s_h, s_conv = [], []
    row = lambda a: a.reshape(DEPTH, 1, -1)
    prm = dict(
        sinks=attn_sinks, ng=row(norm_g), retg=row(ret_norm_g),
        convw=conv_w, convb=row(conv_b), wr=_block_diag(w_rgate).astype(BF16), wi=_block_diag(w_igate).astype(BF16),
        br=row(b_rgate), bi=row(b_igate), lam=row(lru_lambda), qg=row(q_norm_g), kg=row(k_norm_g))
    proj, prm["win"], prm["wout"] = _sample_proj(0, ys, prm["ng"], w_in, w_out)
    for l in range(DEPTH):
        yp, *p_state = _prompt_layer(l, yp, prm, ptab, tuple(p_state))
        if l > 0:
            proj = _sample_proj(l, ys, prm["ng"], prm["win"])
        mix, *s_state = _sample_mix(l, proj, state_ret, cache_k, cache_v, prm, stab, tuple(s_state))
        conv8 = jnp.pad(state_conv[l], ((0, 0), (0, SEQ_PAD - (CONV_W - 1)), (0, 0))).reshape(Bs * SEQ_PAD, LRU_WIDTH)
        h8 = jnp.pad(state_lru[l][:, None, :], ((0, 0), (TOK0 - 1, SEQ_PAD - TOK0), (0, 0))).reshape(Bs * SEQ_PAD, LRU_WIDTH)
        ys, convo, ho = _sample_out(l, ys, proj, mix, conv8, h8, prm)
        s_h.append(ho.reshape(Bs, SEQ_PAD, LRU_WIDTH)[:, TOK0 + Ts - 1])
        s_conv.append(convo.reshape(Bs, SEQ_PAD, LRU_WIDTH)[:, TOK0 + Ts - (CONV_W - 1):TOK0 + Ts])

    p_ret, p_h, p_conv, p_k, p_v = p_state
    s_ret, s_k, s_v = s_state
    kv5 = lambda a, n: a.reshape(DEPTH, n, WINDOW, SWA_KV_HEADS, SWA_HEAD_DIM)
    y_sample = ys.reshape(Bs, SEQ_PAD, D_MODEL)[:, TOK0:TOK0 + Ts]
    return (yp, y_sample,
            p_ret, p_h.reshape(DEPTH, Bp, LRU_WIDTH), p_conv, kv5(p_k, Bp), kv5(p_v, Bp),
            s_ret, jnp.stack(s_h), jnp.stack(s_conv), kv5(s_k, Bs), kv5(s_v, Bs))
```
